```python
import math
import jax, jax.numpy as jnp
from jax import lax
import numpy as np

D_MODEL = 1024
BATCH = 8
SEQ = 4096
DEPTH = 2

BLOCK_Q = 128
N_BRANCHES = 4
BRANCH_WIDTH = 256
A_HEADS = 4
A_DIM = 64
IDX_HEADS = 4
IDX_DIM = 64
TOPK_MAX = 256
MLA_HEADS = 4
MLA_Q_RANK = 256
MLA_KV_RANK = 128
MLA_NOPE = 64
MLA_ROPE = 32
MLA_V = 64
ROPE_THETA = 10000.0
SB_HEADS = 4
SB_DIM = 64
DIFF_HEADS = 4
DIFF_QK = 32
DIFF_V = 2 * DIFF_QK
N_ALIBI_HEADS = A_HEADS + DIFF_HEADS
D_FF = 2816
N_EXPERTS = 8
TOP_K = 2
D_FF_EXPERT = 3584
EPS = 1e-6

IN_SIZES = (
    A_HEADS * A_DIM, A_DIM, A_DIM,
    IDX_HEADS * IDX_DIM, IDX_DIM, IDX_HEADS,
    MLA_Q_RANK, MLA_KV_RANK, MLA_ROPE,
    SB_HEADS * SB_DIM, SB_HEADS * SB_DIM, SB_HEADS * SB_DIM,
    DIFF_HEADS * 2 * DIFF_QK, DIFF_HEADS * 2 * DIFF_QK, DIFF_HEADS * DIFF_V,
    N_BRANCHES * D_MODEL,
)
IN_COLS = sum(IN_SIZES)

kernel_name = 'hybrid_dsa_mla_stickbreak_diff_moe_block'

f32 = jnp.float32


def _rmsnorm(x, g):
    xf = x.astype(f32)
    y = xf * lax.rsqrt(jnp.mean(xf * xf, axis=-1, keepdims=True) + EPS)
    return (y * g.astype(f32)).astype(x.dtype)


def _split_cols(p):
    out, off = [], 0
    for n in IN_SIZES:
        out.append(p[..., off:off + n])
        off += n
    return out


def _qslice(a, i):
    return lax.dynamic_slice_in_dim(a, i * BLOCK_Q, BLOCK_Q, axis=1)


def _blocks(fn, seq):
    ys = lax.map(fn, jnp.arange(seq // BLOCK_Q))
    nb, b, bq = ys.shape[:3]
    return jnp.moveaxis(ys, 0, 1).reshape((b, nb * bq) + ys.shape[3:])


def _alibi_slopes():
    n = N_ALIBI_HEADS
    sl = 2.0 ** (-(8.0 / n) * jnp.arange(1, n + 1, dtype=f32))
    return sl[0::2], sl[1::2]


def _rope_angles(positions):
    half = MLA_ROPE // 2
    inv = ROPE_THETA ** (-jnp.arange(half, dtype=f32) / half)
    ang = positions.astype(f32)[..., None] * inv
    return jnp.cos(ang), jnp.sin(ang)


def _rope(x, cos, sin):
    xf = x.astype(f32)
    half = xf.shape[-1] // 2
    x1, x2 = xf[..., :half], xf[..., half:]
    return jnp.concatenate([x1 * cos - x2 * sin, x1 * sin + x2 * cos], axis=-1).astype(x.dtype)


def _dsa(q, k, v, iq, ik, iw, slopes):
    b, s = q.shape[:2]
    topk = min(TOPK_MAX, s // 4)
    keys = jnp.arange(s)
    w = iw.astype(f32) * (IDX_HEADS ** -0.5)

    def block(i):
        t = i * BLOCK_Q + jnp.arange(BLOCK_Q)
        rel = jax.nn.relu(jnp.einsum('bqhd,bkd->bqhk', _qslice(iq, i), ik, preferred_element_type=f32) * (IDX_DIM ** -0.5))
        score = jnp.einsum('bqhk,bqh->bqk', rel, _qslice(w, i))
        score = jnp.where(keys[None, None, :] <= t[None, :, None], score, -jnp.inf)
        _, sel = lax.top_k(score, topk)
        ksel = jax.vmap(lambda kb, ib: kb[ib])(k, sel)
        vsel = jax.vmap(lambda vb, ib: vb[ib])(v, sel)
        dist = t[None, :, None] - sel
        lg = jnp.einsum('bqhd,bqkd->bqhk', _qslice(q, i), ksel, preferred_element_type=f32) * (A_DIM ** -0.5)
        lg = lg - slopes[None, None, :, None] * dist[:, :, None, :].astype(f32)
        lg = jnp.where((dist >= 0)[:, :, None, :], lg, -jnp.inf)
        p = jax.nn.softmax(lg, axis=-1).astype(v.dtype)
        return jnp.einsum('bqhk,bqkd->bqhd', p, vsel)

    return _blocks(block, s)


def _mla(cq, ckv, kr, q_norm_g, kv_norm_g, w_uq, w_ukv, positions):
    b, s = cq.shape[:2]
    qf = (_rmsnorm(cq, q_norm_g) @ w_uq).reshape(b, s, MLA_HEADS, MLA_NOPE + MLA_ROPE)
    kv = (_rmsnorm(ckv, kv_norm_g) @ w_ukv).reshape(b, s, MLA_HEADS, MLA_NOPE + MLA_V)
    q_nope, k_nope, v = qf[..., :MLA_NOPE], kv[..., :MLA_NOPE], kv[..., MLA_NOPE:]
    cos, sin = _rope_angles(positions)
    q_rope = _rope(qf[..., MLA_NOPE:], cos[:, :, None, :], sin[:, :, None, :])
    k_rope = _rope(kr, cos, sin)
    scale = (MLA_NOPE + MLA_ROPE) ** -0.5
    keys = jnp.arange(s)

    def block(i):
        t = i * BLOCK_Q + jnp.arange(BLOCK_Q)
        lg = (jnp.einsum('bqhd,bkhd->bhqk', _qslice(q_nope, i), k_nope, preferred_element_type=f32)
              + jnp.einsum('bqhd,bkd->bhqk', _qslice(q_rope, i), k_rope, preferred_element_type=f32)) * scale
        lg = jnp.where(keys[None, :] <= t[:, None], lg, -jnp.inf)
        p = jax.nn.softmax(lg, axis=-1).astype(v.dtype)
        return jnp.einsum('bhqk,bkhd->bqhd', p, v)

    return _blocks(block, s)


def _stick_breaking(q, k, v):
    b, s = q.shape[:2]
    keys = jnp.arange(s)

    def block(i):
        t = i * BLOCK_Q + jnp.arange(BLOCK_Q)
        z = jnp.einsum('bqhd,bkhd->bhqk', _qslice(q, i), k, preferred_element_type=f32) * (SB_DIM ** -0.5)
        strict = keys[None, :] < t[:, None]
        log_1m = jnp.where(strict, jax.nn.log_sigmoid(-z), 0.0)
        between = lax.cumsum(log_1m, axis=3, reverse=True) - log_1m
        a = jnp.where(strict, jnp.exp(jax.nn.log_sigmoid(z) + between), 0.0).astype(v.dtype)
        return jnp.einsum('bhqk,bkhd->bqhd', a, v)

    return _blocks(block, s)


def _diff_attn(q, k, v, lq1, lk1, lq2, lk2, norm_g, slopes, lam_init):
    b, s = q.shape[:2]
    lam = (jnp.exp(jnp.sum(lq1.astype(f32) * lk1.astype(f32)))
           - jnp.exp(jnp.sum(lq2.astype(f32) * lk2.astype(f32))) + lam_init)
    keys = jnp.arange(s)

    def block(i):
        t = i * BLOCK_Q + jnp.arange(BLOCK_Q)
        lg = jnp.einsum('bqhmd,bkhmd->bhmqk', _qslice(q, i), k, preferred_element_type=f32) * (DIFF_QK ** -0.5)
        dist = (t[:, None] - keys[None, :]).astype(f32)
        lg = lg - slopes[None, :, None, None, None] * dist
        lg = jnp.where(keys[None, :] <= t[:, None], lg, -jnp.inf)
        p = jax.nn.softmax(lg, axis=-1)
        attn = (p[:, :, 0] - lam * p[:, :, 1]).astype(v.dtype)
        return jnp.einsum('bhqk,bkhd->bqhd', attn, v)

    o = _blocks(block, s)
    return _rmsnorm(o, norm_g) * (1.0 - lam_init)


def _hybrid_mixer(h, positions, w_in, q_norm_g, kv_norm_g, w_uq, w_ukv,
                  lq1, lk1, lq2, lk2, diff_norm_g, w_branch, w_out, lam_init):
    b, s, _ = h.shape
    (aq, ak, av, iq, ik, iw, cq, ckv, kr, sq, sk, sv, dq, dk, dv, gates) = _split_cols(h @ w_in)
    slopes_a, slopes_d = _alibi_slopes()
    ya = _dsa(aq.reshape(b, s, A_HEADS, A_DIM), ak, av,
              iq.reshape(b, s, IDX_HEADS, IDX_DIM), ik, iw, slopes_a)
    yb = _mla(cq, ckv, kr, q_norm_g, kv_norm_g, w_uq, w_ukv, positions)
    yc = _stick_breaking(sq.reshape(b, s, SB_HEADS, SB_DIM), sk.reshape(b, s, SB_HEADS, SB_DIM),
                         sv.reshape(b, s, SB_HEADS, SB_DIM))
    yd = _diff_attn(dq.reshape(b, s, DIFF_HEADS, 2, DIFF_QK), dk.reshape(b, s, DIFF_HEADS, 2, DIFF_QK),
                    dv.reshape(b, s, DIFF_HEADS, DIFF_V), lq1, lk1, lq2, lk2, diff_norm_g, slopes_d, lam_init)
    g = jax.nn.sigmoid(gates.astype(f32)).astype(h.dtype).reshape(b, s, N_BRANCHES, D_MODEL)
    merged = jnp.zeros_like(h)
    for n, y in enumerate((ya, yb, yc, yd)):
        merged = merged + g[:, :, n] * (y.reshape(b, s, BRANCH_WIDTH) @ w_branch[n])
    return merged @ w_out


def _swiglu(h, w1, w3, w2):
    return (jax.nn.silu(h @ w1) * (h @ w3)) @ w2


def _moe(h, router_w, router_b, w1, w3, w2):
    b, s, d = h.shape
    tok = h.reshape(b * s, d)
    logits = (tok @ router_w).astype(f32) + router_b.astype(f32)
    top_val, top_idx = lax.top_k(logits, TOP_K)
    top_w = jax.nn.softmax(top_val, axis=-1)
    combine = jnp.sum(jax.nn.one_hot(top_idx, N_EXPERTS, dtype=f32) * top_w[..., None], axis=1).astype(h.dtype)
    out = jnp.zeros_like(tok)
    for e in range(N_EXPERTS):
        out = out + combine[:, e:e + 1] * _swiglu(tok, w1[e], w3[e], w2[e])
    return out.reshape(b, s, d)


def setup_inputs(seed: int = 0) -> dict:
    key = jax.random.key(seed)
    ks = iter(jax.random.split(key, 32))

    def nrm(shape, scale):
        return jax.random.normal(next(ks), shape, f32) * scale

    def gain(shape):
        return 1.0 + nrm(shape, 0.02)

    L, D = DEPTH, D_MODEL
    nd, nm = (DEPTH + 1) // 2, DEPTH // 2
    return {
        'x': nrm((BATCH, SEQ, D), 1.0),
        'c': nrm((BATCH, D), 1.0),
        'positions': jax.random.randint(next(ks), (BATCH, 1), 0, SEQ, dtype=jnp.int32) + jnp.arange(SEQ, dtype=jnp.int32)[None, :],
        'ada_w': nrm((L, D, 6 * D), 0.5 * D ** -0.5),
        'ada_b': nrm((L, 6 * D), 0.02),
        'norm1_g': gain((L, D)),
        'norm2_g': gain((L, D)),
        'w_in': nrm((L, D, IN_COLS), D ** -0.5),
        'mla_q_norm_g': gain((L, MLA_Q_RANK)),
        'mla_kv_norm_g': gain((L, MLA_KV_RANK)),
        'mla_w_uq': nrm((L, MLA_Q_RANK, MLA_HEADS * (MLA_NOPE + MLA_ROPE)), MLA_Q_RANK ** -0.5),
        'mla_w_ukv': nrm((L, MLA_KV_RANK, MLA_HEADS * (MLA_NOPE + MLA_V)), MLA_KV_RANK ** -0.5),
        'diff_lq1': nrm((L, DIFF_QK), 0.1),
        'diff_lk1': nrm((L, DIFF_QK), 0.1),
        'diff_lq2': nrm((L, DIFF_QK), 0.1),
        'diff_lk2': nrm((L, DIFF_QK), 0.1),
        'diff_norm_g': gain((L, DIFF_V)),
        'w_branch': nrm((L, N_BRANCHES, BRANCH_WIDTH, D), BRANCH_WIDTH ** -0.5),
        'w_out': nrm((L, D, D), D ** -0.5),
        'ffn_w1': nrm((nd, D, D_FF), D ** -0.5),
        'ffn_w3': nrm((nd, D, D_FF), D ** -0.5),
        'ffn_w2': nrm((nd, D_FF, D), D_FF ** -0.5),
        'router_w': nrm((nm, D, N_EXPERTS), D ** -0.5),
        'router_b': nrm((nm, N_EXPERTS), 0.01),
        'moe_w1': nrm((nm, N_EXPERTS, D, D_FF_EXPERT), D ** -0.5),
        'moe_w3': nrm((nm, N_EXPERTS, D, D_FF_EXPERT), D ** -0.5),
        'moe_w2': nrm((nm, N_EXPERTS, D_FF_EXPERT, D), D_FF_EXPERT ** -0.5),
        'final_norm_g': gain((D,)),
    }


def reference(x, c, positions, ada_w, ada_b, norm1_g, norm2_g, w_in, mla_q_norm_g, mla_kv_norm_g,
              mla_w_uq, mla_w_ukv, diff_lq1, diff_lk1, diff_lq2, diff_lk2, diff_norm_g, w_branch, w_out,
              ffn_w1, ffn_w3, ffn_w2, router_w, router_b, moe_w1, moe_w3, moe_w2, final_norm_g):
    cond = jax.nn.silu(c)
    for l in range(DEPTH):
        mod = (cond @ ada_w[l] + ada_b[l])[:, None, :]
        sh1, sc1, g1, sh2, sc2, g2 = jnp.split(mod, 6, axis=-1)
        lam_init = 0.8 - 0.6 * math.exp(-0.3 * l)
        h = _rmsnorm(x, norm1_g[l]) * (1.0 + sc1) + sh1
        x = x + g1 * _hybrid_mixer(h, positions, w_in[l], mla_q_norm_g[l], mla_kv_norm_g[l], mla_w_uq[l],
                                   mla_w_ukv[l], diff_lq1[l], diff_lk1[l], diff_lq2[l], diff_lk2[l],
                                   diff_norm_g[l], w_branch[l], w_out[l], lam_init)
        h = _rmsnorm(x, norm2_g[l]) * (1.0 + sc2) + sh2
        if l % 2 == 0:
            f = _swiglu(h, ffn_w1[l // 2], ffn_w3[l // 2], ffn_w2[l // 2])
        else:
            f = _moe(h, router_w[l // 2], router_b[l // 2], moe_w1[l // 2], moe_w3[l // 2], moe_w2[l // 2])
        x = x + g2 * f
    return _rmsnorm(x, final_norm_g)
```

```python
import functools
import math

import jax
import jax.numpy as jnp
from jax import lax
from jax.experimental import pallas as pl
from jax.experimental.pallas import tpu as pltpu

f32 = jnp.float32
bf16 = jnp.bfloat16
i32 = jnp.int32

N_BRANCHES = 4
HEADS = 4
HEAD_W = 64
IDX_HEADS = 4
TOPK_MAX = 256
MLA_Q_RANK = 256
MLA_KV_RANK = 128
MLA_NOPE = 64
MLA_ROPE = 32
MLA_V = 64
MLA_QK_PAD = 128
ROPE_THETA = 10000.0
DIFF_QK = 32
N_EXPERTS = 8
TOP_K = 2
EPS = 1e-6
NEG_INF = float("-inf")
INT_MIN = -2 ** 31

VMEM_LIMIT = 56 * 1024 * 1024


def _cparams(sem):
    return pltpu.CompilerParams(dimension_semantics=sem, vmem_limit_bytes=VMEM_LIMIT)


def _nt_dot(a, b):
    return lax.dot_general(a, b, (((1,), (1,)), ((), ())), preferred_element_type=f32)


def _dot(a, b):
    return jnp.dot(a, b, preferred_element_type=f32)


def _norm_mod(x, g, sc, sh):
    y = x * lax.rsqrt(jnp.mean(x * x, axis=-1, keepdims=True) + EPS)
    return (y * g) * (1.0 + sc) + sh


def _mod_kernel(c_ref, w_ref, b_ref, o_ref):
    c = c_ref[...]
    cond = (c * jax.nn.sigmoid(c)).astype(bf16)
    o_ref[0] = _dot(cond, w_ref[0].astype(bf16)) + b_ref[0]


def _modulation(c, ada_w, ada_b):
    L, D, N = ada_w.shape
    B = c.shape[0]
    tn = 1536
    return pl.pallas_call(
        _mod_kernel,
        grid=(L, N // tn),
        in_specs=[
            pl.BlockSpec((B, D), lambda l, j: (0, 0)),
            pl.BlockSpec((1, D, tn), lambda l, j: (l, 0, j)),
            pl.BlockSpec((1, 1, tn), lambda l, j: (l, 0, j)),
        ],
        out_specs=pl.BlockSpec((1, B, tn), lambda l, j: (l, 0, j)),
        out_shape=jax.ShapeDtypeStruct((L, B, N), f32),
        compiler_params=_cparams(("arbitrary", "arbitrary")),
        name="modulation",
    )(c, ada_w, ada_b.reshape(L, 1, N))


def _inproj_kernel(x_ref, g_ref, sc_ref, sh_ref, w_ref, o_ref, h_scr, *, grouped):
    @pl.when(pl.program_id(1) == 0)
    def _():
        h_scr[...] = _norm_mod(x_ref[...], g_ref[...], sc_ref[0], sh_ref[0]).astype(bf16)

    res = _dot(h_scr[...], w_ref[...])
    if grouped:
        for k in range(res.shape[1] // HEAD_W):
            o_ref[k] = res[:, k * HEAD_W:(k + 1) * HEAD_W].astype(o_ref.dtype)
    else:
        o_ref[...] = res.astype(o_ref.dtype)


def _inproj(x2, g, sc, sh, w, S, *, grouped, out_dtype, tm, tn):
    T, D = x2.shape
    N = w.shape[1]
    tpb = S // tm
    if grouped:
        out_shape = jax.ShapeDtypeStruct((N // HEAD_W, T, HEAD_W), out_dtype)
        out_spec = pl.BlockSpec((tn // HEAD_W, tm, HEAD_W), lambda i, j: (j, i, 0))
    else:
        out_shape = jax.ShapeDtypeStruct((T, N), out_dtype)
        out_spec = pl.BlockSpec((tm, tn), lambda i, j: (i, j))
    return pl.pallas_call(
        functools.partial(_inproj_kernel, grouped=grouped),
        grid=(T // tm, N // tn),
        in_specs=[
            pl.BlockSpec((tm, D), lambda i, j: (i, 0)),
            pl.BlockSpec((1, D), lambda i, j: (0, 0)),
            pl.BlockSpec((1, 1, D), lambda i, j: (i // tpb, 0, 0)),
            pl.BlockSpec((1, 1, D), lambda i, j: (i // tpb, 0, 0)),
            pl.BlockSpec((D, tn), lambda i, j: (0, j)),
        ],
        out_specs=out_spec,
        out_shape=out_shape,
        scratch_shapes=[pltpu.VMEM((tm, D), bf16)],
        compiler_params=_cparams(("arbitrary", "arbitrary")),
        name="inproj_grouped" if grouped else "inproj_plain",
    )(x2, g, sc, sh, w)


def _mla_prep_kernel(f_ref, pos_ref, gq_ref, gkv_ref, wq_ref, wqp_ref, wk_ref, wv_ref, inv_ref,
                     q_ref, k_ref, v_ref, *, scale):
    cq = f_ref[:, 0:MLA_Q_RANK]
    ckv = f_ref[:, MLA_Q_RANK:MLA_Q_RANK + MLA_KV_RANK]
    kr = f_ref[:, 384:512]
    krp = f_ref[:, 512:640]
    nq = (cq * lax.rsqrt(jnp.mean(cq * cq, axis=-1, keepdims=True) + EPS) * gq_ref[...]).astype(bf16)
    nkv = (ckv * lax.rsqrt(jnp.mean(ckv * ckv, axis=-1, keepdims=True) + EPS) * gkv_ref[...]).astype(bf16)
    ang = pos_ref[...].astype(f32) * inv_ref[...]
    cosf = jnp.cos(ang)
    sinf = jnp.sin(ang)
    q1 = _dot(nq, wq_ref[...])
    q2 = _dot(nq, wqp_ref[...])
    kn = _dot(nkv, wk_ref[...])
    vv = _dot(nkv, wv_ref[...])
    krope = kr * cosf + krp * sinf
    for h in range(HEADS):
        sl = slice(h * MLA_QK_PAD, (h + 1) * MLA_QK_PAD)
        q_ref[h] = ((q1[:, sl] * cosf + q2[:, sl] * sinf) * scale).astype(bf16)
        k_ref[h] = (kn[:, sl] + krope).astype(bf16)
        v_ref[h] = vv[:, h * MLA_V:(h + 1) * MLA_V].astype(bf16)


def _mla_prep(F, pos2, gq, gkv, wq, wqp, wk, wv, inv_full, *, tm):
    T = F.shape[0]
    scale = (MLA_NOPE + MLA_ROPE) ** -0.5
    full = lambda a: pl.BlockSpec(a.shape, lambda i: (0,) * a.ndim)
    return pl.pallas_call(
        functools.partial(_mla_prep_kernel, scale=scale),
        grid=(T // tm,),
        in_specs=[
            pl.BlockSpec((tm, F.shape[1]), lambda i: (i, 0)),
            pl.BlockSpec((tm, 1), lambda i: (i, 0)),
            full(gq), full(gkv), full(wq), full(wqp), full(wk), full(wv), full(inv_full),
        ],
        out_specs=[
            pl.BlockSpec((HEADS, tm, MLA_QK_PAD), lambda i: (0, i, 0)),
            pl.BlockSpec((HEADS, tm, MLA_QK_PAD), lambda i: (0, i, 0)),
            pl.BlockSpec((HEADS, tm, MLA_V), lambda i: (0, i, 0)),
        ],
        out_shape=[
            jax.ShapeDtypeStruct((HEADS, T, MLA_QK_PAD), bf16),
            jax.ShapeDtypeStruct((HEADS, T, MLA_QK_PAD), bf16),
            jax.ShapeDtypeStruct((HEADS, T, MLA_V), bf16),
        ],
        compiler_params=_cparams(("arbitrary",)),
        name="mla_prep",
    )(F, pos2, gq, gkv, wq, wqp, wk, wv, inv_full)


def _softmax_step(s, v, m, l, acc):
    m_new = jnp.maximum(m, jnp.max(s, axis=1, keepdims=True))
    alpha = jnp.exp(m - m_new)
    p = jnp.exp(s - m_new)
    l = alpha * l + jnp.sum(p, axis=1, keepdims=True)
    acc = alpha * acc + _dot(p.astype(bf16), v)
    return m_new, l, acc


def _causal_mask(bq, bk):
    row = lax.broadcasted_iota(i32, (bq, bk), 0)
    col = lax.broadcasted_iota(i32, (bq, bk), 1)
    return col <= row


def _mla_attn_kernel(q_ref, k_ref, v_ref, o_ref, *, blk):
    i = pl.program_id(2)
    q = q_ref[0]
    dv = v_ref.shape[2]

    def kv(kb):
        off = pl.multiple_of(kb * blk, blk)
        return k_ref[0, pl.ds(off, blk), :], v_ref[0, pl.ds(off, blk), :]

    def body(kb, carry):
        k, v = kv(kb)
        return _softmax_step(_nt_dot(q, k), v, *carry)

    init = (jnp.full((blk, 1), NEG_INF, f32), jnp.zeros((blk, 1), f32), jnp.zeros((blk, dv), f32))
    carry = lax.fori_loop(0, i, body, init)
    k, v = kv(i)
    s = jnp.where(_causal_mask(blk, blk), _nt_dot(q, k), NEG_INF)
    m, l, acc = _softmax_step(s, v, *carry)
    o_ref[0] = (acc / l).astype(o_ref.dtype)


def _mla_attention(qm, km, vm, B, S, *, blk):
    H, T, dk = qm.shape
    dv = vm.shape[2]
    nq = S // blk
    return pl.pallas_call(
        functools.partial(_mla_attn_kernel, blk=blk),
        grid=(B, H, nq),
        in_specs=[
            pl.BlockSpec((1, blk, dk), lambda b, h, i: (h, b * nq + i, 0)),
            pl.BlockSpec((1, S, dk), lambda b, h, i: (h, b, 0)),
            pl.BlockSpec((1, S, dv), lambda b, h, i: (h, b, 0)),
        ],
        out_specs=pl.BlockSpec((1, blk, dv), lambda b, h, i: (h, b * nq + i, 0)),
        out_shape=jax.ShapeDtypeStruct((H, T, dv), bf16),
        compiler_params=_cparams(("arbitrary", "arbitrary", "arbitrary")),
        name="mla_attention",
    )(qm, km, vm)


def _diff_attn_kernel(q_ref, k_ref, v_ref, slope_ref, lq1_ref, lk1_ref, lq2_ref, lk2_ref, g_ref, o_ref,
                      *, blk, lam_init):
    i = pl.program_id(2)
    q = q_ref[0]
    lane = lax.broadcasted_iota(i32, q.shape, 1)
    zero = jnp.zeros_like(q)
    q0 = jnp.where(lane < DIFF_QK, q, zero)
    q1 = jnp.where(lane < DIFF_QK, zero, q)
    dv = v_ref.shape[2]
    slope = slope_ref[0]
    col = lax.broadcasted_iota(i32, (1, blk), 1).astype(f32)

    def kv(kb):
        off = pl.multiple_of(kb * blk, blk)
        return k_ref[0, pl.ds(off, blk), :], v_ref[0, pl.ds(off, blk), :]

    def bias(kb):
        return slope * (col + ((kb - i) * blk).astype(f32))

    def body(kb, carry):
        k, v = kv(kb)
        b = bias(kb)
        c0 = _softmax_step(_nt_dot(q0, k) + b, v, *carry[0:3])
        c1 = _softmax_step(_nt_dot(q1, k) + b, v, *carry[3:6])
        return c0 + c1

    one = (jnp.full((blk, 1), NEG_INF, f32), jnp.zeros((blk, 1), f32), jnp.zeros((blk, dv), f32))
    carry = lax.fori_loop(0, i, body, one + one)
    k, v = kv(i)
    b = bias(i)
    mask = _causal_mask(blk, blk)
    m0, l0, a0 = _softmax_step(jnp.where(mask, _nt_dot(q0, k) + b, NEG_INF), v, *carry[0:3])
    m1, l1, a1 = _softmax_step(jnp.where(mask, _nt_dot(q1, k) + b, NEG_INF), v, *carry[3:6])
    lam = (jnp.exp(jnp.sum(lq1_ref[...] * lk1_ref[...], axis=1, keepdims=True))
           - jnp.exp(jnp.sum(lq2_ref[...] * lk2_ref[...], axis=1, keepdims=True)) + lam_init)
    o = a0 / l0 - lam * (a1 / l1)
    y = o * lax.rsqrt(jnp.mean(o * o, axis=-1, keepdims=True) + EPS) * g_ref[...]
    o_ref[0] = (y * (1.0 - lam_init)).astype(o_ref.dtype)


def _diff_attention(P, slopes, lq1, lk1, lq2, lk2, g, B, S, *, blk, lam_init, gq, gk, gv):
    G, T, hw = P.shape
    nq = S // blk
    vec = lambda a: pl.BlockSpec(a.shape, lambda b, h, i: (0, 0))
    return pl.pallas_call(
        functools.partial(_diff_attn_kernel, blk=blk, lam_init=lam_init),
        grid=(B, HEADS, nq),
        in_specs=[
            pl.BlockSpec((1, blk, hw), lambda b, h, i: (gq + h, b * nq + i, 0)),
            pl.BlockSpec((1, S, hw), lambda b, h, i: (gk + h, b, 0)),
            pl.BlockSpec((1, S, hw), lambda b, h, i: (gv + h, b, 0)),
            pl.BlockSpec((1, 1, blk), lambda b, h, i: (h, 0, 0)),
            vec(lq1), vec(lk1), vec(lq2), vec(lk2), vec(g),
        ],
        out_specs=pl.BlockSpec((1, blk, hw), lambda b, h, i: (h, b * nq + i, 0)),
        out_shape=jax.ShapeDtypeStruct((HEADS, T, hw), bf16),
        compiler_params=_cparams(("arbitrary", "arbitrary", "arbitrary")),
        name="diff_attention",
    )(P, P, P, slopes, lq1, lk1, lq2, lk2, g)


def _sb_attn_kernel(q_ref, k_ref, v_ref, o_ref, *, blk):
    i = pl.program_id(2)
    q = q_ref[0]
    dv = v_ref.shape[2]
    rr = lax.broadcasted_iota(i32, (blk, blk), 0)
    cc = lax.broadcasted_iota(i32, (blk, blk), 1)
    upper = jnp.where(rr > cc, 1.0, 0.0).astype(bf16)
    strict = cc < rr

    def kv(kb):
        off = pl.multiple_of(kb * blk, blk)
        return k_ref[0, pl.ds(off, blk), :], v_ref[0, pl.ds(off, blk), :]

    def block(kb, run, acc, diag):
        k, v = kv(kb)
        z = _nt_dot(q, k)
        lsm = -(jnp.maximum(z, 0.0) + jnp.log(1.0 + jnp.exp(-jnp.abs(z))))
        if diag:
            lsm = jnp.where(strict, lsm, 0.0)
        hi = lsm.astype(bf16)
        lo = (lsm - hi.astype(f32)).astype(bf16)
        between = _dot(hi, upper) + _dot(lo, upper) + run
        a = jnp.exp(z + lsm + between)
        if diag:
            a = jnp.where(strict, a, 0.0)
        acc = acc + _dot(a.astype(bf16), v)
        run = run + jnp.sum(lsm, axis=1, keepdims=True)
        return run, acc

    run, acc = block(i, jnp.zeros((blk, 1), f32), jnp.zeros((blk, dv), f32), True)

    def body(n, carry):
        return block(i - 1 - n, carry[0], carry[1], False)

    run, acc = lax.fori_loop(0, i, body, (run, acc))
    o_ref[0] = acc.astype(o_ref.dtype)


def _sb_attention(P, B, S, *, blk, gq, gk, gv):
    G, T, hw = P.shape
    nq = S // blk
    return pl.pallas_call(
        functools.partial(_sb_attn_kernel, blk=blk),
        grid=(B, HEADS, nq),
        in_specs=[
            pl.BlockSpec((1, blk, hw), lambda b, h, i: (gq + h, b * nq + i, 0)),
            pl.BlockSpec((1, S, hw), lambda b, h, i: (gk + h, b, 0)),
            pl.BlockSpec((1, S, hw), lambda b, h, i: (gv + h, b, 0)),
        ],
        out_specs=pl.BlockSpec((1, blk, hw), lambda b, h, i: (h, b * nq + i, 0)),
        out_shape=jax.ShapeDtypeStruct((HEADS, T, hw), bf16),
        compiler_params=_cparams(("arbitrary", "arbitrary", "arbitrary")),
        name="sb_attention",
    )(P, P, P)


def _dsa_kernel(aq_ref, iq_ref, ak_ref, av_ref, ik_ref, iw_ref, slope_ref, o_ref, key_scr, *, blk, topk):
    i = pl.program_id(1)
    nblk = i + 1
    w = iw_ref[...]
    rr = lax.broadcasted_iota(i32, (blk, blk), 0)
    cc = lax.broadcasted_iota(i32, (blk, blk), 1)
    incl = jnp.where(rr <= cc, 1.0, 0.0).astype(bf16)
    col = lax.broadcasted_iota(i32, (1, blk), 1).astype(f32)

    def koff(kb):
        return pl.multiple_of(kb * blk, blk)

    def valid(kb):
        return (cc + (kb - i) * blk) <= rr

    def score_body(kb, _):
        ik = ik_ref[0, pl.ds(koff(kb), blk), :]
        sc = jnp.zeros((blk, blk), f32)
        for h in range(IDX_HEADS):
            sc = sc + jnp.maximum(_nt_dot(iq_ref[h], ik), 0.0) * w[:, h:h + 1]
        sc = jnp.where(valid(kb), sc, NEG_INF) + 0.0
        bits = lax.bitcast_convert_type(sc, i32)
        key_scr[kb] = jnp.where(bits < 0, bits ^ jnp.int32(0x7FFFFFFF), bits)
        return 0

    lax.fori_loop(0, nblk, score_body, 0)

    def count(pred):
        def body(kb, acc):
            return acc + jnp.where(pred(key_scr[kb]), 1.0, 0.0)
        acc = lax.fori_loop(0, nblk, body, jnp.zeros((blk, blk), f32))
        return jnp.sum(acc, axis=1, keepdims=True)

    def search_body(it, prefix):
        cand = prefix + lax.shift_left(jnp.int32(1), jnp.int32(31) - it)
        cand_b = jnp.broadcast_to(cand, (blk, blk))
        cnt = count(lambda key: key >= cand_b)
        return jnp.where(cnt >= float(topk), cand, prefix)

    tau = lax.fori_loop(0, 32, search_body, jnp.full((blk, 1), INT_MIN, i32))
    tau_b = jnp.broadcast_to(tau, (blk, blk))
    n_take = float(topk) - count(lambda key: key > tau_b)

    def attn_body(kb, carry):
        eq_run = carry[0]
        key = key_scr[kb]
        eq = key == tau_b
        pc = _dot(jnp.where(eq, 1.0, 0.0).astype(bf16), incl) + eq_run
        addm = jnp.where(key > tau_b, 0.0, jnp.where(eq, jnp.where(pc <= n_take, 0.0, NEG_INF), NEG_INF))
        addm = jnp.where(valid(kb), addm, NEG_INF)
        eq_run = eq_run + jnp.sum(jnp.where(eq, 1.0, 0.0), axis=1, keepdims=True)
        k = ak_ref[0, pl.ds(koff(kb), blk), :]
        v = av_ref[0, pl.ds(koff(kb), blk), :]
        pos = col + ((kb - i) * blk).astype(f32)
        out = [eq_run]
        for h in range(HEADS):
            m, l, acc = carry[1 + 3 * h:4 + 3 * h]
            s = _nt_dot(aq_ref[h], k) + slope_ref[h] * pos + addm
            m_new = jnp.maximum(m, jnp.max(s, axis=1, keepdims=True))
            m_safe = jnp.where(m_new == NEG_INF, 0.0, m_new)
            alpha = jnp.exp(m - m_safe)
            p = jnp.exp(s - m_safe)
            l = alpha * l + jnp.sum(p, axis=1, keepdims=True)
            acc = alpha * acc + _dot(p.astype(bf16), v)
            out += [m_new, l, acc]
        return tuple(out)

    dv = av_ref.shape[2]
    init = [jnp.zeros((blk, 1), f32)]
    for h in range(HEADS):
        init += [jnp.full((blk, 1), NEG_INF, f32), jnp.zeros((blk, 1), f32), jnp.zeros((blk, dv), f32)]
    res = lax.fori_loop(0, nblk, attn_body, tuple(init))
    for h in range(HEADS):
        m, l, acc = res[1 + 3 * h:4 + 3 * h]
        o_ref[h] = (acc / l).astype(o_ref.dtype)


def _dsa_attention(P, F, slopes, B, S, *, blk, topk, gaq, giq, gak, gav, gik, iw_col):
    G, T, hw = P.shape
    nq = S // blk
    return pl.pallas_call(
        functools.partial(_dsa_kernel, blk=blk, topk=topk),
        grid=(B, nq),
        in_specs=[
            pl.BlockSpec((HEADS, blk, hw), lambda b, i: (gaq // HEADS, b * nq + i, 0)),
            pl.BlockSpec((IDX_HEADS, blk, hw), lambda b, i: (giq // IDX_HEADS, b * nq + i, 0)),
            pl.BlockSpec((1, S, hw), lambda b, i: (gak, b, 0)),
            pl.BlockSpec((1, S, hw), lambda b, i: (gav, b, 0)),
            pl.BlockSpec((1, S, hw), lambda b, i: (gik, b, 0)),
            pl.BlockSpec((blk, 128), lambda b, i: (b * nq + i, iw_col // 128)),
            pl.BlockSpec(slopes.shape, lambda b, i: (0, 0, 0)),
        ],
        out_specs=pl.BlockSpec((HEADS, blk, hw), lambda b, i: (0, b * nq + i, 0)),
        out_shape=jax.ShapeDtypeStruct((HEADS, T, hw), bf16),
        scratch_shapes=[pltpu.VMEM((nq, blk, blk), i32)],
        compiler_params=_cparams(("arbitrary", "arbitrary")),
        name="dsa_attention",
    )(P, P, P, P, P, F, slopes)


def _merge_kernel(x_ref, g_ref, sc_ref, sh_ref, g1_ref, ya_ref, yb_ref, yc_ref, yd_ref,
                  wg_ref, wb_ref, wo_ref, o_ref):
    x = x_ref[...]
    D = x.shape[1]
    h = _norm_mod(x, g_ref[...], sc_ref[0], sh_ref[0]).astype(bf16)
    merged = jnp.zeros(x.shape, f32)
    for n, y_ref in enumerate((ya_ref, yb_ref, yc_ref, yd_ref)):
        gate = jax.nn.sigmoid(_dot(h, wg_ref[:, n * D:(n + 1) * D]))
        y = jnp.concatenate([y_ref[hh] for hh in range(HEADS)], axis=1)
        merged = merged + gate * _dot(y, wb_ref[n])
    o_ref[...] = x + g1_ref[0] * _dot(merged.astype(bf16), wo_ref[...])


def _merge(x2, g, sc, sh, g1, ys, wg, wb, wo, S, *, tm):
    T, D = x2.shape
    tpb = S // tm
    row = pl.BlockSpec((tm, D), lambda i: (i, 0))
    per_b = pl.BlockSpec((1, 1, D), lambda i: (i // tpb, 0, 0))
    yspec = pl.BlockSpec((HEADS, tm, HEAD_W), lambda i: (0, i, 0))
    full = lambda a: pl.BlockSpec(a.shape, lambda i: (0,) * a.ndim)
    return pl.pallas_call(
        _merge_kernel,
        grid=(T // tm,),
        in_specs=[row, full(g), per_b, per_b, per_b, yspec, yspec, yspec, yspec, full(wg), full(wb), full(wo)],
        out_specs=row,
        out_shape=jax.ShapeDtypeStruct((T, D), f32),
        compiler_params=_cparams(("arbitrary",)),
        name="merge",
    )(x2, g, sc, sh, g1, *ys, wg, wb, wo)


def _finish(x, g2, f, fg_ref, o_ref):
    out = x + g2 * f
    if fg_ref is not None:
        out = out * lax.rsqrt(jnp.mean(out * out, axis=-1, keepdims=True) + EPS) * fg_ref[...]
    o_ref[...] = out


def _ffn_kernel(x_ref, g_ref, sc_ref, sh_ref, g2_ref, w1_ref, w3_ref, w2_ref, *rest, final):
    fg_ref = rest[0] if final else None
    o_ref, h_scr, acc_scr = rest[-3:]
    j = pl.program_id(1)

    @pl.when(j == 0)
    def _():
        h_scr[...] = _norm_mod(x_ref[...], g_ref[...], sc_ref[0], sh_ref[0]).astype(bf16)
        acc_scr[...] = jnp.zeros_like(acc_scr)

    h = h_scr[...]
    a = _dot(h, w1_ref[...])
    u = (a * jax.nn.sigmoid(a)) * _dot(h, w3_ref[...])
    acc_scr[...] += _dot(u.astype(bf16), w2_ref[...])

    @pl.when(j == pl.num_programs(1) - 1)
    def _():
        _finish(x_ref[...], g2_ref[0], acc_scr[...], fg_ref, o_ref)


def _ffn(x2, g, sc, sh, g2, w1, w3, w2, final_g, S, *, tm, tn):
    T, D = x2.shape
    F = w1.shape[1]
    tpb = S // tm
    row = pl.BlockSpec((tm, D), lambda i, j: (i, 0))
    per_b = pl.BlockSpec((1, 1, D), lambda i, j: (i // tpb, 0, 0))
    vec = pl.BlockSpec((1, D), lambda i, j: (0, 0))
    in_specs = [row, vec, per_b, per_b, per_b,
                pl.BlockSpec((D, tn), lambda i, j: (0, j)),
                pl.BlockSpec((D, tn), lambda i, j: (0, j)),
                pl.BlockSpec((tn, D), lambda i, j: (j, 0))]
    args = [x2, g, sc, sh, g2, w1, w3, w2]
    if final_g is not None:
        in_specs.append(vec)
        args.append(final_g)
    return pl.pallas_call(
        functools.partial(_ffn_kernel, final=final_g is not None),
        grid=(T // tm, F // tn),
        in_specs=in_specs,
        out_specs=row,
        out_shape=jax.ShapeDtypeStruct((T, D), f32),
        scratch_shapes=[pltpu.VMEM((tm, D), bf16), pltpu.VMEM((tm, D), f32)],
        compiler_params=_cparams(("arbitrary", "arbitrary")),
        name="ffn",
    )(*args)


def _moe_kernel(x_ref, g_ref, sc_ref, sh_ref, g2_ref, rw_ref, rb_ref, w1_ref, w3_ref, w2_ref, *rest, final):
    fg_ref = rest[0] if final else None
    o_ref, h_scr, comb_scr, acc_scr, out_scr = rest[-5:]
    e = pl.program_id(1)
    j = pl.program_id(2)
    first = jnp.logical_and(e == 0, j == 0)

    @pl.when(first)
    def _():
        h = _norm_mod(x_ref[...], g_ref[...], sc_ref[0], sh_ref[0]).astype(bf16)
        h_scr[...] = h
        out_scr[...] = jnp.zeros_like(out_scr)
        logits = _dot(h, rw_ref[...]) + rb_ref[...]
        lane = lax.broadcasted_iota(i32, logits.shape, 1)
        big = jnp.int32(logits.shape[1])
        m1 = jnp.max(logits, axis=1, keepdims=True)
        i1 = jnp.min(jnp.where(logits == m1, lane, big), axis=1, keepdims=True)
        rest_l = jnp.where(lane == i1, NEG_INF, logits)
        m2 = jnp.max(rest_l, axis=1, keepdims=True)
        i2 = jnp.min(jnp.where(rest_l == m2, lane, big), axis=1, keepdims=True)
        e2 = jnp.exp(m2 - m1)
        den = 1.0 + e2
        comb_scr[...] = jnp.where(lane == i1, 1.0 / den, jnp.where(lane == i2, e2 / den, 0.0))

    @pl.when(j == 0)
    def _():
        acc_scr[...] = jnp.zeros_like(acc_scr)

    h = h_scr[...]
    a = _dot(h, w1_ref[0])
    u = (a * jax.nn.sigmoid(a)) * _dot(h, w3_ref[0])
    acc_scr[...] += _dot(u.astype(bf16), w2_ref[0])

    last_j = j == pl.num_programs(2) - 1

    @pl.when(last_j)
    def _():
        comb = comb_scr[...]
        lane = lax.broadcasted_iota(i32, comb.shape, 1)
        ce = jnp.sum(jnp.where(lane == e, comb, 0.0), axis=1, keepdims=True)
        out_scr[...] += ce * acc_scr[...]

    @pl.when(jnp.logical_and(last_j, e == pl.num_programs(1) - 1))
    def _():
        _finish(x_ref[...], g2_ref[0], out_scr[...], fg_ref, o_ref)


def _moe(x2, g, sc, sh, g2, rw, rb, w1, w3, w2, final_g, S, *, tm, tn):
    T, D = x2.shape
    E, _, F = w1.shape
    tpb = S // tm
    row = pl.BlockSpec((tm, D), lambda i, e, j: (i, 0))
    per_b = pl.BlockSpec((1, 1, D), lambda i, e, j: (i // tpb, 0, 0))
    vec = pl.BlockSpec((1, D), lambda i, e, j: (0, 0))
    in_specs = [row, vec, per_b, per_b, per_b,
                pl.BlockSpec(rw.shape, lambda i, e, j: (0, 0)),
                pl.BlockSpec(rb.shape, lambda i, e, j: (0, 0)),
                pl.BlockSpec((1, D, tn), lambda i, e, j: (e, 0, j)),
                pl.BlockSpec((1, D, tn), lambda i, e, j: (e, 0, j)),
                pl.BlockSpec((1, tn, D), lambda i, e, j: (e, j, 0))]
    args = [x2, g, sc, sh, g2, rw, rb, w1, w3, w2]
    if final_g is not None:
        in_specs.append(vec)
        args.append(final_g)
    return pl.pallas_call(
        functools.partial(_moe_kernel, final=final_g is not None),
        grid=(T // tm, E, F // tn),
        in_specs=in_specs,
        out_specs=row,
        out_shape=jax.ShapeDtypeStruct((T, D), f32),
        scratch_shapes=[pltpu.VMEM((tm, D), bf16), pltpu.VMEM((tm, 128), f32),
                        pltpu.VMEM((tm, D), f32), pltpu.VMEM((tm, D), f32)],
        compiler_params=_cparams(("arbitrary", "arbitrary", "arbitrary")),
        name="moe",
    )(*args)


_IN_SIZES = (256, 64, 64, 256, 64, 4, 256, 128, 32, 256, 256, 256, 256, 256, 256, 4096)
_IN_NAMES = ("aq", "ak", "av", "iq", "ik", "iw", "cq", "ckv", "kr", "sq", "sk", "sv", "dq", "dk", "dv", "gates")

G_AQ, G_IQ, G_SQ, G_SK, G_SV, G_DQ, G_DK, G_DV, G_AK, G_AV, G_IK = 0, 4, 8, 12, 16, 20, 24, 28, 32, 33, 34
N_GROUPS = 36
F_IW_COL = 640


def _split_w_in(w):
    out, off = {}, 0
    for name, n in zip(_IN_NAMES, _IN_SIZES):
        out[name] = w[:, off:off + n]
        off += n
    return out


def _rot_half_cols(w):
    half = w.shape[1] // 2
    return jnp.concatenate([-w[:, half:], w[:, :half]], axis=1)


def _layer_weights(w_in, w_uq, w_ukv):
    c = _split_w_in(w_in)
    D = w_in.shape[0]
    z = lambda n: jnp.zeros((D, n), f32)
    w_main = jnp.concatenate([
        c["aq"] * 0.125, c["iq"] * 0.125, c["sq"] * 0.125, c["sk"], c["sv"],
        c["dq"] * (DIFF_QK ** -0.5), c["dk"], c["dv"], c["ak"], c["av"], c["ik"], z(64)], axis=1).astype(bf16)
    w_f = jnp.concatenate([
        c["cq"], c["ckv"],
        z(64), c["kr"], z(32),
        z(64), _rot_half_cols(c["kr"]), z(32),
        c["iw"] * (IDX_HEADS ** -0.5), z(124)], axis=1).astype(bf16)
    w_gates = c["gates"].astype(bf16)

    qr = w_uq.shape[0]
    kvr = w_ukv.shape[0]
    wq, wqp, wk, wv = [], [], [], []
    per_q = MLA_NOPE + MLA_ROPE
    per_kv = MLA_NOPE + MLA_V
    for h in range(HEADS):
        nope = w_uq[:, h * per_q:h * per_q + MLA_NOPE]
        rope = w_uq[:, h * per_q + MLA_NOPE:(h + 1) * per_q]
        wq += [nope, rope, jnp.zeros((qr, 32), f32)]
        wqp += [jnp.zeros((qr, 64), f32), _rot_half_cols(rope), jnp.zeros((qr, 32), f32)]
        wk += [w_ukv[:, h * per_kv:h * per_kv + MLA_NOPE], jnp.zeros((kvr, 64), f32)]
        wv += [w_ukv[:, h * per_kv + MLA_NOPE:(h + 1) * per_kv]]
    cat = lambda xs: jnp.concatenate(xs, axis=1).astype(bf16)
    return w_main, w_f, w_gates, cat(wq), cat(wqp), cat(wk), cat(wv)


def _alibi_slopes():
    n = 2 * HEADS
    sl = [2.0 ** (-(8.0 / n) * (k + 1)) for k in range(n)]
    return sl[0::2], sl[1::2]


def kernel(x, c, positions, ada_w, ada_b, norm1_g, norm2_g, w_in, mla_q_norm_g, mla_kv_norm_g, mla_w_uq,
           mla_w_ukv, diff_lq1, diff_lk1, diff_lq2, diff_lk2, diff_norm_g, w_branch, w_out, ffn_w1, ffn_w3,
           ffn_w2, router_w, router_b, moe_w1, moe_w3, moe_w2, final_norm_g):
    B, S, D = x.shape
    T = B * S
    depth = ada_w.shape[0]
    topk = min(TOPK_MAX, S // 4)
    tm = min(1024, S)
    tm_merge = min(512, S)
    blk = min(256, S)
    blk_dsa = min(128, S)

    x2 = x.reshape(T, D)
    pos2 = positions.reshape(T, 1)
    mod = _modulation(c, ada_w, ada_b)

    half = MLA_ROPE // 2
    inv = ROPE_THETA ** (-jnp.arange(half, dtype=f32) / half)
    inv_full = jnp.concatenate([jnp.zeros((64,), f32), inv, inv, jnp.zeros((32,), f32)]).reshape(1, 128)
    sl_a, sl_d = _alibi_slopes()
    slopes_a = jnp.broadcast_to(jnp.asarray(sl_a, f32)[:, None, None], (HEADS, 1, blk_dsa))
    slopes_d = jnp.broadcast_to(jnp.asarray(sl_d, f32)[:, None, None], (HEADS, 1, blk))

    row = lambda v: v.reshape(1, -1)
    for l in range(depth):
        sh1, sc1, g1, sh2, sc2, g2 = [mod[l, :, k * D:(k + 1) * D].reshape(B, 1, D) for k in range(6)]
        lam_init = 0.8 - 0.6 * math.exp(-0.3 * l)
        w_main, w_f, w_gates, wq, wqp, wk, wv = _layer_weights(w_in[l], mla_w_uq[l], mla_w_ukv[l])
        n1 = row(norm1_g[l])

        P = _inproj(x2, n1, sc1, sh1, w_main, S, grouped=True, out_dtype=bf16, tm=tm, tn=256)
        F = _inproj(x2, n1, sc1, sh1, w_f, S, grouped=False, out_dtype=f32, tm=tm, tn=w_f.shape[1])
        qm, km, vm = _mla_prep(F, pos2, row(mla_q_norm_g[l]), row(mla_kv_norm_g[l]), wq, wqp, wk, wv,
                               inv_full, tm=tm)

        ya = _dsa_attention(P, F, slopes_a, B, S, blk=blk_dsa, topk=topk, gaq=G_AQ, giq=G_IQ, gak=G_AK,
                            gav=G_AV, gik=G_IK, iw_col=F_IW_COL)
        yb = _mla_attention(qm, km, vm, B, S, blk=blk)
        yc = _sb_attention(P, B, S, blk=blk, gq=G_SQ, gk=G_SK, gv=G_SV)
        yd = _diff_attention(P, slopes_d, row(diff_lq1[l]), row(diff_lk1[l]), row(diff_lq2[l]),
                             row(diff_lk2[l]), row(diff_norm_g[l]), B, S, blk=blk, lam_init=lam_init,
                             gq=G_DQ, gk=G_DK, gv=G_DV)

        x2 = _merge(x2, n1, sc1, sh1, g1, (ya, yb, yc, yd), w_gates, w_branch[l].astype(bf16),
                    w_out[l].astype(bf16), S, tm=tm_merge)

        final_g = row(final_norm_g) if l == depth - 1 else None
        n2 = row(norm2_g[l])
        if l % 2 == 0:
            k = l // 2
            x2 = _ffn(x2, n2, sc2, sh2, g2, ffn_w1[k].astype(bf16), ffn_w3[k].astype(bf16),
                      ffn_w2[k].astype(bf16), final_g, S, tm=tm, tn=256)
        else:
            k = l // 2
            E = router_w.shape[2]
            rw = jnp.concatenate([router_w[k], jnp.zeros((D, 128 - E), f32)], axis=1).astype(bf16)
            rb = jnp.concatenate([router_b[k], jnp.full((128 - E,), NEG_INF, f32)]).reshape(1, 128)
            x2 = _moe(x2, n2, sc2, sh2, g2, rw, rb, moe_w1[k].astype(bf16), moe_w3[k].astype(bf16),
                      moe_w2[k].astype(bf16), final_g, S, tm=tm, tn=512)
    return x2.reshape(B, S, D)
```

```python
import functools
import math

import jax
import jax.numpy as jnp
from jax import lax
from jax.experimental import pallas as pl
from jax.experimental.pallas import tpu as pltpu

f32 = jnp.float32
bf16 = jnp.bfloat16
i32 = jnp.int32

N_BRANCHES = 4
HEADS = 4
HEAD_W = 64
IDX_HEADS = 4
TOPK_MAX = 256
MLA_Q_RANK = 256
MLA_KV_RANK = 128
MLA_NOPE = 64
MLA_ROPE = 32
MLA_V = 64
MLA_QK_PAD = 128
ROPE_THETA = 10000.0
DIFF_QK = 32
N_EXPERTS = 8
TOP_K = 2
EPS = 1e-6
NEG_INF = float("-inf")
INT_MIN = -2 ** 31

VMEM_LIMIT = 56 * 1024 * 1024


def _cparams(sem):
    return pltpu.CompilerParams(dimension_semantics=sem, vmem_limit_bytes=VMEM_LIMIT)


def _nt_dot(a, b):
    return lax.dot_general(a, b, (((1,), (1,)), ((), ())), preferred_element_type=f32)


def _dot(a, b):
    return jnp.dot(a, b, preferred_element_type=f32)


def _norm_mod(x, g, sc, sh):
    y = x * lax.rsqrt(jnp.mean(x * x, axis=-1, keepdims=True) + EPS)
    return (y * g) * (1.0 + sc) + sh


def _mod_kernel(c_ref, w_ref, b_ref, o_ref):
    c = c_ref[...]
    cond = (c * jax.nn.sigmoid(c)).astype(bf16)
    o_ref[0] = _dot(cond, w_ref[0].astype(bf16)) + b_ref[0]


def _modulation(c, ada_w, ada_b):
    L, D, N = ada_w.shape
    B = c.shape[0]
    tn = 1536
    return pl.pallas_call(
        _mod_kernel,
        grid=(L, N // tn),
        in_specs=[
            pl.BlockSpec((B, D), lambda l, j: (0, 0)),
            pl.BlockSpec((1, D, tn), lambda l, j: (l, 0, j)),
            pl.BlockSpec((1, 1, tn), lambda l, j: (l, 0, j)),
        ],
        out_specs=pl.BlockSpec((1, B, tn), lambda l, j: (l, 0, j)),
        out_shape=jax.ShapeDtypeStruct((L, B, N), f32),
        compiler_params=_cparams(("arbitrary", "arbitrary")),
        name="modulation",
    )(c, ada_w, ada_b.reshape(L, 1, N))


def _inproj_kernel(x_ref, g_ref, sc_ref, sh_ref, w_ref, o_ref, h_scr, *, grouped):
    @pl.when(pl.program_id(1) == 0)
    def _():
        h_scr[...] = _norm_mod(x_ref[...], g_ref[...], sc_ref[0], sh_ref[0]).astype(bf16)

    res = _dot(h_scr[...], w_ref[...])
    if grouped:
        for k in range(res.shape[1] // HEAD_W):
            o_ref[k] = res[:, k * HEAD_W:(k + 1) * HEAD_W].astype(o_ref.dtype)
    else:
        o_ref[...] = res.astype(o_ref.dtype)


def _inproj(x2, g, sc, sh, w, S, *, grouped, out_dtype, tm, tn):
    T, D = x2.shape
    N = w.shape[1]
    tpb = S // tm
    if grouped:
        out_shape = jax.ShapeDtypeStruct((N // HEAD_W, T, HEAD_W), out_dtype)
        out_spec = pl.BlockSpec((tn // HEAD_W, tm, HEAD_W), lambda i, j: (j, i, 0))
    else:
        out_shape = jax.ShapeDtypeStruct((T, N), out_dtype)
        out_spec = pl.BlockSpec((tm, tn), lambda i, j: (i, j))
    return pl.pallas_call(
        functools.partial(_inproj_kernel, grouped=grouped),
        grid=(T // tm, N // tn),
        in_specs=[
            pl.BlockSpec((tm, D), lambda i, j: (i, 0)),
            pl.BlockSpec((1, D), lambda i, j: (0, 0)),
            pl.BlockSpec((1, 1, D), lambda i, j: (i // tpb, 0, 0)),
            pl.BlockSpec((1, 1, D), lambda i, j: (i // tpb, 0, 0)),
            pl.BlockSpec((D, tn), lambda i, j: (0, j)),
        ],
        out_specs=out_spec,
        out_shape=out_shape,
        scratch_shapes=[pltpu.VMEM((tm, D), bf16)],
        compiler_params=_cparams(("arbitrary", "arbitrary")),
        name="inproj_grouped" if grouped else "inproj_plain",
    )(x2, g, sc, sh, w)


def _mla_prep_kernel(f_ref, pos_ref, gq_ref, gkv_ref, wq_ref, wqp_ref, wk_ref, wv_ref, inv_ref,
                     q_ref, k_ref, v_ref, *, scale):
    cq = f_ref[:, 0:MLA_Q_RANK]
    ckv = f_ref[:, MLA_Q_RANK:MLA_Q_RANK + MLA_KV_RANK]
    kr = f_ref[:, 384:512]
    krp = f_ref[:, 512:640]
    nq = (cq * lax.rsqrt(jnp.mean(cq * cq, axis=-1, keepdims=True) + EPS) * gq_ref[...]).astype(bf16)
    nkv = (ckv * lax.rsqrt(jnp.mean(ckv * ckv, axis=-1, keepdims=True) + EPS) * gkv_ref[...]).astype(bf16)
    ang = pos_ref[...].astype(f32) * inv_ref[...]
    cosf = jnp.cos(ang)
    sinf = jnp.sin(ang)
    q1 = _dot(nq, wq_ref[...])
    q2 = _dot(nq, wqp_ref[...])
    kn = _dot(nkv, wk_ref[...])
    vv = _dot(nkv, wv_ref[...])
    krope = kr * cosf + krp * sinf
    for h in range(HEADS):
        sl = slice(h * MLA_QK_PAD, (h + 1) * MLA_QK_PAD)
        q_ref[h] = ((q1[:, sl] * cosf + q2[:, sl] * sinf) * scale).astype(bf16)
        k_ref[h] = (kn[:, sl] + krope).astype(bf16)
        v_ref[h] = vv[:, h * MLA_V:(h + 1) * MLA_V].astype(bf16)


def _mla_prep(F, pos2, gq, gkv, wq, wqp, wk, wv, inv_full, *, tm):
    T = F.shape[0]
    scale = (MLA_NOPE + MLA_ROPE) ** -0.5
    full = lambda a: pl.BlockSpec(a.shape, lambda i: (0,) * a.ndim)
    return pl.pallas_call(
        functools.partial(_mla_prep_kernel, scale=scale),
        grid=(T // tm,),
        in_specs=[
            pl.BlockSpec((tm, F.shape[1]), lambda i: (i, 0)),
            pl.BlockSpec((tm, 1), lambda i: (i, 0)),
            full(gq), full(gkv), full(wq), full(wqp), full(wk), full(wv), full(inv_full),
        ],
        out_specs=[
            pl.BlockSpec((HEADS, tm, MLA_QK_PAD), lambda i: (0, i, 0)),
            pl.BlockSpec((HEADS, tm, MLA_QK_PAD), lambda i: (0, i, 0)),
            pl.BlockSpec((HEADS, tm, MLA_V), lambda i: (0, i, 0)),
        ],
        out_shape=[
            jax.ShapeDtypeStruct((HEADS, T, MLA_QK_PAD), bf16),
            jax.ShapeDtypeStruct((HEADS, T, MLA_QK_PAD), bf16),
            jax.ShapeDtypeStruct((HEADS, T, MLA_V), bf16),
        ],
        compiler_params=_cparams(("arbitrary",)),
        name="mla_prep",
    )(F, pos2, gq, gkv, wq, wqp, wk, wv, inv_full)


def _chunk_plan(i, bq, ck):
    n_full = (i * bq) // ck
    return n_full, n_full * ck


def _tail_valid(i, bq, ck, tail_off):
    row = lax.broadcasted_iota(i32, (bq, ck), 0)
    col = lax.broadcasted_iota(i32, (bq, ck), 1)
    return col + (tail_off - i * bq) <= row


def _softmax_step(s, v, m, l, acc):
    m_new = jnp.maximum(m, jnp.max(s, axis=1, keepdims=True))
    alpha = jnp.exp(m - m_new)
    p = jnp.exp(s - m_new)
    l = alpha * l + jnp.sum(p, axis=1, keepdims=True)
    acc = alpha * acc + _dot(p.astype(bf16), v)
    return m_new, l, acc


def _softmax_init(bq, dv):
    return (jnp.full((bq, 1), NEG_INF, f32), jnp.zeros((bq, 1), f32), jnp.zeros((bq, dv), f32))


def _mla_attn_kernel(q_ref, k_ref, v_ref, o_ref, *, bq, ck):
    i = pl.program_id(1)
    n_full, tail_off = _chunk_plan(i, bq, ck)
    dv = v_ref.shape[2]

    def chunk(off, carry, valid):
        off = pl.multiple_of(off, ck)
        out = ()
        for h in range(HEADS):
            s = _nt_dot(q_ref[h], k_ref[h, pl.ds(off, ck), :])
            if valid is not None:
                s = jnp.where(valid, s, NEG_INF)
            out += _softmax_step(s, v_ref[h, pl.ds(off, ck), :], *carry[3 * h:3 * h + 3])
        return out

    carry = lax.fori_loop(0, n_full, lambda c, carry: chunk(c * ck, carry, None), _softmax_init(bq, dv) * HEADS)
    carry = chunk(tail_off, carry, _tail_valid(i, bq, ck, tail_off))
    for h in range(HEADS):
        m, l, acc = carry[3 * h:3 * h + 3]
        o_ref[h] = (acc / l).astype(o_ref.dtype)


def _mla_attention(qm, km, vm, B, S, *, bq, ck):
    H, T, dk = qm.shape
    dv = vm.shape[2]
    nq = S // bq
    return pl.pallas_call(
        functools.partial(_mla_attn_kernel, bq=bq, ck=ck),
        grid=(B, nq),
        in_specs=[
            pl.BlockSpec((H, bq, dk), lambda b, i: (0, b * nq + i, 0)),
            pl.BlockSpec((H, S, dk), lambda b, i: (0, b, 0)),
            pl.BlockSpec((H, S, dv), lambda b, i: (0, b, 0)),
        ],
        out_specs=pl.BlockSpec((H, bq, dv), lambda b, i: (0, b * nq + i, 0)),
        out_shape=jax.ShapeDtypeStruct((H, T, dv), bf16),
        compiler_params=_cparams(("arbitrary", "arbitrary")),
        name="mla_attention",
    )(qm, km, vm)


def _diff_attn_kernel(q_ref, k_ref, v_ref, slope_ref, lq1_ref, lk1_ref, lq2_ref, lk2_ref, g_ref, o_ref,
                      *, bq, ck, lam_init):
    i = pl.program_id(1)
    n_full, tail_off = _chunk_plan(i, bq, ck)
    dv = v_ref.shape[2]
    lane = lax.broadcasted_iota(i32, (bq, q_ref.shape[2]), 1)
    col = lax.broadcasted_iota(i32, (1, ck), 1).astype(f32)

    def chunk(off, carry, valid):
        off = pl.multiple_of(off, ck)
        pos = col + (off - i * bq).astype(f32)
        out = ()
        for h in range(HEADS):
            q = q_ref[h]
            zero = jnp.zeros_like(q)
            k = k_ref[h, pl.ds(off, ck), :]
            v = v_ref[h, pl.ds(off, ck), :]
            bias = slope_ref[h] * pos
            for mp, qm in enumerate((jnp.where(lane < DIFF_QK, q, zero), jnp.where(lane < DIFF_QK, zero, q))):
                s = _nt_dot(qm, k) + bias
                if valid is not None:
                    s = jnp.where(valid, s, NEG_INF)
                base = 3 * (2 * h + mp)
                out += _softmax_step(s, v, *carry[base:base + 3])
        return out

    carry = lax.fori_loop(0, n_full, lambda c, carry: chunk(c * ck, carry, None),
                          _softmax_init(bq, dv) * (2 * HEADS))
    carry = chunk(tail_off, carry, _tail_valid(i, bq, ck, tail_off))
    lam = (jnp.exp(jnp.sum(lq1_ref[...] * lk1_ref[...], axis=1, keepdims=True))
           - jnp.exp(jnp.sum(lq2_ref[...] * lk2_ref[...], axis=1, keepdims=True)) + lam_init)
    for h in range(HEADS):
        m0, l0, a0, m1, l1, a1 = carry[6 * h:6 * h + 6]
        o = a0 / l0 - lam * (a1 / l1)
        y = o * lax.rsqrt(jnp.mean(o * o, axis=-1, keepdims=True) + EPS) * g_ref[...]
        o_ref[h] = (y * (1.0 - lam_init)).astype(o_ref.dtype)


def _diff_attention(P, slopes, lq1, lk1, lq2, lk2, g, B, S, *, bq, ck, lam_init, gq, gk, gv):
    G, T, hw = P.shape
    nq = S // bq
    vec = lambda a: pl.BlockSpec(a.shape, lambda b, i: (0,) * a.ndim)
    return pl.pallas_call(
        functools.partial(_diff_attn_kernel, bq=bq, ck=ck, lam_init=lam_init),
        grid=(B, nq),
        in_specs=[
            pl.BlockSpec((HEADS, bq, hw), lambda b, i: (gq // HEADS, b * nq + i, 0)),
            pl.BlockSpec((HEADS, S, hw), lambda b, i: (gk // HEADS, b, 0)),
            pl.BlockSpec((HEADS, S, hw), lambda b, i: (gv // HEADS, b, 0)),
            vec(slopes), vec(lq1), vec(lk1), vec(lq2), vec(lk2), vec(g),
        ],
        out_specs=pl.BlockSpec((HEADS, bq, hw), lambda b, i: (0, b * nq + i, 0)),
        out_shape=jax.ShapeDtypeStruct((HEADS, T, hw), bf16),
        compiler_params=_cparams(("arbitrary", "arbitrary")),
        name="diff_attention",
    )(P, P, P, slopes, lq1, lk1, lq2, lk2, g)


SB_GROUP = 256


def _sb_attn_kernel(q_ref, k_ref, v_ref, o_ref, *, bq, ck):
    i = pl.program_id(1)
    n_full, tail_off = _chunk_plan(i, bq, ck)
    dv = v_ref.shape[2]
    gw = min(SB_GROUP, ck)
    rr = lax.broadcasted_iota(i32, (gw, gw), 0)
    cc = lax.broadcasted_iota(i32, (gw, gw), 1)
    upper = jnp.where(rr > cc, 1.0, 0.0).astype(bf16)

    def chunk(off, carry, strict):
        off = pl.multiple_of(off, ck)
        out = ()
        for h in range(HEADS):
            run, acc = carry[2 * h:2 * h + 2]
            z = _nt_dot(q_ref[h], k_ref[h, pl.ds(off, ck), :])
            lsm = -(jnp.maximum(z, 0.0) + jnp.log(1.0 + jnp.exp(-jnp.abs(z))))
            if strict is not None:
                lsm = jnp.where(strict, lsm, 0.0)
            parts = []
            for g in reversed(range(ck // gw)):
                x = lsm[:, g * gw:(g + 1) * gw]
                hi = x.astype(bf16)
                lo = (x - hi.astype(f32)).astype(bf16)
                parts.append(_dot(hi, upper) + _dot(lo, upper) + run)
                run = run + jnp.sum(x, axis=1, keepdims=True)
            between = jnp.concatenate(parts[::-1], axis=1) if len(parts) > 1 else parts[0]
            a = jnp.exp(z + lsm + between)
            if strict is not None:
                a = jnp.where(strict, a, 0.0)
            acc = acc + _dot(a.astype(bf16), v_ref[h, pl.ds(off, ck), :])
            out += (run, acc)
        return out

    row = lax.broadcasted_iota(i32, (bq, ck), 0)
    col = lax.broadcasted_iota(i32, (bq, ck), 1)
    strict = col + (tail_off - i * bq) < row
    init = (jnp.zeros((bq, 1), f32), jnp.zeros((bq, dv), f32)) * HEADS
    carry = chunk(tail_off, init, strict)
    carry = lax.fori_loop(0, n_full, lambda n, carry: chunk((n_full - 1 - n) * ck, carry, None), carry)
    for h in range(HEADS):
        o_ref[h] = carry[2 * h + 1].astype(o_ref.dtype)


def _sb_attention(P, B, S, *, bq, ck, gq, gk, gv):
    G, T, hw = P.shape
    nq = S // bq
    return pl.pallas_call(
        functools.partial(_sb_attn_kernel, bq=bq, ck=ck),
        grid=(B, nq),
        in_specs=[
            pl.BlockSpec((HEADS, bq, hw), lambda b, i: (gq // HEADS, b * nq + i, 0)),
            pl.BlockSpec((HEADS, S, hw), lambda b, i: (gk // HEADS, b, 0)),
            pl.BlockSpec((HEADS, S, hw), lambda b, i: (gv // HEADS, b, 0)),
        ],
        out_specs=pl.BlockSpec((HEADS, bq, hw), lambda b, i: (0, b * nq + i, 0)),
        out_shape=jax.ShapeDtypeStruct((HEADS, T, hw), bf16),
        compiler_params=_cparams(("arbitrary", "arbitrary")),
        name="sb_attention",
    )(P, P, P)


def _dsa_kernel(aq_ref, iq_ref, ak_ref, av_ref, ik_ref, iw_ref, slope_ref, o_ref, key_scr, *, bq, ck, topk):
    i = pl.program_id(1)
    n_full, tail_off = _chunk_plan(i, bq, ck)
    n_chunks = n_full + 1
    dv = av_ref.shape[2]
    w = iw_ref[...]
    gw = min(SB_GROUP, ck)
    rr = lax.broadcasted_iota(i32, (gw, gw), 0)
    cc = lax.broadcasted_iota(i32, (gw, gw), 1)
    incl = jnp.where(rr <= cc, 1.0, 0.0).astype(bf16)
    col = lax.broadcasted_iota(i32, (1, ck), 1).astype(f32)
    tail_valid = _tail_valid(i, bq, ck, tail_off)

    def score_chunk(c, valid):
        off = pl.multiple_of(c * ck, ck)
        ik = ik_ref[0, pl.ds(off, ck), :]
        sc = jnp.zeros((bq, ck), f32)
        for h in range(IDX_HEADS):
            sc = sc + jnp.maximum(_nt_dot(iq_ref[h], ik), 0.0) * w[:, h:h + 1]
        if valid is not None:
            sc = jnp.where(valid, sc, NEG_INF)
        bits = lax.bitcast_convert_type(sc + 0.0, i32)
        key_scr[c] = jnp.where(bits < 0, bits ^ jnp.int32(0x7FFFFFFF), bits)

    def score_body(c, _):
        score_chunk(c, None)
        return 0

    lax.fori_loop(0, n_full, score_body, 0)
    score_chunk(n_full, tail_valid)

    def count(pred):
        def body(c, acc):
            hit = jnp.where(pred(key_scr[c]), 1.0, 0.0)
            for g in range(ck // 128):
                acc = acc + hit[:, g * 128:(g + 1) * 128]
            return acc
        acc = lax.fori_loop(0, n_chunks, body, jnp.zeros((bq, 128), f32))
        return jnp.sum(acc, axis=1, keepdims=True)

    def search_body(it, prefix):
        cand = prefix + lax.shift_left(jnp.int32(1), jnp.int32(31) - it)
        cand_b = jnp.broadcast_to(cand, (bq, ck))
        cnt = count(lambda key: key >= cand_b)
        return jnp.where(cnt >= float(topk), cand, prefix)

    tau = lax.fori_loop(0, 32, search_body, jnp.full((bq, 1), INT_MIN, i32))
    tau_b = jnp.broadcast_to(tau, (bq, ck))
    n_take = float(topk) - count(lambda key: key > tau_b)

    def attn_chunk(c, carry, valid):
        off = pl.multiple_of(c * ck, ck)
        eq_run = carry[0]
        key = key_scr[c]
        eq = key == tau_b
        eqf = jnp.where(eq, 1.0, 0.0)
        pcs = []
        for g in range(ck // gw):
            e = eqf[:, g * gw:(g + 1) * gw]
            pcs.append(_dot(e.astype(bf16), incl) + eq_run)
            eq_run = eq_run + jnp.sum(e, axis=1, keepdims=True)
        pc = jnp.concatenate(pcs, axis=1) if len(pcs) > 1 else pcs[0]
        addm = jnp.where(key > tau_b, 0.0, jnp.where(eq, jnp.where(pc <= n_take, 0.0, NEG_INF), NEG_INF))
        if valid is not None:
            addm = jnp.where(valid, addm, NEG_INF)
        k = ak_ref[0, pl.ds(off, ck), :]
        v = av_ref[0, pl.ds(off, ck), :]
        pos = col + (off - i * bq).astype(f32)
        out = (eq_run,)
        for h in range(HEADS):
            m, l, acc = carry[1 + 3 * h:4 + 3 * h]
            s = _nt_dot(aq_ref[h], k) + slope_ref[h] * pos + addm
            m_new = jnp.maximum(m, jnp.max(s, axis=1, keepdims=True))
            m_safe = jnp.where(m_new == NEG_INF, 0.0, m_new)
            alpha = jnp.exp(m - m_safe)
            p = jnp.exp(s - m_safe)
            l = alpha * l + jnp.sum(p, axis=1, keepdims=True)
            acc = alpha * acc + _dot(p.astype(bf16), v)
            out += (m_new, l, acc)
        return out

    init = (jnp.zeros((bq, 1), f32),) + _softmax_init(bq, dv) * HEADS
    carry = lax.fori_loop(0, n_full, lambda c, carry: attn_chunk(c, carry, None), init)
    carry = attn_chunk(n_full, carry, tail_valid)
    for h in range(HEADS):
        m, l, acc = carry[1 + 3 * h:4 + 3 * h]
        o_ref[h] = (acc / l).astype(o_ref.dtype)


def _dsa_attention(P, F, slopes, B, S, *, bq, ck, topk, gaq, giq, gak, gav, gik, iw_col):
    G, T, hw = P.shape
    nq = S // bq
    return pl.pallas_call(
        functools.partial(_dsa_kernel, bq=bq, ck=ck, topk=topk),
        grid=(B, nq),
        in_specs=[
            pl.BlockSpec((HEADS, bq, hw), lambda b, i: (gaq // HEADS, b * nq + i, 0)),
            pl.BlockSpec((IDX_HEADS, bq, hw), lambda b, i: (giq // IDX_HEADS, b * nq + i, 0)),
            pl.BlockSpec((1, S, hw), lambda b, i: (gak, b, 0)),
            pl.BlockSpec((1, S, hw), lambda b, i: (gav, b, 0)),
            pl.BlockSpec((1, S, hw), lambda b, i: (gik, b, 0)),
            pl.BlockSpec((bq, 128), lambda b, i: (b * nq + i, iw_col // 128)),
            pl.BlockSpec(slopes.shape, lambda b, i: (0, 0, 0)),
        ],
        out_specs=pl.BlockSpec((HEADS, bq, hw), lambda b, i: (0, b * nq + i, 0)),
        out_shape=jax.ShapeDtypeStruct((HEADS, T, hw), bf16),
        scratch_shapes=[pltpu.VMEM((S // ck, bq, ck), i32)],
        compiler_params=_cparams(("arbitrary", "arbitrary")),
        name="dsa_attention",
    )(P, P, P, P, P, F, slopes)


def _merge_kernel(x_ref, g_ref, sc_ref, sh_ref, g1_ref, ya_ref, yb_ref, yc_ref, yd_ref,
                  wg_ref, wb_ref, wo_ref, o_ref):
    x = x_ref[...]
    D = x.shape[1]
    h = _norm_mod(x, g_ref[...], sc_ref[0], sh_ref[0]).astype(bf16)
    merged = jnp.zeros(x.shape, f32)
    for n, y_ref in enumerate((ya_ref, yb_ref, yc_ref, yd_ref)):
        gate = jax.nn.sigmoid(_dot(h, wg_ref[:, n * D:(n + 1) * D]))
        y = jnp.concatenate([y_ref[hh] for hh in range(HEADS)], axis=1)
        merged = merged + gate * _dot(y, wb_ref[n])
    o_ref[...] = x + g1_ref[0] * _dot(merged.astype(bf16), wo_ref[...])


def _merge(x2, g, sc, sh, g1, ys, wg, wb, wo, S, *, tm):
    T, D = x2.shape
    tpb = S // tm
    row = pl.BlockSpec((tm, D), lambda i: (i, 0))
    per_b = pl.BlockSpec((1, 1, D), lambda i: (i // tpb, 0, 0))
    yspec = pl.BlockSpec((HEADS, tm, HEAD_W), lambda i: (0, i, 0))
    full = lambda a: pl.BlockSpec(a.shape, lambda i: (0,) * a.ndim)
    return pl.pallas_call(
        _merge_kernel,
        grid=(T // tm,),
        in_specs=[row, full(g), per_b, per_b, per_b, yspec, yspec, yspec, yspec, full(wg), full(wb), full(wo)],
        out_specs=row,
        out_shape=jax.ShapeDtypeStruct((T, D), f32),
        compiler_params=_cparams(("arbitrary",)),
        name="merge",
    )(x2, g, sc, sh, g1, *ys, wg, wb, wo)


def _finish(x, g2, f, fg_ref, o_ref):
    out = x + g2 * f
    if fg_ref is not None:
        out = out * lax.rsqrt(jnp.mean(out * out, axis=-1, keepdims=True) + EPS) * fg_ref[...]
    o_ref[...] = out


def _ffn_kernel(x_ref, g_ref, sc_ref, sh_ref, g2_ref, w1_ref, w3_ref, w2_ref, *rest, final):
    fg_ref = rest[0] if final else None
    o_ref, h_scr, acc_scr = rest[-3:]
    j = pl.program_id(1)

    @pl.when(j == 0)
    def _():
        h_scr[...] = _norm_mod(x_ref[...], g_ref[...], sc_ref[0], sh_ref[0]).astype(bf16)
        acc_scr[...] = jnp.zeros_like(acc_scr)

    h = h_scr[...]
    a = _dot(h, w1_ref[...])
    u = (a * jax.nn.sigmoid(a)) * _dot(h, w3_ref[...])
    acc_scr[...] += _dot(u.astype(bf16), w2_ref[...])

    @pl.when(j == pl.num_programs(1) - 1)
    def _():
        _finish(x_ref[...], g2_ref[0], acc_scr[...], fg_ref, o_ref)


def _ffn(x2, g, sc, sh, g2, w1, w3, w2, final_g, S, *, tm, tn):
    T, D = x2.shape
    F = w1.shape[1]
    tpb = S // tm
    row = pl.BlockSpec((tm, D), lambda i, j: (i, 0))
    per_b = pl.BlockSpec((1, 1, D), lambda i, j: (i // tpb, 0, 0))
    vec = pl.BlockSpec((1, D), lambda i, j: (0, 0))
    in_specs = [row, vec, per_b, per_b, per_b,
                pl.BlockSpec((D, tn), lambda i, j: (0, j)),
                pl.BlockSpec((D, tn), lambda i, j: (0, j)),
                pl.BlockSpec((tn, D), lambda i, j: (j, 0))]
    args = [x2, g, sc, sh, g2, w1, w3, w2]
    if final_g is not None:
        in_specs.append(vec)
        args.append(final_g)
    return pl.pallas_call(
        functools.partial(_ffn_kernel, final=final_g is not None),
        grid=(T // tm, F // tn),
        in_specs=in_specs,
        out_specs=row,
        out_shape=jax.ShapeDtypeStruct((T, D), f32),
        scratch_shapes=[pltpu.VMEM((tm, D), bf16), pltpu.VMEM((tm, D), f32)],
        compiler_params=_cparams(("arbitrary", "arbitrary")),
        name="ffn",
    )(*args)


def _moe_kernel(x_ref, g_ref, sc_ref, sh_ref, g2_ref, rw_ref, rb_ref, w1_ref, w3_ref, w2_ref, *rest, final):
    fg_ref = rest[0] if final else None
    o_ref, h_scr, comb_scr, acc_scr, out_scr = rest[-5:]
    e = pl.program_id(1)
    j = pl.program_id(2)
    first = jnp.logical_and(e == 0, j == 0)

    @pl.when(first)
    def _():
        h = _norm_mod(x_ref[...], g_ref[...], sc_ref[0], sh_ref[0]).astype(bf16)
        h_scr[...] = h
        out_scr[...] = jnp.zeros_like(out_scr)
        logits = _dot(h, rw_ref[...]) + rb_ref[...]
        lane = lax.broadcasted_iota(i32, logits.shape, 1)
        big = jnp.int32(logits.shape[1])
        m1 = jnp.max(logits, axis=1, keepdims=True)
        i1 = jnp.min(jnp.where(logits == m1, lane, big), axis=1, keepdims=True)
        rest_l = jnp.where(lane == i1, NEG_INF, logits)
        m2 = jnp.max(rest_l, axis=1, keepdims=True)
        i2 = jnp.min(jnp.where(rest_l == m2, lane, big), axis=1, keepdims=True)
        e2 = jnp.exp(m2 - m1)
        den = 1.0 + e2
        comb_scr[...] = jnp.where(lane == i1, 1.0 / den, jnp.where(lane == i2, e2 / den, 0.0))

    @pl.when(j == 0)
    def _():
        acc_scr[...] = jnp.zeros_like(acc_scr)

    h = h_scr[...]
    a = _dot(h, w1_ref[0])
    u = (a * jax.nn.sigmoid(a)) * _dot(h, w3_ref[0])
    acc_scr[...] += _dot(u.astype(bf16), w2_ref[0])

    last_j = j == pl.num_programs(2) - 1

    @pl.when(last_j)
    def _():
        comb = comb_scr[...]
        lane = lax.broadcasted_iota(i32, comb.shape, 1)
        ce = jnp.sum(jnp.where(lane == e, comb, 0.0), axis=1, keepdims=True)
        out_scr[...] += ce * acc_scr[...]

    @pl.when(jnp.logical_and(last_j, e == pl.num_programs(1) - 1))
    def _():
        _finish(x_ref[...], g2_ref[0], out_scr[...], fg_ref, o_ref)


def _moe(x2, g, sc, sh, g2, rw, rb, w1, w3, w2, final_g, S, *, tm, tn):
    T, D = x2.shape
    E, _, F = w1.shape
    tpb = S // tm
    row = pl.BlockSpec((tm, D), lambda i, e, j: (i, 0))
    per_b = pl.BlockSpec((1, 1, D), lambda i, e, j: (i // tpb, 0, 0))
    vec = pl.BlockSpec((1, D), lambda i, e, j: (0, 0))
    in_specs = [row, vec, per_b, per_b, per_b,
                pl.BlockSpec(rw.shape, lambda i, e, j: (0, 0)),
                pl.BlockSpec(rb.shape, lambda i, e, j: (0, 0)),
                pl.BlockSpec((1, D, tn), lambda i, e, j: (e, 0, j)),
                pl.BlockSpec((1, D, tn), lambda i, e, j: (e, 0, j)),
                pl.BlockSpec((1, tn, D), lambda i, e, j: (e, j, 0))]
    args = [x2, g, sc, sh, g2, rw, rb, w1, w3, w2]
    if final_g is not None:
        in_specs.append(vec)
        args.append(final_g)
    return pl.pallas_call(
        functools.partial(_moe_kernel, final=final_g is not None),
        grid=(T // tm, E, F // tn),
        in_specs=in_specs,
        out_specs=row,
        out_shape=jax.ShapeDtypeStruct((T, D), f32),
        scratch_shapes=[pltpu.VMEM((tm, D), bf16), pltpu.VMEM((tm, 128), f32),
                        pltpu.VMEM((tm, D), f32), pltpu.VMEM((tm, D), f32)],
        compiler_params=_cparams(("arbitrary", "arbitrary", "arbitrary")),
        name="moe",
    )(*args)


_IN_SIZES = (256, 64, 64, 256, 64, 4, 256, 128, 32, 256, 256, 256, 256, 256, 256, 4096)
_IN_NAMES = ("aq", "ak", "av", "iq", "ik", "iw", "cq", "ckv", "kr", "sq", "sk", "sv", "dq", "dk", "dv", "gates")

G_AQ, G_IQ, G_SQ, G_SK, G_SV, G_DQ, G_DK, G_DV, G_AK, G_AV, G_IK = 0, 4, 8, 12, 16, 20, 24, 28, 32, 33, 34
N_GROUPS = 36
F_IW_COL = 640


def _split_w_in(w):
    out, off = {}, 0
    for name, n in zip(_IN_NAMES, _IN_SIZES):
        out[name] = w[:, off:off + n]
        off += n
    return out


def _rot_half_cols(w):
    half = w.shape[1] // 2
    return jnp.concatenate([-w[:, half:], w[:, :half]], axis=1)


def _layer_weights(w_in, w_uq, w_ukv):
    c = _split_w_in(w_in)
    D = w_in.shape[0]
    z = lambda n: jnp.zeros((D, n), f32)
    w_main = jnp.concatenate([
        c["aq"] * 0.125, c["iq"] * 0.125, c["sq"] * 0.125, c["sk"], c["sv"],
        c["dq"] * (DIFF_QK ** -0.5), c["dk"], c["dv"], c["ak"], c["av"], c["ik"], z(64)], axis=1).astype(bf16)
    w_f = jnp.concatenate([
        c["cq"], c["ckv"],
        z(64), c["kr"], z(32),
        z(64), _rot_half_cols(c["kr"]), z(32),
        c["iw"] * (IDX_HEADS ** -0.5), z(124)], axis=1).astype(bf16)
    w_gates = c["gates"].astype(bf16)

    qr = w_uq.shape[0]
    kvr = w_ukv.shape[0]
    wq, wqp, wk, wv = [], [], [], []
    per_q = MLA_NOPE + MLA_ROPE
    per_kv = MLA_NOPE + MLA_V
    for h in range(HEADS):
        nope = w_uq[:, h * per_q:h * per_q + MLA_NOPE]
        rope = w_uq[:, h * per_q + MLA_NOPE:(h + 1) * per_q]
        wq += [nope, rope, jnp.zeros((qr, 32), f32)]
        wqp += [jnp.zeros((qr, 64), f32), _rot_half_cols(rope), jnp.zeros((qr, 32), f32)]
        wk += [w_ukv[:, h * per_kv:h * per_kv + MLA_NOPE], jnp.zeros((kvr, 64), f32)]
        wv += [w_ukv[:, h * per_kv + MLA_NOPE:(h + 1) * per_kv]]
    cat = lambda xs: jnp.concatenate(xs, axis=1).astype(bf16)
    return w_main, w_f, w_gates, cat(wq), cat(wqp), cat(wk), cat(wv)


def _alibi_slopes():
    n = 2 * HEADS
    sl = [2.0 ** (-(8.0 / n) * (k + 1)) for k in range(n)]
    return sl[0::2], sl[1::2]


def kernel(x, c, positions, ada_w, ada_b, norm1_g, norm2_g, w_in, mla_q_norm_g, mla_kv_norm_g, mla_w_uq,
           mla_w_ukv, diff_lq1, diff_lk1, diff_lq2, diff_lk2, diff_norm_g, w_branch, w_out, ffn_w1, ffn_w3,
           ffn_w2, router_w, router_b, moe_w1, moe_w3, moe_w2, final_norm_g):
    B, S, D = x.shape
    T = B * S
    depth = ada_w.shape[0]
    topk = min(TOPK_MAX, S // 4)
    tm = min(1024, S)
    tm_merge = min(512, S)
    bq = min(256, S)
    ck = min(512, S)

    x2 = x.reshape(T, D)
    pos2 = positions.reshape(T, 1)
    mod = _modulation(c, ada_w, ada_b)

    half = MLA_ROPE // 2
    inv = ROPE_THETA ** (-jnp.arange(half, dtype=f32) / half)
    inv_full = jnp.concatenate([jnp.zeros((64,), f32), inv, inv, jnp.zeros((32,), f32)]).reshape(1, 128)
    sl_a, sl_d = _alibi_slopes()
    slopes_a = jnp.broadcast_to(jnp.asarray(sl_a, f32)[:, None, None], (HEADS, 1, ck))
    slopes_d = jnp.broadcast_to(jnp.asarray(sl_d, f32)[:, None, None], (HEADS, 1, ck))

    row = lambda v: v.reshape(1, -1)
    for l in range(depth):
        sh1, sc1, g1, sh2, sc2, g2 = [mod[l, :, k * D:(k + 1) * D].reshape(B, 1, D) for k in range(6)]
        lam_init = 0.8 - 0.6 * math.exp(-0.3 * l)
        w_main, w_f, w_gates, wq, wqp, wk, wv = _layer_weights(w_in[l], mla_w_uq[l], mla_w_ukv[l])
        n1 = row(norm1_g[l])

        P = _inproj(x2, n1, sc1, sh1, w_main, S, grouped=True, out_dtype=bf16, tm=tm, tn=256)
        F = _inproj(x2, n1, sc1, sh1, w_f, S, grouped=False, out_dtype=f32, tm=tm, tn=w_f.shape[1])
        qm, km, vm = _mla_prep(F, pos2, row(mla_q_norm_g[l]), row(mla_kv_norm_g[l]), wq, wqp, wk, wv,
                               inv_full, tm=tm)

        ya = _dsa_attention(P, F, slopes_a, B, S, bq=bq, ck=ck, topk=topk, gaq=G_AQ, giq=G_IQ, gak=G_AK,
                            gav=G_AV, gik=G_IK, iw_col=F_IW_COL)
        yb = _mla_attention(qm, km, vm, B, S, bq=bq, ck=ck)
        yc = _sb_attention(P, B, S, bq=bq, ck=ck, gq=G_SQ, gk=G_SK, gv=G_SV)
        yd = _diff_attention(P, slopes_d, row(diff_lq1[l]), row(diff_lk1[l]), row(diff_lq2[l]),
                             row(diff_lk2[l]), row(diff_norm_g[l]), B, S, bq=bq, ck=ck, lam_init=lam_init,
                             gq=G_DQ, gk=G_DK, gv=G_DV)

        x2 = _merge(x2, n1, sc1, sh1, g1, (ya, yb, yc, yd), w_gates, w_branch[l].astype(bf16),
                    w_out[l].astype(bf16), S, tm=tm_merge)

        final_g = row(final_norm_g) if l == depth - 1 else None
        n2 = row(norm2_g[l])
        if l % 2 == 0:
            k = l // 2
            x2 = _ffn(x2, n2, sc2, sh2, g2, ffn_w1[k].astype(bf16), ffn_w3[k].astype(bf16),
                      ffn_w2[k].astype(bf16), final_g, S, tm=tm, tn=256)
        else:
            k = l // 2
            E = router_w.shape[2]
            rw = jnp.concatenate([router_w[k], jnp.zeros((D, 128 - E), f32)], axis=1).astype(bf16)
            rb = jnp.concatenate([router_b[k], jnp.full((128 - E,), NEG_INF, f32)]).reshape(1, 128)
            x2 = _moe(x2, n2, sc2, sh2, g2, rw, rb, moe_w1[k].astype(bf16), moe_w3[k].astype(bf16),
                      moe_w2[k].astype(bf16), final_g, S, tm=tm, tn=512)
    return x2.reshape(B, S, D)
```

```python
import functools
import math

import jax
import jax.numpy as jnp
from jax import lax
from jax.experimental import pallas as pl
from jax.experimental.pallas import tpu as pltpu

f32 = jnp.float32
bf16 = jnp.bfloat16
i32 = jnp.int32

N_BRANCHES = 4
HEADS = 4
HEAD_W = 64
IDX_HEADS = 4
TOPK_MAX = 256
MLA_Q_RANK = 256
MLA_KV_RANK = 128
MLA_NOPE = 64
MLA_ROPE = 32
MLA_V = 64
MLA_QK_PAD = 128
ROPE_THETA = 10000.0
DIFF_QK = 32
N_EXPERTS = 8
TOP_K = 2
MOE_ROWS = 288
EPS = 1e-6
NEG_INF = float("-inf")
INT_MIN = -2 ** 31

VMEM_LIMIT = 56 * 1024 * 1024


def _cparams(sem):
    return pltpu.CompilerParams(dimension_semantics=sem, vmem_limit_bytes=VMEM_LIMIT)


def _nt_dot(a, b):
    return lax.dot_general(a, b, (((1,), (1,)), ((), ())), preferred_element_type=f32)


def _dot(a, b):
    return jnp.dot(a, b, preferred_element_type=f32)


def _norm_mod(x, g, sc, sh):
    y = x * lax.rsqrt(jnp.mean(x * x, axis=-1, keepdims=True) + EPS)
    return (y * g) * (1.0 + sc) + sh


def _mod_kernel(c_ref, w_ref, b_ref, o_ref):
    c = c_ref[...]
    cond = (c * jax.nn.sigmoid(c)).astype(bf16)
    o_ref[0] = _dot(cond, w_ref[0].astype(bf16)) + b_ref[0]


def _modulation(c, ada_w, ada_b):
    L, D, N = ada_w.shape
    B = c.shape[0]
    tn = 1536
    return pl.pallas_call(
        _mod_kernel,
        grid=(L, N // tn),
        in_specs=[
            pl.BlockSpec((B, D), lambda l, j: (0, 0)),
            pl.BlockSpec((1, D, tn), lambda l, j: (l, 0, j)),
            pl.BlockSpec((1, 1, tn), lambda l, j: (l, 0, j)),
        ],
        out_specs=pl.BlockSpec((1, B, tn), lambda l, j: (l, 0, j)),
        out_shape=jax.ShapeDtypeStruct((L, B, N), f32),
        compiler_params=_cparams(("arbitrary", "arbitrary")),
        name="modulation",
    )(c, ada_w, ada_b.reshape(L, 1, N))


def _inproj_kernel(x_ref, g_ref, sc_ref, sh_ref, w_ref, o_ref, h_scr, *, grouped):
    @pl.when(pl.program_id(1) == 0)
    def _():
        h_scr[...] = _norm_mod(x_ref[...], g_ref[...], sc_ref[0], sh_ref[0]).astype(bf16)

    res = _dot(h_scr[...], w_ref[...])
    if grouped:
        for k in range(res.shape[1] // HEAD_W):
            o_ref[k] = res[:, k * HEAD_W:(k + 1) * HEAD_W].astype(o_ref.dtype)
    else:
        o_ref[...] = res.astype(o_ref.dtype)


def _inproj(x2, g, sc, sh, w, S, *, grouped, out_dtype, tm, tn):
    T, D = x2.shape
    N = w.shape[1]
    tpb = S // tm
    if grouped:
        out_shape = jax.ShapeDtypeStruct((N // HEAD_W, T, HEAD_W), out_dtype)
        out_spec = pl.BlockSpec((tn // HEAD_W, tm, HEAD_W), lambda i, j: (j, i, 0))
    else:
        out_shape = jax.ShapeDtypeStruct((T, N), out_dtype)
        out_spec = pl.BlockSpec((tm, tn), lambda i, j: (i, j))
    return pl.pallas_call(
        functools.partial(_inproj_kernel, grouped=grouped),
        grid=(T // tm, N // tn),
        in_specs=[
            pl.BlockSpec((tm, D), lambda i, j: (i, 0)),
            pl.BlockSpec((1, D), lambda i, j: (0, 0)),
            pl.BlockSpec((1, 1, D), lambda i, j: (i // tpb, 0, 0)),
            pl.BlockSpec((1, 1, D), lambda i, j: (i // tpb, 0, 0)),
            pl.BlockSpec((D, tn), lambda i, j: (0, j)),
        ],
        out_specs=out_spec,
        out_shape=out_shape,
        scratch_shapes=[pltpu.VMEM((tm, D), bf16)],
        compiler_params=_cparams(("arbitrary", "arbitrary")),
        name="inproj_grouped" if grouped else "inproj_plain",
    )(x2, g, sc, sh, w)


def _mla_prep_kernel(f_ref, pos_ref, gq_ref, gkv_ref, wq_ref, wqp_ref, wk_ref, wv_ref, inv_ref,
                     q_ref, k_ref, v_ref, *, scale):
    cq = f_ref[:, 0:MLA_Q_RANK]
    ckv = f_ref[:, MLA_Q_RANK:MLA_Q_RANK + MLA_KV_RANK]
    kr = f_ref[:, 384:512]
    krp = f_ref[:, 512:640]
    nq = (cq * lax.rsqrt(jnp.mean(cq * cq, axis=-1, keepdims=True) + EPS) * gq_ref[...]).astype(bf16)
    nkv = (ckv * lax.rsqrt(jnp.mean(ckv * ckv, axis=-1, keepdims=True) + EPS) * gkv_ref[...]).astype(bf16)
    ang = pos_ref[...].astype(f32) * inv_ref[...]
    cosf = jnp.cos(ang)
    sinf = jnp.sin(ang)
    q1 = _dot(nq, wq_ref[...])
    q2 = _dot(nq, wqp_ref[...])
    kn = _dot(nkv, wk_ref[...])
    vv = _dot(nkv, wv_ref[...])
    krope = kr * cosf + krp * sinf
    for h in range(HEADS):
        sl = slice(h * MLA_QK_PAD, (h + 1) * MLA_QK_PAD)
        q_ref[h] = ((q1[:, sl] * cosf + q2[:, sl] * sinf) * scale).astype(bf16)
        k_ref[h] = (kn[:, sl] + krope).astype(bf16)
        v_ref[h] = vv[:, h * MLA_V:(h + 1) * MLA_V].astype(bf16)


def _mla_prep(F, pos2, gq, gkv, wq, wqp, wk, wv, inv_full, *, tm):
    T = F.shape[0]
    scale = (MLA_NOPE + MLA_ROPE) ** -0.5
    full = lambda a: pl.BlockSpec(a.shape, lambda i: (0,) * a.ndim)
    return pl.pallas_call(
        functools.partial(_mla_prep_kernel, scale=scale),
        grid=(T // tm,),
        in_specs=[
            pl.BlockSpec((tm, F.shape[1]), lambda i: (i, 0)),
            pl.BlockSpec((tm, 1), lambda i: (i, 0)),
            full(gq), full(gkv), full(wq), full(wqp), full(wk), full(wv), full(inv_full),
        ],
        out_specs=[
            pl.BlockSpec((HEADS, tm, MLA_QK_PAD), lambda i: (0, i, 0)),
            pl.BlockSpec((HEADS, tm, MLA_QK_PAD), lambda i: (0, i, 0)),
            pl.BlockSpec((HEADS, tm, MLA_V), lambda i: (0, i, 0)),
        ],
        out_shape=[
            jax.ShapeDtypeStruct((HEADS, T, MLA_QK_PAD), bf16),
            jax.ShapeDtypeStruct((HEADS, T, MLA_QK_PAD), bf16),
            jax.ShapeDtypeStruct((HEADS, T, MLA_V), bf16),
        ],
        compiler_params=_cparams(("arbitrary",)),
        name="mla_prep",
    )(F, pos2, gq, gkv, wq, wqp, wk, wv, inv_full)


def _chunk_plan(i, bq, ck):
    n_full = (i * bq) // ck
    return n_full, n_full * ck


def _tail_valid(i, bq, ck, tail_off):
    row = lax.broadcasted_iota(i32, (bq, ck), 0)
    col = lax.broadcasted_iota(i32, (bq, ck), 1)
    return col + (tail_off - i * bq) <= row


def _softmax_step(s, v, m, l, acc):
    m_new = jnp.maximum(m, jnp.max(s, axis=1, keepdims=True))
    alpha = jnp.exp(m - m_new)
    p = jnp.exp(s - m_new)
    l = alpha * l + jnp.sum(p, axis=1, keepdims=True)
    acc = alpha * acc + _dot(p.astype(bf16), v)
    return m_new, l, acc


def _softmax_init(bq, dv):
    return (jnp.full((bq, 1), NEG_INF, f32), jnp.zeros((bq, 1), f32), jnp.zeros((bq, dv), f32))


def _mla_attn_kernel(q_ref, k_ref, v_ref, o_ref, *, bq, ck):
    i = pl.program_id(1)
    n_full, tail_off = _chunk_plan(i, bq, ck)
    dv = v_ref.shape[2]

    def chunk(off, carry, valid):
        off = pl.multiple_of(off, ck)
        out = ()
        for h in range(HEADS):
            s = _nt_dot(q_ref[h], k_ref[h, pl.ds(off, ck), :])
            if valid is not None:
                s = jnp.where(valid, s, NEG_INF)
            out += _softmax_step(s, v_ref[h, pl.ds(off, ck), :], *carry[3 * h:3 * h + 3])
        return out

    carry = lax.fori_loop(0, n_full, lambda c, carry: chunk(c * ck, carry, None), _softmax_init(bq, dv) * HEADS)
    carry = chunk(tail_off, carry, _tail_valid(i, bq, ck, tail_off))
    for h in range(HEADS):
        m, l, acc = carry[3 * h:3 * h + 3]
        o_ref[h] = (acc / l).astype(o_ref.dtype)


def _mla_attention(qm, km, vm, B, S, *, bq, ck):
    H, T, dk = qm.shape
    dv = vm.shape[2]
    nq = S // bq
    return pl.pallas_call(
        functools.partial(_mla_attn_kernel, bq=bq, ck=ck),
        grid=(B, nq),
        in_specs=[
            pl.BlockSpec((H, bq, dk), lambda b, i: (0, b * nq + i, 0)),
            pl.BlockSpec((H, S, dk), lambda b, i: (0, b, 0)),
            pl.BlockSpec((H, S, dv), lambda b, i: (0, b, 0)),
        ],
        out_specs=pl.BlockSpec((H, bq, dv), lambda b, i: (0, b * nq + i, 0)),
        out_shape=jax.ShapeDtypeStruct((H, T, dv), bf16),
        compiler_params=_cparams(("arbitrary", "arbitrary")),
        name="mla_attention",
    )(qm, km, vm)


def _diff_attn_kernel(q_ref, k_ref, v_ref, slope_ref, lq1_ref, lk1_ref, lq2_ref, lk2_ref, g_ref, o_ref,
                      *, bq, ck, lam_init):
    i = pl.program_id(1)
    n_full, tail_off = _chunk_plan(i, bq, ck)
    dv = v_ref.shape[2]
    lane = lax.broadcasted_iota(i32, (bq, q_ref.shape[2]), 1)
    col = lax.broadcasted_iota(i32, (1, ck), 1).astype(f32)

    def chunk(off, carry, valid):
        off = pl.multiple_of(off, ck)
        pos = col + (off - i * bq).astype(f32)
        out = ()
        for h in range(HEADS):
            q = q_ref[h]
            zero = jnp.zeros_like(q)
            k = k_ref[h, pl.ds(off, ck), :]
            v = v_ref[h, pl.ds(off, ck), :]
            bias = slope_ref[h] * pos
            for mp, qm in enumerate((jnp.where(lane < DIFF_QK, q, zero), jnp.where(lane < DIFF_QK, zero, q))):
                s = _nt_dot(qm, k) + bias
                if valid is not None:
                    s = jnp.where(valid, s, NEG_INF)
                base = 3 * (2 * h + mp)
                out += _softmax_step(s, v, *carry[base:base + 3])
        return out

    carry = lax.fori_loop(0, n_full, lambda c, carry: chunk(c * ck, carry, None),
                          _softmax_init(bq, dv) * (2 * HEADS))
    carry = chunk(tail_off, carry, _tail_valid(i, bq, ck, tail_off))
    lam = (jnp.exp(jnp.sum(lq1_ref[...] * lk1_ref[...], axis=1, keepdims=True))
           - jnp.exp(jnp.sum(lq2_ref[...] * lk2_ref[...], axis=1, keepdims=True)) + lam_init)
    for h in range(HEADS):
        m0, l0, a0, m1, l1, a1 = carry[6 * h:6 * h + 6]
        o = a0 / l0 - lam * (a1 / l1)
        y = o * lax.rsqrt(jnp.mean(o * o, axis=-1, keepdims=True) + EPS) * g_ref[...]
        o_ref[h] = (y * (1.0 - lam_init)).astype(o_ref.dtype)


def _diff_attention(P, slopes, lq1, lk1, lq2, lk2, g, B, S, *, bq, ck, lam_init, gq, gk, gv):
    G, T, hw = P.shape
    nq = S // bq
    vec = lambda a: pl.BlockSpec(a.shape, lambda b, i: (0,) * a.ndim)
    return pl.pallas_call(
        functools.partial(_diff_attn_kernel, bq=bq, ck=ck, lam_init=lam_init),
        grid=(B, nq),
        in_specs=[
            pl.BlockSpec((HEADS, bq, hw), lambda b, i: (gq // HEADS, b * nq + i, 0)),
            pl.BlockSpec((HEADS, S, hw), lambda b, i: (gk // HEADS, b, 0)),
            pl.BlockSpec((HEADS, S, hw), lambda b, i: (gv // HEADS, b, 0)),
            vec(slopes), vec(lq1), vec(lk1), vec(lq2), vec(lk2), vec(g),
        ],
        out_specs=pl.BlockSpec((HEADS, bq, hw), lambda b, i: (0, b * nq + i, 0)),
        out_shape=jax.ShapeDtypeStruct((HEADS, T, hw), bf16),
        compiler_params=_cparams(("arbitrary", "arbitrary")),
        name="diff_attention",
    )(P, P, P, slopes, lq1, lk1, lq2, lk2, g)


SB_GROUP = 256


def _sb_attn_kernel(q_ref, k_ref, v_ref, o_ref, *, bq, ck):
    i = pl.program_id(1)
    n_full, tail_off = _chunk_plan(i, bq, ck)
    dv = v_ref.shape[2]
    gw = min(SB_GROUP, ck)
    rr = lax.broadcasted_iota(i32, (gw, gw), 0)
    cc = lax.broadcasted_iota(i32, (gw, gw), 1)
    upper = jnp.where(rr > cc, 1.0, 0.0).astype(bf16)

    def chunk(off, carry, strict):
        off = pl.multiple_of(off, ck)
        out = ()
        for h in range(HEADS):
            run, acc = carry[2 * h:2 * h + 2]
            z = _nt_dot(q_ref[h], k_ref[h, pl.ds(off, ck), :])
            lsm = -(jnp.maximum(z, 0.0) + jnp.log(1.0 + jnp.exp(-jnp.abs(z))))
            if strict is not None:
                lsm = jnp.where(strict, lsm, 0.0)
            parts = []
            for g in reversed(range(ck // gw)):
                x = lsm[:, g * gw:(g + 1) * gw]
                hi = x.astype(bf16)
                lo = (x - hi.astype(f32)).astype(bf16)
                parts.append(_dot(hi, upper) + _dot(lo, upper) + run)
                run = run + jnp.sum(x, axis=1, keepdims=True)
            between = jnp.concatenate(parts[::-1], axis=1) if len(parts) > 1 else parts[0]
            a = jnp.exp(z + lsm + between)
            if strict is not None:
                a = jnp.where(strict, a, 0.0)
            acc = acc + _dot(a.astype(bf16), v_ref[h, pl.ds(off, ck), :])
            out += (run, acc)
        return out

    row = lax.broadcasted_iota(i32, (bq, ck), 0)
    col = lax.broadcasted_iota(i32, (bq, ck), 1)
    strict = col + (tail_off - i * bq) < row
    init = (jnp.zeros((bq, 1), f32), jnp.zeros((bq, dv), f32)) * HEADS
    carry = chunk(tail_off, init, strict)
    carry = lax.fori_loop(0, n_full, lambda n, carry: chunk((n_full - 1 - n) * ck, carry, None), carry)
    for h in range(HEADS):
        o_ref[h] = carry[2 * h + 1].astype(o_ref.dtype)


def _sb_attention(P, B, S, *, bq, ck, gq, gk, gv):
    G, T, hw = P.shape
    nq = S // bq
    return pl.pallas_call(
        functools.partial(_sb_attn_kernel, bq=bq, ck=ck),
        grid=(B, nq),
        in_specs=[
            pl.BlockSpec((HEADS, bq, hw), lambda b, i: (gq // HEADS, b * nq + i, 0)),
            pl.BlockSpec((HEADS, S, hw), lambda b, i: (gk // HEADS, b, 0)),
            pl.BlockSpec((HEADS, S, hw), lambda b, i: (gv // HEADS, b, 0)),
        ],
        out_specs=pl.BlockSpec((HEADS, bq, hw), lambda b, i: (0, b * nq + i, 0)),
        out_shape=jax.ShapeDtypeStruct((HEADS, T, hw), bf16),
        compiler_params=_cparams(("arbitrary", "arbitrary")),
        name="sb_attention",
    )(P, P, P)


def _dsa_kernel(aq_ref, iq_ref, ak_ref, av_ref, ik_ref, iw_ref, slope_ref, o_ref, key_scr, *, bq, ck, topk):
    i = pl.program_id(1)
    n_full, tail_off = _chunk_plan(i, bq, ck)
    n_chunks = n_full + 1
    dv = av_ref.shape[2]
    w = iw_ref[...]
    gw = min(SB_GROUP, ck)
    rr = lax.broadcasted_iota(i32, (gw, gw), 0)
    cc = lax.broadcasted_iota(i32, (gw, gw), 1)
    incl = jnp.where(rr <= cc, 1.0, 0.0).astype(bf16)
    col = lax.broadcasted_iota(i32, (1, ck), 1).astype(f32)
    tail_valid = _tail_valid(i, bq, ck, tail_off)

    def score_chunk(c, valid):
        off = pl.multiple_of(c * ck, ck)
        ik = ik_ref[0, pl.ds(off, ck), :]
        sc = jnp.zeros((bq, ck), f32)
        for h in range(IDX_HEADS):
            sc = sc + jnp.maximum(_nt_dot(iq_ref[h], ik), 0.0) * w[:, h:h + 1]
        if valid is not None:
            sc = jnp.where(valid, sc, NEG_INF)
        bits = lax.bitcast_convert_type(sc + 0.0, i32)
        key_scr[c] = jnp.where(bits < 0, bits ^ jnp.int32(0x7FFFFFFF), bits)

    def score_body(c, _):
        score_chunk(c, None)
        return 0

    lax.fori_loop(0, n_full, score_body, 0)
    score_chunk(n_full, tail_valid)

    def count(pred):
        def body(c, acc):
            hit = jnp.where(pred(key_scr[c]), 1.0, 0.0)
            for g in range(ck // 128):
                acc = acc + hit[:, g * 128:(g + 1) * 128]
            return acc
        acc = lax.fori_loop(0, n_chunks, body, jnp.zeros((bq, 128), f32))
        return jnp.sum(acc, axis=1, keepdims=True)

    def search_body(it, prefix):
        cand = prefix + lax.shift_left(jnp.int32(1), jnp.int32(31) - it)
        cand_b = jnp.broadcast_to(cand, (bq, ck))
        cnt = count(lambda key: key >= cand_b)
        return jnp.where(cnt >= float(topk), cand, prefix)

    tau = lax.fori_loop(0, 32, search_body, jnp.full((bq, 1), INT_MIN, i32))
    tau_b = jnp.broadcast_to(tau, (bq, ck))
    n_take = float(topk) - count(lambda key: key > tau_b)

    def attn_chunk(c, carry, valid):
        off = pl.multiple_of(c * ck, ck)
        eq_run = carry[0]
        key = key_scr[c]
        eq = key == tau_b
        eqf = jnp.where(eq, 1.0, 0.0)
        pcs = []
        for g in range(ck // gw):
            e = eqf[:, g * gw:(g + 1) * gw]
            pcs.append(_dot(e.astype(bf16), incl) + eq_run)
            eq_run = eq_run + jnp.sum(e, axis=1, keepdims=True)
        pc = jnp.concatenate(pcs, axis=1) if len(pcs) > 1 else pcs[0]
        addm = jnp.where(key > tau_b, 0.0, jnp.where(eq, jnp.where(pc <= n_take, 0.0, NEG_INF), NEG_INF))
        if valid is not None:
            addm = jnp.where(valid, addm, NEG_INF)
        k = ak_ref[0, pl.ds(off, ck), :]
        v = av_ref[0, pl.ds(off, ck), :]
        pos = col + (off - i * bq).astype(f32)
        out = (eq_run,)
        for h in range(HEADS):
            m, l, acc = carry[1 + 3 * h:4 + 3 * h]
            s = _nt_dot(aq_ref[h], k) + slope_ref[h] * pos + addm
            m_new = jnp.maximum(m, jnp.max(s, axis=1, keepdims=True))
            m_safe = jnp.where(m_new == NEG_INF, 0.0, m_new)
            alpha = jnp.exp(m - m_safe)
            p = jnp.exp(s - m_safe)
            l = alpha * l + jnp.sum(p, axis=1, keepdims=True)
            acc = alpha * acc + _dot(p.astype(bf16), v)
            out += (m_new, l, acc)
        return out

    init = (jnp.zeros((bq, 1), f32),) + _softmax_init(bq, dv) * HEADS
    carry = lax.fori_loop(0, n_full, lambda c, carry: attn_chunk(c, carry, None), init)
    carry = attn_chunk(n_full, carry, tail_valid)
    for h in range(HEADS):
        m, l, acc = carry[1 + 3 * h:4 + 3 * h]
        o_ref[h] = (acc / l).astype(o_ref.dtype)


def _dsa_attention(P, F, slopes, B, S, *, bq, ck, topk, gaq, giq, gak, gav, gik, iw_col):
    G, T, hw = P.shape
    nq = S // bq
    return pl.pallas_call(
        functools.partial(_dsa_kernel, bq=bq, ck=ck, topk=topk),
        grid=(B, nq),
        in_specs=[
            pl.BlockSpec((HEADS, bq, hw), lambda b, i: (gaq // HEADS, b * nq + i, 0)),
            pl.BlockSpec((IDX_HEADS, bq, hw), lambda b, i: (giq // IDX_HEADS, b * nq + i, 0)),
            pl.BlockSpec((1, S, hw), lambda b, i: (gak, b, 0)),
            pl.BlockSpec((1, S, hw), lambda b, i: (gav, b, 0)),
            pl.BlockSpec((1, S, hw), lambda b, i: (gik, b, 0)),
            pl.BlockSpec((bq, 128), lambda b, i: (b * nq + i, iw_col // 128)),
            pl.BlockSpec(slopes.shape, lambda b, i: (0, 0, 0)),
        ],
        out_specs=pl.BlockSpec((HEADS, bq, hw), lambda b, i: (0, b * nq + i, 0)),
        out_shape=jax.ShapeDtypeStruct((HEADS, T, hw), bf16),
        scratch_shapes=[pltpu.VMEM((S // ck, bq, ck), i32)],
        compiler_params=_cparams(("arbitrary", "arbitrary")),
        name="dsa_attention",
    )(P, P, P, P, P, F, slopes)


def _merge_kernel(x_ref, g_ref, sc_ref, sh_ref, g1_ref, ya_ref, yb_ref, yc_ref, yd_ref,
                  wg_ref, wb_ref, wo_ref, o_ref):
    x = x_ref[...]
    D = x.shape[1]
    h = _norm_mod(x, g_ref[...], sc_ref[0], sh_ref[0]).astype(bf16)
    merged = jnp.zeros(x.shape, f32)
    for n, y_ref in enumerate((ya_ref, yb_ref, yc_ref, yd_ref)):
        gate = jax.nn.sigmoid(_dot(h, wg_ref[:, n * D:(n + 1) * D]))
        y = jnp.concatenate([y_ref[hh] for hh in range(HEADS)], axis=1)
        merged = merged + gate * _dot(y, wb_ref[n])
    o_ref[...] = x + g1_ref[0] * _dot(merged.astype(bf16), wo_ref[...])


def _merge(x2, g, sc, sh, g1, ys, wg, wb, wo, S, *, tm):
    T, D = x2.shape
    tpb = S // tm
    row = pl.BlockSpec((tm, D), lambda i: (i, 0))
    per_b = pl.BlockSpec((1, 1, D), lambda i: (i // tpb, 0, 0))
    yspec = pl.BlockSpec((HEADS, tm, HEAD_W), lambda i: (0, i, 0))
    full = lambda a: pl.BlockSpec(a.shape, lambda i: (0,) * a.ndim)
    return pl.pallas_call(
        _merge_kernel,
        grid=(T // tm,),
        in_specs=[row, full(g), per_b, per_b, per_b, yspec, yspec, yspec, yspec, full(wg), full(wb), full(wo)],
        out_specs=row,
        out_shape=jax.ShapeDtypeStruct((T, D), f32),
        compiler_params=_cparams(("arbitrary",)),
        name="merge",
    )(x2, g, sc, sh, g1, *ys, wg, wb, wo)


def _finish(x, g2, f, fg_ref, o_ref):
    out = x + g2 * f
    if fg_ref is not None:
        out = out * lax.rsqrt(jnp.mean(out * out, axis=-1, keepdims=True) + EPS) * fg_ref[...]
    o_ref[...] = out


def _ffn_kernel(x_ref, g_ref, sc_ref, sh_ref, g2_ref, w1_ref, w3_ref, w2_ref, *rest, final):
    fg_ref = rest[0] if final else None
    o_ref, h_scr, acc_scr = rest[-3:]
    j = pl.program_id(1)

    @pl.when(j == 0)
    def _():
        h_scr[...] = _norm_mod(x_ref[...], g_ref[...], sc_ref[0], sh_ref[0]).astype(bf16)
        acc_scr[...] = jnp.zeros_like(acc_scr)

    h = h_scr[...]
    a = _dot(h, w1_ref[...])
    u = (a * jax.nn.sigmoid(a)) * _dot(h, w3_ref[...])
    acc_scr[...] += _dot(u.astype(bf16), w2_ref[...])

    @pl.when(j == pl.num_programs(1) - 1)
    def _():
        _finish(x_ref[...], g2_ref[0], acc_scr[...], fg_ref, o_ref)


def _ffn(x2, g, sc, sh, g2, w1, w3, w2, final_g, S, *, tm, tn):
    T, D = x2.shape
    F = w1.shape[1]
    tpb = S // tm
    row = pl.BlockSpec((tm, D), lambda i, j: (i, 0))
    per_b = pl.BlockSpec((1, 1, D), lambda i, j: (i // tpb, 0, 0))
    vec = pl.BlockSpec((1, D), lambda i, j: (0, 0))
    in_specs = [row, vec, per_b, per_b, per_b,
                pl.BlockSpec((D, tn), lambda i, j: (0, j)),
                pl.BlockSpec((D, tn), lambda i, j: (0, j)),
                pl.BlockSpec((tn, D), lambda i, j: (j, 0))]
    args = [x2, g, sc, sh, g2, w1, w3, w2]
    if final_g is not None:
        in_specs.append(vec)
        args.append(final_g)
    return pl.pallas_call(
        functools.partial(_ffn_kernel, final=final_g is not None),
        grid=(T // tm, F // tn),
        in_specs=in_specs,
        out_specs=row,
        out_shape=jax.ShapeDtypeStruct((T, D), f32),
        scratch_shapes=[pltpu.VMEM((tm, D), bf16), pltpu.VMEM((tm, D), f32)],
        compiler_params=_cparams(("arbitrary", "arbitrary")),
        name="ffn",
    )(*args)


def _router_kernel(x_ref, g_ref, sc_ref, sh_ref, rw_ref, rb_ref, comb_ref, cnt_ref):
    h = _norm_mod(x_ref[...], g_ref[...], sc_ref[0], sh_ref[0]).astype(bf16)
    logits = _dot(h, rw_ref[...]) + rb_ref[...]
    lane = lax.broadcasted_iota(i32, logits.shape, 1)
    big = jnp.int32(logits.shape[1])
    m1 = jnp.max(logits, axis=1, keepdims=True)
    i1 = jnp.min(jnp.where(logits == m1, lane, big), axis=1, keepdims=True)
    rest_l = jnp.where(lane == i1, NEG_INF, logits)
    m2 = jnp.max(rest_l, axis=1, keepdims=True)
    i2 = jnp.min(jnp.where(rest_l == m2, lane, big), axis=1, keepdims=True)
    e2 = jnp.exp(m2 - m1)
    den = 1.0 + e2
    comb = jnp.where(lane == i1, 1.0 / den, jnp.where(lane == i2, e2 / den, 0.0))
    comb_ref[...] = comb
    cnt_ref[0] = jnp.sum(jnp.where(comb > 0.0, 1.0, 0.0), axis=0, keepdims=True)


def _router(x2, g, sc, sh, rw, rb, S, *, tm):
    T, D = x2.shape
    tpb = S // tm
    nt = T // tm
    return pl.pallas_call(
        _router_kernel,
        grid=(nt,),
        in_specs=[pl.BlockSpec((tm, D), lambda i: (i, 0)),
                  pl.BlockSpec((1, D), lambda i: (0, 0)),
                  pl.BlockSpec((1, 1, D), lambda i: (i // tpb, 0, 0)),
                  pl.BlockSpec((1, 1, D), lambda i: (i // tpb, 0, 0)),
                  pl.BlockSpec(rw.shape, lambda i: (0, 0)),
                  pl.BlockSpec(rb.shape, lambda i: (0, 0))],
        out_specs=[pl.BlockSpec((tm, 128), lambda i: (i, 0)),
                   pl.BlockSpec((1, 1, 128), lambda i: (i, 0, 0))],
        out_shape=[jax.ShapeDtypeStruct((T, 128), f32), jax.ShapeDtypeStruct((nt, 1, 128), f32)],
        compiler_params=_cparams(("arbitrary",)),
        name="router",
    )(x2, g, sc, sh, rw, rb)


def _moe_kernel(nch_ref, x_ref, g_ref, sc_ref, sh_ref, g2_ref, comb_ref, w1_ref, w3_ref, w2_ref, *rest,
                final, rows):
    fg_ref = rest[0] if final else None
    o_ref, h_scr, tri_scr, rank_scr, rank_t_scr, xg_scr, yg_scr, out_scr = rest[-8:]
    i = pl.program_id(0)
    e = pl.program_id(1)
    j = pl.program_id(2)
    n_e = pl.num_programs(1)
    tm = x_ref.shape[0]
    nch = nch_ref[i * n_e + e]

    @pl.when(jnp.logical_and(i == 0, jnp.logical_and(e == 0, j == 0)))
    def _():
        rr = lax.broadcasted_iota(i32, (tm, tm), 0)
        cc = lax.broadcasted_iota(i32, (tm, tm), 1)
        tri_scr[...] = jnp.where(cc < rr, 1.0, 0.0).astype(bf16)

    @pl.when(jnp.logical_and(e == 0, j == 0))
    def _():
        h_scr[...] = _norm_mod(x_ref[...], g_ref[...], sc_ref[0], sh_ref[0]).astype(bf16)
        out_scr[...] = jnp.zeros_like(out_scr)
        member = jnp.where(comb_ref[...] > 0.0, 1.0, 0.0)
        rank = jnp.where(member > 0.0, _dot(tri_scr[...], member.astype(bf16)), -1.0)
        rank_scr[...] = rank
        rank_t_scr[...] = rank.T

    def chunk_rows(c):
        return pl.ds(pl.multiple_of(c * rows, 16), rows)

    @pl.when(j == 0)
    def _():
        rank_row = rank_t_scr[pl.ds(e, 1), :]
        slot = lax.broadcasted_iota(i32, (rows, tm), 0).astype(f32)

        def gather(c, _):
            pick = jnp.where(slot + (c * rows).astype(f32) == rank_row, 1.0, 0.0).astype(bf16)
            xg_scr[chunk_rows(c), :] = _dot(pick, h_scr[...]).astype(bf16)
            yg_scr[chunk_rows(c), :] = jnp.zeros((rows, yg_scr.shape[1]), f32)
            return 0

        lax.fori_loop(0, nch, gather, 0)

    def expert(c, _):
        xg = xg_scr[chunk_rows(c), :]
        a = _dot(xg, w1_ref[0])
        u = (a * jax.nn.sigmoid(a)) * _dot(xg, w3_ref[0])
        yg_scr[chunk_rows(c), :] += _dot(u.astype(bf16), w2_ref[0])
        return 0

    lax.fori_loop(0, nch, expert, 0)

    last_j = j == pl.num_programs(2) - 1

    @pl.when(last_j)
    def _():
        lane = lax.broadcasted_iota(i32, (tm, 128), 1)
        sel = lane == e
        rank_col = jnp.sum(jnp.where(sel, rank_scr[...], 0.0), axis=1, keepdims=True)
        w_col = jnp.sum(jnp.where(sel, comb_ref[...], 0.0), axis=1, keepdims=True)
        slot = lax.broadcasted_iota(i32, (tm, rows), 1).astype(f32)

        def scatter(c, _):
            place = jnp.where(slot + (c * rows).astype(f32) == rank_col, 1.0, 0.0).astype(bf16)
            out_scr[...] += w_col * _dot(place, yg_scr[chunk_rows(c), :].astype(bf16))
            return 0

        lax.fori_loop(0, nch, scatter, 0)

    @pl.when(jnp.logical_and(last_j, e == n_e - 1))
    def _():
        _finish(x_ref[...], g2_ref[0], out_scr[...], fg_ref, o_ref)


def _moe(x2, g, sc, sh, g2, comb, nch, w1, w3, w2, final_g, S, *, tm, tn, rows):
    T, D = x2.shape
    E, _, F = w1.shape
    tpb = S // tm
    cap = -(-tm // rows) * rows
    row = pl.BlockSpec((tm, D), lambda i, e, j, n: (i, 0))
    per_b = pl.BlockSpec((1, 1, D), lambda i, e, j, n: (i // tpb, 0, 0))
    vec = pl.BlockSpec((1, D), lambda i, e, j, n: (0, 0))
    in_specs = [row, vec, per_b, per_b, per_b,
                pl.BlockSpec((tm, 128), lambda i, e, j, n: (i, 0)),
                pl.BlockSpec((1, D, tn), lambda i, e, j, n: (e, 0, j)),
                pl.BlockSpec((1, D, tn), lambda i, e, j, n: (e, 0, j)),
                pl.BlockSpec((1, tn, D), lambda i, e, j, n: (e, j, 0))]
    args = [x2, g, sc, sh, g2, comb, w1, w3, w2]
    if final_g is not None:
        in_specs.append(vec)
        args.append(final_g)
    return pl.pallas_call(
        functools.partial(_moe_kernel, final=final_g is not None, rows=rows),
        grid_spec=pltpu.PrefetchScalarGridSpec(
            num_scalar_prefetch=1,
            grid=(T // tm, E, F // tn),
            in_specs=in_specs,
            out_specs=row,
            scratch_shapes=[pltpu.VMEM((tm, D), bf16),
                            pltpu.VMEM((tm, tm), bf16),
                            pltpu.VMEM((tm, 128), f32),
                            pltpu.VMEM((128, tm), f32),
                            pltpu.VMEM((cap, D), bf16),
                            pltpu.VMEM((cap, D), f32),
                            pltpu.VMEM((tm, D), f32)]),
        out_shape=jax.ShapeDtypeStruct((T, D), f32),
        compiler_params=_cparams(("arbitrary", "arbitrary", "arbitrary")),
        name="moe",
    )(nch, *args)


_IN_SIZES = (256, 64, 64, 256, 64, 4, 256, 128, 32, 256, 256, 256, 256, 256, 256, 4096)
_IN_NAMES = ("aq", "ak", "av", "iq", "ik", "iw", "cq", "ckv", "kr", "sq", "sk", "sv", "dq", "dk", "dv", "gates")

G_AQ, G_IQ, G_SQ, G_SK, G_SV, G_DQ, G_DK, G_DV, G_AK, G_AV, G_IK = 0, 4, 8, 12, 16, 20, 24, 28, 32, 33, 34
N_GROUPS = 36
F_IW_COL = 640


def _split_w_in(w):
    out, off = {}, 0
    for name, n in zip(_IN_NAMES, _IN_SIZES):
        out[name] = w[:, off:off + n]
        off += n
    return out


def _rot_half_cols(w):
    half = w.shape[1] // 2
    return jnp.concatenate([-w[:, half:], w[:, :half]], axis=1)


def _layer_weights(w_in, w_uq, w_ukv):
    c = _split_w_in(w_in)
    D = w_in.shape[0]
    z = lambda n: jnp.zeros((D, n), f32)
    w_main = jnp.concatenate([
        c["aq"] * 0.125, c["iq"] * 0.125, c["sq"] * 0.125, c["sk"], c["sv"],
        c["dq"] * (DIFF_QK ** -0.5), c["dk"], c["dv"], c["ak"], c["av"], c["ik"], z(64)], axis=1).astype(bf16)
    w_f = jnp.concatenate([
        c["cq"], c["ckv"],
        z(64), c["kr"], z(32),
        z(64), _rot_half_cols(c["kr"]), z(32),
        c["iw"] * (IDX_HEADS ** -0.5), z(124)], axis=1).astype(bf16)
    w_gates = c["gates"].astype(bf16)

    qr = w_uq.shape[0]
    kvr = w_ukv.shape[0]
    wq, wqp, wk, wv = [], [], [], []
    per_q = MLA_NOPE + MLA_ROPE
    per_kv = MLA_NOPE + MLA_V
    for h in range(HEADS):
        nope = w_uq[:, h * per_q:h * per_q + MLA_NOPE]
        rope = w_uq[:, h * per_q + MLA_NOPE:(h + 1) * per_q]
        wq += [nope, rope, jnp.zeros((qr, 32), f32)]
        wqp += [jnp.zeros((qr, 64), f32), _rot_half_cols(rope), jnp.zeros((qr, 32), f32)]
        wk += [w_ukv[:, h * per_kv:h * per_kv + MLA_NOPE], jnp.zeros((kvr, 64), f32)]
        wv += [w_ukv[:, h * per_kv + MLA_NOPE:(h + 1) * per_kv]]
    cat = lambda xs: jnp.concatenate(xs, axis=1).astype(bf16)
    return w_main, w_f, w_gates, cat(wq), cat(wqp), cat(wk), cat(wv)


def _alibi_slopes():
    n = 2 * HEADS
    sl = [2.0 ** (-(8.0 / n) * (k + 1)) for k in range(n)]
    return sl[0::2], sl[1::2]


def kernel(x, c, positions, ada_w, ada_b, norm1_g, norm2_g, w_in, mla_q_norm_g, mla_kv_norm_g, mla_w_uq,
           mla_w_ukv, diff_lq1, diff_lk1, diff_lq2, diff_lk2, diff_norm_g, w_branch, w_out, ffn_w1, ffn_w3,
           ffn_w2, router_w, router_b, moe_w1, moe_w3, moe_w2, final_norm_g):
    B, S, D = x.shape
    T = B * S
    depth = ada_w.shape[0]
    topk = min(TOPK_MAX, S // 4)
    tm = min(1024, S)
    tm_merge = min(512, S)
    bq = min(256, S)
    ck = min(512, S)

    x2 = x.reshape(T, D)
    pos2 = positions.reshape(T, 1)
    mod = _modulation(c, ada_w, ada_b)

    half = MLA_ROPE // 2
    inv = ROPE_THETA ** (-jnp.arange(half, dtype=f32) / half)
    inv_full = jnp.concatenate([jnp.zeros((64,), f32), inv, inv, jnp.zeros((32,), f32)]).reshape(1, 128)
    sl_a, sl_d = _alibi_slopes()
    slopes_a = jnp.broadcast_to(jnp.asarray(sl_a, f32)[:, None, None], (HEADS, 1, ck))
    slopes_d = jnp.broadcast_to(jnp.asarray(sl_d, f32)[:, None, None], (HEADS, 1, ck))

    row = lambda v: v.reshape(1, -1)
    for l in range(depth):
        sh1, sc1, g1, sh2, sc2, g2 = [mod[l, :, k * D:(k + 1) * D].reshape(B, 1, D) for k in range(6)]
        lam_init = 0.8 - 0.6 * math.exp(-0.3 * l)
        w_main, w_f, w_gates, wq, wqp, wk, wv = _layer_weights(w_in[l], mla_w_uq[l], mla_w_ukv[l])
        n1 = row(norm1_g[l])

        P = _inproj(x2, n1, sc1, sh1, w_main, S, grouped=True, out_dtype=bf16, tm=tm, tn=256)
        F = _inproj(x2, n1, sc1, sh1, w_f, S, grouped=False, out_dtype=f32, tm=tm, tn=w_f.shape[1])
        qm, km, vm = _mla_prep(F, pos2, row(mla_q_norm_g[l]), row(mla_kv_norm_g[l]), wq, wqp, wk, wv,
                               inv_full, tm=tm)

        ya = _dsa_attention(P, F, slopes_a, B, S, bq=bq, ck=ck, topk=topk, gaq=G_AQ, giq=G_IQ, gak=G_AK,
                            gav=G_AV, gik=G_IK, iw_col=F_IW_COL)
        yb = _mla_attention(qm, km, vm, B, S, bq=bq, ck=ck)
        yc = _sb_attention(P, B, S, bq=bq, ck=ck, gq=G_SQ, gk=G_SK, gv=G_SV)
        yd = _diff_attention(P, slopes_d, row(diff_lq1[l]), row(diff_lk1[l]), row(diff_lq2[l]),
                             row(diff_lk2[l]), row(diff_norm_g[l]), B, S, bq=bq, ck=ck, lam_init=lam_init,
                             gq=G_DQ, gk=G_DK, gv=G_DV)

        x2 = _merge(x2, n1, sc1, sh1, g1, (ya, yb, yc, yd), w_gates, w_branch[l].astype(bf16),
                    w_out[l].astype(bf16), S, tm=tm_merge)

        final_g = row(final_norm_g) if l == depth - 1 else None
        n2 = row(norm2_g[l])
        if l % 2 == 0:
            k = l // 2
            x2 = _ffn(x2, n2, sc2, sh2, g2, ffn_w1[k].astype(bf16), ffn_w3[k].astype(bf16),
                      ffn_w2[k].astype(bf16), final_g, S, tm=tm, tn=256)
        else:
            k = l // 2
            E = router_w.shape[2]
            rw = jnp.concatenate([router_w[k], jnp.zeros((D, 128 - E), f32)], axis=1).astype(bf16)
            rb = jnp.concatenate([router_b[k], jnp.full((128 - E,), NEG_INF, f32)]).reshape(1, 128)
            comb, cnt = _router(x2, n2, sc2, sh2, rw, rb, S, tm=tm)
            nch = ((cnt[:, 0, :E].astype(i32) + (MOE_ROWS - 1)) // MOE_ROWS).reshape(-1)
            x2 = _moe(x2, n2, sc2, sh2, g2, comb, nch, moe_w1[k].astype(bf16), moe_w3[k].astype(bf16),
                      moe_w2[k].astype(bf16), final_g, S, tm=tm, tn=512, rows=MOE_ROWS)
    return x2.reshape(B, S, D)
```

```python
import functools
import math

import jax
import jax.numpy as jnp
from jax import lax
from jax.experimental import pallas as pl
from jax.experimental.pallas import tpu as pltpu

f32 = jnp.float32
bf16 = jnp.bfloat16
i32 = jnp.int32

N_BRANCHES = 4
HEADS = 4
HEAD_W = 64
IDX_HEADS = 4
TOPK_MAX = 256
MLA_Q_RANK = 256
MLA_KV_RANK = 128
MLA_NOPE = 64
MLA_ROPE = 32
MLA_V = 64
MLA_QK_PAD = 128
ROPE_THETA = 10000.0
DIFF_QK = 32
N_EXPERTS = 8
TOP_K = 2
MOE_TN = 512
MOE_ROWS = 288
EPS = 1e-6
NEG_INF = float("-inf")
INT_MIN = -2 ** 31

VMEM_LIMIT = 56 * 1024 * 1024


def _cparams(sem):
    return pltpu.CompilerParams(dimension_semantics=sem, vmem_limit_bytes=VMEM_LIMIT)


def _nt_dot(a, b):
    return lax.dot_general(a, b, (((1,), (1,)), ((), ())), preferred_element_type=f32)


def _dot(a, b):
    return jnp.dot(a, b, preferred_element_type=f32)


def _norm_mod(x, g, sc, sh):
    y = x * lax.rsqrt(jnp.mean(x * x, axis=-1, keepdims=True) + EPS)
    return (y * g) * (1.0 + sc) + sh


def _mod_kernel(c_ref, w_ref, b_ref, o_ref):
    c = c_ref[...]
    cond = (c * jax.nn.sigmoid(c)).astype(bf16)
    o_ref[0] = _dot(cond, w_ref[0].astype(bf16)) + b_ref[0]


def _modulation(c, ada_w, ada_b):
    L, D, N = ada_w.shape
    B = c.shape[0]
    tn = 1536
    return pl.pallas_call(
        _mod_kernel,
        grid=(L, N // tn),
        in_specs=[
            pl.BlockSpec((B, D), lambda l, j: (0, 0)),
            pl.BlockSpec((1, D, tn), lambda l, j: (l, 0, j)),
            pl.BlockSpec((1, 1, tn), lambda l, j: (l, 0, j)),
        ],
        out_specs=pl.BlockSpec((1, B, tn), lambda l, j: (l, 0, j)),
        out_shape=jax.ShapeDtypeStruct((L, B, N), f32),
        compiler_params=_cparams(("arbitrary", "arbitrary")),
        name="modulation",
    )(c, ada_w, ada_b.reshape(L, 1, N))


def _inproj_kernel(x_ref, g_ref, sc_ref, sh_ref, w_ref, o_ref, h_scr, *, grouped):
    @pl.when(pl.program_id(1) == 0)
    def _():
        h_scr[...] = _norm_mod(x_ref[...], g_ref[...], sc_ref[0], sh_ref[0]).astype(bf16)

    res = _dot(h_scr[...], w_ref[...])
    if grouped:
        for k in range(res.shape[1] // HEAD_W):
            o_ref[k] = res[:, k * HEAD_W:(k + 1) * HEAD_W].astype(o_ref.dtype)
    else:
        o_ref[...] = res.astype(o_ref.dtype)


def _inproj(x2, g, sc, sh, w, S, *, grouped, out_dtype, tm, tn):
    T, D = x2.shape
    N = w.shape[1]
    tpb = S // tm
    if grouped:
        out_shape = jax.ShapeDtypeStruct((N // HEAD_W, T, HEAD_W), out_dtype)
        out_spec = pl.BlockSpec((tn // HEAD_W, tm, HEAD_W), lambda i, j: (j, i, 0))
    else:
        out_shape = jax.ShapeDtypeStruct((T, N), out_dtype)
        out_spec = pl.BlockSpec((tm, tn), lambda i, j: (i, j))
    return pl.pallas_call(
        functools.partial(_inproj_kernel, grouped=grouped),
        grid=(T // tm, N // tn),
        in_specs=[
            pl.BlockSpec((tm, D), lambda i, j: (i, 0)),
            pl.BlockSpec((1, D), lambda i, j: (0, 0)),
            pl.BlockSpec((1, 1, D), lambda i, j: (i // tpb, 0, 0)),
            pl.BlockSpec((1, 1, D), lambda i, j: (i // tpb, 0, 0)),
            pl.BlockSpec((D, tn), lambda i, j: (0, j)),
        ],
        out_specs=out_spec,
        out_shape=out_shape,
        scratch_shapes=[pltpu.VMEM((tm, D), bf16)],
        compiler_params=_cparams(("arbitrary", "arbitrary")),
        name="inproj_grouped" if grouped else "inproj_plain",
    )(x2, g, sc, sh, w)


def _mla_prep_kernel(f_ref, pos_ref, gq_ref, gkv_ref, wq_ref, wqp_ref, wk_ref, wv_ref, inv_ref,
                     q_ref, k_ref, v_ref, *, scale):
    cq = f_ref[:, 0:MLA_Q_RANK]
    ckv = f_ref[:, MLA_Q_RANK:MLA_Q_RANK + MLA_KV_RANK]
    kr = f_ref[:, 384:512]
    krp = f_ref[:, 512:640]
    nq = (cq * lax.rsqrt(jnp.mean(cq * cq, axis=-1, keepdims=True) + EPS) * gq_ref[...]).astype(bf16)
    nkv = (ckv * lax.rsqrt(jnp.mean(ckv * ckv, axis=-1, keepdims=True) + EPS) * gkv_ref[...]).astype(bf16)
    ang = pos_ref[...].astype(f32) * inv_ref[...]
    cosf = jnp.cos(ang)
    sinf = jnp.sin(ang)
    q1 = _dot(nq, wq_ref[...])
    q2 = _dot(nq, wqp_ref[...])
    kn = _dot(nkv, wk_ref[...])
    vv = _dot(nkv, wv_ref[...])
    krope = kr * cosf + krp * sinf
    for h in range(HEADS):
        sl = slice(h * MLA_QK_PAD, (h + 1) * MLA_QK_PAD)
        q_ref[h] = ((q1[:, sl] * cosf + q2[:, sl] * sinf) * scale).astype(bf16)
        k_ref[h] = (kn[:, sl] + krope).astype(bf16)
        v_ref[h] = vv[:, h * MLA_V:(h + 1) * MLA_V].astype(bf16)


def _mla_prep(F, pos2, gq, gkv, wq, wqp, wk, wv, inv_full, *, tm):
    T = F.shape[0]
    scale = (MLA_NOPE + MLA_ROPE) ** -0.5
    full = lambda a: pl.BlockSpec(a.shape, lambda i: (0,) * a.ndim)
    return pl.pallas_call(
        functools.partial(_mla_prep_kernel, scale=scale),
        grid=(T // tm,),
        in_specs=[
            pl.BlockSpec((tm, F.shape[1]), lambda i: (i, 0)),
            pl.BlockSpec((tm, 1), lambda i: (i, 0)),
            full(gq), full(gkv), full(wq), full(wqp), full(wk), full(wv), full(inv_full),
        ],
        out_specs=[
            pl.BlockSpec((HEADS, tm, MLA_QK_PAD), lambda i: (0, i, 0)),
            pl.BlockSpec((HEADS, tm, MLA_QK_PAD), lambda i: (0, i, 0)),
            pl.BlockSpec((HEADS, tm, MLA_V), lambda i: (0, i, 0)),
        ],
        out_shape=[
            jax.ShapeDtypeStruct((HEADS, T, MLA_QK_PAD), bf16),
            jax.ShapeDtypeStruct((HEADS, T, MLA_QK_PAD), bf16),
            jax.ShapeDtypeStruct((HEADS, T, MLA_V), bf16),
        ],
        compiler_params=_cparams(("arbitrary",)),
        name="mla_prep",
    )(F, pos2, gq, gkv, wq, wqp, wk, wv, inv_full)


def _chunk_plan(i, bq, ck):
    n_full = (i * bq) // ck
    return n_full, n_full * ck


def _tail_valid(i, bq, ck, tail_off):
    row = lax.broadcasted_iota(i32, (bq, ck), 0)
    col = lax.broadcasted_iota(i32, (bq, ck), 1)
    return col + (tail_off - i * bq) <= row


def _softmax_step(s, v, m, l, acc):
    m_new = jnp.maximum(m, jnp.max(s, axis=1, keepdims=True))
    alpha = jnp.exp(m - m_new)
    p = jnp.exp(s - m_new)
    l = alpha * l + jnp.sum(p, axis=1, keepdims=True)
    acc = alpha * acc + _dot(p.astype(bf16), v)
    return m_new, l, acc


def _softmax_init(bq, dv):
    return (jnp.full((bq, 1), NEG_INF, f32), jnp.zeros((bq, 1), f32), jnp.zeros((bq, dv), f32))


def _mla_attn_kernel(q_ref, k_ref, v_ref, o_ref, *, bq, ck):
    i = pl.program_id(1)
    n_full, tail_off = _chunk_plan(i, bq, ck)
    dv = v_ref.shape[2]

    def chunk(off, carry, valid):
        off = pl.multiple_of(off, ck)
        out = ()
        for h in range(HEADS):
            s = _nt_dot(q_ref[h], k_ref[h, pl.ds(off, ck), :])
            if valid is not None:
                s = jnp.where(valid, s, NEG_INF)
            out += _softmax_step(s, v_ref[h, pl.ds(off, ck), :], *carry[3 * h:3 * h + 3])
        return out

    carry = lax.fori_loop(0, n_full, lambda c, carry: chunk(c * ck, carry, None), _softmax_init(bq, dv) * HEADS)
    carry = chunk(tail_off, carry, _tail_valid(i, bq, ck, tail_off))
    for h in range(HEADS):
        m, l, acc = carry[3 * h:3 * h + 3]
        o_ref[h] = (acc / l).astype(o_ref.dtype)


def _mla_attention(qm, km, vm, B, S, *, bq, ck):
    H, T, dk = qm.shape
    dv = vm.shape[2]
    nq = S // bq
    return pl.pallas_call(
        functools.partial(_mla_attn_kernel, bq=bq, ck=ck),
        grid=(B, nq),
        in_specs=[
            pl.BlockSpec((H, bq, dk), lambda b, i: (0, b * nq + i, 0)),
            pl.BlockSpec((H, S, dk), lambda b, i: (0, b, 0)),
            pl.BlockSpec((H, S, dv), lambda b, i: (0, b, 0)),
        ],
        out_specs=pl.BlockSpec((H, bq, dv), lambda b, i: (0, b * nq + i, 0)),
        out_shape=jax.ShapeDtypeStruct((H, T, dv), bf16),
        compiler_params=_cparams(("arbitrary", "arbitrary")),
        name="mla_attention",
    )(qm, km, vm)


def _diff_attn_kernel(q_ref, k_ref, v_ref, slope_ref, lq1_ref, lk1_ref, lq2_ref, lk2_ref, g_ref, o_ref,
                      *, bq, ck, lam_init):
    i = pl.program_id(1)
    n_full, tail_off = _chunk_plan(i, bq, ck)
    dv = v_ref.shape[2]
    lane = lax.broadcasted_iota(i32, (bq, q_ref.shape[2]), 1)
    col = lax.broadcasted_iota(i32, (1, ck), 1).astype(f32)

    def chunk(off, carry, valid):
        off = pl.multiple_of(off, ck)
        pos = col + (off - i * bq).astype(f32)
        out = ()
        for h in range(HEADS):
            q = q_ref[h]
            zero = jnp.zeros_like(q)
            k = k_ref[h, pl.ds(off, ck), :]
            v = v_ref[h, pl.ds(off, ck), :]
            bias = slope_ref[h] * pos
            for mp, qm in enumerate((jnp.where(lane < DIFF_QK, q, zero), jnp.where(lane < DIFF_QK, zero, q))):
                s = _nt_dot(qm, k) + bias
                if valid is not None:
                    s = jnp.where(valid, s, NEG_INF)
                base = 3 * (2 * h + mp)
                out += _softmax_step(s, v, *carry[base:base + 3])
        return out

    carry = lax.fori_loop(0, n_full, lambda c, carry: chunk(c * ck, carry, None),
                          _softmax_init(bq, dv) * (2 * HEADS))
    carry = chunk(tail_off, carry, _tail_valid(i, bq, ck, tail_off))
    lam = (jnp.exp(jnp.sum(lq1_ref[...] * lk1_ref[...], axis=1, keepdims=True))
           - jnp.exp(jnp.sum(lq2_ref[...] * lk2_ref[...], axis=1, keepdims=True)) + lam_init)
    for h in range(HEADS):
        m0, l0, a0, m1, l1, a1 = carry[6 * h:6 * h + 6]
        o = a0 / l0 - lam * (a1 / l1)
        y = o * lax.rsqrt(jnp.mean(o * o, axis=-1, keepdims=True) + EPS) * g_ref[...]
        o_ref[h] = (y * (1.0 - lam_init)).astype(o_ref.dtype)


def _diff_attention(P, slopes, lq1, lk1, lq2, lk2, g, B, S, *, bq, ck, lam_init, gq, gk, gv):
    G, T, hw = P.shape
    nq = S // bq
    vec = lambda a: pl.BlockSpec(a.shape, lambda b, i: (0,) * a.ndim)
    return pl.pallas_call(
        functools.partial(_diff_attn_kernel, bq=bq, ck=ck, lam_init=lam_init),
        grid=(B, nq),
        in_specs=[
            pl.BlockSpec((HEADS, bq, hw), lambda b, i: (gq // HEADS, b * nq + i, 0)),
            pl.BlockSpec((HEADS, S, hw), lambda b, i: (gk // HEADS, b, 0)),
            pl.BlockSpec((HEADS, S, hw), lambda b, i: (gv // HEADS, b, 0)),
            vec(slopes), vec(lq1), vec(lk1), vec(lq2), vec(lk2), vec(g),
        ],
        out_specs=pl.BlockSpec((HEADS, bq, hw), lambda b, i: (0, b * nq + i, 0)),
        out_shape=jax.ShapeDtypeStruct((HEADS, T, hw), bf16),
        compiler_params=_cparams(("arbitrary", "arbitrary")),
        name="diff_attention",
    )(P, P, P, slopes, lq1, lk1, lq2, lk2, g)


SB_GROUP = 256


def _sb_attn_kernel(q_ref, k_ref, v_ref, o_ref, *, bq, ck):
    i = pl.program_id(1)
    n_full, tail_off = _chunk_plan(i, bq, ck)
    dv = v_ref.shape[2]
    gw = min(SB_GROUP, ck)
    rr = lax.broadcasted_iota(i32, (gw, gw), 0)
    cc = lax.broadcasted_iota(i32, (gw, gw), 1)
    upper = jnp.where(rr > cc, 1.0, 0.0).astype(bf16)

    def chunk(off, carry, strict):
        off = pl.multiple_of(off, ck)
        out = ()
        for h in range(HEADS):
            run, acc = carry[2 * h:2 * h + 2]
            z = _nt_dot(q_ref[h], k_ref[h, pl.ds(off, ck), :])
            lsm = -(jnp.maximum(z, 0.0) + jnp.log(1.0 + jnp.exp(-jnp.abs(z))))
            if strict is not None:
                lsm = jnp.where(strict, lsm, 0.0)
            parts = []
            for g in reversed(range(ck // gw)):
                x = lsm[:, g * gw:(g + 1) * gw]
                hi = x.astype(bf16)
                lo = (x - hi.astype(f32)).astype(bf16)
                parts.append(_dot(hi, upper) + _dot(lo, upper) + run)
                run = run + jnp.sum(x, axis=1, keepdims=True)
            between = jnp.concatenate(parts[::-1], axis=1) if len(parts) > 1 else parts[0]
            a = jnp.exp(z + lsm + between)
            if strict is not None:
                a = jnp.where(strict, a, 0.0)
            acc = acc + _dot(a.astype(bf16), v_ref[h, pl.ds(off, ck), :])
            out += (run, acc)
        return out

    row = lax.broadcasted_iota(i32, (bq, ck), 0)
    col = lax.broadcasted_iota(i32, (bq, ck), 1)
    strict = col + (tail_off - i * bq) < row
    init = (jnp.zeros((bq, 1), f32), jnp.zeros((bq, dv), f32)) * HEADS
    carry = chunk(tail_off, init, strict)
    carry = lax.fori_loop(0, n_full, lambda n, carry: chunk((n_full - 1 - n) * ck, carry, None), carry)
    for h in range(HEADS):
        o_ref[h] = carry[2 * h + 1].astype(o_ref.dtype)


def _sb_attention(P, B, S, *, bq, ck, gq, gk, gv):
    G, T, hw = P.shape
    nq = S // bq
    return pl.pallas_call(
        functools.partial(_sb_attn_kernel, bq=bq, ck=ck),
        grid=(B, nq),
        in_specs=[
            pl.BlockSpec((HEADS, bq, hw), lambda b, i: (gq // HEADS, b * nq + i, 0)),
            pl.BlockSpec((HEADS, S, hw), lambda b, i: (gk // HEADS, b, 0)),
            pl.BlockSpec((HEADS, S, hw), lambda b, i: (gv // HEADS, b, 0)),
        ],
        out_specs=pl.BlockSpec((HEADS, bq, hw), lambda b, i: (0, b * nq + i, 0)),
        out_shape=jax.ShapeDtypeStruct((HEADS, T, hw), bf16),
        compiler_params=_cparams(("arbitrary", "arbitrary")),
        name="sb_attention",
    )(P, P, P)


def _dsa_kernel(aq_ref, iq_ref, ak_ref, av_ref, ik_ref, iw_ref, slope_ref, o_ref, key_scr, *, bq, ck, topk):
    i = pl.program_id(1)
    n_full, tail_off = _chunk_plan(i, bq, ck)
    n_chunks = n_full + 1
    dv = av_ref.shape[2]
    w_t = iw_ref[...].T
    gw = min(SB_GROUP, ck)
    rr = lax.broadcasted_iota(i32, (gw, gw), 0)
    cc = lax.broadcasted_iota(i32, (gw, gw), 1)
    incl = jnp.where(cc <= rr, 1.0, 0.0).astype(bf16)
    col = lax.broadcasted_iota(i32, (1, ck), 1).astype(f32)
    key_i = lax.broadcasted_iota(i32, (ck, bq), 0)
    qry_i = lax.broadcasted_iota(i32, (ck, bq), 1)
    tail_valid_t = key_i + (tail_off - i * bq) <= qry_i

    def score_chunk(c, valid_t):
        off = pl.multiple_of(c * ck, ck)
        ik = ik_ref[0, pl.ds(off, ck), :]
        sc = jnp.zeros((ck, bq), f32)
        for h in range(IDX_HEADS):
            sc = sc + jnp.maximum(_nt_dot(ik, iq_ref[h]), 0.0) * w_t[h:h + 1, :]
        if valid_t is not None:
            sc = jnp.where(valid_t, sc, NEG_INF)
        bits = lax.bitcast_convert_type(sc + 0.0, i32)
        key_scr[c] = jnp.where(bits < 0, bits ^ jnp.int32(0x7FFFFFFF), bits)

    def score_body(c, _):
        score_chunk(c, None)
        return 0

    lax.fori_loop(0, n_full, score_body, 0)
    score_chunk(n_full, tail_valid_t)

    lanes_acc = 32

    def count(pred):
        def body(c, acc):
            key = key_scr[c].reshape(ck // lanes_acc, lanes_acc, bq)
            return acc + jnp.sum(jnp.where(pred(key), 1.0, 0.0), axis=0)
        acc = lax.fori_loop(0, n_chunks, body, jnp.zeros((lanes_acc, bq), f32))
        return jnp.sum(acc, axis=0, keepdims=True)

    def search_body(it, prefix):
        cand = prefix + lax.shift_left(jnp.int32(1), jnp.int32(31) - it)
        cand_r = jnp.broadcast_to(cand, (lanes_acc, bq))[None]
        cnt = count(lambda key: key >= cand_r)
        return jnp.where(cnt >= float(topk), cand, prefix)

    tau = lax.fori_loop(0, 32, search_body, jnp.full((1, bq), INT_MIN, i32))
    tau_r = jnp.broadcast_to(tau, (lanes_acc, bq))[None]
    n_take = float(topk) - count(lambda key: key > tau_r)

    def attn_chunk(c, carry, valid_t):
        off = pl.multiple_of(c * ck, ck)
        eq_run = carry[0]
        key = key_scr[c]
        eq = key == tau
        eqf = jnp.where(eq, 1.0, 0.0)
        pcs = []
        for g in range(ck // gw):
            e = eqf[g * gw:(g + 1) * gw, :]
            pcs.append(_dot(incl, e.astype(bf16)) + eq_run)
            eq_run = eq_run + jnp.sum(e, axis=0, keepdims=True)
        pc = jnp.concatenate(pcs, axis=0) if len(pcs) > 1 else pcs[0]
        addm_t = jnp.where(key > tau, 0.0, jnp.where(eq, jnp.where(pc <= n_take, 0.0, NEG_INF), NEG_INF))
        if valid_t is not None:
            addm_t = jnp.where(valid_t, addm_t, NEG_INF)
        addm = addm_t.T
        k = ak_ref[0, pl.ds(off, ck), :]
        v = av_ref[0, pl.ds(off, ck), :]
        pos = col + (off - i * bq).astype(f32)
        out = (eq_run,)
        for h in range(HEADS):
            m, l, acc = carry[1 + 3 * h:4 + 3 * h]
            s = _nt_dot(aq_ref[h], k) + slope_ref[h] * pos + addm
            m_new = jnp.maximum(m, jnp.max(s, axis=1, keepdims=True))
            m_safe = jnp.where(m_new == NEG_INF, 0.0, m_new)
            alpha = jnp.exp(m - m_safe)
            p = jnp.exp(s - m_safe)
            l = alpha * l + jnp.sum(p, axis=1, keepdims=True)
            acc = alpha * acc + _dot(p.astype(bf16), v)
            out += (m_new, l, acc)
        return out

    init = (jnp.zeros((1, bq), f32),) + _softmax_init(bq, dv) * HEADS
    carry = lax.fori_loop(0, n_full, lambda c, carry: attn_chunk(c, carry, None), init)
    carry = attn_chunk(n_full, carry, tail_valid_t)
    for h in range(HEADS):
        m, l, acc = carry[1 + 3 * h:4 + 3 * h]
        o_ref[h] = (acc / l).astype(o_ref.dtype)


def _dsa_attention(P, F, slopes, B, S, *, bq, ck, topk, gaq, giq, gak, gav, gik, iw_col):
    G, T, hw = P.shape
    nq = S // bq
    return pl.pallas_call(
        functools.partial(_dsa_kernel, bq=bq, ck=ck, topk=topk),
        grid=(B, nq),
        in_specs=[
            pl.BlockSpec((HEADS, bq, hw), lambda b, i: (gaq // HEADS, b * nq + i, 0)),
            pl.BlockSpec((IDX_HEADS, bq, hw), lambda b, i: (giq // IDX_HEADS, b * nq + i, 0)),
            pl.BlockSpec((1, S, hw), lambda b, i: (gak, b, 0)),
            pl.BlockSpec((1, S, hw), lambda b, i: (gav, b, 0)),
            pl.BlockSpec((1, S, hw), lambda b, i: (gik, b, 0)),
            pl.BlockSpec((bq, 128), lambda b, i: (b * nq + i, iw_col // 128)),
            pl.BlockSpec(slopes.shape, lambda b, i: (0, 0, 0)),
        ],
        out_specs=pl.BlockSpec((HEADS, bq, hw), lambda b, i: (0, b * nq + i, 0)),
        out_shape=jax.ShapeDtypeStruct((HEADS, T, hw), bf16),
        scratch_shapes=[pltpu.VMEM((S // ck, ck, bq), i32)],
        compiler_params=_cparams(("arbitrary", "arbitrary")),
        name="dsa_attention",
    )(P, P, P, P, P, F, slopes)


def _merge_kernel(x_ref, g_ref, sc_ref, sh_ref, g1_ref, ya_ref, yb_ref, yc_ref, yd_ref,
                  wg_ref, wb_ref, wo_ref, o_ref):
    x = x_ref[...]
    D = x.shape[1]
    h = _norm_mod(x, g_ref[...], sc_ref[0], sh_ref[0]).astype(bf16)
    merged = jnp.zeros(x.shape, f32)
    for n, y_ref in enumerate((ya_ref, yb_ref, yc_ref, yd_ref)):
        gate = jax.nn.sigmoid(_dot(h, wg_ref[:, n * D:(n + 1) * D]))
        y = jnp.concatenate([y_ref[hh] for hh in range(HEADS)], axis=1)
        merged = merged + gate * _dot(y, wb_ref[n])
    o_ref[...] = x + g1_ref[0] * _dot(merged.astype(bf16), wo_ref[...])


def _merge(x2, g, sc, sh, g1, ys, wg, wb, wo, S, *, tm):
    T, D = x2.shape
    tpb = S // tm
    row = pl.BlockSpec((tm, D), lambda i: (i, 0))
    per_b = pl.BlockSpec((1, 1, D), lambda i: (i // tpb, 0, 0))
    yspec = pl.BlockSpec((HEADS, tm, HEAD_W), lambda i: (0, i, 0))
    full = lambda a: pl.BlockSpec(a.shape, lambda i: (0,) * a.ndim)
    return pl.pallas_call(
        _merge_kernel,
        grid=(T // tm,),
        in_specs=[row, full(g), per_b, per_b, per_b, yspec, yspec, yspec, yspec, full(wg), full(wb), full(wo)],
        out_specs=row,
        out_shape=jax.ShapeDtypeStruct((T, D), f32),
        compiler_params=_cparams(("arbitrary",)),
        name="merge",
    )(x2, g, sc, sh, g1, *ys, wg, wb, wo)


def _finish(x, g2, f, fg_ref, o_ref):
    out = x + g2 * f
    if fg_ref is not None:
        out = out * lax.rsqrt(jnp.mean(out * out, axis=-1, keepdims=True) + EPS) * fg_ref[...]
    o_ref[...] = out


def _ffn_kernel(x_ref, g_ref, sc_ref, sh_ref, g2_ref, w1_ref, w3_ref, w2_ref, *rest, final):
    fg_ref = rest[0] if final else None
    o_ref, h_scr, acc_scr = rest[-3:]
    j = pl.program_id(1)

    @pl.when(j == 0)
    def _():
        h_scr[...] = _norm_mod(x_ref[...], g_ref[...], sc_ref[0], sh_ref[0]).astype(bf16)
        acc_scr[...] = jnp.zeros_like(acc_scr)

    h = h_scr[...]
    a = _dot(h, w1_ref[...])
    u = (a * jax.nn.sigmoid(a)) * _dot(h, w3_ref[...])
    acc_scr[...] += _dot(u.astype(bf16), w2_ref[...])

    @pl.when(j == pl.num_programs(1) - 1)
    def _():
        _finish(x_ref[...], g2_ref[0], acc_scr[...], fg_ref, o_ref)


def _ffn(x2, g, sc, sh, g2, w1, w3, w2, final_g, S, *, tm, tn):
    T, D = x2.shape
    F = w1.shape[1]
    tpb = S // tm
    row = pl.BlockSpec((tm, D), lambda i, j: (i, 0))
    per_b = pl.BlockSpec((1, 1, D), lambda i, j: (i // tpb, 0, 0))
    vec = pl.BlockSpec((1, D), lambda i, j: (0, 0))
    in_specs = [row, vec, per_b, per_b, per_b,
                pl.BlockSpec((D, tn), lambda i, j: (0, j)),
                pl.BlockSpec((D, tn), lambda i, j: (0, j)),
                pl.BlockSpec((tn, D), lambda i, j: (j, 0))]
    args = [x2, g, sc, sh, g2, w1, w3, w2]
    if final_g is not None:
        in_specs.append(vec)
        args.append(final_g)
    return pl.pallas_call(
        functools.partial(_ffn_kernel, final=final_g is not None),
        grid=(T // tm, F // tn),
        in_specs=in_specs,
        out_specs=row,
        out_shape=jax.ShapeDtypeStruct((T, D), f32),
        scratch_shapes=[pltpu.VMEM((tm, D), bf16), pltpu.VMEM((tm, D), f32)],
        compiler_params=_cparams(("arbitrary", "arbitrary")),
        name="ffn",
    )(*args)


def _router_kernel(x_ref, g_ref, sc_ref, sh_ref, rw_ref, rb_ref, comb_ref, cnt_ref):
    h = _norm_mod(x_ref[...], g_ref[...], sc_ref[0], sh_ref[0]).astype(bf16)
    logits = _dot(h, rw_ref[...]) + rb_ref[...]
    lane = lax.broadcasted_iota(i32, logits.shape, 1)
    big = jnp.int32(logits.shape[1])
    m1 = jnp.max(logits, axis=1, keepdims=True)
    i1 = jnp.min(jnp.where(logits == m1, lane, big), axis=1, keepdims=True)
    rest_l = jnp.where(lane == i1, NEG_INF, logits)
    m2 = jnp.max(rest_l, axis=1, keepdims=True)
    i2 = jnp.min(jnp.where(rest_l == m2, lane, big), axis=1, keepdims=True)
    e2 = jnp.exp(m2 - m1)
    den = 1.0 + e2
    comb = jnp.where(lane == i1, 1.0 / den, jnp.where(lane == i2, e2 / den, 0.0))
    comb_ref[...] = comb
    cnt_ref[0] = jnp.sum(jnp.where(comb > 0.0, 1.0, 0.0), axis=0, keepdims=True)


def _router(x2, g, sc, sh, rw, rb, S, *, tm):
    T, D = x2.shape
    tpb = S // tm
    nt = T // tm
    return pl.pallas_call(
        _router_kernel,
        grid=(nt,),
        in_specs=[pl.BlockSpec((tm, D), lambda i: (i, 0)),
                  pl.BlockSpec((1, D), lambda i: (0, 0)),
                  pl.BlockSpec((1, 1, D), lambda i: (i // tpb, 0, 0)),
                  pl.BlockSpec((1, 1, D), lambda i: (i // tpb, 0, 0)),
                  pl.BlockSpec(rw.shape, lambda i: (0, 0)),
                  pl.BlockSpec(rb.shape, lambda i: (0, 0))],
        out_specs=[pl.BlockSpec((tm, 128), lambda i: (i, 0)),
                   pl.BlockSpec((1, 1, 128), lambda i: (i, 0, 0))],
        out_shape=[jax.ShapeDtypeStruct((T, 128), f32), jax.ShapeDtypeStruct((nt, 1, 128), f32)],
        compiler_params=_cparams(("arbitrary",)),
        name="router",
    )(x2, g, sc, sh, rw, rb)


def _moe_kernel(nch_ref, x_ref, g_ref, sc_ref, sh_ref, g2_ref, comb_ref, w13_ref, w2_ref, *rest,
                final, rows):
    fg_ref = rest[0] if final else None
    o_ref, h_scr, tri_scr, rank_scr, rank_t_scr, xg_scr, yg_scr, out_scr = rest[-8:]
    i = pl.program_id(0)
    e = pl.program_id(1)
    j = pl.program_id(2)
    n_e = pl.num_programs(1)
    tm = x_ref.shape[0]
    nch = nch_ref[i * n_e + e]

    @pl.when(jnp.logical_and(i == 0, jnp.logical_and(e == 0, j == 0)))
    def _():
        rr = lax.broadcasted_iota(i32, (tm, tm), 0)
        cc = lax.broadcasted_iota(i32, (tm, tm), 1)
        tri_scr[...] = jnp.where(cc < rr, 1.0, 0.0).astype(bf16)

    @pl.when(jnp.logical_and(e == 0, j == 0))
    def _():
        h_scr[...] = _norm_mod(x_ref[...], g_ref[...], sc_ref[0], sh_ref[0]).astype(bf16)
        out_scr[...] = jnp.zeros_like(out_scr)
        member = jnp.where(comb_ref[...] > 0.0, 1.0, 0.0)
        rank = jnp.where(member > 0.0, _dot(tri_scr[...], member.astype(bf16)), -1.0)
        rank_scr[...] = rank
        rank_t_scr[...] = rank.T

    def chunk_rows(c):
        return pl.ds(pl.multiple_of(c * rows, 16), rows)

    @pl.when(j == 0)
    def _():
        rank_row = rank_t_scr[pl.ds(e, 1), :]
        slot = lax.broadcasted_iota(i32, (rows, tm), 0).astype(f32)

        def gather(c, _):
            pick = jnp.where(slot + (c * rows).astype(f32) == rank_row, 1.0, 0.0).astype(bf16)
            xg_scr[chunk_rows(c), :] = _dot(pick, h_scr[...]).astype(bf16)
            yg_scr[chunk_rows(c), :] = jnp.zeros((rows, yg_scr.shape[1]), f32)
            return 0

        lax.fori_loop(0, nch, gather, 0)

    def expert(c, _):
        xg = xg_scr[chunk_rows(c), :]
        ab = _dot(xg, w13_ref[0, 0])
        tn = ab.shape[1] // 2
        a = ab[:, :tn]
        u = (a * jax.nn.sigmoid(a)) * ab[:, tn:]
        yg_scr[chunk_rows(c), :] += _dot(u.astype(bf16), w2_ref[0])
        return 0

    lax.fori_loop(0, nch, expert, 0)

    last_j = j == pl.num_programs(2) - 1

    @pl.when(last_j)
    def _():
        lane = lax.broadcasted_iota(i32, (tm, 128), 1)
        sel = lane == e
        rank_col = jnp.sum(jnp.where(sel, rank_scr[...], 0.0), axis=1, keepdims=True)
        w_col = jnp.sum(jnp.where(sel, comb_ref[...], 0.0), axis=1, keepdims=True)
        slot = lax.broadcasted_iota(i32, (tm, rows), 1).astype(f32)

        def scatter(c, _):
            place = jnp.where(slot + (c * rows).astype(f32) == rank_col, 1.0, 0.0).astype(bf16)
            out_scr[...] += w_col * _dot(place, yg_scr[chunk_rows(c), :].astype(bf16))
            return 0

        lax.fori_loop(0, nch, scatter, 0)

    @pl.when(jnp.logical_and(last_j, e == n_e - 1))
    def _():
        _finish(x_ref[...], g2_ref[0], out_scr[...], fg_ref, o_ref)


def _moe(x2, g, sc, sh, g2, comb, nch, w13, w2, final_g, S, *, tm, rows):
    T, D = x2.shape
    E, nj, _, tn2 = w13.shape
    tn = tn2 // 2
    tpb = S // tm
    cap = -(-tm // rows) * rows
    row = pl.BlockSpec((tm, D), lambda i, e, j, n: (i, 0))
    per_b = pl.BlockSpec((1, 1, D), lambda i, e, j, n: (i // tpb, 0, 0))
    vec = pl.BlockSpec((1, D), lambda i, e, j, n: (0, 0))
    in_specs = [row, vec, per_b, per_b, per_b,
                pl.BlockSpec((tm, 128), lambda i, e, j, n: (i, 0)),
                pl.BlockSpec((1, 1, D, tn2), lambda i, e, j, n: (e, j, 0, 0)),
                pl.BlockSpec((1, tn, D), lambda i, e, j, n: (e, j, 0))]
    args = [x2, g, sc, sh, g2, comb, w13, w2]
    if final_g is not None:
        in_specs.append(vec)
        args.append(final_g)
    return pl.pallas_call(
        functools.partial(_moe_kernel, final=final_g is not None, rows=rows),
        grid_spec=pltpu.PrefetchScalarGridSpec(
            num_scalar_prefetch=1,
            grid=(T // tm, E, nj),
            in_specs=in_specs,
            out_specs=row,
            scratch_shapes=[pltpu.VMEM((tm, D), bf16),
                            pltpu.VMEM((tm, tm), bf16),
                            pltpu.VMEM((tm, 128), f32),
                            pltpu.VMEM((128, tm), f32),
                            pltpu.VMEM((cap, D), bf16),
                            pltpu.VMEM((cap, D), f32),
                            pltpu.VMEM((tm, D), f32)]),
        out_shape=jax.ShapeDtypeStruct((T, D), f32),
        compiler_params=_cparams(("arbitrary", "arbitrary", "arbitrary")),
        name="moe",
    )(nch, *args)


_IN_SIZES = (256, 64, 64, 256, 64, 4, 256, 128, 32, 256, 256, 256, 256, 256, 256, 4096)
_IN_NAMES = ("aq", "ak", "av", "iq", "ik", "iw", "cq", "ckv", "kr", "sq", "sk", "sv", "dq", "dk", "dv", "gates")

G_AQ, G_IQ, G_SQ, G_SK, G_SV, G_DQ, G_DK, G_DV, G_AK, G_AV, G_IK = 0, 4, 8, 12, 16, 20, 24, 28, 32, 33, 34
N_GROUPS = 36
F_IW_COL = 640


def _split_w_in(w):
    out, off = {}, 0
    for name, n in zip(_IN_NAMES, _IN_SIZES):
        out[name] = w[:, off:off + n]
        off += n
    return out


def _rot_half_cols(w):
    half = w.shape[1] // 2
    return jnp.concatenate([-w[:, half:], w[:, :half]], axis=1)


def _layer_weights(w_in, w_uq, w_ukv):
    c = _split_w_in(w_in)
    D = w_in.shape[0]
    z = lambda n: jnp.zeros((D, n), f32)
    w_main = jnp.concatenate([
        c["aq"] * 0.125, c["iq"] * 0.125, c["sq"] * 0.125, c["sk"], c["sv"],
        c["dq"] * (DIFF_QK ** -0.5), c["dk"], c["dv"], c["ak"], c["av"], c["ik"], z(64)], axis=1).astype(bf16)
    w_f = jnp.concatenate([
        c["cq"], c["ckv"],
        z(64), c["kr"], z(32),
        z(64), _rot_half_cols(c["kr"]), z(32),
        c["iw"] * (IDX_HEADS ** -0.5), z(124)], axis=1).astype(bf16)
    w_gates = c["gates"].astype(bf16)

    qr = w_uq.shape[0]
    kvr = w_ukv.shape[0]
    wq, wqp, wk, wv = [], [], [], []
    per_q = MLA_NOPE + MLA_ROPE
    per_kv = MLA_NOPE + MLA_V
    for h in range(HEADS):
        nope = w_uq[:, h * per_q:h * per_q + MLA_NOPE]
        rope = w_uq[:, h * per_q + MLA_NOPE:(h + 1) * per_q]
        wq += [nope, rope, jnp.zeros((qr, 32), f32)]
        wqp += [jnp.zeros((qr, 64), f32), _rot_half_cols(rope), jnp.zeros((qr, 32), f32)]
        wk += [w_ukv[:, h * per_kv:h * per_kv + MLA_NOPE], jnp.zeros((kvr, 64), f32)]
        wv += [w_ukv[:, h * per_kv + MLA_NOPE:(h + 1) * per_kv]]
    cat = lambda xs: jnp.concatenate(xs, axis=1).astype(bf16)
    return w_main, w_f, w_gates, cat(wq), cat(wqp), cat(wk), cat(wv)


def _alibi_slopes():
    n = 2 * HEADS
    sl = [2.0 ** (-(8.0 / n) * (k + 1)) for k in range(n)]
    return sl[0::2], sl[1::2]


def kernel(x, c, positions, ada_w, ada_b, norm1_g, norm2_g, w_in, mla_q_norm_g, mla_kv_norm_g, mla_w_uq,
           mla_w_ukv, diff_lq1, diff_lk1, diff_lq2, diff_lk2, diff_norm_g, w_branch, w_out, ffn_w1, ffn_w3,
           ffn_w2, router_w, router_b, moe_w1, moe_w3, moe_w2, final_norm_g):
    B, S, D = x.shape
    T = B * S
    depth = ada_w.shape[0]
    topk = min(TOPK_MAX, S // 4)
    tm = min(1024, S)
    tm_merge = min(512, S)
    bq = min(256, S)
    ck = min(512, S)

    x2 = x.reshape(T, D)
    pos2 = positions.reshape(T, 1)
    mod = _modulation(c, ada_w, ada_b)

    half = MLA_ROPE // 2
    inv = ROPE_THETA ** (-jnp.arange(half, dtype=f32) / half)
    inv_full = jnp.concatenate([jnp.zeros((64,), f32), inv, inv, jnp.zeros((32,), f32)]).reshape(1, 128)
    sl_a, sl_d = _alibi_slopes()
    slopes_a = jnp.broadcast_to(jnp.asarray(sl_a, f32)[:, None, None], (HEADS, 1, ck))
    slopes_d = jnp.broadcast_to(jnp.asarray(sl_d, f32)[:, None, None], (HEADS, 1, ck))

    row = lambda v: v.reshape(1, -1)
    for l in range(depth):
        sh1, sc1, g1, sh2, sc2, g2 = [mod[l, :, k * D:(k + 1) * D].reshape(B, 1, D) for k in range(6)]
        lam_init = 0.8 - 0.6 * math.exp(-0.3 * l)
        w_main, w_f, w_gates, wq, wqp, wk, wv = _layer_weights(w_in[l], mla_w_uq[l], mla_w_ukv[l])
        n1 = row(norm1_g[l])

        P = _inproj(x2, n1, sc1, sh1, w_main, S, grouped=True, out_dtype=bf16, tm=tm, tn=256)
        F = _inproj(x2, n1, sc1, sh1, w_f, S, grouped=False, out_dtype=f32, tm=tm, tn=w_f.shape[1])
        qm, km, vm = _mla_prep(F, pos2, row(mla_q_norm_g[l]), row(mla_kv_norm_g[l]), wq, wqp, wk, wv,
                               inv_full, tm=tm)

        ya = _dsa_attention(P, F, slopes_a, B, S, bq=bq, ck=ck, topk=topk, gaq=G_AQ, giq=G_IQ, gak=G_AK,
                            gav=G_AV, gik=G_IK, iw_col=F_IW_COL)
        yb = _mla_attention(qm, km, vm, B, S, bq=bq, ck=ck)
        yc = _sb_attention(P, B, S, bq=bq, ck=ck, gq=G_SQ, gk=G_SK, gv=G_SV)
        yd = _diff_attention(P, slopes_d, row(diff_lq1[l]), row(diff_lk1[l]), row(diff_lq2[l]),
                             row(diff_lk2[l]), row(diff_norm_g[l]), B, S, bq=bq, ck=ck, lam_init=lam_init,
                             gq=G_DQ, gk=G_DK, gv=G_DV)

        x2 = _merge(x2, n1, sc1, sh1, g1, (ya, yb, yc, yd), w_gates, w_branch[l].astype(bf16),
                    w_out[l].astype(bf16), S, tm=tm_merge)

        final_g = row(final_norm_g) if l == depth - 1 else None
        n2 = row(norm2_g[l])
        if l % 2 == 0:
            k = l // 2
            x2 = _ffn(x2, n2, sc2, sh2, g2, ffn_w1[k].astype(bf16), ffn_w3[k].astype(bf16),
                      ffn_w2[k].astype(bf16), final_g, S, tm=tm, tn=256)
        else:
            k = l // 2
            E = router_w.shape[2]
            rw = jnp.concatenate([router_w[k], jnp.zeros((D, 128 - E), f32)], axis=1).astype(bf16)
            rb = jnp.concatenate([router_b[k], jnp.full((128 - E,), NEG_INF, f32)]).reshape(1, 128)
            comb, cnt = _router(x2, n2, sc2, sh2, rw, rb, S, tm=tm)
            nch = ((cnt[:, 0, :E].astype(i32) + (MOE_ROWS - 1)) // MOE_ROWS).reshape(-1)
            ff = moe_w1.shape[3]
            nj = ff // MOE_TN
            tiles = lambda w: w.astype(bf16).reshape(E, D, nj, MOE_TN).transpose(0, 2, 1, 3)
            w13 = jnp.concatenate([tiles(moe_w1[k]), tiles(moe_w3[k])], axis=3)
            x2 = _moe(x2, n2, sc2, sh2, g2, comb, nch, w13, moe_w2[k].astype(bf16), final_g, S,
                      tm=tm, rows=MOE_ROWS)
    return x2.reshape(B, S, D)
```

```python
import functools
import math

import jax
import jax.numpy as jnp
from jax import lax
from jax.experimental import pallas as pl
from jax.experimental.pallas import tpu as pltpu

f32 = jnp.float32
bf16 = jnp.bfloat16
i32 = jnp.int32

N_BRANCHES = 4
HEADS = 4
HEAD_W = 64
IDX_HEADS = 4
TOPK_MAX = 256
MLA_Q_RANK = 256
MLA_KV_RANK = 128
MLA_NOPE = 64
MLA_ROPE = 32
MLA_V = 64
MLA_QK_PAD = 128
ROPE_THETA = 10000.0
DIFF_QK = 32
N_EXPERTS = 8
TOP_K = 2
MOE_TN = 896
MOE_ROWS = 320
EPS = 1e-6
NEG_INF = float("-inf")
INT_MIN = -2 ** 31

VMEM_LIMIT = 56 * 1024 * 1024


def _cparams(sem):
    return pltpu.CompilerParams(dimension_semantics=sem, vmem_limit_bytes=VMEM_LIMIT)


def _nt_dot(a, b):
    return lax.dot_general(a, b, (((1,), (1,)), ((), ())), preferred_element_type=f32)


def _dot(a, b):
    return jnp.dot(a, b, preferred_element_type=f32)


def _norm_mod(x, g, sc, sh):
    y = x * lax.rsqrt(jnp.mean(x * x, axis=-1, keepdims=True) + EPS)
    return (y * g) * (1.0 + sc) + sh


def _mod_kernel(c_ref, w_ref, b_ref, o_ref):
    c = c_ref[...]
    cond = (c * jax.nn.sigmoid(c)).astype(bf16)
    o_ref[0] = _dot(cond, w_ref[0].astype(bf16)) + b_ref[0]


def _modulation(c, ada_w, ada_b):
    L, D, N = ada_w.shape
    B = c.shape[0]
    tn = 1536
    return pl.pallas_call(
        _mod_kernel,
        grid=(L, N // tn),
        in_specs=[
            pl.BlockSpec((B, D), lambda l, j: (0, 0)),
            pl.BlockSpec((1, D, tn), lambda l, j: (l, 0, j)),
            pl.BlockSpec((1, 1, tn), lambda l, j: (l, 0, j)),
        ],
        out_specs=pl.BlockSpec((1, B, tn), lambda l, j: (l, 0, j)),
        out_shape=jax.ShapeDtypeStruct((L, B, N), f32),
        compiler_params=_cparams(("arbitrary", "arbitrary")),
        name="modulation",
    )(c, ada_w, ada_b.reshape(L, 1, N))


def _inproj_kernel(x_ref, g_ref, sc_ref, sh_ref, w_ref, o_ref, h_scr, *, grouped):
    @pl.when(pl.program_id(1) == 0)
    def _():
        h_scr[...] = _norm_mod(x_ref[...], g_ref[...], sc_ref[0], sh_ref[0]).astype(bf16)

    res = _dot(h_scr[...], w_ref[...])
    if grouped:
        for k in range(res.shape[1] // HEAD_W):
            o_ref[k] = res[:, k * HEAD_W:(k + 1) * HEAD_W].astype(o_ref.dtype)
    else:
        o_ref[...] = res.astype(o_ref.dtype)


def _inproj(x2, g, sc, sh, w, S, *, grouped, out_dtype, tm, tn):
    T, D = x2.shape
    N = w.shape[1]
    tpb = S // tm
    if grouped:
        out_shape = jax.ShapeDtypeStruct((N // HEAD_W, T, HEAD_W), out_dtype)
        out_spec = pl.BlockSpec((tn // HEAD_W, tm, HEAD_W), lambda i, j: (j, i, 0))
    else:
        out_shape = jax.ShapeDtypeStruct((T, N), out_dtype)
        out_spec = pl.BlockSpec((tm, tn), lambda i, j: (i, j))
    return pl.pallas_call(
        functools.partial(_inproj_kernel, grouped=grouped),
        grid=(T // tm, N // tn),
        in_specs=[
            pl.BlockSpec((tm, D), lambda i, j: (i, 0)),
            pl.BlockSpec((1, D), lambda i, j: (0, 0)),
            pl.BlockSpec((1, 1, D), lambda i, j: (i // tpb, 0, 0)),
            pl.BlockSpec((1, 1, D), lambda i, j: (i // tpb, 0, 0)),
            pl.BlockSpec((D, tn), lambda i, j: (0, j)),
        ],
        out_specs=out_spec,
        out_shape=out_shape,
        scratch_shapes=[pltpu.VMEM((tm, D), bf16)],
        compiler_params=_cparams(("arbitrary", "arbitrary")),
        name="inproj_grouped" if grouped else "inproj_plain",
    )(x2, g, sc, sh, w)


def _mla_prep_kernel(f_ref, pos_ref, gq_ref, gkv_ref, wq_ref, wqp_ref, wk_ref, wv_ref, inv_ref,
                     q_ref, k_ref, v_ref, *, scale):
    cq = f_ref[:, 0:MLA_Q_RANK]
    ckv = f_ref[:, MLA_Q_RANK:MLA_Q_RANK + MLA_KV_RANK]
    kr = f_ref[:, 384:512]
    krp = f_ref[:, 512:640]
    nq = (cq * lax.rsqrt(jnp.mean(cq * cq, axis=-1, keepdims=True) + EPS) * gq_ref[...]).astype(bf16)
    nkv = (ckv * lax.rsqrt(jnp.mean(ckv * ckv, axis=-1, keepdims=True) + EPS) * gkv_ref[...]).astype(bf16)
    ang = pos_ref[...].astype(f32) * inv_ref[...]
    cosf = jnp.cos(ang)
    sinf = jnp.sin(ang)
    q1 = _dot(nq, wq_ref[...])
    q2 = _dot(nq, wqp_ref[...])
    kn = _dot(nkv, wk_ref[...])
    vv = _dot(nkv, wv_ref[...])
    krope = kr * cosf + krp * sinf
    for h in range(HEADS):
        sl = slice(h * MLA_QK_PAD, (h + 1) * MLA_QK_PAD)
        q_ref[h] = ((q1[:, sl] * cosf + q2[:, sl] * sinf) * scale).astype(bf16)
        k_ref[h] = (kn[:, sl] + krope).astype(bf16)
        v_ref[h] = vv[:, h * MLA_V:(h + 1) * MLA_V].astype(bf16)


def _mla_prep(F, pos2, gq, gkv, wq, wqp, wk, wv, inv_full, *, tm):
    T = F.shape[0]
    scale = (MLA_NOPE + MLA_ROPE) ** -0.5
    full = lambda a: pl.BlockSpec(a.shape, lambda i: (0,) * a.ndim)
    return pl.pallas_call(
        functools.partial(_mla_prep_kernel, scale=scale),
        grid=(T // tm,),
        in_specs=[
            pl.BlockSpec((tm, F.shape[1]), lambda i: (i, 0)),
            pl.BlockSpec((tm, 1), lambda i: (i, 0)),
            full(gq), full(gkv), full(wq), full(wqp), full(wk), full(wv), full(inv_full),
        ],
        out_specs=[
            pl.BlockSpec((HEADS, tm, MLA_QK_PAD), lambda i: (0, i, 0)),
            pl.BlockSpec((HEADS, tm, MLA_QK_PAD), lambda i: (0, i, 0)),
            pl.BlockSpec((HEADS, tm, MLA_V), lambda i: (0, i, 0)),
        ],
        out_shape=[
            jax.ShapeDtypeStruct((HEADS, T, MLA_QK_PAD), bf16),
            jax.ShapeDtypeStruct((HEADS, T, MLA_QK_PAD), bf16),
            jax.ShapeDtypeStruct((HEADS, T, MLA_V), bf16),
        ],
        compiler_params=_cparams(("arbitrary",)),
        name="mla_prep",
    )(F, pos2, gq, gkv, wq, wqp, wk, wv, inv_full)


def _chunk_plan(i, bq, ck):
    n_full = (i * bq) // ck
    return n_full, n_full * ck


def _tail_valid(i, bq, ck, tail_off):
    row = lax.broadcasted_iota(i32, (bq, ck), 0)
    col = lax.broadcasted_iota(i32, (bq, ck), 1)
    return col + (tail_off - i * bq) <= row


def _softmax_step(s, v, m, l, acc):
    m_new = jnp.maximum(m, jnp.max(s, axis=1, keepdims=True))
    alpha = jnp.exp(m - m_new)
    p = jnp.exp(s - m_new)
    l = alpha * l + jnp.sum(p, axis=1, keepdims=True)
    acc = alpha * acc + _dot(p.astype(bf16), v)
    return m_new, l, acc


def _softmax_init(bq, dv):
    return (jnp.full((bq, 1), NEG_INF, f32), jnp.zeros((bq, 1), f32), jnp.zeros((bq, dv), f32))


def _mla_attn_kernel(q_ref, k_ref, v_ref, o_ref, *, bq, ck):
    i = pl.program_id(1)
    n_full, tail_off = _chunk_plan(i, bq, ck)
    dv = v_ref.shape[2]

    def chunk(off, carry, valid):
        off = pl.multiple_of(off, ck)
        out = ()
        for h in range(HEADS):
            s = _nt_dot(q_ref[h], k_ref[h, pl.ds(off, ck), :])
            if valid is not None:
                s = jnp.where(valid, s, NEG_INF)
            out += _softmax_step(s, v_ref[h, pl.ds(off, ck), :], *carry[3 * h:3 * h + 3])
        return out

    carry = lax.fori_loop(0, n_full, lambda c, carry: chunk(c * ck, carry, None), _softmax_init(bq, dv) * HEADS)
    carry = chunk(tail_off, carry, _tail_valid(i, bq, ck, tail_off))
    for h in range(HEADS):
        m, l, acc = carry[3 * h:3 * h + 3]
        o_ref[h] = (acc / l).astype(o_ref.dtype)


def _mla_attention(qm, km, vm, B, S, *, bq, ck):
    H, T, dk = qm.shape
    dv = vm.shape[2]
    nq = S // bq
    return pl.pallas_call(
        functools.partial(_mla_attn_kernel, bq=bq, ck=ck),
        grid=(B, nq),
        in_specs=[
            pl.BlockSpec((H, bq, dk), lambda b, i: (0, b * nq + i, 0)),
            pl.BlockSpec((H, S, dk), lambda b, i: (0, b, 0)),
            pl.BlockSpec((H, S, dv), lambda b, i: (0, b, 0)),
        ],
        out_specs=pl.BlockSpec((H, bq, dv), lambda b, i: (0, b * nq + i, 0)),
        out_shape=jax.ShapeDtypeStruct((H, T, dv), bf16),
        compiler_params=_cparams(("arbitrary", "arbitrary")),
        name="mla_attention",
    )(qm, km, vm)


def _diff_attn_kernel(q_ref, k_ref, v_ref, slope_ref, lq1_ref, lk1_ref, lq2_ref, lk2_ref, g_ref, o_ref,
                      *, bq, ck, lam_init):
    i = pl.program_id(1)
    n_full, tail_off = _chunk_plan(i, bq, ck)
    dv = v_ref.shape[2]
    lane = lax.broadcasted_iota(i32, (bq, q_ref.shape[2]), 1)
    col = lax.broadcasted_iota(i32, (1, ck), 1).astype(f32)

    def chunk(off, carry, valid):
        off = pl.multiple_of(off, ck)
        pos = col + (off - i * bq).astype(f32)
        out = ()
        for h in range(HEADS):
            q = q_ref[h]
            zero = jnp.zeros_like(q)
            k = k_ref[h, pl.ds(off, ck), :]
            v = v_ref[h, pl.ds(off, ck), :]
            bias = slope_ref[h] * pos
            for mp, qm in enumerate((jnp.where(lane < DIFF_QK, q, zero), jnp.where(lane < DIFF_QK, zero, q))):
                s = _nt_dot(qm, k) + bias
                if valid is not None:
                    s = jnp.where(valid, s, NEG_INF)
                base = 3 * (2 * h + mp)
                out += _softmax_step(s, v, *carry[base:base + 3])
        return out

    carry = lax.fori_loop(0, n_full, lambda c, carry: chunk(c * ck, carry, None),
                          _softmax_init(bq, dv) * (2 * HEADS))
    carry = chunk(tail_off, carry, _tail_valid(i, bq, ck, tail_off))
    lam = (jnp.exp(jnp.sum(lq1_ref[...] * lk1_ref[...], axis=1, keepdims=True))
           - jnp.exp(jnp.sum(lq2_ref[...] * lk2_ref[...], axis=1, keepdims=True)) + lam_init)
    for h in range(HEADS):
        m0, l0, a0, m1, l1, a1 = carry[6 * h:6 * h + 6]
        o = a0 / l0 - lam * (a1 / l1)
        y = o * lax.rsqrt(jnp.mean(o * o, axis=-1, keepdims=True) + EPS) * g_ref[...]
        o_ref[h] = (y * (1.0 - lam_init)).astype(o_ref.dtype)


def _diff_attention(P, slopes, lq1, lk1, lq2, lk2, g, B, S, *, bq, ck, lam_init, gq, gk, gv):
    G, T, hw = P.shape
    nq = S // bq
    vec = lambda a: pl.BlockSpec(a.shape, lambda b, i: (0,) * a.ndim)
    return pl.pallas_call(
        functools.partial(_diff_attn_kernel, bq=bq, ck=ck, lam_init=lam_init),
        grid=(B, nq),
        in_specs=[
            pl.BlockSpec((HEADS, bq, hw), lambda b, i: (gq // HEADS, b * nq + i, 0)),
            pl.BlockSpec((HEADS, S, hw), lambda b, i: (gk // HEADS, b, 0)),
            pl.BlockSpec((HEADS, S, hw), lambda b, i: (gv // HEADS, b, 0)),
            vec(slopes), vec(lq1), vec(lk1), vec(lq2), vec(lk2), vec(g),
        ],
        out_specs=pl.BlockSpec((HEADS, bq, hw), lambda b, i: (0, b * nq + i, 0)),
        out_shape=jax.ShapeDtypeStruct((HEADS, T, hw), bf16),
        compiler_params=_cparams(("arbitrary", "arbitrary")),
        name="diff_attention",
    )(P, P, P, slopes, lq1, lk1, lq2, lk2, g)


SB_GROUP = 256


def _sb_attn_kernel(q_ref, k_ref, v_ref, o_ref, *, bq, ck):
    i = pl.program_id(1)
    n_full, tail_off = _chunk_plan(i, bq, ck)
    dv = v_ref.shape[2]
    gw = min(SB_GROUP, ck)
    rr = lax.broadcasted_iota(i32, (gw, gw), 0)
    cc = lax.broadcasted_iota(i32, (gw, gw), 1)
    upper = jnp.where(rr > cc, 1.0, 0.0).astype(bf16)

    def chunk(off, carry, strict):
        off = pl.multiple_of(off, ck)
        out = ()
        for h in range(HEADS):
            run, acc = carry[2 * h:2 * h + 2]
            z = _nt_dot(q_ref[h], k_ref[h, pl.ds(off, ck), :])
            lsm = -(jnp.maximum(z, 0.0) + jnp.log(1.0 + jnp.exp(-jnp.abs(z))))
            if strict is not None:
                lsm = jnp.where(strict, lsm, 0.0)
            parts = []
            for g in reversed(range(ck // gw)):
                x = lsm[:, g * gw:(g + 1) * gw]
                hi = x.astype(bf16)
                lo = (x - hi.astype(f32)).astype(bf16)
                parts.append(_dot(hi, upper) + _dot(lo, upper) + run)
                run = run + jnp.sum(x, axis=1, keepdims=True)
            between = jnp.concatenate(parts[::-1], axis=1) if len(parts) > 1 else parts[0]
            a = jnp.exp(z + lsm + between)
            if strict is not None:
                a = jnp.where(strict, a, 0.0)
            acc = acc + _dot(a.astype(bf16), v_ref[h, pl.ds(off, ck), :])
            out += (run, acc)
        return out

    row = lax.broadcasted_iota(i32, (bq, ck), 0)
    col = lax.broadcasted_iota(i32, (bq, ck), 1)
    strict = col + (tail_off - i * bq) < row
    init = (jnp.zeros((bq, 1), f32), jnp.zeros((bq, dv), f32)) * HEADS
    carry = chunk(tail_off, init, strict)
    carry = lax.fori_loop(0, n_full, lambda n, carry: chunk((n_full - 1 - n) * ck, carry, None), carry)
    for h in range(HEADS):
        o_ref[h] = carry[2 * h + 1].astype(o_ref.dtype)


def _sb_attention(P, B, S, *, bq, ck, gq, gk, gv):
    G, T, hw = P.shape
    nq = S // bq
    return pl.pallas_call(
        functools.partial(_sb_attn_kernel, bq=bq, ck=ck),
        grid=(B, nq),
        in_specs=[
            pl.BlockSpec((HEADS, bq, hw), lambda b, i: (gq // HEADS, b * nq + i, 0)),
            pl.BlockSpec((HEADS, S, hw), lambda b, i: (gk // HEADS, b, 0)),
            pl.BlockSpec((HEADS, S, hw), lambda b, i: (gv // HEADS, b, 0)),
        ],
        out_specs=pl.BlockSpec((HEADS, bq, hw), lambda b, i: (0, b * nq + i, 0)),
        out_shape=jax.ShapeDtypeStruct((HEADS, T, hw), bf16),
        compiler_params=_cparams(("arbitrary", "arbitrary")),
        name="sb_attention",
    )(P, P, P)


def _dsa_kernel(aq_ref, iq_ref, ak_ref, av_ref, ik_ref, iw_ref, slope_ref, o_ref, key_scr, *, bq, ck, topk):
    i = pl.program_id(1)
    n_full, tail_off = _chunk_plan(i, bq, ck)
    n_chunks = n_full + 1
    dv = av_ref.shape[2]
    w_t = iw_ref[...].T
    gw = min(SB_GROUP, ck)
    rr = lax.broadcasted_iota(i32, (gw, gw), 0)
    cc = lax.broadcasted_iota(i32, (gw, gw), 1)
    incl = jnp.where(cc <= rr, 1.0, 0.0).astype(bf16)
    col = lax.broadcasted_iota(i32, (1, ck), 1).astype(f32)
    key_i = lax.broadcasted_iota(i32, (ck, bq), 0)
    qry_i = lax.broadcasted_iota(i32, (ck, bq), 1)
    tail_valid_t = key_i + (tail_off - i * bq) <= qry_i

    def score_chunk(c, valid_t):
        off = pl.multiple_of(c * ck, ck)
        ik = ik_ref[0, pl.ds(off, ck), :]
        sc = jnp.zeros((ck, bq), f32)
        for h in range(IDX_HEADS):
            sc = sc + jnp.maximum(_nt_dot(ik, iq_ref[h]), 0.0) * w_t[h:h + 1, :]
        if valid_t is not None:
            sc = jnp.where(valid_t, sc, NEG_INF)
        bits = lax.bitcast_convert_type(sc + 0.0, i32)
        key_scr[c] = jnp.where(bits < 0, bits ^ jnp.int32(0x7FFFFFFF), bits)

    def score_body(c, _):
        score_chunk(c, None)
        return 0

    lax.fori_loop(0, n_full, score_body, 0)
    score_chunk(n_full, tail_valid_t)

    lanes_acc = 32

    def count(pred):
        def body(c, acc):
            key = key_scr[c].reshape(ck // lanes_acc, lanes_acc, bq)
            return acc + jnp.sum(jnp.where(pred(key), 1.0, 0.0), axis=0)
        acc = lax.fori_loop(0, n_chunks, body, jnp.zeros((lanes_acc, bq), f32))
        return jnp.sum(acc, axis=0, keepdims=True)

    def search_body(it, prefix):
        cand = prefix + lax.shift_left(jnp.int32(1), jnp.int32(31) - it)
        cand_r = jnp.broadcast_to(cand, (lanes_acc, bq))[None]
        cnt = count(lambda key: key >= cand_r)
        return jnp.where(cnt >= float(topk), cand, prefix)

    tau = lax.fori_loop(0, 32, search_body, jnp.full((1, bq), INT_MIN, i32))
    tau_r = jnp.broadcast_to(tau, (lanes_acc, bq))[None]
    n_take = float(topk) - count(lambda key: key > tau_r)

    def attn_chunk(c, carry, valid_t):
        off = pl.multiple_of(c * ck, ck)
        eq_run = carry[0]
        key = key_scr[c]
        eq = key == tau
        eqf = jnp.where(eq, 1.0, 0.0)
        pcs = []
        for g in range(ck // gw):
            e = eqf[g * gw:(g + 1) * gw, :]
            pcs.append(_dot(incl, e.astype(bf16)) + eq_run)
            eq_run = eq_run + jnp.sum(e, axis=0, keepdims=True)
        pc = jnp.concatenate(pcs, axis=0) if len(pcs) > 1 else pcs[0]
        addm_t = jnp.where(key > tau, 0.0, jnp.where(eq, jnp.where(pc <= n_take, 0.0, NEG_INF), NEG_INF))
        if valid_t is not None:
            addm_t = jnp.where(valid_t, addm_t, NEG_INF)
        addm = addm_t.T
        k = ak_ref[0, pl.ds(off, ck), :]
        v = av_ref[0, pl.ds(off, ck), :]
        pos = col + (off - i * bq).astype(f32)
        out = (eq_run,)
        for h in range(HEADS):
            m, l, acc = carry[1 + 3 * h:4 + 3 * h]
            s = _nt_dot(aq_ref[h], k) + slope_ref[h] * pos + addm
            m_new = jnp.maximum(m, jnp.max(s, axis=1, keepdims=True))
            m_safe = jnp.where(m_new == NEG_INF, 0.0, m_new)
            alpha = jnp.exp(m - m_safe)
            p = jnp.exp(s - m_safe)
            l = alpha * l + jnp.sum(p, axis=1, keepdims=True)
            acc = alpha * acc + _dot(p.astype(bf16), v)
            out += (m_new, l, acc)
        return out

    init = (jnp.zeros((1, bq), f32),) + _softmax_init(bq, dv) * HEADS
    carry = lax.fori_loop(0, n_full, lambda c, carry: attn_chunk(c, carry, None), init)
    carry = attn_chunk(n_full, carry, tail_valid_t)
    for h in range(HEADS):
        m, l, acc = carry[1 + 3 * h:4 + 3 * h]
        o_ref[h] = (acc / l).astype(o_ref.dtype)


def _dsa_attention(P, F, slopes, B, S, *, bq, ck, topk, gaq, giq, gak, gav, gik, iw_col):
    G, T, hw = P.shape
    nq = S // bq
    return pl.pallas_call(
        functools.partial(_dsa_kernel, bq=bq, ck=ck, topk=topk),
        grid=(B, nq),
        in_specs=[
            pl.BlockSpec((HEADS, bq, hw), lambda b, i: (gaq // HEADS, b * nq + i, 0)),
            pl.BlockSpec((IDX_HEADS, bq, hw), lambda b, i: (giq // IDX_HEADS, b * nq + i, 0)),
            pl.BlockSpec((1, S, hw), lambda b, i: (gak, b, 0)),
            pl.BlockSpec((1, S, hw), lambda b, i: (gav, b, 0)),
            pl.BlockSpec((1, S, hw), lambda b, i: (gik, b, 0)),
            pl.BlockSpec((bq, 128), lambda b, i: (b * nq + i, iw_col // 128)),
            pl.BlockSpec(slopes.shape, lambda b, i: (0, 0, 0)),
        ],
        out_specs=pl.BlockSpec((HEADS, bq, hw), lambda b, i: (0, b * nq + i, 0)),
        out_shape=jax.ShapeDtypeStruct((HEADS, T, hw), bf16),
        scratch_shapes=[pltpu.VMEM((S // ck, ck, bq), i32)],
        compiler_params=_cparams(("arbitrary", "arbitrary")),
        name="dsa_attention",
    )(P, P, P, P, P, F, slopes)


def _merge_kernel(x_ref, g_ref, sc_ref, sh_ref, g1_ref, ya_ref, yb_ref, yc_ref, yd_ref,
                  wg_ref, wb_ref, wo_ref, o_ref):
    x = x_ref[...]
    D = x.shape[1]
    h = _norm_mod(x, g_ref[...], sc_ref[0], sh_ref[0]).astype(bf16)
    merged = jnp.zeros(x.shape, f32)
    for n, y_ref in enumerate((ya_ref, yb_ref, yc_ref, yd_ref)):
        gate = jax.nn.sigmoid(_dot(h, wg_ref[:, n * D:(n + 1) * D]))
        y = jnp.concatenate([y_ref[hh] for hh in range(HEADS)], axis=1)
        merged = merged + gate * _dot(y, wb_ref[n])
    o_ref[...] = x + g1_ref[0] * _dot(merged.astype(bf16), wo_ref[...])


def _merge(x2, g, sc, sh, g1, ys, wg, wb, wo, S, *, tm):
    T, D = x2.shape
    tpb = S // tm
    row = pl.BlockSpec((tm, D), lambda i: (i, 0))
    per_b = pl.BlockSpec((1, 1, D), lambda i: (i // tpb, 0, 0))
    yspec = pl.BlockSpec((HEADS, tm, HEAD_W), lambda i: (0, i, 0))
    full = lambda a: pl.BlockSpec(a.shape, lambda i: (0,) * a.ndim)
    return pl.pallas_call(
        _merge_kernel,
        grid=(T // tm,),
        in_specs=[row, full(g), per_b, per_b, per_b, yspec, yspec, yspec, yspec, full(wg), full(wb), full(wo)],
        out_specs=row,
        out_shape=jax.ShapeDtypeStruct((T, D), f32),
        compiler_params=_cparams(("arbitrary",)),
        name="merge",
    )(x2, g, sc, sh, g1, *ys, wg, wb, wo)


def _finish(x, g2, f, fg_ref, o_ref):
    out = x + g2 * f
    if fg_ref is not None:
        out = out * lax.rsqrt(jnp.mean(out * out, axis=-1, keepdims=True) + EPS) * fg_ref[...]
    o_ref[...] = out


def _ffn_kernel(x_ref, g_ref, sc_ref, sh_ref, g2_ref, w1_ref, w3_ref, w2_ref, *rest, final):
    fg_ref = rest[0] if final else None
    o_ref, h_scr, acc_scr = rest[-3:]
    j = pl.program_id(1)

    @pl.when(j == 0)
    def _():
        h_scr[...] = _norm_mod(x_ref[...], g_ref[...], sc_ref[0], sh_ref[0]).astype(bf16)
        acc_scr[...] = jnp.zeros_like(acc_scr)

    h = h_scr[...]
    a = _dot(h, w1_ref[...])
    u = (a * jax.nn.sigmoid(a)) * _dot(h, w3_ref[...])
    acc_scr[...] += _dot(u.astype(bf16), w2_ref[...])

    @pl.when(j == pl.num_programs(1) - 1)
    def _():
        _finish(x_ref[...], g2_ref[0], acc_scr[...], fg_ref, o_ref)


def _ffn(x2, g, sc, sh, g2, w1, w3, w2, final_g, S, *, tm, tn):
    T, D = x2.shape
    F = w1.shape[1]
    tpb = S // tm
    row = pl.BlockSpec((tm, D), lambda i, j: (i, 0))
    per_b = pl.BlockSpec((1, 1, D), lambda i, j: (i // tpb, 0, 0))
    vec = pl.BlockSpec((1, D), lambda i, j: (0, 0))
    in_specs = [row, vec, per_b, per_b, per_b,
                pl.BlockSpec((D, tn), lambda i, j: (0, j)),
                pl.BlockSpec((D, tn), lambda i, j: (0, j)),
                pl.BlockSpec((tn, D), lambda i, j: (j, 0))]
    args = [x2, g, sc, sh, g2, w1, w3, w2]
    if final_g is not None:
        in_specs.append(vec)
        args.append(final_g)
    return pl.pallas_call(
        functools.partial(_ffn_kernel, final=final_g is not None),
        grid=(T // tm, F // tn),
        in_specs=in_specs,
        out_specs=row,
        out_shape=jax.ShapeDtypeStruct((T, D), f32),
        scratch_shapes=[pltpu.VMEM((tm, D), bf16), pltpu.VMEM((tm, D), f32)],
        compiler_params=_cparams(("arbitrary", "arbitrary")),
        name="ffn",
    )(*args)


def _router_kernel(x_ref, g_ref, sc_ref, sh_ref, rw_ref, rb_ref, comb_ref, cnt_ref):
    h = _norm_mod(x_ref[...], g_ref[...], sc_ref[0], sh_ref[0]).astype(bf16)
    logits = _dot(h, rw_ref[...]) + rb_ref[...]
    lane = lax.broadcasted_iota(i32, logits.shape, 1)
    big = jnp.int32(logits.shape[1])
    m1 = jnp.max(logits, axis=1, keepdims=True)
    i1 = jnp.min(jnp.where(logits == m1, lane, big), axis=1, keepdims=True)
    rest_l = jnp.where(lane == i1, NEG_INF, logits)
    m2 = jnp.max(rest_l, axis=1, keepdims=True)
    i2 = jnp.min(jnp.where(rest_l == m2, lane, big), axis=1, keepdims=True)
    e2 = jnp.exp(m2 - m1)
    den = 1.0 + e2
    comb = jnp.where(lane == i1, 1.0 / den, jnp.where(lane == i2, e2 / den, 0.0))
    comb_ref[...] = comb
    cnt_ref[0] = jnp.sum(jnp.where(comb > 0.0, 1.0, 0.0), axis=0, keepdims=True)


def _router(x2, g, sc, sh, rw, rb, S, *, tm):
    T, D = x2.shape
    tpb = S // tm
    nt = T // tm
    return pl.pallas_call(
        _router_kernel,
        grid=(nt,),
        in_specs=[pl.BlockSpec((tm, D), lambda i: (i, 0)),
                  pl.BlockSpec((1, D), lambda i: (0, 0)),
                  pl.BlockSpec((1, 1, D), lambda i: (i // tpb, 0, 0)),
                  pl.BlockSpec((1, 1, D), lambda i: (i // tpb, 0, 0)),
                  pl.BlockSpec(rw.shape, lambda i: (0, 0)),
                  pl.BlockSpec(rb.shape, lambda i: (0, 0))],
        out_specs=[pl.BlockSpec((tm, 128), lambda i: (i, 0)),
                   pl.BlockSpec((1, 1, 128), lambda i: (i, 0, 0))],
        out_shape=[jax.ShapeDtypeStruct((T, 128), f32), jax.ShapeDtypeStruct((nt, 1, 128), f32)],
        compiler_params=_cparams(("arbitrary",)),
        name="router",
    )(x2, g, sc, sh, rw, rb)


def _moe_kernel(nch_ref, x_ref, g_ref, sc_ref, sh_ref, g2_ref, comb_ref, w13_ref, w2_ref, *rest,
                final, rows):
    fg_ref = rest[0] if final else None
    o_ref, h_scr, tri_scr, rank_scr, rank_t_scr, xg_scr, yg_scr, out_scr = rest[-8:]
    i = pl.program_id(0)
    e = pl.program_id(1)
    j = pl.program_id(2)
    n_e = pl.num_programs(1)
    tm = x_ref.shape[0]
    nch = nch_ref[i * n_e + e]

    @pl.when(jnp.logical_and(i == 0, jnp.logical_and(e == 0, j == 0)))
    def _():
        rr = lax.broadcasted_iota(i32, (tm, tm), 0)
        cc = lax.broadcasted_iota(i32, (tm, tm), 1)
        tri_scr[...] = jnp.where(cc < rr, 1.0, 0.0).astype(bf16)

    @pl.when(jnp.logical_and(e == 0, j == 0))
    def _():
        h_scr[...] = _norm_mod(x_ref[...], g_ref[...], sc_ref[0], sh_ref[0]).astype(bf16)
        out_scr[...] = jnp.zeros_like(out_scr)
        member = jnp.where(comb_ref[...] > 0.0, 1.0, 0.0)
        rank = jnp.where(member > 0.0, _dot(tri_scr[...], member.astype(bf16)), -1.0)
        rank_scr[...] = rank
        rank_t_scr[...] = rank.T

    def chunk_rows(c):
        return pl.ds(pl.multiple_of(c * rows, 16), rows)

    @pl.when(j == 0)
    def _():
        rank_row = rank_t_scr[pl.ds(e, 1), :]
        slot = lax.broadcasted_iota(i32, (rows, tm), 0).astype(f32)

        def gather(c, _):
            pick = jnp.where(slot + (c * rows).astype(f32) == rank_row, 1.0, 0.0).astype(bf16)
            xg_scr[chunk_rows(c), :] = _dot(pick, h_scr[...]).astype(bf16)
            yg_scr[chunk_rows(c), :] = jnp.zeros((rows, yg_scr.shape[1]), f32)
            return 0

        lax.fori_loop(0, nch, gather, 0)

    def expert(c, _):
        xg = xg_scr[chunk_rows(c), :]
        ab = _dot(xg, w13_ref[0, 0])
        tn = ab.shape[1] // 2
        a = ab[:, :tn]
        u = (a * jax.nn.sigmoid(a)) * ab[:, tn:]
        yg_scr[chunk_rows(c), :] += _dot(u.astype(bf16), w2_ref[0])
        return 0

    lax.fori_loop(0, nch, expert, 0)

    last_j = j == pl.num_programs(2) - 1

    @pl.when(last_j)
    def _():
        lane = lax.broadcasted_iota(i32, (tm, 128), 1)
        sel = lane == e
        rank_col = jnp.sum(jnp.where(sel, rank_scr[...], 0.0), axis=1, keepdims=True)
        w_col = jnp.sum(jnp.where(sel, comb_ref[...], 0.0), axis=1, keepdims=True)
        slot = lax.broadcasted_iota(i32, (tm, rows), 1).astype(f32)

        def scatter(c, _):
            place = jnp.where(slot + (c * rows).astype(f32) == rank_col, 1.0, 0.0).astype(bf16)
            out_scr[...] += w_col * _dot(place, yg_scr[chunk_rows(c), :].astype(bf16))
            return 0

        lax.fori_loop(0, nch, scatter, 0)

    @pl.when(jnp.logical_and(last_j, e == n_e - 1))
    def _():
        _finish(x_ref[...], g2_ref[0], out_scr[...], fg_ref, o_ref)


def _moe(x2, g, sc, sh, g2, comb, nch, w13, w2, final_g, S, *, tm, rows):
    T, D = x2.shape
    E, nj, _, tn2 = w13.shape
    tn = tn2 // 2
    tpb = S // tm
    cap = -(-tm // rows) * rows
    row = pl.BlockSpec((tm, D), lambda i, e, j, n: (i, 0))
    per_b = pl.BlockSpec((1, 1, D), lambda i, e, j, n: (i // tpb, 0, 0))
    vec = pl.BlockSpec((1, D), lambda i, e, j, n: (0, 0))
    in_specs = [row, vec, per_b, per_b, per_b,
                pl.BlockSpec((tm, 128), lambda i, e, j, n: (i, 0)),
                pl.BlockSpec((1, 1, D, tn2), lambda i, e, j, n: (e, j, 0, 0)),
                pl.BlockSpec((1, tn, D), lambda i, e, j, n: (e, j, 0))]
    args = [x2, g, sc, sh, g2, comb, w13, w2]
    if final_g is not None:
        in_specs.append(vec)
        args.append(final_g)
    return pl.pallas_call(
        functools.partial(_moe_kernel, final=final_g is not None, rows=rows),
        grid_spec=pltpu.PrefetchScalarGridSpec(
            num_scalar_prefetch=1,
            grid=(T // tm, E, nj),
            in_specs=in_specs,
            out_specs=row,
            scratch_shapes=[pltpu.VMEM((tm, D), bf16),
                            pltpu.VMEM((tm, tm), bf16),
                            pltpu.VMEM((tm, 128), f32),
                            pltpu.VMEM((128, tm), f32),
                            pltpu.VMEM((cap, D), bf16),
                            pltpu.VMEM((cap, D), f32),
                            pltpu.VMEM((tm, D), f32)]),
        out_shape=jax.ShapeDtypeStruct((T, D), f32),
        compiler_params=_cparams(("arbitrary", "arbitrary", "arbitrary")),
        name="moe",
    )(nch, *args)


_IN_SIZES = (256, 64, 64, 256, 64, 4, 256, 128, 32, 256, 256, 256, 256, 256, 256, 4096)
_IN_NAMES = ("aq", "ak", "av", "iq", "ik", "iw", "cq", "ckv", "kr", "sq", "sk", "sv", "dq", "dk", "dv", "gates")

G_AQ, G_IQ, G_SQ, G_SK, G_SV, G_DQ, G_DK, G_DV, G_AK, G_AV, G_IK = 0, 4, 8, 12, 16, 20, 24, 28, 32, 33, 34
N_GROUPS = 36
F_IW_COL = 640


def _split_w_in(w):
    out, off = {}, 0
    for name, n in zip(_IN_NAMES, _IN_SIZES):
        out[name] = w[:, off:off + n]
        off += n
    return out


def _rot_half_cols(w):
    half = w.shape[1] // 2
    return jnp.concatenate([-w[:, half:], w[:, :half]], axis=1)


def _layer_weights(w_in, w_uq, w_ukv):
    c = _split_w_in(w_in)
    D = w_in.shape[0]
    z = lambda n: jnp.zeros((D, n), f32)
    w_main = jnp.concatenate([
        c["aq"] * 0.125, c["iq"] * 0.125, c["sq"] * 0.125, c["sk"], c["sv"],
        c["dq"] * (DIFF_QK ** -0.5), c["dk"], c["dv"], c["ak"], c["av"], c["ik"], z(64)], axis=1).astype(bf16)
    w_f = jnp.concatenate([
        c["cq"], c["ckv"],
        z(64), c["kr"], z(32),
        z(64), _rot_half_cols(c["kr"]), z(32),
        c["iw"] * (IDX_HEADS ** -0.5), z(124)], axis=1).astype(bf16)
    w_gates = c["gates"].astype(bf16)

    qr = w_uq.shape[0]
    kvr = w_ukv.shape[0]
    wq, wqp, wk, wv = [], [], [], []
    per_q = MLA_NOPE + MLA_ROPE
    per_kv = MLA_NOPE + MLA_V
    for h in range(HEADS):
        nope = w_uq[:, h * per_q:h * per_q + MLA_NOPE]
        rope = w_uq[:, h * per_q + MLA_NOPE:(h + 1) * per_q]
        wq += [nope, rope, jnp.zeros((qr, 32), f32)]
        wqp += [jnp.zeros((qr, 64), f32), _rot_half_cols(rope), jnp.zeros((qr, 32), f32)]
        wk += [w_ukv[:, h * per_kv:h * per_kv + MLA_NOPE], jnp.zeros((kvr, 64), f32)]
        wv += [w_ukv[:, h * per_kv + MLA_NOPE:(h + 1) * per_kv]]
    cat = lambda xs: jnp.concatenate(xs, axis=1).astype(bf16)
    return w_main, w_f, w_gates, cat(wq), cat(wqp), cat(wk), cat(wv)


def _alibi_slopes():
    n = 2 * HEADS
    sl = [2.0 ** (-(8.0 / n) * (k + 1)) for k in range(n)]
    return sl[0::2], sl[1::2]


def kernel(x, c, positions, ada_w, ada_b, norm1_g, norm2_g, w_in, mla_q_norm_g, mla_kv_norm_g, mla_w_uq,
           mla_w_ukv, diff_lq1, diff_lk1, diff_lq2, diff_lk2, diff_norm_g, w_branch, w_out, ffn_w1, ffn_w3,
           ffn_w2, router_w, router_b, moe_w1, moe_w3, moe_w2, final_norm_g):
    B, S, D = x.shape
    T = B * S
    depth = ada_w.shape[0]
    topk = min(TOPK_MAX, S // 4)
    tm = min(1024, S)
    tm_merge = min(512, S)
    bq = min(512, S)
    ck = min(512, S)

    x2 = x.reshape(T, D)
    pos2 = positions.reshape(T, 1)
    mod = _modulation(c, ada_w, ada_b)

    half = MLA_ROPE // 2
    inv = ROPE_THETA ** (-jnp.arange(half, dtype=f32) / half)
    inv_full = jnp.concatenate([jnp.zeros((64,), f32), inv, inv, jnp.zeros((32,), f32)]).reshape(1, 128)
    sl_a, sl_d = _alibi_slopes()
    slopes_a = jnp.broadcast_to(jnp.asarray(sl_a, f32)[:, None, None], (HEADS, 1, ck))
    slopes_d = jnp.broadcast_to(jnp.asarray(sl_d, f32)[:, None, None], (HEADS, 1, ck))

    row = lambda v: v.reshape(1, -1)
    for l in range(depth):
        sh1, sc1, g1, sh2, sc2, g2 = [mod[l, :, k * D:(k + 1) * D].reshape(B, 1, D) for k in range(6)]
        lam_init = 0.8 - 0.6 * math.exp(-0.3 * l)
        w_main, w_f, w_gates, wq, wqp, wk, wv = _layer_weights(w_in[l], mla_w_uq[l], mla_w_ukv[l])
        n1 = row(norm1_g[l])

        P = _inproj(x2, n1, sc1, sh1, w_main, S, grouped=True, out_dtype=bf16, tm=tm, tn=256)
        F = _inproj(x2, n1, sc1, sh1, w_f, S, grouped=False, out_dtype=f32, tm=tm, tn=w_f.shape[1])
        qm, km, vm = _mla_prep(F, pos2, row(mla_q_norm_g[l]), row(mla_kv_norm_g[l]), wq, wqp, wk, wv,
                               inv_full, tm=tm)

        ya = _dsa_attention(P, F, slopes_a, B, S, bq=bq, ck=ck, topk=topk, gaq=G_AQ, giq=G_IQ, gak=G_AK,
                            gav=G_AV, gik=G_IK, iw_col=F_IW_COL)
        yb = _mla_attention(qm, km, vm, B, S, bq=bq, ck=ck)
        yc = _sb_attention(P, B, S, bq=bq, ck=ck, gq=G_SQ, gk=G_SK, gv=G_SV)
        yd = _diff_attention(P, slopes_d, row(diff_lq1[l]), row(diff_lk1[l]), row(diff_lq2[l]),
                             row(diff_lk2[l]), row(diff_norm_g[l]), B, S, bq=bq, ck=ck, lam_init=lam_init,
                             gq=G_DQ, gk=G_DK, gv=G_DV)

        x2 = _merge(x2, n1, sc1, sh1, g1, (ya, yb, yc, yd), w_gates, w_branch[l].astype(bf16),
                    w_out[l].astype(bf16), S, tm=tm_merge)

        final_g = row(final_norm_g) if l == depth - 1 else None
        n2 = row(norm2_g[l])
        if l % 2 == 0:
            k = l // 2
            x2 = _ffn(x2, n2, sc2, sh2, g2, ffn_w1[k].astype(bf16), ffn_w3[k].astype(bf16),
                      ffn_w2[k].astype(bf16), final_g, S, tm=tm, tn=256)
        else:
            k = l // 2
            E = router_w.shape[2]
            rw = jnp.concatenate([router_w[k], jnp.zeros((D, 128 - E), f32)], axis=1).astype(bf16)
            rb = jnp.concatenate([router_b[k], jnp.full((128 - E,), NEG_INF, f32)]).reshape(1, 128)
            comb, cnt = _router(x2, n2, sc2, sh2, rw, rb, S, tm=tm)
            nch = ((cnt[:, 0, :E].astype(i32) + (MOE_ROWS - 1)) // MOE_ROWS).reshape(-1)
            ff = moe_w1.shape[3]
            nj = ff // MOE_TN
            tiles = lambda w: w.astype(bf16).reshape(E, D, nj, MOE_TN).transpose(0, 2, 1, 3)
            w13 = jnp.concatenate([tiles(moe_w1[k]), tiles(moe_w3[k])], axis=3)
            x2 = _moe(x2, n2, sc2, sh2, g2, comb, nch, w13, moe_w2[k].astype(bf16), final_g, S,
                      tm=tm, rows=MOE_ROWS)
    return x2.reshape(B, S, D)
```

```python
import functools
import math

import jax
import jax.numpy as jnp
from jax import lax
from jax.experimental import pallas as pl
from jax.experimental.pallas import tpu as pltpu

f32 = jnp.float32
bf16 = jnp.bfloat16
i32 = jnp.int32

N_BRANCHES = 4
HEADS = 4
HEAD_W = 64
IDX_HEADS = 4
TOPK_MAX = 256
MLA_Q_RANK = 256
MLA_KV_RANK = 128
MLA_NOPE = 64
MLA_ROPE = 32
MLA_V = 64
MLA_QK_PAD = 128
ROPE_THETA = 10000.0
DIFF_QK = 32
N_EXPERTS = 8
TOP_K = 2
MOE_TN = 896
MOE_ROWS = 320
EPS = 1e-6
NEG_INF = float("-inf")
INT_MIN = -2 ** 31

VMEM_LIMIT = 56 * 1024 * 1024


def _cparams(sem):
    return pltpu.CompilerParams(dimension_semantics=sem, vmem_limit_bytes=VMEM_LIMIT)


def _nt_dot(a, b):
    return lax.dot_general(a, b, (((1,), (1,)), ((), ())), preferred_element_type=f32)


def _dot(a, b):
    return jnp.dot(a, b, preferred_element_type=f32)


def _norm_mod(x, g, sc, sh):
    y = x * lax.rsqrt(jnp.mean(x * x, axis=-1, keepdims=True) + EPS)
    return (y * g) * (1.0 + sc) + sh


def _mod_kernel(c_ref, w_ref, b_ref, o_ref):
    c = c_ref[...]
    cond = (c * jax.nn.sigmoid(c)).astype(bf16)
    o_ref[0] = _dot(cond, w_ref[0].astype(bf16)) + b_ref[0]


def _modulation(c, ada_w, ada_b):
    L, D, N = ada_w.shape
    B = c.shape[0]
    tn = 1536
    return pl.pallas_call(
        _mod_kernel,
        grid=(L, N // tn),
        in_specs=[
            pl.BlockSpec((B, D), lambda l, j: (0, 0)),
            pl.BlockSpec((1, D, tn), lambda l, j: (l, 0, j)),
            pl.BlockSpec((1, 1, tn), lambda l, j: (l, 0, j)),
        ],
        out_specs=pl.BlockSpec((1, B, tn), lambda l, j: (l, 0, j)),
        out_shape=jax.ShapeDtypeStruct((L, B, N), f32),
        compiler_params=_cparams(("arbitrary", "arbitrary")),
        name="modulation",
    )(c, ada_w, ada_b.reshape(L, 1, N))


def _inproj_kernel(x_ref, g_ref, sc_ref, sh_ref, w_ref, o_ref, h_scr, *, grouped):
    @pl.when(pl.program_id(1) == 0)
    def _():
        h_scr[...] = _norm_mod(x_ref[...], g_ref[...], sc_ref[0], sh_ref[0]).astype(bf16)

    res = _dot(h_scr[...], w_ref[...])
    if grouped:
        for k in range(res.shape[1] // HEAD_W):
            o_ref[k] = res[:, k * HEAD_W:(k + 1) * HEAD_W].astype(o_ref.dtype)
    else:
        o_ref[...] = res.astype(o_ref.dtype)


def _inproj(x2, g, sc, sh, w, S, *, grouped, out_dtype, tm, tn):
    T, D = x2.shape
    N = w.shape[1]
    tpb = S // tm
    if grouped:
        out_shape = jax.ShapeDtypeStruct((N // HEAD_W, T, HEAD_W), out_dtype)
        out_spec = pl.BlockSpec((tn // HEAD_W, tm, HEAD_W), lambda i, j: (j, i, 0))
    else:
        out_shape = jax.ShapeDtypeStruct((T, N), out_dtype)
        out_spec = pl.BlockSpec((tm, tn), lambda i, j: (i, j))
    return pl.pallas_call(
        functools.partial(_inproj_kernel, grouped=grouped),
        grid=(T // tm, N // tn),
        in_specs=[
            pl.BlockSpec((tm, D), lambda i, j: (i, 0)),
            pl.BlockSpec((1, D), lambda i, j: (0, 0)),
            pl.BlockSpec((1, 1, D), lambda i, j: (i // tpb, 0, 0)),
            pl.BlockSpec((1, 1, D), lambda i, j: (i // tpb, 0, 0)),
            pl.BlockSpec((D, tn), lambda i, j: (0, j)),
        ],
        out_specs=out_spec,
        out_shape=out_shape,
        scratch_shapes=[pltpu.VMEM((tm, D), bf16)],
        compiler_params=_cparams(("arbitrary", "arbitrary")),
        name="inproj_grouped" if grouped else "inproj_plain",
    )(x2, g, sc, sh, w)


def _mla_prep_kernel(f_ref, pos_ref, gq_ref, gkv_ref, wq_ref, wqp_ref, wk_ref, wv_ref, inv_ref,
                     q_ref, k_ref, v_ref, *, scale):
    cq = f_ref[:, 0:MLA_Q_RANK]
    ckv = f_ref[:, MLA_Q_RANK:MLA_Q_RANK + MLA_KV_RANK]
    kr = f_ref[:, 384:512]
    krp = f_ref[:, 512:640]
    nq = (cq * lax.rsqrt(jnp.mean(cq * cq, axis=-1, keepdims=True) + EPS) * gq_ref[...]).astype(bf16)
    nkv = (ckv * lax.rsqrt(jnp.mean(ckv * ckv, axis=-1, keepdims=True) + EPS) * gkv_ref[...]).astype(bf16)
    ang = pos_ref[...].astype(f32) * inv_ref[...]
    cosf = jnp.cos(ang)
    sinf = jnp.sin(ang)
    q1 = _dot(nq, wq_ref[...])
    q2 = _dot(nq, wqp_ref[...])
    kn = _dot(nkv, wk_ref[...])
    vv = _dot(nkv, wv_ref[...])
    krope = kr * cosf + krp * sinf
    for h in range(HEADS):
        sl = slice(h * MLA_QK_PAD, (h + 1) * MLA_QK_PAD)
        q_ref[h] = ((q1[:, sl] * cosf + q2[:, sl] * sinf) * scale).astype(bf16)
        k_ref[h] = (kn[:, sl] + krope).astype(bf16)
        v_ref[h] = vv[:, h * MLA_V:(h + 1) * MLA_V].astype(bf16)


def _mla_prep(F, pos2, gq, gkv, wq, wqp, wk, wv, inv_full, *, tm):
    T = F.shape[0]
    scale = (MLA_NOPE + MLA_ROPE) ** -0.5
    full = lambda a: pl.BlockSpec(a.shape, lambda i: (0,) * a.ndim)
    return pl.pallas_call(
        functools.partial(_mla_prep_kernel, scale=scale),
        grid=(T // tm,),
        in_specs=[
            pl.BlockSpec((tm, F.shape[1]), lambda i: (i, 0)),
            pl.BlockSpec((tm, 1), lambda i: (i, 0)),
            full(gq), full(gkv), full(wq), full(wqp), full(wk), full(wv), full(inv_full),
        ],
        out_specs=[
            pl.BlockSpec((HEADS, tm, MLA_QK_PAD), lambda i: (0, i, 0)),
            pl.BlockSpec((HEADS, tm, MLA_QK_PAD), lambda i: (0, i, 0)),
            pl.BlockSpec((HEADS, tm, MLA_V), lambda i: (0, i, 0)),
        ],
        out_shape=[
            jax.ShapeDtypeStruct((HEADS, T, MLA_QK_PAD), bf16),
            jax.ShapeDtypeStruct((HEADS, T, MLA_QK_PAD), bf16),
            jax.ShapeDtypeStruct((HEADS, T, MLA_V), bf16),
        ],
        compiler_params=_cparams(("arbitrary",)),
        name="mla_prep",
    )(F, pos2, gq, gkv, wq, wqp, wk, wv, inv_full)


def _chunk_plan(i, bq, ck):
    n_full = (i * bq) // ck
    return n_full, n_full * ck


def _tail_valid(i, bq, ck, tail_off):
    row = lax.broadcasted_iota(i32, (bq, ck), 0)
    col = lax.broadcasted_iota(i32, (bq, ck), 1)
    return col + (tail_off - i * bq) <= row


def _fill_v_ones(v_ref, vp_scr):
    for h in range(v_ref.shape[0]):
        vp_scr[h, :, 0:HEAD_W] = v_ref[h]
        vp_scr[h, :, HEAD_W:2 * HEAD_W] = jnp.ones((v_ref.shape[1], HEAD_W), bf16)


def _softmax_step_v1(s, vp, m, acc):
    m_new = jnp.maximum(m, jnp.max(s, axis=1, keepdims=True))
    alpha = jnp.exp(m - m_new)
    p = jnp.exp((s - m_new).astype(bf16))
    return m_new, alpha * acc + _dot(p, vp)


def _softmax_init_v1(bq):
    return (jnp.full((bq, 1), NEG_INF, f32), jnp.zeros((bq, 2 * HEAD_W), f32))


def _softmax_finish_v1(acc):
    return acc[:, 0:HEAD_W] / acc[:, HEAD_W:HEAD_W + 1]


def _mla_attn_kernel(q_ref, k_ref, v_ref, o_ref, vp_scr, *, bq, ck):
    i = pl.program_id(1)
    n_full, tail_off = _chunk_plan(i, bq, ck)

    @pl.when(i == 0)
    def _():
        _fill_v_ones(v_ref, vp_scr)

    def chunk(off, carry, valid):
        off = pl.multiple_of(off, ck)
        out = ()
        for h in range(HEADS):
            s = _nt_dot(q_ref[h], k_ref[h, pl.ds(off, ck), :])
            if valid is not None:
                s = jnp.where(valid, s, NEG_INF)
            out += _softmax_step_v1(s, vp_scr[h, pl.ds(off, ck), :], *carry[2 * h:2 * h + 2])
        return out

    carry = lax.fori_loop(0, n_full, lambda c, carry: chunk(c * ck, carry, None), _softmax_init_v1(bq) * HEADS)
    carry = chunk(tail_off, carry, _tail_valid(i, bq, ck, tail_off))
    for h in range(HEADS):
        o_ref[h] = _softmax_finish_v1(carry[2 * h + 1]).astype(o_ref.dtype)


def _mla_attention(qm, km, vm, B, S, *, bq, ck):
    H, T, dk = qm.shape
    dv = vm.shape[2]
    nq = S // bq
    return pl.pallas_call(
        functools.partial(_mla_attn_kernel, bq=bq, ck=ck),
        grid=(B, nq),
        in_specs=[
            pl.BlockSpec((H, bq, dk), lambda b, i: (0, b * nq + i, 0)),
            pl.BlockSpec((H, S, dk), lambda b, i: (0, b, 0)),
            pl.BlockSpec((H, S, dv), lambda b, i: (0, b, 0)),
        ],
        out_specs=pl.BlockSpec((H, bq, dv), lambda b, i: (0, b * nq + i, 0)),
        out_shape=jax.ShapeDtypeStruct((H, T, dv), bf16),
        scratch_shapes=[pltpu.VMEM((H, S, 2 * HEAD_W), bf16)],
        compiler_params=_cparams(("arbitrary", "arbitrary")),
        name="mla_attention",
    )(qm, km, vm)


def _diff_attn_kernel(q_ref, k_ref, v_ref, slope_ref, lq1_ref, lk1_ref, lq2_ref, lk2_ref, g_ref, o_ref,
                      vp_scr, *, bq, ck, lam_init):
    i = pl.program_id(1)
    n_full, tail_off = _chunk_plan(i, bq, ck)
    lane = lax.broadcasted_iota(i32, (bq, q_ref.shape[2]), 1)
    col = lax.broadcasted_iota(i32, (1, ck), 1).astype(f32)

    @pl.when(i == 0)
    def _():
        _fill_v_ones(v_ref, vp_scr)

    def chunk(off, carry, valid):
        off = pl.multiple_of(off, ck)
        pos = col + (off - i * bq).astype(f32)
        out = ()
        for h in range(HEADS):
            q = q_ref[h]
            zero = jnp.zeros_like(q)
            k = k_ref[h, pl.ds(off, ck), :]
            vp = vp_scr[h, pl.ds(off, ck), :]
            bias = slope_ref[h] * pos
            for mp, qm in enumerate((jnp.where(lane < DIFF_QK, q, zero), jnp.where(lane < DIFF_QK, zero, q))):
                s = _nt_dot(qm, k) + bias
                if valid is not None:
                    s = jnp.where(valid, s, NEG_INF)
                base = 2 * (2 * h + mp)
                out += _softmax_step_v1(s, vp, *carry[base:base + 2])
        return out

    carry = lax.fori_loop(0, n_full, lambda c, carry: chunk(c * ck, carry, None),
                          _softmax_init_v1(bq) * (2 * HEADS))
    carry = chunk(tail_off, carry, _tail_valid(i, bq, ck, tail_off))
    lam = (jnp.exp(jnp.sum(lq1_ref[...] * lk1_ref[...], axis=1, keepdims=True))
           - jnp.exp(jnp.sum(lq2_ref[...] * lk2_ref[...], axis=1, keepdims=True)) + lam_init)
    for h in range(HEADS):
        m0, a0, m1, a1 = carry[4 * h:4 * h + 4]
        o = _softmax_finish_v1(a0) - lam * _softmax_finish_v1(a1)
        y = o * lax.rsqrt(jnp.mean(o * o, axis=-1, keepdims=True) + EPS) * g_ref[...]
        o_ref[h] = (y * (1.0 - lam_init)).astype(o_ref.dtype)


def _diff_attention(P, slopes, lq1, lk1, lq2, lk2, g, B, S, *, bq, ck, lam_init, gq, gk, gv):
    G, T, hw = P.shape
    nq = S // bq
    vec = lambda a: pl.BlockSpec(a.shape, lambda b, i: (0,) * a.ndim)
    return pl.pallas_call(
        functools.partial(_diff_attn_kernel, bq=bq, ck=ck, lam_init=lam_init),
        grid=(B, nq),
        in_specs=[
            pl.BlockSpec((HEADS, bq, hw), lambda b, i: (gq // HEADS, b * nq + i, 0)),
            pl.BlockSpec((HEADS, S, hw), lambda b, i: (gk // HEADS, b, 0)),
            pl.BlockSpec((HEADS, S, hw), lambda b, i: (gv // HEADS, b, 0)),
            vec(slopes), vec(lq1), vec(lk1), vec(lq2), vec(lk2), vec(g),
        ],
        out_specs=pl.BlockSpec((HEADS, bq, hw), lambda b, i: (0, b * nq + i, 0)),
        out_shape=jax.ShapeDtypeStruct((HEADS, T, hw), bf16),
        scratch_shapes=[pltpu.VMEM((HEADS, S, 2 * HEAD_W), bf16)],
        compiler_params=_cparams(("arbitrary", "arbitrary")),
        name="diff_attention",
    )(P, P, P, slopes, lq1, lk1, lq2, lk2, g)


SB_GROUP = 256


def _sb_attn_kernel(q_ref, k_ref, v_ref, o_ref, *, bq, ck):
    i = pl.program_id(1)
    n_full, tail_off = _chunk_plan(i, bq, ck)
    dv = v_ref.shape[2]
    gw = min(SB_GROUP, ck)
    rr = lax.broadcasted_iota(i32, (gw, gw), 0)
    cc = lax.broadcasted_iota(i32, (gw, gw), 1)
    upper = jnp.where(rr > cc, 1.0, 0.0).astype(bf16)

    def chunk(off, carry, strict):
        off = pl.multiple_of(off, ck)
        out = ()
        for h in range(HEADS):
            run, acc = carry[2 * h:2 * h + 2]
            z = _nt_dot(q_ref[h], k_ref[h, pl.ds(off, ck), :])
            lsm = -(jnp.maximum(z, 0.0) + jnp.log(1.0 + jnp.exp(-jnp.abs(z))))
            if strict is not None:
                lsm = jnp.where(strict, lsm, 0.0)
            parts = []
            for g in reversed(range(ck // gw)):
                x = lsm[:, g * gw:(g + 1) * gw]
                parts.append(_dot(x.astype(bf16), upper) + run)
                run = run + jnp.sum(x, axis=1, keepdims=True)
            between = jnp.concatenate(parts[::-1], axis=1) if len(parts) > 1 else parts[0]
            arg = z + lsm + between
            if strict is not None:
                arg = jnp.where(strict, arg, NEG_INF)
            acc = acc + _dot(jnp.exp(arg.astype(bf16)), v_ref[h, pl.ds(off, ck), :])
            out += (run, acc)
        return out

    row = lax.broadcasted_iota(i32, (bq, ck), 0)
    col = lax.broadcasted_iota(i32, (bq, ck), 1)
    strict = col + (tail_off - i * bq) < row
    init = (jnp.zeros((bq, 1), f32), jnp.zeros((bq, dv), f32)) * HEADS
    carry = chunk(tail_off, init, strict)
    carry = lax.fori_loop(0, n_full, lambda n, carry: chunk((n_full - 1 - n) * ck, carry, None), carry)
    for h in range(HEADS):
        o_ref[h] = carry[2 * h + 1].astype(o_ref.dtype)


def _sb_attention(P, B, S, *, bq, ck, gq, gk, gv):
    G, T, hw = P.shape
    nq = S // bq
    return pl.pallas_call(
        functools.partial(_sb_attn_kernel, bq=bq, ck=ck),
        grid=(B, nq),
        in_specs=[
            pl.BlockSpec((HEADS, bq, hw), lambda b, i: (gq // HEADS, b * nq + i, 0)),
            pl.BlockSpec((HEADS, S, hw), lambda b, i: (gk // HEADS, b, 0)),
            pl.BlockSpec((HEADS, S, hw), lambda b, i: (gv // HEADS, b, 0)),
        ],
        out_specs=pl.BlockSpec((HEADS, bq, hw), lambda b, i: (0, b * nq + i, 0)),
        out_shape=jax.ShapeDtypeStruct((HEADS, T, hw), bf16),
        compiler_params=_cparams(("arbitrary", "arbitrary")),
        name="sb_attention",
    )(P, P, P)


def _dsa_kernel(aq_ref, iq_ref, ak_ref, av_ref, ik_ref, iw_ref, slope_ref, o_ref, key_scr, vp_scr,
                *, bq, ck, topk):
    i = pl.program_id(1)
    n_full, tail_off = _chunk_plan(i, bq, ck)
    n_chunks = n_full + 1

    @pl.when(i == 0)
    def _():
        _fill_v_ones(av_ref, vp_scr)

    w_t = iw_ref[...].T
    gw = min(SB_GROUP, ck)
    rr = lax.broadcasted_iota(i32, (gw, gw), 0)
    cc = lax.broadcasted_iota(i32, (gw, gw), 1)
    incl = jnp.where(cc <= rr, 1.0, 0.0).astype(bf16)
    col = lax.broadcasted_iota(i32, (1, ck), 1).astype(f32)
    key_i = lax.broadcasted_iota(i32, (ck, bq), 0)
    qry_i = lax.broadcasted_iota(i32, (ck, bq), 1)
    tail_valid_t = key_i + (tail_off - i * bq) <= qry_i

    def score_chunk(c, valid_t):
        off = pl.multiple_of(c * ck, ck)
        ik = ik_ref[0, pl.ds(off, ck), :]
        sc = jnp.zeros((ck, bq), f32)
        for h in range(IDX_HEADS):
            sc = sc + jnp.maximum(_nt_dot(ik, iq_ref[h]), 0.0) * w_t[h:h + 1, :]
        if valid_t is not None:
            sc = jnp.where(valid_t, sc, NEG_INF)
        bits = lax.bitcast_convert_type(sc + 0.0, i32)
        key_scr[c] = jnp.where(bits < 0, bits ^ jnp.int32(0x7FFFFFFF), bits)

    def score_body(c, _):
        score_chunk(c, None)
        return 0

    lax.fori_loop(0, n_full, score_body, 0)
    score_chunk(n_full, tail_valid_t)

    lanes_acc = 32

    def count(pred):
        def body(c, acc):
            key = key_scr[c].reshape(ck // lanes_acc, lanes_acc, bq)
            return acc + jnp.sum(jnp.where(pred(key), 1.0, 0.0), axis=0)
        acc = lax.fori_loop(0, n_chunks, body, jnp.zeros((lanes_acc, bq), f32))
        return jnp.sum(acc, axis=0, keepdims=True)

    def search_body(it, prefix):
        cand = prefix + lax.shift_left(jnp.int32(1), jnp.int32(31) - it)
        cand_r = jnp.broadcast_to(cand, (lanes_acc, bq))[None]
        cnt = count(lambda key: key >= cand_r)
        return jnp.where(cnt >= float(topk), cand, prefix)

    tau = lax.fori_loop(0, 32, search_body, jnp.full((1, bq), INT_MIN, i32))
    tau_r = jnp.broadcast_to(tau, (lanes_acc, bq))[None]
    n_take = float(topk) - count(lambda key: key > tau_r)

    def attn_chunk(c, carry, valid_t):
        off = pl.multiple_of(c * ck, ck)
        eq_run = carry[0]
        key = key_scr[c]
        eq = key == tau
        eqf = jnp.where(eq, 1.0, 0.0)
        pcs = []
        for g in range(ck // gw):
            e = eqf[g * gw:(g + 1) * gw, :]
            pcs.append(_dot(incl, e.astype(bf16)) + eq_run)
            eq_run = eq_run + jnp.sum(e, axis=0, keepdims=True)
        pc = jnp.concatenate(pcs, axis=0) if len(pcs) > 1 else pcs[0]
        addm_t = jnp.where(key > tau, 0.0, jnp.where(eq, jnp.where(pc <= n_take, 0.0, NEG_INF), NEG_INF))
        if valid_t is not None:
            addm_t = jnp.where(valid_t, addm_t, NEG_INF)
        addm = addm_t.T
        k = ak_ref[0, pl.ds(off, ck), :]
        vp = vp_scr[0, pl.ds(off, ck), :]
        pos = col + (off - i * bq).astype(f32)
        out = (eq_run,)
        for h in range(HEADS):
            m, acc = carry[1 + 2 * h:3 + 2 * h]
            s = _nt_dot(aq_ref[h], k) + slope_ref[h] * pos + addm
            m_new = jnp.maximum(m, jnp.max(s, axis=1, keepdims=True))
            m_safe = jnp.where(m_new == NEG_INF, 0.0, m_new)
            alpha = jnp.exp(m - m_safe)
            p = jnp.exp((s - m_safe).astype(bf16))
            out += (m_new, alpha * acc + _dot(p, vp))
        return out

    init = (jnp.zeros((1, bq), f32),) + _softmax_init_v1(bq) * HEADS
    carry = lax.fori_loop(0, n_full, lambda c, carry: attn_chunk(c, carry, None), init)
    carry = attn_chunk(n_full, carry, tail_valid_t)
    for h in range(HEADS):
        o_ref[h] = _softmax_finish_v1(carry[2 + 2 * h]).astype(o_ref.dtype)


def _dsa_attention(P, F, slopes, B, S, *, bq, ck, topk, gaq, giq, gak, gav, gik, iw_col):
    G, T, hw = P.shape
    nq = S // bq
    return pl.pallas_call(
        functools.partial(_dsa_kernel, bq=bq, ck=ck, topk=topk),
        grid=(B, nq),
        in_specs=[
            pl.BlockSpec((HEADS, bq, hw), lambda b, i: (gaq // HEADS, b * nq + i, 0)),
            pl.BlockSpec((IDX_HEADS, bq, hw), lambda b, i: (giq // IDX_HEADS, b * nq + i, 0)),
            pl.BlockSpec((1, S, hw), lambda b, i: (gak, b, 0)),
            pl.BlockSpec((1, S, hw), lambda b, i: (gav, b, 0)),
            pl.BlockSpec((1, S, hw), lambda b, i: (gik, b, 0)),
            pl.BlockSpec((bq, 128), lambda b, i: (b * nq + i, iw_col // 128)),
            pl.BlockSpec(slopes.shape, lambda b, i: (0, 0, 0)),
        ],
        out_specs=pl.BlockSpec((HEADS, bq, hw), lambda b, i: (0, b * nq + i, 0)),
        out_shape=jax.ShapeDtypeStruct((HEADS, T, hw), bf16),
        scratch_shapes=[pltpu.VMEM((S // ck, ck, bq), i32), pltpu.VMEM((1, S, 2 * HEAD_W), bf16)],
        compiler_params=_cparams(("arbitrary", "arbitrary")),
        name="dsa_attention",
    )(P, P, P, P, P, F, slopes)


def _merge_kernel(x_ref, g_ref, sc_ref, sh_ref, g1_ref, ya_ref, yb_ref, yc_ref, yd_ref,
                  wg_ref, wb_ref, wo_ref, o_ref):
    x = x_ref[...]
    D = x.shape[1]
    h = _norm_mod(x, g_ref[...], sc_ref[0], sh_ref[0]).astype(bf16)
    merged = jnp.zeros(x.shape, f32)
    for n, y_ref in enumerate((ya_ref, yb_ref, yc_ref, yd_ref)):
        gate = jax.nn.sigmoid(_dot(h, wg_ref[:, n * D:(n + 1) * D]))
        y = jnp.concatenate([y_ref[hh] for hh in range(HEADS)], axis=1)
        merged = merged + gate * _dot(y, wb_ref[n])
    o_ref[...] = x + g1_ref[0] * _dot(merged.astype(bf16), wo_ref[...])


def _merge(x2, g, sc, sh, g1, ys, wg, wb, wo, S, *, tm):
    T, D = x2.shape
    tpb = S // tm
    row = pl.BlockSpec((tm, D), lambda i: (i, 0))
    per_b = pl.BlockSpec((1, 1, D), lambda i: (i // tpb, 0, 0))
    yspec = pl.BlockSpec((HEADS, tm, HEAD_W), lambda i: (0, i, 0))
    full = lambda a: pl.BlockSpec(a.shape, lambda i: (0,) * a.ndim)
    return pl.pallas_call(
        _merge_kernel,
        grid=(T // tm,),
        in_specs=[row, full(g), per_b, per_b, per_b, yspec, yspec, yspec, yspec, full(wg), full(wb), full(wo)],
        out_specs=row,
        out_shape=jax.ShapeDtypeStruct((T, D), f32),
        compiler_params=_cparams(("arbitrary",)),
        name="merge",
    )(x2, g, sc, sh, g1, *ys, wg, wb, wo)


def _finish(x, g2, f, fg_ref, o_ref):
    out = x + g2 * f
    if fg_ref is not None:
        out = out * lax.rsqrt(jnp.mean(out * out, axis=-1, keepdims=True) + EPS) * fg_ref[...]
    o_ref[...] = out


def _ffn_kernel(x_ref, g_ref, sc_ref, sh_ref, g2_ref, w1_ref, w3_ref, w2_ref, *rest, final):
    fg_ref = rest[0] if final else None
    o_ref, h_scr, acc_scr = rest[-3:]
    j = pl.program_id(1)

    @pl.when(j == 0)
    def _():
        h_scr[...] = _norm_mod(x_ref[...], g_ref[...], sc_ref[0], sh_ref[0]).astype(bf16)
        acc_scr[...] = jnp.zeros_like(acc_scr)

    h = h_scr[...]
    a = _dot(h, w1_ref[...])
    u = (a * jax.nn.sigmoid(a)) * _dot(h, w3_ref[...])
    acc_scr[...] += _dot(u.astype(bf16), w2_ref[...])

    @pl.when(j == pl.num_programs(1) - 1)
    def _():
        _finish(x_ref[...], g2_ref[0], acc_scr[...], fg_ref, o_ref)


def _ffn(x2, g, sc, sh, g2, w1, w3, w2, final_g, S, *, tm, tn):
    T, D = x2.shape
    F = w1.shape[1]
    tpb = S // tm
    row = pl.BlockSpec((tm, D), lambda i, j: (i, 0))
    per_b = pl.BlockSpec((1, 1, D), lambda i, j: (i // tpb, 0, 0))
    vec = pl.BlockSpec((1, D), lambda i, j: (0, 0))
    in_specs = [row, vec, per_b, per_b, per_b,
                pl.BlockSpec((D, tn), lambda i, j: (0, j)),
                pl.BlockSpec((D, tn), lambda i, j: (0, j)),
                pl.BlockSpec((tn, D), lambda i, j: (j, 0))]
    args = [x2, g, sc, sh, g2, w1, w3, w2]
    if final_g is not None:
        in_specs.append(vec)
        args.append(final_g)
    return pl.pallas_call(
        functools.partial(_ffn_kernel, final=final_g is not None),
        grid=(T // tm, F // tn),
        in_specs=in_specs,
        out_specs=row,
        out_shape=jax.ShapeDtypeStruct((T, D), f32),
        scratch_shapes=[pltpu.VMEM((tm, D), bf16), pltpu.VMEM((tm, D), f32)],
        compiler_params=_cparams(("arbitrary", "arbitrary")),
        name="ffn",
    )(*args)


def _router_kernel(x_ref, g_ref, sc_ref, sh_ref, rw_ref, rb_ref, comb_ref, cnt_ref):
    h = _norm_mod(x_ref[...], g_ref[...], sc_ref[0], sh_ref[0]).astype(bf16)
    logits = _dot(h, rw_ref[...]) + rb_ref[...]
    lane = lax.broadcasted_iota(i32, logits.shape, 1)
    big = jnp.int32(logits.shape[1])
    m1 = jnp.max(logits, axis=1, keepdims=True)
    i1 = jnp.min(jnp.where(logits == m1, lane, big), axis=1, keepdims=True)
    rest_l = jnp.where(lane == i1, NEG_INF, logits)
    m2 = jnp.max(rest_l, axis=1, keepdims=True)
    i2 = jnp.min(jnp.where(rest_l == m2, lane, big), axis=1, keepdims=True)
    e2 = jnp.exp(m2 - m1)
    den = 1.0 + e2
    comb = jnp.where(lane == i1, 1.0 / den, jnp.where(lane == i2, e2 / den, 0.0))
    comb_ref[...] = comb
    cnt_ref[0] = jnp.sum(jnp.where(comb > 0.0, 1.0, 0.0), axis=0, keepdims=True)


def _router(x2, g, sc, sh, rw, rb, S, *, tm):
    T, D = x2.shape
    tpb = S // tm
    nt = T // tm
    return pl.pallas_call(
        _router_kernel,
        grid=(nt,),
        in_specs=[pl.BlockSpec((tm, D), lambda i: (i, 0)),
                  pl.BlockSpec((1, D), lambda i: (0, 0)),
                  pl.BlockSpec((1, 1, D), lambda i: (i // tpb, 0, 0)),
                  pl.BlockSpec((1, 1, D), lambda i: (i // tpb, 0, 0)),
                  pl.BlockSpec(rw.shape, lambda i: (0, 0)),
                  pl.BlockSpec(rb.shape, lambda i: (0, 0))],
        out_specs=[pl.BlockSpec((tm, 128), lambda i: (i, 0)),
                   pl.BlockSpec((1, 1, 128), lambda i: (i, 0, 0))],
        out_shape=[jax.ShapeDtypeStruct((T, 128), f32), jax.ShapeDtypeStruct((nt, 1, 128), f32)],
        compiler_params=_cparams(("arbitrary",)),
        name="router",
    )(x2, g, sc, sh, rw, rb)


def _moe_kernel(nch_ref, x_ref, g_ref, sc_ref, sh_ref, g2_ref, comb_ref, w1_ref, w3_ref, w2_ref, *rest,
                final, rows):
    fg_ref = rest[0] if final else None
    o_ref, h_scr, tri_scr, rank_scr, rank_t_scr, xg_scr, yg_scr, out_scr = rest[-8:]
    i = pl.program_id(0)
    e = pl.program_id(1)
    j = pl.program_id(2)
    n_e = pl.num_programs(1)
    tm = x_ref.shape[0]
    nch = nch_ref[i * n_e + e]

    @pl.when(jnp.logical_and(i == 0, jnp.logical_and(e == 0, j == 0)))
    def _():
        rr = lax.broadcasted_iota(i32, (tm, tm), 0)
        cc = lax.broadcasted_iota(i32, (tm, tm), 1)
        tri_scr[...] = jnp.where(cc < rr, 1.0, 0.0).astype(bf16)

    @pl.when(jnp.logical_and(e == 0, j == 0))
    def _():
        h_scr[...] = _norm_mod(x_ref[...], g_ref[...], sc_ref[0], sh_ref[0]).astype(bf16)
        out_scr[...] = jnp.zeros_like(out_scr)
        member = jnp.where(comb_ref[...] > 0.0, 1.0, 0.0)
        rank = jnp.where(member > 0.0, _dot(tri_scr[...], member.astype(bf16)), -1.0)
        rank_scr[...] = rank
        rank_t_scr[...] = rank.T

    def chunk_rows(c):
        return pl.ds(pl.multiple_of(c * rows, 16), rows)

    @pl.when(j == 0)
    def _():
        rank_row = rank_t_scr[pl.ds(e, 1), :]
        slot = lax.broadcasted_iota(i32, (rows, tm), 0).astype(f32)

        def gather(c, _):
            pick = jnp.where(slot + (c * rows).astype(f32) == rank_row, 1.0, 0.0).astype(bf16)
            xg_scr[chunk_rows(c), :] = _dot(pick, h_scr[...]).astype(bf16)
            yg_scr[chunk_rows(c), :] = jnp.zeros((rows, yg_scr.shape[1]), f32)
            return 0

        lax.fori_loop(0, nch, gather, 0)

    def expert(c, _):
        xg = xg_scr[chunk_rows(c), :]
        a = _dot(xg, w1_ref[0])
        u = (a * jax.nn.sigmoid(a)) * _dot(xg, w3_ref[0])
        yg_scr[chunk_rows(c), :] += _dot(u.astype(bf16), w2_ref[0])
        return 0

    lax.fori_loop(0, nch, expert, 0)

    last_j = j == pl.num_programs(2) - 1

    @pl.when(last_j)
    def _():
        lane = lax.broadcasted_iota(i32, (tm, 128), 1)
        sel = lane == e
        rank_col = jnp.sum(jnp.where(sel, rank_scr[...], 0.0), axis=1, keepdims=True)
        w_col = jnp.sum(jnp.where(sel, comb_ref[...], 0.0), axis=1, keepdims=True)
        slot = lax.broadcasted_iota(i32, (tm, rows), 1).astype(f32)

        def scatter(c, _):
            place = jnp.where(slot + (c * rows).astype(f32) == rank_col, 1.0, 0.0).astype(bf16)
            out_scr[...] += w_col * _dot(place, yg_scr[chunk_rows(c), :].astype(bf16))
            return 0

        lax.fori_loop(0, nch, scatter, 0)

    @pl.when(jnp.logical_and(last_j, e == n_e - 1))
    def _():
        _finish(x_ref[...], g2_ref[0], out_scr[...], fg_ref, o_ref)


def _moe(x2, g, sc, sh, g2, comb, nch, w1, w3, w2, final_g, S, *, tm, tn, rows):
    T, D = x2.shape
    E, _, F = w1.shape
    nj = F // tn
    tpb = S // tm
    cap = -(-tm // rows) * rows
    row = pl.BlockSpec((tm, D), lambda i, e, j, n: (i, 0))
    per_b = pl.BlockSpec((1, 1, D), lambda i, e, j, n: (i // tpb, 0, 0))
    vec = pl.BlockSpec((1, D), lambda i, e, j, n: (0, 0))
    in_specs = [row, vec, per_b, per_b, per_b,
                pl.BlockSpec((tm, 128), lambda i, e, j, n: (i, 0)),
                pl.BlockSpec((1, D, tn), lambda i, e, j, n: (e, 0, j)),
                pl.BlockSpec((1, D, tn), lambda i, e, j, n: (e, 0, j)),
                pl.BlockSpec((1, tn, D), lambda i, e, j, n: (e, j, 0))]
    args = [x2, g, sc, sh, g2, comb, w1, w3, w2]
    if final_g is not None:
        in_specs.append(vec)
        args.append(final_g)
    return pl.pallas_call(
        functools.partial(_moe_kernel, final=final_g is not None, rows=rows),
        grid_spec=pltpu.PrefetchScalarGridSpec(
            num_scalar_prefetch=1,
            grid=(T // tm, E, nj),
            in_specs=in_specs,
            out_specs=row,
            scratch_shapes=[pltpu.VMEM((tm, D), bf16),
                            pltpu.VMEM((tm, tm), bf16),
                            pltpu.VMEM((tm, 128), f32),
                            pltpu.VMEM((128, tm), f32),
                            pltpu.VMEM((cap, D), bf16),
                            pltpu.VMEM((cap, D), f32),
                            pltpu.VMEM((tm, D), f32)]),
        out_shape=jax.ShapeDtypeStruct((T, D), f32),
        compiler_params=_cparams(("arbitrary", "arbitrary", "arbitrary")),
        name="moe",
    )(nch, *args)


_IN_SIZES = (256, 64, 64, 256, 64, 4, 256, 128, 32, 256, 256, 256, 256, 256, 256, 4096)
_IN_NAMES = ("aq", "ak", "av", "iq", "ik", "iw", "cq", "ckv", "kr", "sq", "sk", "sv", "dq", "dk", "dv", "gates")

G_AQ, G_IQ, G_SQ, G_SK, G_SV, G_DQ, G_DK, G_DV, G_AK, G_AV, G_IK = 0, 4, 8, 12, 16, 20, 24, 28, 32, 33, 34
N_GROUPS = 36
F_IW_COL = 640


def _split_w_in(w):
    out, off = {}, 0
    for name, n in zip(_IN_NAMES, _IN_SIZES):
        out[name] = w[:, off:off + n]
        off += n
    return out


def _rot_half_cols(w):
    half = w.shape[1] // 2
    return jnp.concatenate([-w[:, half:], w[:, :half]], axis=1)


def _layer_weights(w_in, w_uq, w_ukv):
    c = _split_w_in(w_in)
    D = w_in.shape[0]
    z = lambda n: jnp.zeros((D, n), f32)
    w_main = jnp.concatenate([
        c["aq"] * 0.125, c["iq"] * 0.125, c["sq"] * 0.125, c["sk"], c["sv"],
        c["dq"] * (DIFF_QK ** -0.5), c["dk"], c["dv"], c["ak"], c["av"], c["ik"], z(64)], axis=1).astype(bf16)
    w_f = jnp.concatenate([
        c["cq"], c["ckv"],
        z(64), c["kr"], z(32),
        z(64), _rot_half_cols(c["kr"]), z(32),
        c["iw"] * (IDX_HEADS ** -0.5), z(124)], axis=1).astype(bf16)
    w_gates = c["gates"].astype(bf16)

    qr = w_uq.shape[0]
    kvr = w_ukv.shape[0]
    wq, wqp, wk, wv = [], [], [], []
    per_q = MLA_NOPE + MLA_ROPE
    per_kv = MLA_NOPE + MLA_V
    for h in range(HEADS):
        nope = w_uq[:, h * per_q:h * per_q + MLA_NOPE]
        rope = w_uq[:, h * per_q + MLA_NOPE:(h + 1) * per_q]
        wq += [nope, rope, jnp.zeros((qr, 32), f32)]
        wqp += [jnp.zeros((qr, 64), f32), _rot_half_cols(rope), jnp.zeros((qr, 32), f32)]
        wk += [w_ukv[:, h * per_kv:h * per_kv + MLA_NOPE], jnp.zeros((kvr, 64), f32)]
        wv += [w_ukv[:, h * per_kv + MLA_NOPE:(h + 1) * per_kv]]
    cat = lambda xs: jnp.concatenate(xs, axis=1).astype(bf16)
    return w_main, w_f, w_gates, cat(wq), cat(wqp), cat(wk), cat(wv)


def _alibi_slopes():
    n = 2 * HEADS
    sl = [2.0 ** (-(8.0 / n) * (k + 1)) for k in range(n)]
    return sl[0::2], sl[1::2]


def kernel(x, c, positions, ada_w, ada_b, norm1_g, norm2_g, w_in, mla_q_norm_g, mla_kv_norm_g, mla_w_uq,
           mla_w_ukv, diff_lq1, diff_lk1, diff_lq2, diff_lk2, diff_norm_g, w_branch, w_out, ffn_w1, ffn_w3,
           ffn_w2, router_w, router_b, moe_w1, moe_w3, moe_w2, final_norm_g):
    B, S, D = x.shape
    T = B * S
    depth = ada_w.shape[0]
    topk = min(TOPK_MAX, S // 4)
    tm = min(1024, S)
    tm_merge = min(512, S)
    bq = min(512, S)
    ck = min(512, S)

    x2 = x.reshape(T, D)
    pos2 = positions.reshape(T, 1)
    mod = _modulation(c, ada_w, ada_b)

    half = MLA_ROPE // 2
    inv = ROPE_THETA ** (-jnp.arange(half, dtype=f32) / half)
    inv_full = jnp.concatenate([jnp.zeros((64,), f32), inv, inv, jnp.zeros((32,), f32)]).reshape(1, 128)
    sl_a, sl_d = _alibi_slopes()
    slopes_a = jnp.broadcast_to(jnp.asarray(sl_a, f32)[:, None, None], (HEADS, 1, ck))
    slopes_d = jnp.broadcast_to(jnp.asarray(sl_d, f32)[:, None, None], (HEADS, 1, ck))

    row = lambda v: v.reshape(1, -1)
    for l in range(depth):
        sh1, sc1, g1, sh2, sc2, g2 = [mod[l, :, k * D:(k + 1) * D].reshape(B, 1, D) for k in range(6)]
        lam_init = 0.8 - 0.6 * math.exp(-0.3 * l)
        w_main, w_f, w_gates, wq, wqp, wk, wv = _layer_weights(w_in[l], mla_w_uq[l], mla_w_ukv[l])
        n1 = row(norm1_g[l])

        P = _inproj(x2, n1, sc1, sh1, w_main, S, grouped=True, out_dtype=bf16, tm=tm, tn=256)
        F = _inproj(x2, n1, sc1, sh1, w_f, S, grouped=False, out_dtype=f32, tm=tm, tn=w_f.shape[1])
        qm, km, vm = _mla_prep(F, pos2, row(mla_q_norm_g[l]), row(mla_kv_norm_g[l]), wq, wqp, wk, wv,
                               inv_full, tm=tm)

        ya = _dsa_attention(P, F, slopes_a, B, S, bq=bq, ck=ck, topk=topk, gaq=G_AQ, giq=G_IQ, gak=G_AK,
                            gav=G_AV, gik=G_IK, iw_col=F_IW_COL)
        yb = _mla_attention(qm, km, vm, B, S, bq=bq, ck=ck)
        yc = _sb_attention(P, B, S, bq=bq, ck=ck, gq=G_SQ, gk=G_SK, gv=G_SV)
        yd = _diff_attention(P, slopes_d, row(diff_lq1[l]), row(diff_lk1[l]), row(diff_lq2[l]),
                             row(diff_lk2[l]), row(diff_norm_g[l]), B, S, bq=bq, ck=ck, lam_init=lam_init,
                             gq=G_DQ, gk=G_DK, gv=G_DV)

        x2 = _merge(x2, n1, sc1, sh1, g1, (ya, yb, yc, yd), w_gates, w_branch[l].astype(bf16),
                    w_out[l].astype(bf16), S, tm=tm_merge)

        final_g = row(final_norm_g) if l == depth - 1 else None
        n2 = row(norm2_g[l])
        if l % 2 == 0:
            k = l // 2
            x2 = _ffn(x2, n2, sc2, sh2, g2, ffn_w1[k].astype(bf16), ffn_w3[k].astype(bf16),
                      ffn_w2[k].astype(bf16), final_g, S, tm=tm, tn=256)
        else:
            k = l // 2
            E = router_w.shape[2]
            rw = jnp.concatenate([router_w[k], jnp.zeros((D, 128 - E), f32)], axis=1).astype(bf16)
            rb = jnp.concatenate([router_b[k], jnp.full((128 - E,), NEG_INF, f32)]).reshape(1, 128)
            comb, cnt = _router(x2, n2, sc2, sh2, rw, rb, S, tm=tm)
            nch = ((cnt[:, 0, :E].astype(i32) + (MOE_ROWS - 1)) // MOE_ROWS).reshape(-1)
            x2 = _moe(x2, n2, sc2, sh2, g2, comb, nch, moe_w1[k].astype(bf16), moe_w3[k].astype(bf16),
                      moe_w2[k].astype(bf16), final_g, S, tm=tm, tn=MOE_TN, rows=MOE_ROWS)
    return x2.reshape(B, S, D)
```

```python
import functools
import math

import jax
import jax.numpy as jnp
from jax import lax
from jax.experimental import pallas as pl
from jax.experimental.pallas import tpu as pltpu

f32 = jnp.float32
bf16 = jnp.bfloat16
i32 = jnp.int32

N_BRANCHES = 4
HEADS = 4
HEAD_W = 64
IDX_HEADS = 4
TOPK_MAX = 256
MLA_Q_RANK = 256
MLA_KV_RANK = 128
MLA_NOPE = 64
MLA_ROPE = 32
MLA_V = 64
MLA_QK_PAD = 128
ROPE_THETA = 10000.0
DIFF_QK = 32
N_EXPERTS = 8
TOP_K = 2
FFN_TN = 1408
FFN_TM = 512
MOE_TN = 896
MOE_SIZES = (128, 192, 256, 320, 384, 448, 512)
EPS = 1e-6
NEG_INF = float("-inf")
INT_MIN = -2 ** 31

VMEM_LIMIT = 56 * 1024 * 1024


def _cparams(sem):
    return pltpu.CompilerParams(dimension_semantics=sem, vmem_limit_bytes=VMEM_LIMIT)


def _nt_dot(a, b):
    return lax.dot_general(a, b, (((1,), (1,)), ((), ())), preferred_element_type=f32)


def _dot(a, b):
    return jnp.dot(a, b, preferred_element_type=f32)


def _norm_mod(x, g, sc, sh):
    y = x * lax.rsqrt(jnp.mean(x * x, axis=-1, keepdims=True) + EPS)
    return (y * g) * (1.0 + sc) + sh


def _mod_kernel(c_ref, w_ref, b_ref, o_ref):
    c = c_ref[...]
    cond = (c * jax.nn.sigmoid(c)).astype(bf16)
    o_ref[0] = _dot(cond, w_ref[0].astype(bf16)) + b_ref[0]


def _modulation(c, ada_w, ada_b):
    L, D, N = ada_w.shape
    B = c.shape[0]
    tn = 1536
    return pl.pallas_call(
        _mod_kernel,
        grid=(L, N // tn),
        in_specs=[
            pl.BlockSpec((B, D), lambda l, j: (0, 0)),
            pl.BlockSpec((1, D, tn), lambda l, j: (l, 0, j)),
            pl.BlockSpec((1, 1, tn), lambda l, j: (l, 0, j)),
        ],
        out_specs=pl.BlockSpec((1, B, tn), lambda l, j: (l, 0, j)),
        out_shape=jax.ShapeDtypeStruct((L, B, N), f32),
        compiler_params=_cparams(("arbitrary", "arbitrary")),
        name="modulation",
    )(c, ada_w, ada_b.reshape(L, 1, N))


def _inproj_kernel(x_ref, g_ref, sc_ref, sh_ref, w_ref, o_ref, h_scr, *, grouped):
    @pl.when(pl.program_id(1) == 0)
    def _():
        h_scr[...] = _norm_mod(x_ref[...], g_ref[...], sc_ref[0], sh_ref[0]).astype(bf16)

    res = _dot(h_scr[...], w_ref[...])
    if grouped:
        for k in range(res.shape[1] // HEAD_W):
            o_ref[k] = res[:, k * HEAD_W:(k + 1) * HEAD_W].astype(o_ref.dtype)
    else:
        o_ref[...] = res.astype(o_ref.dtype)


def _inproj(x2, g, sc, sh, w, S, *, grouped, out_dtype, tm, tn):
    T, D = x2.shape
    N = w.shape[1]
    tpb = S // tm
    if grouped:
        out_shape = jax.ShapeDtypeStruct((N // HEAD_W, T, HEAD_W), out_dtype)
        out_spec = pl.BlockSpec((tn // HEAD_W, tm, HEAD_W), lambda i, j: (j, i, 0))
    else:
        out_shape = jax.ShapeDtypeStruct((T, N), out_dtype)
        out_spec = pl.BlockSpec((tm, tn), lambda i, j: (i, j))
    return pl.pallas_call(
        functools.partial(_inproj_kernel, grouped=grouped),
        grid=(T // tm, N // tn),
        in_specs=[
            pl.BlockSpec((tm, D), lambda i, j: (i, 0)),
            pl.BlockSpec((1, D), lambda i, j: (0, 0)),
            pl.BlockSpec((1, 1, D), lambda i, j: (i // tpb, 0, 0)),
            pl.BlockSpec((1, 1, D), lambda i, j: (i // tpb, 0, 0)),
            pl.BlockSpec((D, tn), lambda i, j: (0, j)),
        ],
        out_specs=out_spec,
        out_shape=out_shape,
        scratch_shapes=[pltpu.VMEM((tm, D), bf16)],
        compiler_params=_cparams(("arbitrary", "arbitrary")),
        name="inproj_grouped" if grouped else "inproj_plain",
    )(x2, g, sc, sh, w)


def _mla_prep_kernel(f_ref, pos_ref, gq_ref, gkv_ref, wq_ref, wqp_ref, wk_ref, wv_ref, inv_ref,
                     q_ref, k_ref, v_ref, *, scale):
    cq = f_ref[:, 0:MLA_Q_RANK]
    ckv = f_ref[:, MLA_Q_RANK:MLA_Q_RANK + MLA_KV_RANK]
    kr = f_ref[:, 384:512]
    krp = f_ref[:, 512:640]
    nq = (cq * lax.rsqrt(jnp.mean(cq * cq, axis=-1, keepdims=True) + EPS) * gq_ref[...]).astype(bf16)
    nkv = (ckv * lax.rsqrt(jnp.mean(ckv * ckv, axis=-1, keepdims=True) + EPS) * gkv_ref[...]).astype(bf16)
    ang = pos_ref[...].astype(f32) * inv_ref[...]
    cosf = jnp.cos(ang)
    sinf = jnp.sin(ang)
    q1 = _dot(nq, wq_ref[...])
    q2 = _dot(nq, wqp_ref[...])
    kn = _dot(nkv, wk_ref[...])
    vv = _dot(nkv, wv_ref[...])
    krope = kr * cosf + krp * sinf
    for h in range(HEADS):
        sl = slice(h * MLA_QK_PAD, (h + 1) * MLA_QK_PAD)
        q_ref[h] = ((q1[:, sl] * cosf + q2[:, sl] * sinf) * scale).astype(bf16)
        k_ref[h] = (kn[:, sl] + krope).astype(bf16)
        v_ref[h] = vv[:, h * MLA_V:(h + 1) * MLA_V].astype(bf16)


def _mla_prep(F, pos2, gq, gkv, wq, wqp, wk, wv, inv_full, *, tm):
    T = F.shape[0]
    scale = (MLA_NOPE + MLA_ROPE) ** -0.5
    full = lambda a: pl.BlockSpec(a.shape, lambda i: (0,) * a.ndim)
    return pl.pallas_call(
        functools.partial(_mla_prep_kernel, scale=scale),
        grid=(T // tm,),
        in_specs=[
            pl.BlockSpec((tm, F.shape[1]), lambda i: (i, 0)),
            pl.BlockSpec((tm, 1), lambda i: (i, 0)),
            full(gq), full(gkv), full(wq), full(wqp), full(wk), full(wv), full(inv_full),
        ],
        out_specs=[
            pl.BlockSpec((HEADS, tm, MLA_QK_PAD), lambda i: (0, i, 0)),
            pl.BlockSpec((HEADS, tm, MLA_QK_PAD), lambda i: (0, i, 0)),
            pl.BlockSpec((HEADS, tm, MLA_V), lambda i: (0, i, 0)),
        ],
        out_shape=[
            jax.ShapeDtypeStruct((HEADS, T, MLA_QK_PAD), bf16),
            jax.ShapeDtypeStruct((HEADS, T, MLA_QK_PAD), bf16),
            jax.ShapeDtypeStruct((HEADS, T, MLA_V), bf16),
        ],
        compiler_params=_cparams(("arbitrary",)),
        name="mla_prep",
    )(F, pos2, gq, gkv, wq, wqp, wk, wv, inv_full)


def _chunk_plan(i, bq, ck):
    n_full = (i * bq) // ck
    return n_full, n_full * ck


def _tail_valid(i, bq, ck, tail_off):
    row = lax.broadcasted_iota(i32, (bq, ck), 0)
    col = lax.broadcasted_iota(i32, (bq, ck), 1)
    return col + (tail_off - i * bq) <= row


def _fill_v_ones(v_ref, vp_scr):
    for h in range(v_ref.shape[0]):
        vp_scr[h, :, 0:HEAD_W] = v_ref[h]
        vp_scr[h, :, HEAD_W:2 * HEAD_W] = jnp.ones((v_ref.shape[1], HEAD_W), bf16)


def _softmax_step_v1(s, vp, m, acc):
    m_new = jnp.maximum(m, jnp.max(s, axis=1, keepdims=True))
    alpha = jnp.exp(m - m_new)
    p = jnp.exp((s - m_new).astype(bf16))
    return m_new, alpha * acc + _dot(p, vp)


def _softmax_init_v1(bq):
    return (jnp.full((bq, 1), NEG_INF, f32), jnp.zeros((bq, 2 * HEAD_W), f32))


def _softmax_finish_v1(acc):
    return acc[:, 0:HEAD_W] / acc[:, HEAD_W:HEAD_W + 1]


def _mla_attn_kernel(q_ref, k_ref, v_ref, o_ref, vp_scr, *, bq, ck):
    i = pl.program_id(1)
    n_full, tail_off = _chunk_plan(i, bq, ck)

    @pl.when(i == 0)
    def _():
        _fill_v_ones(v_ref, vp_scr)

    def chunk(off, carry, valid):
        off = pl.multiple_of(off, ck)
        out = ()
        for h in range(HEADS):
            s = _nt_dot(q_ref[h], k_ref[h, pl.ds(off, ck), :])
            if valid is not None:
                s = jnp.where(valid, s, NEG_INF)
            out += _softmax_step_v1(s, vp_scr[h, pl.ds(off, ck), :], *carry[2 * h:2 * h + 2])
        return out

    carry = lax.fori_loop(0, n_full, lambda c, carry: chunk(c * ck, carry, None), _softmax_init_v1(bq) * HEADS)
    carry = chunk(tail_off, carry, _tail_valid(i, bq, ck, tail_off))
    for h in range(HEADS):
        o_ref[h] = _softmax_finish_v1(carry[2 * h + 1]).astype(o_ref.dtype)


def _mla_attention(qm, km, vm, B, S, *, bq, ck):
    H, T, dk = qm.shape
    dv = vm.shape[2]
    nq = S // bq
    return pl.pallas_call(
        functools.partial(_mla_attn_kernel, bq=bq, ck=ck),
        grid=(B, nq),
        in_specs=[
            pl.BlockSpec((H, bq, dk), lambda b, i: (0, b * nq + i, 0)),
            pl.BlockSpec((H, S, dk), lambda b, i: (0, b, 0)),
            pl.BlockSpec((H, S, dv), lambda b, i: (0, b, 0)),
        ],
        out_specs=pl.BlockSpec((H, bq, dv), lambda b, i: (0, b * nq + i, 0)),
        out_shape=jax.ShapeDtypeStruct((H, T, dv), bf16),
        scratch_shapes=[pltpu.VMEM((H, S, 2 * HEAD_W), bf16)],
        compiler_params=_cparams(("arbitrary", "arbitrary")),
        name="mla_attention",
    )(qm, km, vm)


def _diff_attn_kernel(q_ref, k_ref, v_ref, slope_ref, lq1_ref, lk1_ref, lq2_ref, lk2_ref, g_ref, o_ref,
                      vp_scr, *, bq, ck, lam_init):
    i = pl.program_id(1)
    n_full, tail_off = _chunk_plan(i, bq, ck)
    lane = lax.broadcasted_iota(i32, (bq, q_ref.shape[2]), 1)
    col = lax.broadcasted_iota(i32, (1, ck), 1).astype(f32)

    @pl.when(i == 0)
    def _():
        _fill_v_ones(v_ref, vp_scr)

    def chunk(off, carry, valid):
        off = pl.multiple_of(off, ck)
        pos = col + (off - i * bq).astype(f32)
        out = ()
        for h in range(HEADS):
            q = q_ref[h]
            zero = jnp.zeros_like(q)
            k = k_ref[h, pl.ds(off, ck), :]
            vp = vp_scr[h, pl.ds(off, ck), :]
            bias = slope_ref[h] * pos
            for mp, qm in enumerate((jnp.where(lane < DIFF_QK, q, zero), jnp.where(lane < DIFF_QK, zero, q))):
                s = _nt_dot(qm, k) + bias
                if valid is not None:
                    s = jnp.where(valid, s, NEG_INF)
                base = 2 * (2 * h + mp)
                out += _softmax_step_v1(s, vp, *carry[base:base + 2])
        return out

    carry = lax.fori_loop(0, n_full, lambda c, carry: chunk(c * ck, carry, None),
                          _softmax_init_v1(bq) * (2 * HEADS))
    carry = chunk(tail_off, carry, _tail_valid(i, bq, ck, tail_off))
    lam = (jnp.exp(jnp.sum(lq1_ref[...] * lk1_ref[...], axis=1, keepdims=True))
           - jnp.exp(jnp.sum(lq2_ref[...] * lk2_ref[...], axis=1, keepdims=True)) + lam_init)
    for h in range(HEADS):
        m0, a0, m1, a1 = carry[4 * h:4 * h + 4]
        o = _softmax_finish_v1(a0) - lam * _softmax_finish_v1(a1)
        y = o * lax.rsqrt(jnp.mean(o * o, axis=-1, keepdims=True) + EPS) * g_ref[...]
        o_ref[h] = (y * (1.0 - lam_init)).astype(o_ref.dtype)


def _diff_attention(P, slopes, lq1, lk1, lq2, lk2, g, B, S, *, bq, ck, lam_init, gq, gk, gv):
    G, T, hw = P.shape
    nq = S // bq
    vec = lambda a: pl.BlockSpec(a.shape, lambda b, i: (0,) * a.ndim)
    return pl.pallas_call(
        functools.partial(_diff_attn_kernel, bq=bq, ck=ck, lam_init=lam_init),
        grid=(B, nq),
        in_specs=[
            pl.BlockSpec((HEADS, bq, hw), lambda b, i: (gq // HEADS, b * nq + i, 0)),
            pl.BlockSpec((HEADS, S, hw), lambda b, i: (gk // HEADS, b, 0)),
            pl.BlockSpec((HEADS, S, hw), lambda b, i: (gv // HEADS, b, 0)),
            vec(slopes), vec(lq1), vec(lk1), vec(lq2), vec(lk2), vec(g),
        ],
        out_specs=pl.BlockSpec((HEADS, bq, hw), lambda b, i: (0, b * nq + i, 0)),
        out_shape=jax.ShapeDtypeStruct((HEADS, T, hw), bf16),
        scratch_shapes=[pltpu.VMEM((HEADS, S, 2 * HEAD_W), bf16)],
        compiler_params=_cparams(("arbitrary", "arbitrary")),
        name="diff_attention",
    )(P, P, P, slopes, lq1, lk1, lq2, lk2, g)


SB_GROUP = 256


def _sb_attn_kernel(q_ref, k_ref, v_ref, o_ref, *, bq, ck):
    i = pl.program_id(1)
    n_full, tail_off = _chunk_plan(i, bq, ck)
    dv = v_ref.shape[2]
    gw = min(SB_GROUP, ck)
    rr = lax.broadcasted_iota(i32, (gw, gw), 0)
    cc = lax.broadcasted_iota(i32, (gw, gw), 1)
    upper = jnp.where(rr > cc, 1.0, 0.0).astype(bf16)

    def chunk(off, carry, strict):
        off = pl.multiple_of(off, ck)
        out = ()
        for h in range(HEADS):
            run, acc = carry[2 * h:2 * h + 2]
            z = _nt_dot(q_ref[h], k_ref[h, pl.ds(off, ck), :])
            lsm = -(jnp.maximum(z, 0.0) + jnp.log(1.0 + jnp.exp(-jnp.abs(z))))
            if strict is not None:
                lsm = jnp.where(strict, lsm, 0.0)
            parts = []
            for g in reversed(range(ck // gw)):
                x = lsm[:, g * gw:(g + 1) * gw]
                parts.append(_dot(x.astype(bf16), upper) + run)
                run = run + jnp.sum(x, axis=1, keepdims=True)
            between = jnp.concatenate(parts[::-1], axis=1) if len(parts) > 1 else parts[0]
            arg = z + lsm + between
            if strict is not None:
                arg = jnp.where(strict, arg, NEG_INF)
            acc = acc + _dot(jnp.exp(arg.astype(bf16)), v_ref[h, pl.ds(off, ck), :])
            out += (run, acc)
        return out

    row = lax.broadcasted_iota(i32, (bq, ck), 0)
    col = lax.broadcasted_iota(i32, (bq, ck), 1)
    strict = col + (tail_off - i * bq) < row
    init = (jnp.zeros((bq, 1), f32), jnp.zeros((bq, dv), f32)) * HEADS
    carry = chunk(tail_off, init, strict)
    carry = lax.fori_loop(0, n_full, lambda n, carry: chunk((n_full - 1 - n) * ck, carry, None), carry)
    for h in range(HEADS):
        o_ref[h] = carry[2 * h + 1].astype(o_ref.dtype)


def _sb_attention(P, B, S, *, bq, ck, gq, gk, gv):
    G, T, hw = P.shape
    nq = S // bq
    return pl.pallas_call(
        functools.partial(_sb_attn_kernel, bq=bq, ck=ck),
        grid=(B, nq),
        in_specs=[
            pl.BlockSpec((HEADS, bq, hw), lambda b, i: (gq // HEADS, b * nq + i, 0)),
            pl.BlockSpec((HEADS, S, hw), lambda b, i: (gk // HEADS, b, 0)),
            pl.BlockSpec((HEADS, S, hw), lambda b, i: (gv // HEADS, b, 0)),
        ],
        out_specs=pl.BlockSpec((HEADS, bq, hw), lambda b, i: (0, b * nq + i, 0)),
        out_shape=jax.ShapeDtypeStruct((HEADS, T, hw), bf16),
        compiler_params=_cparams(("arbitrary", "arbitrary")),
        name="sb_attention",
    )(P, P, P)


def _dsa_kernel(aq_ref, iq_ref, ak_ref, av_ref, ik_ref, iw_ref, slope_ref, o_ref, key_scr, vp_scr,
                *, bq, ck, topk):
    i = pl.program_id(1)
    n_full, tail_off = _chunk_plan(i, bq, ck)
    n_chunks = n_full + 1

    @pl.when(i == 0)
    def _():
        _fill_v_ones(av_ref, vp_scr)

    w_t = iw_ref[...].T
    gw = min(SB_GROUP, ck)
    rr = lax.broadcasted_iota(i32, (gw, gw), 0)
    cc = lax.broadcasted_iota(i32, (gw, gw), 1)
    incl = jnp.where(cc <= rr, 1.0, 0.0).astype(bf16)
    col = lax.broadcasted_iota(i32, (1, ck), 1).astype(f32)
    key_i = lax.broadcasted_iota(i32, (ck, bq), 0)
    qry_i = lax.broadcasted_iota(i32, (ck, bq), 1)
    tail_valid_t = key_i + (tail_off - i * bq) <= qry_i

    def score_chunk(c, valid_t):
        off = pl.multiple_of(c * ck, ck)
        ik = ik_ref[0, pl.ds(off, ck), :]
        sc = jnp.zeros((ck, bq), f32)
        for h in range(IDX_HEADS):
            sc = sc + jnp.maximum(_nt_dot(ik, iq_ref[h]), 0.0) * w_t[h:h + 1, :]
        if valid_t is not None:
            sc = jnp.where(valid_t, sc, NEG_INF)
        bits = lax.bitcast_convert_type(sc + 0.0, i32)
        key_scr[c] = jnp.where(bits < 0, bits ^ jnp.int32(0x7FFFFFFF), bits)

    def score_body(c, _):
        score_chunk(c, None)
        return 0

    lax.fori_loop(0, n_full, score_body, 0)
    score_chunk(n_full, tail_valid_t)

    lanes_acc = 32

    def count(pred):
        def body(c, acc):
            key = key_scr[c].reshape(ck // lanes_acc, lanes_acc, bq)
            return acc + jnp.sum(jnp.where(pred(key), 1.0, 0.0), axis=0)
        acc = lax.fori_loop(0, n_chunks, body, jnp.zeros((lanes_acc, bq), f32))
        return jnp.sum(acc, axis=0, keepdims=True)

    def search_body(it, prefix):
        cand = prefix + lax.shift_left(jnp.int32(1), jnp.int32(31) - it)
        cand_r = jnp.broadcast_to(cand, (lanes_acc, bq))[None]
        cnt = count(lambda key: key >= cand_r)
        return jnp.where(cnt >= float(topk), cand, prefix)

    tau = lax.fori_loop(0, 32, search_body, jnp.full((1, bq), INT_MIN, i32))
    tau_r = jnp.broadcast_to(tau, (lanes_acc, bq))[None]
    n_take = float(topk) - count(lambda key: key > tau_r)

    def attn_chunk(c, carry, valid_t):
        off = pl.multiple_of(c * ck, ck)
        eq_run = carry[0]
        key = key_scr[c]
        eq = key == tau
        eqf = jnp.where(eq, 1.0, 0.0)
        pcs = []
        for g in range(ck // gw):
            e = eqf[g * gw:(g + 1) * gw, :]
            pcs.append(_dot(incl, e.astype(bf16)) + eq_run)
            eq_run = eq_run + jnp.sum(e, axis=0, keepdims=True)
        pc = jnp.concatenate(pcs, axis=0) if len(pcs) > 1 else pcs[0]
        addm_t = jnp.where(key > tau, 0.0, jnp.where(eq, jnp.where(pc <= n_take, 0.0, NEG_INF), NEG_INF))
        if valid_t is not None:
            addm_t = jnp.where(valid_t, addm_t, NEG_INF)
        addm = addm_t.T
        k = ak_ref[0, pl.ds(off, ck), :]
        vp = vp_scr[0, pl.ds(off, ck), :]
        pos = col + (off - i * bq).astype(f32)
        out = (eq_run,)
        for h in range(HEADS):
            m, acc = carry[1 + 2 * h:3 + 2 * h]
            s = _nt_dot(aq_ref[h], k) + slope_ref[h] * pos + addm
            m_new = jnp.maximum(m, jnp.max(s, axis=1, keepdims=True))
            m_safe = jnp.where(m_new == NEG_INF, 0.0, m_new)
            alpha = jnp.exp(m - m_safe)
            p = jnp.exp((s - m_safe).astype(bf16))
            out += (m_new, alpha * acc + _dot(p, vp))
        return out

    init = (jnp.zeros((1, bq), f32),) + _softmax_init_v1(bq) * HEADS
    carry = lax.fori_loop(0, n_full, lambda c, carry: attn_chunk(c, carry, None), init)
    carry = attn_chunk(n_full, carry, tail_valid_t)
    for h in range(HEADS):
        o_ref[h] = _softmax_finish_v1(carry[2 + 2 * h]).astype(o_ref.dtype)


def _dsa_attention(P, F, slopes, B, S, *, bq, ck, topk, gaq, giq, gak, gav, gik, iw_col):
    G, T, hw = P.shape
    nq = S // bq
    return pl.pallas_call(
        functools.partial(_dsa_kernel, bq=bq, ck=ck, topk=topk),
        grid=(B, nq),
        in_specs=[
            pl.BlockSpec((HEADS, bq, hw), lambda b, i: (gaq // HEADS, b * nq + i, 0)),
            pl.BlockSpec((IDX_HEADS, bq, hw), lambda b, i: (giq // IDX_HEADS, b * nq + i, 0)),
            pl.BlockSpec((1, S, hw), lambda b, i: (gak, b, 0)),
            pl.BlockSpec((1, S, hw), lambda b, i: (gav, b, 0)),
            pl.BlockSpec((1, S, hw), lambda b, i: (gik, b, 0)),
            pl.BlockSpec((bq, 128), lambda b, i: (b * nq + i, iw_col // 128)),
            pl.BlockSpec(slopes.shape, lambda b, i: (0, 0, 0)),
        ],
        out_specs=pl.BlockSpec((HEADS, bq, hw), lambda b, i: (0, b * nq + i, 0)),
        out_shape=jax.ShapeDtypeStruct((HEADS, T, hw), bf16),
        scratch_shapes=[pltpu.VMEM((S // ck, ck, bq), i32), pltpu.VMEM((1, S, 2 * HEAD_W), bf16)],
        compiler_params=_cparams(("arbitrary", "arbitrary")),
        name="dsa_attention",
    )(P, P, P, P, P, F, slopes)


def _merge_kernel(x_ref, g_ref, sc_ref, sh_ref, g1_ref, ya_ref, yb_ref, yc_ref, yd_ref,
                  wg_ref, wb_ref, wo_ref, o_ref):
    x = x_ref[...]
    D = x.shape[1]
    h = _norm_mod(x, g_ref[...], sc_ref[0], sh_ref[0]).astype(bf16)
    merged = jnp.zeros(x.shape, f32)
    for n, y_ref in enumerate((ya_ref, yb_ref, yc_ref, yd_ref)):
        gate = jax.nn.sigmoid(_dot(h, wg_ref[:, n * D:(n + 1) * D]))
        y = jnp.concatenate([y_ref[hh] for hh in range(HEADS)], axis=1)
        merged = merged + gate * _dot(y, wb_ref[n])
    o_ref[...] = x + g1_ref[0] * _dot(merged.astype(bf16), wo_ref[...])


def _merge(x2, g, sc, sh, g1, ys, wg, wb, wo, S, *, tm):
    T, D = x2.shape
    tpb = S // tm
    row = pl.BlockSpec((tm, D), lambda i: (i, 0))
    per_b = pl.BlockSpec((1, 1, D), lambda i: (i // tpb, 0, 0))
    yspec = pl.BlockSpec((HEADS, tm, HEAD_W), lambda i: (0, i, 0))
    full = lambda a: pl.BlockSpec(a.shape, lambda i: (0,) * a.ndim)
    return pl.pallas_call(
        _merge_kernel,
        grid=(T // tm,),
        in_specs=[row, full(g), per_b, per_b, per_b, yspec, yspec, yspec, yspec, full(wg), full(wb), full(wo)],
        out_specs=row,
        out_shape=jax.ShapeDtypeStruct((T, D), f32),
        compiler_params=_cparams(("arbitrary",)),
        name="merge",
    )(x2, g, sc, sh, g1, *ys, wg, wb, wo)


def _finish(x, g2, f, fg_ref, o_ref):
    out = x + g2 * f
    if fg_ref is not None:
        out = out * lax.rsqrt(jnp.mean(out * out, axis=-1, keepdims=True) + EPS) * fg_ref[...]
    o_ref[...] = out


def _ffn_kernel(x_ref, g_ref, sc_ref, sh_ref, g2_ref, w1_ref, w3_ref, w2_ref, *rest, final):
    fg_ref = rest[0] if final else None
    o_ref, h_scr, acc_scr = rest[-3:]
    j = pl.program_id(1)

    @pl.when(j == 0)
    def _():
        h_scr[...] = _norm_mod(x_ref[...], g_ref[...], sc_ref[0], sh_ref[0]).astype(bf16)
        acc_scr[...] = jnp.zeros_like(acc_scr)

    h = h_scr[...]
    a = _dot(h, w1_ref[...])
    u = (a * jax.nn.sigmoid(a)) * _dot(h, w3_ref[...])
    acc_scr[...] += _dot(u.astype(bf16), w2_ref[...])

    @pl.when(j == pl.num_programs(1) - 1)
    def _():
        _finish(x_ref[...], g2_ref[0], acc_scr[...], fg_ref, o_ref)


def _ffn(x2, g, sc, sh, g2, w1, w3, w2, final_g, S, *, tm, tn):
    T, D = x2.shape
    F = w1.shape[1]
    tpb = S // tm
    row = pl.BlockSpec((tm, D), lambda i, j: (i, 0))
    per_b = pl.BlockSpec((1, 1, D), lambda i, j: (i // tpb, 0, 0))
    vec = pl.BlockSpec((1, D), lambda i, j: (0, 0))
    in_specs = [row, vec, per_b, per_b, per_b,
                pl.BlockSpec((D, tn), lambda i, j: (0, j)),
                pl.BlockSpec((D, tn), lambda i, j: (0, j)),
                pl.BlockSpec((tn, D), lambda i, j: (j, 0))]
    args = [x2, g, sc, sh, g2, w1, w3, w2]
    if final_g is not None:
        in_specs.append(vec)
        args.append(final_g)
    return pl.pallas_call(
        functools.partial(_ffn_kernel, final=final_g is not None),
        grid=(T // tm, F // tn),
        in_specs=in_specs,
        out_specs=row,
        out_shape=jax.ShapeDtypeStruct((T, D), f32),
        scratch_shapes=[pltpu.VMEM((tm, D), bf16), pltpu.VMEM((tm, D), f32)],
        compiler_params=_cparams(("arbitrary", "arbitrary")),
        name="ffn",
    )(*args)


def _router_kernel(x_ref, g_ref, sc_ref, sh_ref, rw_ref, rb_ref, comb_ref, cnt_ref):
    h = _norm_mod(x_ref[...], g_ref[...], sc_ref[0], sh_ref[0]).astype(bf16)
    logits = _dot(h, rw_ref[...]) + rb_ref[...]
    lane = lax.broadcasted_iota(i32, logits.shape, 1)
    big = jnp.int32(logits.shape[1])
    m1 = jnp.max(logits, axis=1, keepdims=True)
    i1 = jnp.min(jnp.where(logits == m1, lane, big), axis=1, keepdims=True)
    rest_l = jnp.where(lane == i1, NEG_INF, logits)
    m2 = jnp.max(rest_l, axis=1, keepdims=True)
    i2 = jnp.min(jnp.where(rest_l == m2, lane, big), axis=1, keepdims=True)
    e2 = jnp.exp(m2 - m1)
    den = 1.0 + e2
    comb = jnp.where(lane == i1, 1.0 / den, jnp.where(lane == i2, e2 / den, 0.0))
    comb_ref[...] = comb
    cnt_ref[0] = jnp.sum(jnp.where(comb > 0.0, 1.0, 0.0), axis=0, keepdims=True)


def _router(x2, g, sc, sh, rw, rb, S, *, tm):
    T, D = x2.shape
    tpb = S // tm
    nt = T // tm
    return pl.pallas_call(
        _router_kernel,
        grid=(nt,),
        in_specs=[pl.BlockSpec((tm, D), lambda i: (i, 0)),
                  pl.BlockSpec((1, D), lambda i: (0, 0)),
                  pl.BlockSpec((1, 1, D), lambda i: (i // tpb, 0, 0)),
                  pl.BlockSpec((1, 1, D), lambda i: (i // tpb, 0, 0)),
                  pl.BlockSpec(rw.shape, lambda i: (0, 0)),
                  pl.BlockSpec(rb.shape, lambda i: (0, 0))],
        out_specs=[pl.BlockSpec((tm, 128), lambda i: (i, 0)),
                   pl.BlockSpec((1, 1, 128), lambda i: (i, 0, 0))],
        out_shape=[jax.ShapeDtypeStruct((T, 128), f32), jax.ShapeDtypeStruct((nt, 1, 128), f32)],
        compiler_params=_cparams(("arbitrary",)),
        name="router",
    )(x2, g, sc, sh, rw, rb)


def _moe_kernel(sid_ref, nch_ref, x_ref, g_ref, sc_ref, sh_ref, g2_ref, comb_ref, w1_ref, w3_ref, w2_ref, *rest,
                final, sizes):
    fg_ref = rest[0] if final else None
    o_ref, h_scr, tri_scr, rank_scr, rank_t_scr, xg_scr, yg_scr, out_scr = rest[-8:]
    i = pl.program_id(0)
    e = pl.program_id(1)
    j = pl.program_id(2)
    n_e = pl.num_programs(1)
    tm = x_ref.shape[0]
    sid = sid_ref[i * n_e + e]
    nch = nch_ref[i * n_e + e]

    def dispatch(phase):
        for k, r in enumerate(sizes):
            @pl.when(sid == k)
            def _(r=r):
                phase(r, 0)

        @pl.when(sid == len(sizes))
        def _():
            def body(c, _):
                phase(sizes[-1], c)
                return 0
            lax.fori_loop(0, nch, body, 0)

    def chunk_rows(rows, c):
        if isinstance(c, int):
            return pl.ds(c * rows, rows)
        return pl.ds(pl.multiple_of(c * rows, 16), rows)

    def first_slot(rows, c):
        return float(c * rows) if isinstance(c, int) else (c * rows).astype(f32)

    @pl.when(jnp.logical_and(i == 0, jnp.logical_and(e == 0, j == 0)))
    def _():
        rr = lax.broadcasted_iota(i32, (tm, tm), 0)
        cc = lax.broadcasted_iota(i32, (tm, tm), 1)
        tri_scr[...] = jnp.where(cc < rr, 1.0, 0.0).astype(bf16)

    @pl.when(jnp.logical_and(e == 0, j == 0))
    def _():
        h_scr[...] = _norm_mod(x_ref[...], g_ref[...], sc_ref[0], sh_ref[0]).astype(bf16)
        out_scr[...] = jnp.zeros_like(out_scr)
        member = jnp.where(comb_ref[...] > 0.0, 1.0, 0.0)
        rank = jnp.where(member > 0.0, _dot(tri_scr[...], member.astype(bf16)), -1.0)
        rank_scr[...] = rank
        rank_t_scr[...] = rank.T

    @pl.when(j == 0)
    def _():
        rank_row = rank_t_scr[pl.ds(e, 1), :]

        def gather(rows, c):
            slot = lax.broadcasted_iota(i32, (rows, tm), 0).astype(f32) + first_slot(rows, c)
            pick = jnp.where(slot == rank_row, 1.0, 0.0).astype(bf16)
            xg_scr[chunk_rows(rows, c), :] = _dot(pick, h_scr[...]).astype(bf16)
            yg_scr[chunk_rows(rows, c), :] = jnp.zeros((rows, yg_scr.shape[1]), f32)

        dispatch(gather)

    def expert(rows, c):
        xg = xg_scr[chunk_rows(rows, c), :]
        a = _dot(xg, w1_ref[0])
        u = (a * jax.nn.sigmoid(a)) * _dot(xg, w3_ref[0])
        yg_scr[chunk_rows(rows, c), :] += _dot(u.astype(bf16), w2_ref[0])

    dispatch(expert)

    last_j = j == pl.num_programs(2) - 1

    @pl.when(last_j)
    def _():
        lane = lax.broadcasted_iota(i32, (tm, 128), 1)
        sel = lane == e
        rank_col = jnp.sum(jnp.where(sel, rank_scr[...], 0.0), axis=1, keepdims=True)
        w_col = jnp.sum(jnp.where(sel, comb_ref[...], 0.0), axis=1, keepdims=True)

        def scatter(rows, c):
            slot = lax.broadcasted_iota(i32, (tm, rows), 1).astype(f32) + first_slot(rows, c)
            place = jnp.where(slot == rank_col, 1.0, 0.0).astype(bf16)
            out_scr[...] += w_col * _dot(place, yg_scr[chunk_rows(rows, c), :].astype(bf16))

        dispatch(scatter)

    @pl.when(jnp.logical_and(last_j, e == n_e - 1))
    def _():
        _finish(x_ref[...], g2_ref[0], out_scr[...], fg_ref, o_ref)


def _moe_plan(cnt, sizes):
    cnt = cnt.astype(i32).reshape(-1)
    sid = sum((cnt > r).astype(i32) for r in sizes)
    nch = (cnt + (sizes[-1] - 1)) // sizes[-1]
    return sid, nch


def _moe(x2, g, sc, sh, g2, comb, sid, nch, w1, w3, w2, final_g, S, *, tm, tn, sizes):
    T, D = x2.shape
    E, _, F = w1.shape
    nj = F // tn
    tpb = S // tm
    cap = -(-tm // sizes[-1]) * sizes[-1]
    row = pl.BlockSpec((tm, D), lambda i, e, j, *_: (i, 0))
    per_b = pl.BlockSpec((1, 1, D), lambda i, e, j, *_: (i // tpb, 0, 0))
    vec = pl.BlockSpec((1, D), lambda i, e, j, *_: (0, 0))
    in_specs = [row, vec, per_b, per_b, per_b,
                pl.BlockSpec((tm, 128), lambda i, e, j, *_: (i, 0)),
                pl.BlockSpec((1, D, tn), lambda i, e, j, *_: (e, 0, j)),
                pl.BlockSpec((1, D, tn), lambda i, e, j, *_: (e, 0, j)),
                pl.BlockSpec((1, tn, D), lambda i, e, j, *_: (e, j, 0))]
    args = [x2, g, sc, sh, g2, comb, w1, w3, w2]
    if final_g is not None:
        in_specs.append(vec)
        args.append(final_g)
    return pl.pallas_call(
        functools.partial(_moe_kernel, final=final_g is not None, sizes=tuple(sizes)),
        grid_spec=pltpu.PrefetchScalarGridSpec(
            num_scalar_prefetch=2,
            grid=(T // tm, E, nj),
            in_specs=in_specs,
            out_specs=row,
            scratch_shapes=[pltpu.VMEM((tm, D), bf16),
                            pltpu.VMEM((tm, tm), bf16),
                            pltpu.VMEM((tm, 128), f32),
                            pltpu.VMEM((128, tm), f32),
                            pltpu.VMEM((cap, D), bf16),
                            pltpu.VMEM((cap, D), f32),
                            pltpu.VMEM((tm, D), f32)]),
        out_shape=jax.ShapeDtypeStruct((T, D), f32),
        compiler_params=_cparams(("arbitrary", "arbitrary", "arbitrary")),
        name="moe",
    )(sid, nch, *args)


_IN_SIZES = (256, 64, 64, 256, 64, 4, 256, 128, 32, 256, 256, 256, 256, 256, 256, 4096)
_IN_NAMES = ("aq", "ak", "av", "iq", "ik", "iw", "cq", "ckv", "kr", "sq", "sk", "sv", "dq", "dk", "dv", "gates")

G_AQ, G_IQ, G_SQ, G_SK, G_SV, G_DQ, G_DK, G_DV, G_AK, G_AV, G_IK = 0, 4, 8, 12, 16, 20, 24, 28, 32, 33, 34
N_GROUPS = 36
F_IW_COL = 640


def _split_w_in(w):
    out, off = {}, 0
    for name, n in zip(_IN_NAMES, _IN_SIZES):
        out[name] = w[:, off:off + n]
        off += n
    return out


def _rot_half_cols(w):
    half = w.shape[1] // 2
    return jnp.concatenate([-w[:, half:], w[:, :half]], axis=1)


def _layer_weights(w_in, w_uq, w_ukv):
    c = _split_w_in(w_in)
    D = w_in.shape[0]
    z = lambda n: jnp.zeros((D, n), f32)
    w_main = jnp.concatenate([
        c["aq"] * 0.125, c["iq"] * 0.125, c["sq"] * 0.125, c["sk"], c["sv"],
        c["dq"] * (DIFF_QK ** -0.5), c["dk"], c["dv"], c["ak"], c["av"], c["ik"], z(64)], axis=1).astype(bf16)
    w_f = jnp.concatenate([
        c["cq"], c["ckv"],
        z(64), c["kr"], z(32),
        z(64), _rot_half_cols(c["kr"]), z(32),
        c["iw"] * (IDX_HEADS ** -0.5), z(124)], axis=1).astype(bf16)
    w_gates = c["gates"].astype(bf16)

    qr = w_uq.shape[0]
    kvr = w_ukv.shape[0]
    wq, wqp, wk, wv = [], [], [], []
    per_q = MLA_NOPE + MLA_ROPE
    per_kv = MLA_NOPE + MLA_V
    for h in range(HEADS):
        nope = w_uq[:, h * per_q:h * per_q + MLA_NOPE]
        rope = w_uq[:, h * per_q + MLA_NOPE:(h + 1) * per_q]
        wq += [nope, rope, jnp.zeros((qr, 32), f32)]
        wqp += [jnp.zeros((qr, 64), f32), _rot_half_cols(rope), jnp.zeros((qr, 32), f32)]
        wk += [w_ukv[:, h * per_kv:h * per_kv + MLA_NOPE], jnp.zeros((kvr, 64), f32)]
        wv += [w_ukv[:, h * per_kv + MLA_NOPE:(h + 1) * per_kv]]
    cat = lambda xs: jnp.concatenate(xs, axis=1).astype(bf16)
    return w_main, w_f, w_gates, cat(wq), cat(wqp), cat(wk), cat(wv)


def _alibi_slopes():
    n = 2 * HEADS
    sl = [2.0 ** (-(8.0 / n) * (k + 1)) for k in range(n)]
    return sl[0::2], sl[1::2]


def kernel(x, c, positions, ada_w, ada_b, norm1_g, norm2_g, w_in, mla_q_norm_g, mla_kv_norm_g, mla_w_uq,
           mla_w_ukv, diff_lq1, diff_lk1, diff_lq2, diff_lk2, diff_norm_g, w_branch, w_out, ffn_w1, ffn_w3,
           ffn_w2, router_w, router_b, moe_w1, moe_w3, moe_w2, final_norm_g):
    B, S, D = x.shape
    T = B * S
    depth = ada_w.shape[0]
    topk = min(TOPK_MAX, S // 4)
    tm = min(1024, S)
    tm_merge = min(512, S)
    bq = min(512, S)
    ck = min(512, S)

    x2 = x.reshape(T, D)
    pos2 = positions.reshape(T, 1)
    mod = _modulation(c, ada_w, ada_b)

    half = MLA_ROPE // 2
    inv = ROPE_THETA ** (-jnp.arange(half, dtype=f32) / half)
    inv_full = jnp.concatenate([jnp.zeros((64,), f32), inv, inv, jnp.zeros((32,), f32)]).reshape(1, 128)
    sl_a, sl_d = _alibi_slopes()
    slopes_a = jnp.broadcast_to(jnp.asarray(sl_a, f32)[:, None, None], (HEADS, 1, ck))
    slopes_d = jnp.broadcast_to(jnp.asarray(sl_d, f32)[:, None, None], (HEADS, 1, ck))

    row = lambda v: v.reshape(1, -1)
    for l in range(depth):
        sh1, sc1, g1, sh2, sc2, g2 = [mod[l, :, k * D:(k + 1) * D].reshape(B, 1, D) for k in range(6)]
        lam_init = 0.8 - 0.6 * math.exp(-0.3 * l)
        w_main, w_f, w_gates, wq, wqp, wk, wv = _layer_weights(w_in[l], mla_w_uq[l], mla_w_ukv[l])
        n1 = row(norm1_g[l])

        P = _inproj(x2, n1, sc1, sh1, w_main, S, grouped=True, out_dtype=bf16, tm=tm, tn=256)
        F = _inproj(x2, n1, sc1, sh1, w_f, S, grouped=False, out_dtype=f32, tm=tm, tn=w_f.shape[1])
        qm, km, vm = _mla_prep(F, pos2, row(mla_q_norm_g[l]), row(mla_kv_norm_g[l]), wq, wqp, wk, wv,
                               inv_full, tm=tm)

        ya = _dsa_attention(P, F, slopes_a, B, S, bq=bq, ck=ck, topk=topk, gaq=G_AQ, giq=G_IQ, gak=G_AK,
                            gav=G_AV, gik=G_IK, iw_col=F_IW_COL)
        yb = _mla_attention(qm, km, vm, B, S, bq=bq, ck=ck)
        yc = _sb_attention(P, B, S, bq=bq, ck=ck, gq=G_SQ, gk=G_SK, gv=G_SV)
        yd = _diff_attention(P, slopes_d, row(diff_lq1[l]), row(diff_lk1[l]), row(diff_lq2[l]),
                             row(diff_lk2[l]), row(diff_norm_g[l]), B, S, bq=bq, ck=ck, lam_init=lam_init,
                             gq=G_DQ, gk=G_DK, gv=G_DV)

        x2 = _merge(x2, n1, sc1, sh1, g1, (ya, yb, yc, yd), w_gates, w_branch[l].astype(bf16),
                    w_out[l].astype(bf16), S, tm=tm_merge)

        final_g = row(final_norm_g) if l == depth - 1 else None
        n2 = row(norm2_g[l])
        if l % 2 == 0:
            k = l // 2
            x2 = _ffn(x2, n2, sc2, sh2, g2, ffn_w1[k].astype(bf16), ffn_w3[k].astype(bf16),
                      ffn_w2[k].astype(bf16), final_g, S, tm=min(FFN_TM, S), tn=FFN_TN)
        else:
            k = l // 2
            E = router_w.shape[2]
            rw = jnp.concatenate([router_w[k], jnp.zeros((D, 128 - E), f32)], axis=1).astype(bf16)
            rb = jnp.concatenate([router_b[k], jnp.full((128 - E,), NEG_INF, f32)]).reshape(1, 128)
            comb, cnt = _router(x2, n2, sc2, sh2, rw, rb, S, tm=tm)
            sid, nch = _moe_plan(cnt[:, 0, :E], MOE_SIZES)
            x2 = _moe(x2, n2, sc2, sh2, g2, comb, sid, nch, moe_w1[k].astype(bf16), moe_w3[k].astype(bf16),
                      moe_w2[k].astype(bf16), final_g, S, tm=tm, tn=MOE_TN, sizes=MOE_SIZES)
    return x2.reshape(B, S, D)
```

```python
import functools
import math

import jax
import jax.numpy as jnp
from jax import lax
from jax.experimental import pallas as pl
from jax.experimental.pallas import tpu as pltpu

f32 = jnp.float32
bf16 = jnp.bfloat16
i32 = jnp.int32
i16 = jnp.int16

N_BRANCHES = 4
HEADS = 4
HEAD_W = 64
IDX_HEADS = 4
TOPK_MAX = 256
MLA_Q_RANK = 256
MLA_KV_RANK = 128
MLA_NOPE = 64
MLA_ROPE = 32
MLA_V = 64
MLA_QK_PAD = 128
ROPE_THETA = 10000.0
DIFF_QK = 32
N_EXPERTS = 8
TOP_K = 2
FFN_TN = 1408
FFN_TM = 512
MOE_TN = 896
MOE_SIZES = (128, 192, 256, 320, 384, 448, 512)
EPS = 1e-6
NEG_INF = float("-inf")
I16_MIN = -2 ** 15

VMEM_LIMIT = 56 * 1024 * 1024


def _cparams(sem):
    return pltpu.CompilerParams(dimension_semantics=sem, vmem_limit_bytes=VMEM_LIMIT)


def _nt_dot(a, b):
    return lax.dot_general(a, b, (((1,), (1,)), ((), ())), preferred_element_type=f32)


def _dot(a, b):
    return jnp.dot(a, b, preferred_element_type=f32)


def _norm_mod(x, g, sc, sh):
    y = x * lax.rsqrt(jnp.mean(x * x, axis=-1, keepdims=True) + EPS)
    return (y * g) * (1.0 + sc) + sh


def _mod_kernel(c_ref, w_ref, b_ref, o_ref):
    c = c_ref[...]
    cond = (c * jax.nn.sigmoid(c)).astype(bf16)
    o_ref[0] = _dot(cond, w_ref[0].astype(bf16)) + b_ref[0]


def _modulation(c, ada_w, ada_b):
    L, D, N = ada_w.shape
    B = c.shape[0]
    tn = 1536
    return pl.pallas_call(
        _mod_kernel,
        grid=(L, N // tn),
        in_specs=[
            pl.BlockSpec((B, D), lambda l, j: (0, 0)),
            pl.BlockSpec((1, D, tn), lambda l, j: (l, 0, j)),
            pl.BlockSpec((1, 1, tn), lambda l, j: (l, 0, j)),
        ],
        out_specs=pl.BlockSpec((1, B, tn), lambda l, j: (l, 0, j)),
        out_shape=jax.ShapeDtypeStruct((L, B, N), f32),
        compiler_params=_cparams(("arbitrary", "arbitrary")),
        name="modulation",
    )(c, ada_w, ada_b.reshape(L, 1, N))


def _inproj_kernel(x_ref, g_ref, sc_ref, sh_ref, w_ref, o_ref, h_scr, *, grouped):
    @pl.when(pl.program_id(1) == 0)
    def _():
        h_scr[...] = _norm_mod(x_ref[...], g_ref[...], sc_ref[0], sh_ref[0]).astype(bf16)

    res = _dot(h_scr[...], w_ref[...])
    if grouped:
        for k in range(res.shape[1] // HEAD_W):
            o_ref[k] = res[:, k * HEAD_W:(k + 1) * HEAD_W].astype(o_ref.dtype)
    else:
        o_ref[...] = res.astype(o_ref.dtype)


def _inproj(x2, g, sc, sh, w, S, *, grouped, out_dtype, tm, tn):
    T, D = x2.shape
    N = w.shape[1]
    tpb = S // tm
    if grouped:
        out_shape = jax.ShapeDtypeStruct((N // HEAD_W, T, HEAD_W), out_dtype)
        out_spec = pl.BlockSpec((tn // HEAD_W, tm, HEAD_W), lambda i, j: (j, i, 0))
    else:
        out_shape = jax.ShapeDtypeStruct((T, N), out_dtype)
        out_spec = pl.BlockSpec((tm, tn), lambda i, j: (i, j))
    return pl.pallas_call(
        functools.partial(_inproj_kernel, grouped=grouped),
        grid=(T // tm, N // tn),
        in_specs=[
            pl.BlockSpec((tm, D), lambda i, j: (i, 0)),
            pl.BlockSpec((1, D), lambda i, j: (0, 0)),
            pl.BlockSpec((1, 1, D), lambda i, j: (i // tpb, 0, 0)),
            pl.BlockSpec((1, 1, D), lambda i, j: (i // tpb, 0, 0)),
            pl.BlockSpec((D, tn), lambda i, j: (0, j)),
        ],
        out_specs=out_spec,
        out_shape=out_shape,
        scratch_shapes=[pltpu.VMEM((tm, D), bf16)],
        compiler_params=_cparams(("arbitrary", "arbitrary")),
        name="inproj_grouped" if grouped else "inproj_plain",
    )(x2, g, sc, sh, w)


def _mla_prep_kernel(f_ref, pos_ref, gq_ref, gkv_ref, wq_ref, wqp_ref, wk_ref, wv_ref, inv_ref,
                     q_ref, k_ref, v_ref, *, scale):
    cq = f_ref[:, 0:MLA_Q_RANK]
    ckv = f_ref[:, MLA_Q_RANK:MLA_Q_RANK + MLA_KV_RANK]
    kr = f_ref[:, 384:512]
    krp = f_ref[:, 512:640]
    nq = (cq * lax.rsqrt(jnp.mean(cq * cq, axis=-1, keepdims=True) + EPS) * gq_ref[...]).astype(bf16)
    nkv = (ckv * lax.rsqrt(jnp.mean(ckv * ckv, axis=-1, keepdims=True) + EPS) * gkv_ref[...]).astype(bf16)
    ang = pos_ref[...].astype(f32) * inv_ref[...]
    cosf = jnp.cos(ang)
    sinf = jnp.sin(ang)
    q1 = _dot(nq, wq_ref[...])
    q2 = _dot(nq, wqp_ref[...])
    kn = _dot(nkv, wk_ref[...])
    vv = _dot(nkv, wv_ref[...])
    krope = kr * cosf + krp * sinf
    for h in range(HEADS):
        sl = slice(h * MLA_QK_PAD, (h + 1) * MLA_QK_PAD)
        q_ref[h] = ((q1[:, sl] * cosf + q2[:, sl] * sinf) * scale).astype(bf16)
        k_ref[h] = (kn[:, sl] + krope).astype(bf16)
        v_ref[h] = vv[:, h * MLA_V:(h + 1) * MLA_V].astype(bf16)


def _mla_prep(F, pos2, gq, gkv, wq, wqp, wk, wv, inv_full, *, tm):
    T = F.shape[0]
    scale = (MLA_NOPE + MLA_ROPE) ** -0.5
    full = lambda a: pl.BlockSpec(a.shape, lambda i: (0,) * a.ndim)
    return pl.pallas_call(
        functools.partial(_mla_prep_kernel, scale=scale),
        grid=(T // tm,),
        in_specs=[
            pl.BlockSpec((tm, F.shape[1]), lambda i: (i, 0)),
            pl.BlockSpec((tm, 1), lambda i: (i, 0)),
            full(gq), full(gkv), full(wq), full(wqp), full(wk), full(wv), full(inv_full),
        ],
        out_specs=[
            pl.BlockSpec((HEADS, tm, MLA_QK_PAD), lambda i: (0, i, 0)),
            pl.BlockSpec((HEADS, tm, MLA_QK_PAD), lambda i: (0, i, 0)),
            pl.BlockSpec((HEADS, tm, MLA_V), lambda i: (0, i, 0)),
        ],
        out_shape=[
            jax.ShapeDtypeStruct((HEADS, T, MLA_QK_PAD), bf16),
            jax.ShapeDtypeStruct((HEADS, T, MLA_QK_PAD), bf16),
            jax.ShapeDtypeStruct((HEADS, T, MLA_V), bf16),
        ],
        compiler_params=_cparams(("arbitrary",)),
        name="mla_prep",
    )(F, pos2, gq, gkv, wq, wqp, wk, wv, inv_full)


def _chunk_plan(i, bq, ck):
    n_full = (i * bq) // ck
    return n_full, n_full * ck


def _tail_valid(i, bq, ck, tail_off):
    row = lax.broadcasted_iota(i32, (bq, ck), 0)
    col = lax.broadcasted_iota(i32, (bq, ck), 1)
    return col + (tail_off - i * bq) <= row


def _fill_v_ones(v_ref, vp_scr):
    for h in range(v_ref.shape[0]):
        vp_scr[h, :, 0:HEAD_W] = v_ref[h]
        vp_scr[h, :, HEAD_W:2 * HEAD_W] = jnp.ones((v_ref.shape[1], HEAD_W), bf16)


def _softmax_step_v1(s, vp, m, acc):
    m_new = jnp.maximum(m, jnp.max(s, axis=1, keepdims=True))
    alpha = jnp.exp(m - m_new)
    p = jnp.exp((s - m_new).astype(bf16))
    return m_new, alpha * acc + _dot(p, vp)


def _softmax_init_v1(bq):
    return (jnp.full((bq, 1), NEG_INF, f32), jnp.zeros((bq, 2 * HEAD_W), f32))


def _softmax_finish_v1(acc):
    return acc[:, 0:HEAD_W] / acc[:, HEAD_W:HEAD_W + 1]


def _mla_attn_kernel(q_ref, k_ref, v_ref, o_ref, vp_scr, *, bq, ck):
    i = pl.program_id(1)
    n_full, tail_off = _chunk_plan(i, bq, ck)

    @pl.when(i == 0)
    def _():
        _fill_v_ones(v_ref, vp_scr)

    def chunk(off, carry, valid):
        off = pl.multiple_of(off, ck)
        out = ()
        for h in range(HEADS):
            s = _nt_dot(q_ref[h], k_ref[h, pl.ds(off, ck), :])
            if valid is not None:
                s = jnp.where(valid, s, NEG_INF)
            out += _softmax_step_v1(s, vp_scr[h, pl.ds(off, ck), :], *carry[2 * h:2 * h + 2])
        return out

    carry = lax.fori_loop(0, n_full, lambda c, carry: chunk(c * ck, carry, None), _softmax_init_v1(bq) * HEADS)
    carry = chunk(tail_off, carry, _tail_valid(i, bq, ck, tail_off))
    for h in range(HEADS):
        o_ref[h] = _softmax_finish_v1(carry[2 * h + 1]).astype(o_ref.dtype)


def _mla_attention(qm, km, vm, B, S, *, bq, ck):
    H, T, dk = qm.shape
    dv = vm.shape[2]
    nq = S // bq
    return pl.pallas_call(
        functools.partial(_mla_attn_kernel, bq=bq, ck=ck),
        grid=(B, nq),
        in_specs=[
            pl.BlockSpec((H, bq, dk), lambda b, i: (0, b * nq + i, 0)),
            pl.BlockSpec((H, S, dk), lambda b, i: (0, b, 0)),
            pl.BlockSpec((H, S, dv), lambda b, i: (0, b, 0)),
        ],
        out_specs=pl.BlockSpec((H, bq, dv), lambda b, i: (0, b * nq + i, 0)),
        out_shape=jax.ShapeDtypeStruct((H, T, dv), bf16),
        scratch_shapes=[pltpu.VMEM((H, S, 2 * HEAD_W), bf16)],
        compiler_params=_cparams(("arbitrary", "arbitrary")),
        name="mla_attention",
    )(qm, km, vm)


def _diff_attn_kernel(q_ref, k_ref, v_ref, slope_ref, lq1_ref, lk1_ref, lq2_ref, lk2_ref, g_ref, o_ref,
                      vp_scr, *, bq, ck, lam_init):
    i = pl.program_id(1)
    n_full, tail_off = _chunk_plan(i, bq, ck)
    lane = lax.broadcasted_iota(i32, (bq, q_ref.shape[2]), 1)
    col = lax.broadcasted_iota(i32, (1, ck), 1).astype(f32)

    @pl.when(i == 0)
    def _():
        _fill_v_ones(v_ref, vp_scr)

    def chunk(off, carry, valid):
        off = pl.multiple_of(off, ck)
        pos = col + (off - i * bq).astype(f32)
        out = ()
        for h in range(HEADS):
            q = q_ref[h]
            zero = jnp.zeros_like(q)
            k = k_ref[h, pl.ds(off, ck), :]
            vp = vp_scr[h, pl.ds(off, ck), :]
            bias = slope_ref[h] * pos
            for mp, qm in enumerate((jnp.where(lane < DIFF_QK, q, zero), jnp.where(lane < DIFF_QK, zero, q))):
                s = _nt_dot(qm, k) + bias
                if valid is not None:
                    s = jnp.where(valid, s, NEG_INF)
                base = 2 * (2 * h + mp)
                out += _softmax_step_v1(s, vp, *carry[base:base + 2])
        return out

    carry = lax.fori_loop(0, n_full, lambda c, carry: chunk(c * ck, carry, None),
                          _softmax_init_v1(bq) * (2 * HEADS))
    carry = chunk(tail_off, carry, _tail_valid(i, bq, ck, tail_off))
    lam = (jnp.exp(jnp.sum(lq1_ref[...] * lk1_ref[...], axis=1, keepdims=True))
           - jnp.exp(jnp.sum(lq2_ref[...] * lk2_ref[...], axis=1, keepdims=True)) + lam_init)
    for h in range(HEADS):
        m0, a0, m1, a1 = carry[4 * h:4 * h + 4]
        o = _softmax_finish_v1(a0) - lam * _softmax_finish_v1(a1)
        y = o * lax.rsqrt(jnp.mean(o * o, axis=-1, keepdims=True) + EPS) * g_ref[...]
        o_ref[h] = (y * (1.0 - lam_init)).astype(o_ref.dtype)


def _diff_attention(P, slopes, lq1, lk1, lq2, lk2, g, B, S, *, bq, ck, lam_init, gq, gk, gv):
    G, T, hw = P.shape
    nq = S // bq
    vec = lambda a: pl.BlockSpec(a.shape, lambda b, i: (0,) * a.ndim)
    return pl.pallas_call(
        functools.partial(_diff_attn_kernel, bq=bq, ck=ck, lam_init=lam_init),
        grid=(B, nq),
        in_specs=[
            pl.BlockSpec((HEADS, bq, hw), lambda b, i: (gq // HEADS, b * nq + i, 0)),
            pl.BlockSpec((HEADS, S, hw), lambda b, i: (gk // HEADS, b, 0)),
            pl.BlockSpec((HEADS, S, hw), lambda b, i: (gv // HEADS, b, 0)),
            vec(slopes), vec(lq1), vec(lk1), vec(lq2), vec(lk2), vec(g),
        ],
        out_specs=pl.BlockSpec((HEADS, bq, hw), lambda b, i: (0, b * nq + i, 0)),
        out_shape=jax.ShapeDtypeStruct((HEADS, T, hw), bf16),
        scratch_shapes=[pltpu.VMEM((HEADS, S, 2 * HEAD_W), bf16)],
        compiler_params=_cparams(("arbitrary", "arbitrary")),
        name="diff_attention",
    )(P, P, P, slopes, lq1, lk1, lq2, lk2, g)


SB_GROUP = 256


def _sb_attn_kernel(q_ref, k_ref, v_ref, o_ref, *, bq, ck):
    i = pl.program_id(1)
    n_full, tail_off = _chunk_plan(i, bq, ck)
    dv = v_ref.shape[2]
    gw = min(SB_GROUP, ck)
    rr = lax.broadcasted_iota(i32, (gw, gw), 0)
    cc = lax.broadcasted_iota(i32, (gw, gw), 1)
    upper = jnp.where(rr > cc, 1.0, 0.0).astype(bf16)

    def chunk(off, carry, strict):
        off = pl.multiple_of(off, ck)
        out = ()
        for h in range(HEADS):
            run, acc = carry[2 * h:2 * h + 2]
            z = _nt_dot(q_ref[h], k_ref[h, pl.ds(off, ck), :])
            lsm = -(jnp.maximum(z, 0.0) + jnp.log(1.0 + jnp.exp(-jnp.abs(z))))
            if strict is not None:
                lsm = jnp.where(strict, lsm, 0.0)
            parts = []
            for g in reversed(range(ck // gw)):
                x = lsm[:, g * gw:(g + 1) * gw]
                parts.append(_dot(x.astype(bf16), upper) + run)
                run = run + jnp.sum(x, axis=1, keepdims=True)
            between = jnp.concatenate(parts[::-1], axis=1) if len(parts) > 1 else parts[0]
            arg = z + lsm + between
            if strict is not None:
                arg = jnp.where(strict, arg, NEG_INF)
            acc = acc + _dot(jnp.exp(arg.astype(bf16)), v_ref[h, pl.ds(off, ck), :])
            out += (run, acc)
        return out

    row = lax.broadcasted_iota(i32, (bq, ck), 0)
    col = lax.broadcasted_iota(i32, (bq, ck), 1)
    strict = col + (tail_off - i * bq) < row
    init = (jnp.zeros((bq, 1), f32), jnp.zeros((bq, dv), f32)) * HEADS
    carry = chunk(tail_off, init, strict)
    carry = lax.fori_loop(0, n_full, lambda n, carry: chunk((n_full - 1 - n) * ck, carry, None), carry)
    for h in range(HEADS):
        o_ref[h] = carry[2 * h + 1].astype(o_ref.dtype)


def _sb_attention(P, B, S, *, bq, ck, gq, gk, gv):
    G, T, hw = P.shape
    nq = S // bq
    return pl.pallas_call(
        functools.partial(_sb_attn_kernel, bq=bq, ck=ck),
        grid=(B, nq),
        in_specs=[
            pl.BlockSpec((HEADS, bq, hw), lambda b, i: (gq // HEADS, b * nq + i, 0)),
            pl.BlockSpec((HEADS, S, hw), lambda b, i: (gk // HEADS, b, 0)),
            pl.BlockSpec((HEADS, S, hw), lambda b, i: (gv // HEADS, b, 0)),
        ],
        out_specs=pl.BlockSpec((HEADS, bq, hw), lambda b, i: (0, b * nq + i, 0)),
        out_shape=jax.ShapeDtypeStruct((HEADS, T, hw), bf16),
        compiler_params=_cparams(("arbitrary", "arbitrary")),
        name="sb_attention",
    )(P, P, P)


def _dsa_kernel(aq_ref, iq_ref, ak_ref, av_ref, ik_ref, iw_ref, slope_ref, o_ref, key_scr, hi_scr, lo_scr,
                vp_scr, *, bq, ck, topk):
    i = pl.program_id(1)
    n_full, tail_off = _chunk_plan(i, bq, ck)
    n_chunks = n_full + 1

    @pl.when(i == 0)
    def _():
        _fill_v_ones(av_ref, vp_scr)

    w_t = iw_ref[...].T
    gw = min(SB_GROUP, ck)
    rr = lax.broadcasted_iota(i32, (gw, gw), 0)
    cc = lax.broadcasted_iota(i32, (gw, gw), 1)
    incl = jnp.where(cc <= rr, 1.0, 0.0).astype(bf16)
    col = lax.broadcasted_iota(i32, (1, ck), 1).astype(f32)
    key_i = lax.broadcasted_iota(i32, (ck, bq), 0)
    qry_i = lax.broadcasted_iota(i32, (ck, bq), 1)
    tail_valid_t = key_i + (tail_off - i * bq) <= qry_i

    def score_chunk(c, valid_t):
        off = pl.multiple_of(c * ck, ck)
        ik = ik_ref[0, pl.ds(off, ck), :]
        sc = jnp.zeros((ck, bq), f32)
        for h in range(IDX_HEADS):
            sc = sc + jnp.maximum(_nt_dot(ik, iq_ref[h]), 0.0) * w_t[h:h + 1, :]
        if valid_t is not None:
            sc = jnp.where(valid_t, sc, NEG_INF)
        bits = lax.bitcast_convert_type(sc + 0.0, i32)
        key = jnp.where(bits < 0, bits ^ jnp.int32(0x7FFFFFFF), bits)
        key_scr[c] = key
        hi_scr[c] = lax.shift_right_arithmetic(key, 16).astype(i16)
        lo_scr[c] = ((key & 0xFFFF) - 32768).astype(i16)

    def score_body(c, _):
        score_chunk(c, None)
        return 0

    lax.fori_loop(0, n_full, score_body, 0)
    score_chunk(n_full, tail_valid_t)

    lanes_acc = 32

    def rows(v):
        return jnp.broadcast_to(v, (lanes_acc, bq))[None]

    def count16(ref, pred):
        def body(c, acc):
            v = ref[c].reshape(ck // lanes_acc, lanes_acc, bq)
            hit = jnp.where(pred(v), jnp.int16(1), jnp.int16(0))
            parts = [hit[g] for g in range(ck // lanes_acc)]
            while len(parts) > 1:
                parts = [a + b for a, b in zip(parts[0::2], parts[1::2])]
            return acc + parts[0]
        acc = lax.fori_loop(0, n_chunks, body, jnp.zeros((lanes_acc, bq), i16))
        return jnp.sum(acc.astype(f32), axis=0, keepdims=True)

    def search16(ref, need):
        def body(it, prefix):
            cand = prefix + lax.shift_left(jnp.int32(1), jnp.int32(15) - it)
            cand_r = rows(cand.astype(i16))
            cnt = count16(ref, lambda v: v >= cand_r)
            return jnp.where(cnt >= need, cand, prefix)
        return lax.fori_loop(0, 16, body, jnp.full((1, bq), I16_MIN, i32))

    tau_hi = search16(hi_scr, float(topk))
    tau_hi_r = rows(tau_hi.astype(i16))
    need_lo = float(topk) - count16(hi_scr, lambda v: v > tau_hi_r)
    tau_hi_b = tau_hi.astype(i16)

    def mask_lo(c, _):
        lo_scr[c] = jnp.where(hi_scr[c] == tau_hi_b, lo_scr[c], jnp.int16(I16_MIN))
        return 0

    lax.fori_loop(0, n_chunks, mask_lo, 0)
    tau_lo = search16(lo_scr, need_lo)
    tau = tau_hi * 65536 + (tau_lo + 32768)

    def count(pred):
        def body(c, acc):
            key = key_scr[c].reshape(ck // lanes_acc, lanes_acc, bq)
            return acc + jnp.sum(jnp.where(pred(key), 1.0, 0.0), axis=0)
        acc = lax.fori_loop(0, n_chunks, body, jnp.zeros((lanes_acc, bq), f32))
        return jnp.sum(acc, axis=0, keepdims=True)

    tau_r = rows(tau)
    n_take = float(topk) - count(lambda key: key > tau_r)

    def attn_chunk(c, carry, valid_t):
        off = pl.multiple_of(c * ck, ck)
        eq_run = carry[0]
        key = key_scr[c]
        eq = key == tau
        eqf = jnp.where(eq, 1.0, 0.0)
        pcs = []
        for g in range(ck // gw):
            e = eqf[g * gw:(g + 1) * gw, :]
            pcs.append(_dot(incl, e.astype(bf16)) + eq_run)
            eq_run = eq_run + jnp.sum(e, axis=0, keepdims=True)
        pc = jnp.concatenate(pcs, axis=0) if len(pcs) > 1 else pcs[0]
        addm_t = jnp.where(key > tau, 0.0, jnp.where(eq, jnp.where(pc <= n_take, 0.0, NEG_INF), NEG_INF))
        if valid_t is not None:
            addm_t = jnp.where(valid_t, addm_t, NEG_INF)
        addm = addm_t.T
        k = ak_ref[0, pl.ds(off, ck), :]
        vp = vp_scr[0, pl.ds(off, ck), :]
        pos = col + (off - i * bq).astype(f32)
        out = (eq_run,)
        for h in range(HEADS):
            m, acc = carry[1 + 2 * h:3 + 2 * h]
            s = _nt_dot(aq_ref[h], k) + slope_ref[h] * pos + addm
            m_new = jnp.maximum(m, jnp.max(s, axis=1, keepdims=True))
            m_safe = jnp.where(m_new == NEG_INF, 0.0, m_new)
            alpha = jnp.exp(m - m_safe)
            p = jnp.exp((s - m_safe).astype(bf16))
            out += (m_new, alpha * acc + _dot(p, vp))
        return out

    init = (jnp.zeros((1, bq), f32),) + _softmax_init_v1(bq) * HEADS
    carry = lax.fori_loop(0, n_full, lambda c, carry: attn_chunk(c, carry, None), init)
    carry = attn_chunk(n_full, carry, tail_valid_t)
    for h in range(HEADS):
        o_ref[h] = _softmax_finish_v1(carry[2 + 2 * h]).astype(o_ref.dtype)


def _dsa_attention(P, F, slopes, B, S, *, bq, ck, topk, gaq, giq, gak, gav, gik, iw_col):
    G, T, hw = P.shape
    nq = S // bq
    return pl.pallas_call(
        functools.partial(_dsa_kernel, bq=bq, ck=ck, topk=topk),
        grid=(B, nq),
        in_specs=[
            pl.BlockSpec((HEADS, bq, hw), lambda b, i: (gaq // HEADS, b * nq + i, 0)),
            pl.BlockSpec((IDX_HEADS, bq, hw), lambda b, i: (giq // IDX_HEADS, b * nq + i, 0)),
            pl.BlockSpec((1, S, hw), lambda b, i: (gak, b, 0)),
            pl.BlockSpec((1, S, hw), lambda b, i: (gav, b, 0)),
            pl.BlockSpec((1, S, hw), lambda b, i: (gik, b, 0)),
            pl.BlockSpec((bq, 128), lambda b, i: (b * nq + i, iw_col // 128)),
            pl.BlockSpec(slopes.shape, lambda b, i: (0, 0, 0)),
        ],
        out_specs=pl.BlockSpec((HEADS, bq, hw), lambda b, i: (0, b * nq + i, 0)),
        out_shape=jax.ShapeDtypeStruct((HEADS, T, hw), bf16),
        scratch_shapes=[pltpu.VMEM((S // ck, ck, bq), i32), pltpu.VMEM((S // ck, ck, bq), i16),
                        pltpu.VMEM((S // ck, ck, bq), i16), pltpu.VMEM((1, S, 2 * HEAD_W), bf16)],
        compiler_params=_cparams(("arbitrary", "arbitrary")),
        name="dsa_attention",
    )(P, P, P, P, P, F, slopes)


def _merge_kernel(x_ref, g_ref, sc_ref, sh_ref, g1_ref, ya_ref, yb_ref, yc_ref, yd_ref,
                  wg_ref, wb_ref, wo_ref, o_ref):
    x = x_ref[...]
    D = x.shape[1]
    h = _norm_mod(x, g_ref[...], sc_ref[0], sh_ref[0]).astype(bf16)
    merged = jnp.zeros(x.shape, f32)
    for n, y_ref in enumerate((ya_ref, yb_ref, yc_ref, yd_ref)):
        gate = jax.nn.sigmoid(_dot(h, wg_ref[:, n * D:(n + 1) * D]))
        y = jnp.concatenate([y_ref[hh] for hh in range(HEADS)], axis=1)
        merged = merged + gate * _dot(y, wb_ref[n])
    o_ref[...] = x + g1_ref[0] * _dot(merged.astype(bf16), wo_ref[...])


def _merge(x2, g, sc, sh, g1, ys, wg, wb, wo, S, *, tm):
    T, D = x2.shape
    tpb = S // tm
    row = pl.BlockSpec((tm, D), lambda i: (i, 0))
    per_b = pl.BlockSpec((1, 1, D), lambda i: (i // tpb, 0, 0))
    yspec = pl.BlockSpec((HEADS, tm, HEAD_W), lambda i: (0, i, 0))
    full = lambda a: pl.BlockSpec(a.shape, lambda i: (0,) * a.ndim)
    return pl.pallas_call(
        _merge_kernel,
        grid=(T // tm,),
        in_specs=[row, full(g), per_b, per_b, per_b, yspec, yspec, yspec, yspec, full(wg), full(wb), full(wo)],
        out_specs=row,
        out_shape=jax.ShapeDtypeStruct((T, D), f32),
        compiler_params=_cparams(("arbitrary",)),
        name="merge",
    )(x2, g, sc, sh, g1, *ys, wg, wb, wo)


def _finish(x, g2, f, fg_ref, o_ref):
    out = x + g2 * f
    if fg_ref is not None:
        out = out * lax.rsqrt(jnp.mean(out * out, axis=-1, keepdims=True) + EPS) * fg_ref[...]
    o_ref[...] = out


def _ffn_kernel(x_ref, g_ref, sc_ref, sh_ref, g2_ref, w1_ref, w3_ref, w2_ref, *rest, final):
    fg_ref = rest[0] if final else None
    o_ref, h_scr, acc_scr = rest[-3:]
    j = pl.program_id(1)

    @pl.when(j == 0)
    def _():
        h_scr[...] = _norm_mod(x_ref[...], g_ref[...], sc_ref[0], sh_ref[0]).astype(bf16)
        acc_scr[...] = jnp.zeros_like(acc_scr)

    h = h_scr[...]
    a = _dot(h, w1_ref[...])
    u = (a * jax.nn.sigmoid(a)) * _dot(h, w3_ref[...])
    acc_scr[...] += _dot(u.astype(bf16), w2_ref[...])

    @pl.when(j == pl.num_programs(1) - 1)
    def _():
        _finish(x_ref[...], g2_ref[0], acc_scr[...], fg_ref, o_ref)


def _ffn(x2, g, sc, sh, g2, w1, w3, w2, final_g, S, *, tm, tn):
    T, D = x2.shape
    F = w1.shape[1]
    tpb = S // tm
    row = pl.BlockSpec((tm, D), lambda i, j: (i, 0))
    per_b = pl.BlockSpec((1, 1, D), lambda i, j: (i // tpb, 0, 0))
    vec = pl.BlockSpec((1, D), lambda i, j: (0, 0))
    in_specs = [row, vec, per_b, per_b, per_b,
                pl.BlockSpec((D, tn), lambda i, j: (0, j)),
                pl.BlockSpec((D, tn), lambda i, j: (0, j)),
                pl.BlockSpec((tn, D), lambda i, j: (j, 0))]
    args = [x2, g, sc, sh, g2, w1, w3, w2]
    if final_g is not None:
        in_specs.append(vec)
        args.append(final_g)
    return pl.pallas_call(
        functools.partial(_ffn_kernel, final=final_g is not None),
        grid=(T // tm, F // tn),
        in_specs=in_specs,
        out_specs=row,
        out_shape=jax.ShapeDtypeStruct((T, D), f32),
        scratch_shapes=[pltpu.VMEM((tm, D), bf16), pltpu.VMEM((tm, D), f32)],
        compiler_params=_cparams(("arbitrary", "arbitrary")),
        name="ffn",
    )(*args)


def _router_kernel(x_ref, g_ref, sc_ref, sh_ref, rw_ref, rb_ref, comb_ref, cnt_ref):
    h = _norm_mod(x_ref[...], g_ref[...], sc_ref[0], sh_ref[0]).astype(bf16)
    logits = _dot(h, rw_ref[...]) + rb_ref[...]
    lane = lax.broadcasted_iota(i32, logits.shape, 1)
    big = jnp.int32(logits.shape[1])
    m1 = jnp.max(logits, axis=1, keepdims=True)
    i1 = jnp.min(jnp.where(logits == m1, lane, big), axis=1, keepdims=True)
    rest_l = jnp.where(lane == i1, NEG_INF, logits)
    m2 = jnp.max(rest_l, axis=1, keepdims=True)
    i2 = jnp.min(jnp.where(rest_l == m2, lane, big), axis=1, keepdims=True)
    e2 = jnp.exp(m2 - m1)
    den = 1.0 + e2
    comb = jnp.where(lane == i1, 1.0 / den, jnp.where(lane == i2, e2 / den, 0.0))
    comb_ref[...] = comb
    cnt_ref[0] = jnp.sum(jnp.where(comb > 0.0, 1.0, 0.0), axis=0, keepdims=True)


def _router(x2, g, sc, sh, rw, rb, S, *, tm):
    T, D = x2.shape
    tpb = S // tm
    nt = T // tm
    return pl.pallas_call(
        _router_kernel,
        grid=(nt,),
        in_specs=[pl.BlockSpec((tm, D), lambda i: (i, 0)),
                  pl.BlockSpec((1, D), lambda i: (0, 0)),
                  pl.BlockSpec((1, 1, D), lambda i: (i // tpb, 0, 0)),
                  pl.BlockSpec((1, 1, D), lambda i: (i // tpb, 0, 0)),
                  pl.BlockSpec(rw.shape, lambda i: (0, 0)),
                  pl.BlockSpec(rb.shape, lambda i: (0, 0))],
        out_specs=[pl.BlockSpec((tm, 128), lambda i: (i, 0)),
                   pl.BlockSpec((1, 1, 128), lambda i: (i, 0, 0))],
        out_shape=[jax.ShapeDtypeStruct((T, 128), f32), jax.ShapeDtypeStruct((nt, 1, 128), f32)],
        compiler_params=_cparams(("arbitrary",)),
        name="router",
    )(x2, g, sc, sh, rw, rb)


def _moe_kernel(sid_ref, nch_ref, x_ref, g_ref, sc_ref, sh_ref, g2_ref, comb_ref, w1_ref, w3_ref, w2_ref, *rest,
                final, sizes):
    fg_ref = rest[0] if final else None
    o_ref, h_scr, tri_scr, rank_scr, rank_t_scr, xg_scr, yg_scr, out_scr = rest[-8:]
    i = pl.program_id(0)
    e = pl.program_id(1)
    j = pl.program_id(2)
    n_e = pl.num_programs(1)
    tm = x_ref.shape[0]
    sid = sid_ref[i * n_e + e]
    nch = nch_ref[i * n_e + e]

    def dispatch(phase):
        for k, r in enumerate(sizes):
            @pl.when(sid == k)
            def _(r=r):
                phase(r, 0)

        @pl.when(sid == len(sizes))
        def _():
            def body(c, _):
                phase(sizes[-1], c)
                return 0
            lax.fori_loop(0, nch, body, 0)

    def chunk_rows(rows, c):
        if isinstance(c, int):
            return pl.ds(c * rows, rows)
        return pl.ds(pl.multiple_of(c * rows, 16), rows)

    def first_slot(rows, c):
        return float(c * rows) if isinstance(c, int) else (c * rows).astype(f32)

    @pl.when(jnp.logical_and(i == 0, jnp.logical_and(e == 0, j == 0)))
    def _():
        rr = lax.broadcasted_iota(i32, (tm, tm), 0)
        cc = lax.broadcasted_iota(i32, (tm, tm), 1)
        tri_scr[...] = jnp.where(cc < rr, 1.0, 0.0).astype(bf16)

    @pl.when(jnp.logical_and(e == 0, j == 0))
    def _():
        h_scr[...] = _norm_mod(x_ref[...], g_ref[...], sc_ref[0], sh_ref[0]).astype(bf16)
        out_scr[...] = jnp.zeros_like(out_scr)
        member = jnp.where(comb_ref[...] > 0.0, 1.0, 0.0)
        rank = jnp.where(member > 0.0, _dot(tri_scr[...], member.astype(bf16)), -1.0)
        rank_scr[...] = rank
        rank_t_scr[...] = rank.T

    @pl.when(j == 0)
    def _():
        rank_row = rank_t_scr[pl.ds(e, 1), :]

        def gather(rows, c):
            slot = lax.broadcasted_iota(i32, (rows, tm), 0).astype(f32) + first_slot(rows, c)
            pick = jnp.where(slot == rank_row, 1.0, 0.0).astype(bf16)
            xg_scr[chunk_rows(rows, c), :] = _dot(pick, h_scr[...]).astype(bf16)
            yg_scr[chunk_rows(rows, c), :] = jnp.zeros((rows, yg_scr.shape[1]), f32)

        dispatch(gather)

    def expert(rows, c):
        xg = xg_scr[chunk_rows(rows, c), :]
        a = _dot(xg, w1_ref[0])
        u = (a * jax.nn.sigmoid(a)) * _dot(xg, w3_ref[0])
        yg_scr[chunk_rows(rows, c), :] += _dot(u.astype(bf16), w2_ref[0])

    dispatch(expert)

    last_j = j == pl.num_programs(2) - 1

    @pl.when(last_j)
    def _():
        lane = lax.broadcasted_iota(i32, (tm, 128), 1)
        sel = lane == e
        rank_col = jnp.sum(jnp.where(sel, rank_scr[...], 0.0), axis=1, keepdims=True)
        w_col = jnp.sum(jnp.where(sel, comb_ref[...], 0.0), axis=1, keepdims=True)

        def scatter(rows, c):
            slot = lax.broadcasted_iota(i32, (tm, rows), 1).astype(f32) + first_slot(rows, c)
            place = jnp.where(slot == rank_col, 1.0, 0.0).astype(bf16)
            out_scr[...] += w_col * _dot(place, yg_scr[chunk_rows(rows, c), :].astype(bf16))

        dispatch(scatter)

    @pl.when(jnp.logical_and(last_j, e == n_e - 1))
    def _():
        _finish(x_ref[...], g2_ref[0], out_scr[...], fg_ref, o_ref)


def _moe_plan(cnt, sizes):
    cnt = cnt.astype(i32).reshape(-1)
    sid = sum((cnt > r).astype(i32) for r in sizes)
    nch = (cnt + (sizes[-1] - 1)) // sizes[-1]
    return sid, nch


def _moe(x2, g, sc, sh, g2, comb, sid, nch, w1, w3, w2, final_g, S, *, tm, tn, sizes):
    T, D = x2.shape
    E, _, F = w1.shape
    nj = F // tn
    tpb = S // tm
    cap = -(-tm // sizes[-1]) * sizes[-1]
    row = pl.BlockSpec((tm, D), lambda i, e, j, *_: (i, 0))
    per_b = pl.BlockSpec((1, 1, D), lambda i, e, j, *_: (i // tpb, 0, 0))
    vec = pl.BlockSpec((1, D), lambda i, e, j, *_: (0, 0))
    in_specs = [row, vec, per_b, per_b, per_b,
                pl.BlockSpec((tm, 128), lambda i, e, j, *_: (i, 0)),
                pl.BlockSpec((1, D, tn), lambda i, e, j, *_: (e, 0, j)),
                pl.BlockSpec((1, D, tn), lambda i, e, j, *_: (e, 0, j)),
                pl.BlockSpec((1, tn, D), lambda i, e, j, *_: (e, j, 0))]
    args = [x2, g, sc, sh, g2, comb, w1, w3, w2]
    if final_g is not None:
        in_specs.append(vec)
        args.append(final_g)
    return pl.pallas_call(
        functools.partial(_moe_kernel, final=final_g is not None, sizes=tuple(sizes)),
        grid_spec=pltpu.PrefetchScalarGridSpec(
            num_scalar_prefetch=2,
            grid=(T // tm, E, nj),
            in_specs=in_specs,
            out_specs=row,
            scratch_shapes=[pltpu.VMEM((tm, D), bf16),
                            pltpu.VMEM((tm, tm), bf16),
                            pltpu.VMEM((tm, 128), f32),
                            pltpu.VMEM((128, tm), f32),
                            pltpu.VMEM((cap, D), bf16),
                            pltpu.VMEM((cap, D), f32),
                            pltpu.VMEM((tm, D), f32)]),
        out_shape=jax.ShapeDtypeStruct((T, D), f32),
        compiler_params=_cparams(("arbitrary", "arbitrary", "arbitrary")),
        name="moe",
    )(sid, nch, *args)


_IN_SIZES = (256, 64, 64, 256, 64, 4, 256, 128, 32, 256, 256, 256, 256, 256, 256, 4096)
_IN_NAMES = ("aq", "ak", "av", "iq", "ik", "iw", "cq", "ckv", "kr", "sq", "sk", "sv", "dq", "dk", "dv", "gates")

G_AQ, G_IQ, G_SQ, G_SK, G_SV, G_DQ, G_DK, G_DV, G_AK, G_AV, G_IK = 0, 4, 8, 12, 16, 20, 24, 28, 32, 33, 34
N_GROUPS = 36
F_IW_COL = 640


def _split_w_in(w):
    out, off = {}, 0
    for name, n in zip(_IN_NAMES, _IN_SIZES):
        out[name] = w[:, off:off + n]
        off += n
    return out


def _rot_half_cols(w):
    half = w.shape[1] // 2
    return jnp.concatenate([-w[:, half:], w[:, :half]], axis=1)


def _layer_weights(w_in, w_uq, w_ukv):
    c = _split_w_in(w_in)
    D = w_in.shape[0]
    z = lambda n: jnp.zeros((D, n), f32)
    w_main = jnp.concatenate([
        c["aq"] * 0.125, c["iq"] * 0.125, c["sq"] * 0.125, c["sk"], c["sv"],
        c["dq"] * (DIFF_QK ** -0.5), c["dk"], c["dv"], c["ak"], c["av"], c["ik"], z(64)], axis=1).astype(bf16)
    w_f = jnp.concatenate([
        c["cq"], c["ckv"],
        z(64), c["kr"], z(32),
        z(64), _rot_half_cols(c["kr"]), z(32),
        c["iw"] * (IDX_HEADS ** -0.5), z(124)], axis=1).astype(bf16)
    w_gates = c["gates"].astype(bf16)

    qr = w_uq.shape[0]
    kvr = w_ukv.shape[0]
    wq, wqp, wk, wv = [], [], [], []
    per_q = MLA_NOPE + MLA_ROPE
    per_kv = MLA_NOPE + MLA_V
    for h in range(HEADS):
        nope = w_uq[:, h * per_q:h * per_q + MLA_NOPE]
        rope = w_uq[:, h * per_q + MLA_NOPE:(h + 1) * per_q]
        wq += [nope, rope, jnp.zeros((qr, 32), f32)]
        wqp += [jnp.zeros((qr, 64), f32), _rot_half_cols(rope), jnp.zeros((qr, 32), f32)]
        wk += [w_ukv[:, h * per_kv:h * per_kv + MLA_NOPE], jnp.zeros((kvr, 64), f32)]
        wv += [w_ukv[:, h * per_kv + MLA_NOPE:(h + 1) * per_kv]]
    cat = lambda xs: jnp.concatenate(xs, axis=1).astype(bf16)
    return w_main, w_f, w_gates, cat(wq), cat(wqp), cat(wk), cat(wv)


def _alibi_slopes():
    n = 2 * HEADS
    sl = [2.0 ** (-(8.0 / n) * (k + 1)) for k in range(n)]
    return sl[0::2], sl[1::2]


def kernel(x, c, positions, ada_w, ada_b, norm1_g, norm2_g, w_in, mla_q_norm_g, mla_kv_norm_g, mla_w_uq,
           mla_w_ukv, diff_lq1, diff_lk1, diff_lq2, diff_lk2, diff_norm_g, w_branch, w_out, ffn_w1, ffn_w3,
           ffn_w2, router_w, router_b, moe_w1, moe_w3, moe_w2, final_norm_g):
    B, S, D = x.shape
    T = B * S
    depth = ada_w.shape[0]
    topk = min(TOPK_MAX, S // 4)
    tm = min(1024, S)
    tm_merge = min(512, S)
    bq = min(512, S)
    ck = min(512, S)

    x2 = x.reshape(T, D)
    pos2 = positions.reshape(T, 1)
    mod = _modulation(c, ada_w, ada_b)

    half = MLA_ROPE // 2
    inv = ROPE_THETA ** (-jnp.arange(half, dtype=f32) / half)
    inv_full = jnp.concatenate([jnp.zeros((64,), f32), inv, inv, jnp.zeros((32,), f32)]).reshape(1, 128)
    sl_a, sl_d = _alibi_slopes()
    slopes_a = jnp.broadcast_to(jnp.asarray(sl_a, f32)[:, None, None], (HEADS, 1, ck))
    slopes_d = jnp.broadcast_to(jnp.asarray(sl_d, f32)[:, None, None], (HEADS, 1, ck))

    row = lambda v: v.reshape(1, -1)
    for l in range(depth):
        sh1, sc1, g1, sh2, sc2, g2 = [mod[l, :, k * D:(k + 1) * D].reshape(B, 1, D) for k in range(6)]
        lam_init = 0.8 - 0.6 * math.exp(-0.3 * l)
        w_main, w_f, w_gates, wq, wqp, wk, wv = _layer_weights(w_in[l], mla_w_uq[l], mla_w_ukv[l])
        n1 = row(norm1_g[l])

        P = _inproj(x2, n1, sc1, sh1, w_main, S, grouped=True, out_dtype=bf16, tm=tm, tn=256)
        F = _inproj(x2, n1, sc1, sh1, w_f, S, grouped=False, out_dtype=f32, tm=tm, tn=w_f.shape[1])
        qm, km, vm = _mla_prep(F, pos2, row(mla_q_norm_g[l]), row(mla_kv_norm_g[l]), wq, wqp, wk, wv,
                               inv_full, tm=tm)

        ya = _dsa_attention(P, F, slopes_a, B, S, bq=bq, ck=ck, topk=topk, gaq=G_AQ, giq=G_IQ, gak=G_AK,
                            gav=G_AV, gik=G_IK, iw_col=F_IW_COL)
        yb = _mla_attention(qm, km, vm, B, S, bq=bq, ck=ck)
        yc = _sb_attention(P, B, S, bq=bq, ck=ck, gq=G_SQ, gk=G_SK, gv=G_SV)
        yd = _diff_attention(P, slopes_d, row(diff_lq1[l]), row(diff_lk1[l]), row(diff_lq2[l]),
                             row(diff_lk2[l]), row(diff_norm_g[l]), B, S, bq=bq, ck=ck, lam_init=lam_init,
                             gq=G_DQ, gk=G_DK, gv=G_DV)

        x2 = _merge(x2, n1, sc1, sh1, g1, (ya, yb, yc, yd), w_gates, w_branch[l].astype(bf16),
                    w_out[l].astype(bf16), S, tm=tm_merge)

        final_g = row(final_norm_g) if l == depth - 1 else None
        n2 = row(norm2_g[l])
        if l % 2 == 0:
            k = l // 2
            x2 = _ffn(x2, n2, sc2, sh2, g2, ffn_w1[k].astype(bf16), ffn_w3[k].astype(bf16),
                      ffn_w2[k].astype(bf16), final_g, S, tm=min(FFN_TM, S), tn=FFN_TN)
        else:
            k = l // 2
            E = router_w.shape[2]
            rw = jnp.concatenate([router_w[k], jnp.zeros((D, 128 - E), f32)], axis=1).astype(bf16)
            rb = jnp.concatenate([router_b[k], jnp.full((128 - E,), NEG_INF, f32)]).reshape(1, 128)
            comb, cnt = _router(x2, n2, sc2, sh2, rw, rb, S, tm=tm)
            sid, nch = _moe_plan(cnt[:, 0, :E], MOE_SIZES)
            x2 = _moe(x2, n2, sc2, sh2, g2, comb, sid, nch, moe_w1[k].astype(bf16), moe_w3[k].astype(bf16),
                      moe_w2[k].astype(bf16), final_g, S, tm=tm, tn=MOE_TN, sizes=MOE_SIZES)
    return x2.reshape(B, S, D)
```

```python
import functools
import math

import jax
import jax.numpy as jnp
from jax import lax
from jax.experimental import pallas as pl
from jax.experimental.pallas import tpu as pltpu

f32 = jnp.float32
bf16 = jnp.bfloat16
i32 = jnp.int32
i16 = jnp.int16

N_BRANCHES = 4
HEADS = 4
HEAD_W = 64
IDX_HEADS = 4
TOPK_MAX = 256
MLA_Q_RANK = 256
MLA_KV_RANK = 128
MLA_NOPE = 64
MLA_ROPE = 32
MLA_V = 64
MLA_QK_PAD = 128
ROPE_THETA = 10000.0
DIFF_QK = 32
N_EXPERTS = 8
TOP_K = 2
INPROJ_TN = 768
FFN_TN = 1408
FFN_TM = 512
MOE_TN = 896
MOE_SIZES = (128, 192, 256, 320, 384, 448, 512)
EPS = 1e-6
NEG_INF = float("-inf")
LOG2E = math.log2(math.e)
I16_MIN = -2 ** 15

VMEM_LIMIT = 56 * 1024 * 1024


def _cparams(sem):
    return pltpu.CompilerParams(dimension_semantics=sem, vmem_limit_bytes=VMEM_LIMIT)


def _nt_dot(a, b):
    return lax.dot_general(a, b, (((1,), (1,)), ((), ())), preferred_element_type=f32)


def _dot(a, b):
    return jnp.dot(a, b, preferred_element_type=f32)


def _norm_mod(x, g, sc, sh):
    y = x * lax.rsqrt(jnp.mean(x * x, axis=-1, keepdims=True) + EPS)
    return (y * g) * (1.0 + sc) + sh


def _mod_kernel(c_ref, w_ref, b_ref, o_ref):
    c = c_ref[...]
    cond = (c * jax.nn.sigmoid(c)).astype(bf16)
    o_ref[0] = _dot(cond, w_ref[0].astype(bf16)) + b_ref[0]


def _modulation(c, ada_w, ada_b):
    L, D, N = ada_w.shape
    B = c.shape[0]
    tn = 1536
    return pl.pallas_call(
        _mod_kernel,
        grid=(L, N // tn),
        in_specs=[
            pl.BlockSpec((B, D), lambda l, j: (0, 0)),
            pl.BlockSpec((1, D, tn), lambda l, j: (l, 0, j)),
            pl.BlockSpec((1, 1, tn), lambda l, j: (l, 0, j)),
        ],
        out_specs=pl.BlockSpec((1, B, tn), lambda l, j: (l, 0, j)),
        out_shape=jax.ShapeDtypeStruct((L, B, N), f32),
        compiler_params=_cparams(("arbitrary", "arbitrary")),
        name="modulation",
    )(c, ada_w, ada_b.reshape(L, 1, N))


def _inproj_kernel(x_ref, g_ref, sc_ref, sh_ref, w_ref, o_ref, h_scr, *, grouped):
    @pl.when(pl.program_id(1) == 0)
    def _():
        h_scr[...] = _norm_mod(x_ref[...], g_ref[...], sc_ref[0], sh_ref[0]).astype(bf16)

    res = _dot(h_scr[...], w_ref[...])
    if grouped:
        for k in range(res.shape[1] // HEAD_W):
            o_ref[k] = res[:, k * HEAD_W:(k + 1) * HEAD_W].astype(o_ref.dtype)
    else:
        o_ref[...] = res.astype(o_ref.dtype)


def _inproj(x2, g, sc, sh, w, S, *, grouped, out_dtype, tm, tn):
    T, D = x2.shape
    N = w.shape[1]
    tpb = S // tm
    if grouped:
        out_shape = jax.ShapeDtypeStruct((N // HEAD_W, T, HEAD_W), out_dtype)
        out_spec = pl.BlockSpec((tn // HEAD_W, tm, HEAD_W), lambda i, j: (j, i, 0))
    else:
        out_shape = jax.ShapeDtypeStruct((T, N), out_dtype)
        out_spec = pl.BlockSpec((tm, tn), lambda i, j: (i, j))
    return pl.pallas_call(
        functools.partial(_inproj_kernel, grouped=grouped),
        grid=(T // tm, N // tn),
        in_specs=[
            pl.BlockSpec((tm, D), lambda i, j: (i, 0)),
            pl.BlockSpec((1, D), lambda i, j: (0, 0)),
            pl.BlockSpec((1, 1, D), lambda i, j: (i // tpb, 0, 0)),
            pl.BlockSpec((1, 1, D), lambda i, j: (i // tpb, 0, 0)),
            pl.BlockSpec((D, tn), lambda i, j: (0, j)),
        ],
        out_specs=out_spec,
        out_shape=out_shape,
        scratch_shapes=[pltpu.VMEM((tm, D), bf16)],
        compiler_params=_cparams(("arbitrary", "arbitrary")),
        name="inproj_grouped" if grouped else "inproj_plain",
    )(x2, g, sc, sh, w)


def _rope_kernel(pos_ref, inv_ref, cos_ref, sin_ref):
    ang = pos_ref[...].astype(f32) * inv_ref[...]
    cos_ref[...] = jnp.cos(ang)
    sin_ref[...] = jnp.sin(ang)


def _rope_tables(pos2, inv_full, *, tm):
    T = pos2.shape[0]
    tile = pl.BlockSpec((tm, 128), lambda i: (i, 0))
    return pl.pallas_call(
        _rope_kernel,
        grid=(T // tm,),
        in_specs=[pl.BlockSpec((tm, 1), lambda i: (i, 0)), pl.BlockSpec((1, 128), lambda i: (0, 0))],
        out_specs=[tile, tile],
        out_shape=[jax.ShapeDtypeStruct((T, 128), f32)] * 2,
        compiler_params=_cparams(("arbitrary",)),
        name="rope_tables",
    )(pos2, inv_full)


def _mla_prep_kernel(f_ref, cos_ref, sin_ref, gq_ref, gkv_ref, wq_ref, wqp_ref, wk_ref, wv_ref,
                     q_ref, k_ref, v_ref, *, scale):
    cq = f_ref[:, 0:MLA_Q_RANK]
    ckv = f_ref[:, MLA_Q_RANK:MLA_Q_RANK + MLA_KV_RANK]
    kr = f_ref[:, 384:512]
    krp = f_ref[:, 512:640]
    nq = (cq * lax.rsqrt(jnp.mean(cq * cq, axis=-1, keepdims=True) + EPS) * gq_ref[...]).astype(bf16)
    nkv = (ckv * lax.rsqrt(jnp.mean(ckv * ckv, axis=-1, keepdims=True) + EPS) * gkv_ref[...]).astype(bf16)
    cosf = cos_ref[...]
    sinf = sin_ref[...]
    q1 = _dot(nq, wq_ref[...])
    q2 = _dot(nq, wqp_ref[...])
    kn = _dot(nkv, wk_ref[...])
    vv = _dot(nkv, wv_ref[...])
    krope = kr * cosf + krp * sinf
    for h in range(HEADS):
        sl = slice(h * MLA_QK_PAD, (h + 1) * MLA_QK_PAD)
        q_ref[h] = ((q1[:, sl] * cosf + q2[:, sl] * sinf) * scale).astype(bf16)
        k_ref[h] = (kn[:, sl] + krope).astype(bf16)
        v_ref[h] = vv[:, h * MLA_V:(h + 1) * MLA_V].astype(bf16)


def _mla_prep(F, cos_t, sin_t, gq, gkv, wq, wqp, wk, wv, *, tm):
    T = F.shape[0]
    scale = (MLA_NOPE + MLA_ROPE) ** -0.5
    full = lambda a: pl.BlockSpec(a.shape, lambda i: (0,) * a.ndim)
    return pl.pallas_call(
        functools.partial(_mla_prep_kernel, scale=scale),
        grid=(T // tm,),
        in_specs=[
            pl.BlockSpec((tm, F.shape[1]), lambda i: (i, 0)),
            pl.BlockSpec((tm, 128), lambda i: (i, 0)),
            pl.BlockSpec((tm, 128), lambda i: (i, 0)),
            full(gq), full(gkv), full(wq), full(wqp), full(wk), full(wv),
        ],
        out_specs=[
            pl.BlockSpec((HEADS, tm, MLA_QK_PAD), lambda i: (0, i, 0)),
            pl.BlockSpec((HEADS, tm, MLA_QK_PAD), lambda i: (0, i, 0)),
            pl.BlockSpec((HEADS, tm, MLA_V), lambda i: (0, i, 0)),
        ],
        out_shape=[
            jax.ShapeDtypeStruct((HEADS, T, MLA_QK_PAD), bf16),
            jax.ShapeDtypeStruct((HEADS, T, MLA_QK_PAD), bf16),
            jax.ShapeDtypeStruct((HEADS, T, MLA_V), bf16),
        ],
        compiler_params=_cparams(("arbitrary",)),
        name="mla_prep",
    )(F, cos_t, sin_t, gq, gkv, wq, wqp, wk, wv)


def _chunk_plan(i, bq, ck):
    n_full = (i * bq) // ck
    return n_full, n_full * ck


def _tail_valid(i, bq, ck, tail_off):
    row = lax.broadcasted_iota(i32, (bq, ck), 0)
    col = lax.broadcasted_iota(i32, (bq, ck), 1)
    return col + (tail_off - i * bq) <= row


def _fill_v_ones(v_ref, vp_scr):
    for h in range(v_ref.shape[0]):
        vp_scr[h, :, 0:HEAD_W] = v_ref[h]
        vp_scr[h, :, HEAD_W:2 * HEAD_W] = jnp.ones((v_ref.shape[1], HEAD_W), bf16)


def _softmax_step_v1(s, vp, m, acc):
    m_new = jnp.maximum(m, jnp.max(s, axis=1, keepdims=True))
    alpha = jnp.exp(m - m_new)
    p = jnp.exp((s - m_new).astype(bf16))
    return m_new, alpha * acc + _dot(p, vp)


def _softmax_init_v1(bq):
    return (jnp.full((bq, 1), NEG_INF, f32), jnp.zeros((bq, 2 * HEAD_W), f32))


def _softmax_finish_v1(acc):
    return acc[:, 0:HEAD_W] / acc[:, HEAD_W:HEAD_W + 1]


def _mla_attn_kernel(q_ref, k_ref, v_ref, o_ref, vp_scr, *, bq, ck):
    i = pl.program_id(1)
    n_full, tail_off = _chunk_plan(i, bq, ck)

    @pl.when(i == 0)
    def _():
        _fill_v_ones(v_ref, vp_scr)

    def chunk(off, carry, valid):
        off = pl.multiple_of(off, ck)
        out = ()
        for h in range(HEADS):
            s = _nt_dot(q_ref[h], k_ref[h, pl.ds(off, ck), :])
            if valid is not None:
                s = jnp.where(valid, s, NEG_INF)
            out += _softmax_step_v1(s, vp_scr[h, pl.ds(off, ck), :], *carry[2 * h:2 * h + 2])
        return out

    carry = lax.fori_loop(0, n_full, lambda c, carry: chunk(c * ck, carry, None), _softmax_init_v1(bq) * HEADS)
    carry = chunk(tail_off, carry, _tail_valid(i, bq, ck, tail_off))
    for h in range(HEADS):
        o_ref[h] = _softmax_finish_v1(carry[2 * h + 1]).astype(o_ref.dtype)


def _mla_attention(qm, km, vm, B, S, *, bq, ck):
    H, T, dk = qm.shape
    dv = vm.shape[2]
    nq = S // bq
    return pl.pallas_call(
        functools.partial(_mla_attn_kernel, bq=bq, ck=ck),
        grid=(B, nq),
        in_specs=[
            pl.BlockSpec((H, bq, dk), lambda b, i: (0, b * nq + i, 0)),
            pl.BlockSpec((H, S, dk), lambda b, i: (0, b, 0)),
            pl.BlockSpec((H, S, dv), lambda b, i: (0, b, 0)),
        ],
        out_specs=pl.BlockSpec((H, bq, dv), lambda b, i: (0, b * nq + i, 0)),
        out_shape=jax.ShapeDtypeStruct((H, T, dv), bf16),
        scratch_shapes=[pltpu.VMEM((H, S, 2 * HEAD_W), bf16)],
        compiler_params=_cparams(("arbitrary", "arbitrary")),
        name="mla_attention",
    )(qm, km, vm)


def _diff_attn_kernel(q_ref, k_ref, v_ref, slope_ref, lq1_ref, lk1_ref, lq2_ref, lk2_ref, g_ref, o_ref,
                      vp_scr, *, bq, ck, lam_init):
    i = pl.program_id(1)
    n_full, tail_off = _chunk_plan(i, bq, ck)
    lane = lax.broadcasted_iota(i32, (bq, q_ref.shape[2]), 1)
    col = lax.broadcasted_iota(i32, (1, ck), 1).astype(f32)

    @pl.when(i == 0)
    def _():
        _fill_v_ones(v_ref, vp_scr)

    def chunk(off, carry, valid):
        off = pl.multiple_of(off, ck)
        pos = col + (off - i * bq).astype(f32)
        out = ()
        for h in range(HEADS):
            q = q_ref[h]
            zero = jnp.zeros_like(q)
            k = k_ref[h, pl.ds(off, ck), :]
            vp = vp_scr[h, pl.ds(off, ck), :]
            bias = slope_ref[h] * pos
            for mp, qm in enumerate((jnp.where(lane < DIFF_QK, q, zero), jnp.where(lane < DIFF_QK, zero, q))):
                s = _nt_dot(qm, k) + bias
                if valid is not None:
                    s = jnp.where(valid, s, NEG_INF)
                base = 2 * (2 * h + mp)
                out += _softmax_step_v1(s, vp, *carry[base:base + 2])
        return out

    carry = lax.fori_loop(0, n_full, lambda c, carry: chunk(c * ck, carry, None),
                          _softmax_init_v1(bq) * (2 * HEADS))
    carry = chunk(tail_off, carry, _tail_valid(i, bq, ck, tail_off))
    lam = (jnp.exp(jnp.sum(lq1_ref[...] * lk1_ref[...], axis=1, keepdims=True))
           - jnp.exp(jnp.sum(lq2_ref[...] * lk2_ref[...], axis=1, keepdims=True)) + lam_init)
    for h in range(HEADS):
        m0, a0, m1, a1 = carry[4 * h:4 * h + 4]
        o = _softmax_finish_v1(a0) - lam * _softmax_finish_v1(a1)
        y = o * lax.rsqrt(jnp.mean(o * o, axis=-1, keepdims=True) + EPS) * g_ref[...]
        o_ref[h] = (y * (1.0 - lam_init)).astype(o_ref.dtype)


def _diff_attention(P, slopes, lq1, lk1, lq2, lk2, g, B, S, *, bq, ck, lam_init, gq, gk, gv):
    G, T, hw = P.shape
    nq = S // bq
    vec = lambda a: pl.BlockSpec(a.shape, lambda b, i: (0,) * a.ndim)
    return pl.pallas_call(
        functools.partial(_diff_attn_kernel, bq=bq, ck=ck, lam_init=lam_init),
        grid=(B, nq),
        in_specs=[
            pl.BlockSpec((HEADS, bq, hw), lambda b, i: (gq // HEADS, b * nq + i, 0)),
            pl.BlockSpec((HEADS, S, hw), lambda b, i: (gk // HEADS, b, 0)),
            pl.BlockSpec((HEADS, S, hw), lambda b, i: (gv // HEADS, b, 0)),
            vec(slopes), vec(lq1), vec(lk1), vec(lq2), vec(lk2), vec(g),
        ],
        out_specs=pl.BlockSpec((HEADS, bq, hw), lambda b, i: (0, b * nq + i, 0)),
        out_shape=jax.ShapeDtypeStruct((HEADS, T, hw), bf16),
        scratch_shapes=[pltpu.VMEM((HEADS, S, 2 * HEAD_W), bf16)],
        compiler_params=_cparams(("arbitrary", "arbitrary")),
        name="diff_attention",
    )(P, P, P, slopes, lq1, lk1, lq2, lk2, g)


SB_GROUP = 256


def _sb_attn_kernel(q_ref, k_ref, v_ref, o_ref, *, bq, ck):
    i = pl.program_id(1)
    n_full, tail_off = _chunk_plan(i, bq, ck)
    dv = v_ref.shape[2]
    gw = min(SB_GROUP, ck)
    rr = lax.broadcasted_iota(i32, (gw, gw), 0)
    cc = lax.broadcasted_iota(i32, (gw, gw), 1)
    upper = jnp.where(rr > cc, 1.0, 0.0).astype(bf16)

    def chunk(off, carry, strict):
        off = pl.multiple_of(off, ck)
        out = ()
        for h in range(HEADS):
            run, acc = carry[2 * h:2 * h + 2]
            z = _nt_dot(q_ref[h], k_ref[h, pl.ds(off, ck), :])
            lsm = -(jnp.maximum(z, 0.0) + jnp.log(1.0 + jnp.exp2(jnp.abs(z) * (-LOG2E))))
            if strict is not None:
                lsm = jnp.where(strict, lsm, 0.0)
            parts = []
            for g in reversed(range(ck // gw)):
                x = lsm[:, g * gw:(g + 1) * gw]
                parts.append(_dot(x.astype(bf16), upper) + run)
                run = run + jnp.sum(x, axis=1, keepdims=True)
            between = jnp.concatenate(parts[::-1], axis=1) if len(parts) > 1 else parts[0]
            arg = z + lsm + between
            if strict is not None:
                arg = jnp.where(strict, arg, NEG_INF)
            acc = acc + _dot(jnp.exp(arg.astype(bf16)), v_ref[h, pl.ds(off, ck), :])
            out += (run, acc)
        return out

    row = lax.broadcasted_iota(i32, (bq, ck), 0)
    col = lax.broadcasted_iota(i32, (bq, ck), 1)
    strict = col + (tail_off - i * bq) < row
    init = (jnp.zeros((bq, 1), f32), jnp.zeros((bq, dv), f32)) * HEADS
    carry = chunk(tail_off, init, strict)
    carry = lax.fori_loop(0, n_full, lambda n, carry: chunk((n_full - 1 - n) * ck, carry, None), carry)
    for h in range(HEADS):
        o_ref[h] = carry[2 * h + 1].astype(o_ref.dtype)


def _sb_attention(P, B, S, *, bq, ck, gq, gk, gv):
    G, T, hw = P.shape
    nq = S // bq
    return pl.pallas_call(
        functools.partial(_sb_attn_kernel, bq=bq, ck=ck),
        grid=(B, nq),
        in_specs=[
            pl.BlockSpec((HEADS, bq, hw), lambda b, i: (gq // HEADS, b * nq + i, 0)),
            pl.BlockSpec((HEADS, S, hw), lambda b, i: (gk // HEADS, b, 0)),
            pl.BlockSpec((HEADS, S, hw), lambda b, i: (gv // HEADS, b, 0)),
        ],
        out_specs=pl.BlockSpec((HEADS, bq, hw), lambda b, i: (0, b * nq + i, 0)),
        out_shape=jax.ShapeDtypeStruct((HEADS, T, hw), bf16),
        compiler_params=_cparams(("arbitrary", "arbitrary")),
        name="sb_attention",
    )(P, P, P)


def _dsa_kernel(aq_ref, iq_ref, ak_ref, av_ref, ik_ref, iw_ref, slope_ref, o_ref, key_scr, hi_scr, lo_scr,
                vp_scr, *, bq, ck, topk):
    i = pl.program_id(1)
    n_full, tail_off = _chunk_plan(i, bq, ck)
    n_chunks = n_full + 1

    @pl.when(i == 0)
    def _():
        _fill_v_ones(av_ref, vp_scr)

    w_t = iw_ref[...].T
    gw = min(SB_GROUP, ck)
    rr = lax.broadcasted_iota(i32, (gw, gw), 0)
    cc = lax.broadcasted_iota(i32, (gw, gw), 1)
    incl = jnp.where(cc <= rr, 1.0, 0.0).astype(bf16)
    col = lax.broadcasted_iota(i32, (1, ck), 1).astype(f32)
    key_i = lax.broadcasted_iota(i32, (ck, bq), 0)
    qry_i = lax.broadcasted_iota(i32, (ck, bq), 1)
    tail_valid_t = key_i + (tail_off - i * bq) <= qry_i

    def score_chunk(c, valid_t):
        off = pl.multiple_of(c * ck, ck)
        ik = ik_ref[0, pl.ds(off, ck), :]
        sc = jnp.zeros((ck, bq), f32)
        for h in range(IDX_HEADS):
            sc = sc + jnp.maximum(_nt_dot(ik, iq_ref[h]), 0.0) * w_t[h:h + 1, :]
        if valid_t is not None:
            sc = jnp.where(valid_t, sc, NEG_INF)
        bits = lax.bitcast_convert_type(sc + 0.0, i32)
        key = jnp.where(bits < 0, bits ^ jnp.int32(0x7FFFFFFF), bits)
        key_scr[c] = key
        hi_scr[c] = lax.shift_right_arithmetic(key, 16).astype(i16)
        lo_scr[c] = ((key & 0xFFFF) - 32768).astype(i16)

    def score_body(c, _):
        score_chunk(c, None)
        return 0

    lax.fori_loop(0, n_full, score_body, 0)
    score_chunk(n_full, tail_valid_t)

    lanes_acc = 32

    def rows(v):
        return jnp.broadcast_to(v, (lanes_acc, bq))[None]

    def count16(ref, pred):
        def body(c, acc):
            v = ref[c].reshape(ck // lanes_acc, lanes_acc, bq)
            hit = jnp.where(pred(v), jnp.int16(1), jnp.int16(0))
            parts = [hit[g] for g in range(ck // lanes_acc)]
            while len(parts) > 1:
                parts = [a + b for a, b in zip(parts[0::2], parts[1::2])]
            return acc + parts[0]
        acc = lax.fori_loop(0, n_chunks, body, jnp.zeros((lanes_acc, bq), i16))
        return jnp.sum(acc.astype(f32), axis=0, keepdims=True)

    def search16(ref, need):
        def body(it, prefix):
            cand = prefix + lax.shift_left(jnp.int32(1), jnp.int32(15) - it)
            cand_r = rows(cand.astype(i16))
            cnt = count16(ref, lambda v: v >= cand_r)
            return jnp.where(cnt >= need, cand, prefix)
        return lax.fori_loop(0, 16, body, jnp.full((1, bq), I16_MIN, i32))

    tau_hi = search16(hi_scr, float(topk))
    tau_hi_r = rows(tau_hi.astype(i16))
    need_lo = float(topk) - count16(hi_scr, lambda v: v > tau_hi_r)
    tau_hi_b = tau_hi.astype(i16)

    def mask_lo(c, _):
        lo_scr[c] = jnp.where(hi_scr[c] == tau_hi_b, lo_scr[c], jnp.int16(I16_MIN))
        return 0

    lax.fori_loop(0, n_chunks, mask_lo, 0)
    tau_lo = search16(lo_scr, need_lo)
    tau = tau_hi * 65536 + (tau_lo + 32768)

    def count(pred):
        def body(c, acc):
            key = key_scr[c].reshape(ck // lanes_acc, lanes_acc, bq)
            return acc + jnp.sum(jnp.where(pred(key), 1.0, 0.0), axis=0)
        acc = lax.fori_loop(0, n_chunks, body, jnp.zeros((lanes_acc, bq), f32))
        return jnp.sum(acc, axis=0, keepdims=True)

    tau_r = rows(tau)
    n_take = float(topk) - count(lambda key: key > tau_r)

    def attn_chunk(c, carry, valid_t):
        off = pl.multiple_of(c * ck, ck)
        eq_run = carry[0]
        key = key_scr[c]
        eq = key == tau
        eqf = jnp.where(eq, 1.0, 0.0)
        pcs = []
        for g in range(ck // gw):
            e = eqf[g * gw:(g + 1) * gw, :]
            pcs.append(_dot(incl, e.astype(bf16)) + eq_run)
            eq_run = eq_run + jnp.sum(e, axis=0, keepdims=True)
        pc = jnp.concatenate(pcs, axis=0) if len(pcs) > 1 else pcs[0]
        addm_t = jnp.where(key > tau, 0.0, jnp.where(eq, jnp.where(pc <= n_take, 0.0, NEG_INF), NEG_INF))
        if valid_t is not None:
            addm_t = jnp.where(valid_t, addm_t, NEG_INF)
        addm = addm_t.T
        k = ak_ref[0, pl.ds(off, ck), :]
        vp = vp_scr[0, pl.ds(off, ck), :]
        pos = col + (off - i * bq).astype(f32)
        out = (eq_run,)
        for h in range(HEADS):
            m, acc = carry[1 + 2 * h:3 + 2 * h]
            s = _nt_dot(aq_ref[h], k) + slope_ref[h] * pos + addm
            m_new = jnp.maximum(m, jnp.max(s, axis=1, keepdims=True))
            m_safe = jnp.where(m_new == NEG_INF, 0.0, m_new)
            alpha = jnp.exp(m - m_safe)
            p = jnp.exp((s - m_safe).astype(bf16))
            out += (m_new, alpha * acc + _dot(p, vp))
        return out

    init = (jnp.zeros((1, bq), f32),) + _softmax_init_v1(bq) * HEADS
    carry = lax.fori_loop(0, n_full, lambda c, carry: attn_chunk(c, carry, None), init)
    carry = attn_chunk(n_full, carry, tail_valid_t)
    for h in range(HEADS):
        o_ref[h] = _softmax_finish_v1(carry[2 + 2 * h]).astype(o_ref.dtype)


def _dsa_attention(P, F, slopes, B, S, *, bq, ck, topk, gaq, giq, gak, gav, gik, iw_col):
    G, T, hw = P.shape
    nq = S // bq
    return pl.pallas_call(
        functools.partial(_dsa_kernel, bq=bq, ck=ck, topk=topk),
        grid=(B, nq),
        in_specs=[
            pl.BlockSpec((HEADS, bq, hw), lambda b, i: (gaq // HEADS, b * nq + i, 0)),
            pl.BlockSpec((IDX_HEADS, bq, hw), lambda b, i: (giq // IDX_HEADS, b * nq + i, 0)),
            pl.BlockSpec((1, S, hw), lambda b, i: (gak, b, 0)),
            pl.BlockSpec((1, S, hw), lambda b, i: (gav, b, 0)),
            pl.BlockSpec((1, S, hw), lambda b, i: (gik, b, 0)),
            pl.BlockSpec((bq, 128), lambda b, i: (b * nq + i, iw_col // 128)),
            pl.BlockSpec(slopes.shape, lambda b, i: (0, 0, 0)),
        ],
        out_specs=pl.BlockSpec((HEADS, bq, hw), lambda b, i: (0, b * nq + i, 0)),
        out_shape=jax.ShapeDtypeStruct((HEADS, T, hw), bf16),
        scratch_shapes=[pltpu.VMEM((S // ck, ck, bq), i32), pltpu.VMEM((S // ck, ck, bq), i16),
                        pltpu.VMEM((S // ck, ck, bq), i16), pltpu.VMEM((1, S, 2 * HEAD_W), bf16)],
        compiler_params=_cparams(("arbitrary", "arbitrary")),
        name="dsa_attention",
    )(P, P, P, P, P, F, slopes)


def _merge_kernel(x_ref, g_ref, sc_ref, sh_ref, g1_ref, ya_ref, yb_ref, yc_ref, yd_ref,
                  wg_ref, wb_ref, wo_ref, o_ref):
    x = x_ref[...]
    D = x.shape[1]
    h = _norm_mod(x, g_ref[...], sc_ref[0], sh_ref[0]).astype(bf16)
    merged = jnp.zeros(x.shape, f32)
    for n, y_ref in enumerate((ya_ref, yb_ref, yc_ref, yd_ref)):
        gate = jax.nn.sigmoid(_dot(h, wg_ref[:, n * D:(n + 1) * D]))
        y = jnp.concatenate([y_ref[hh] for hh in range(HEADS)], axis=1)
        merged = merged + gate * _dot(y, wb_ref[n])
    o_ref[...] = x + g1_ref[0] * _dot(merged.astype(bf16), wo_ref[...])


def _merge(x2, g, sc, sh, g1, ys, wg, wb, wo, S, *, tm):
    T, D = x2.shape
    tpb = S // tm
    row = pl.BlockSpec((tm, D), lambda i: (i, 0))
    per_b = pl.BlockSpec((1, 1, D), lambda i: (i // tpb, 0, 0))
    yspec = pl.BlockSpec((HEADS, tm, HEAD_W), lambda i: (0, i, 0))
    full = lambda a: pl.BlockSpec(a.shape, lambda i: (0,) * a.ndim)
    return pl.pallas_call(
        _merge_kernel,
        grid=(T // tm,),
        in_specs=[row, full(g), per_b, per_b, per_b, yspec, yspec, yspec, yspec, full(wg), full(wb), full(wo)],
        out_specs=row,
        out_shape=jax.ShapeDtypeStruct((T, D), f32),
        compiler_params=_cparams(("arbitrary",)),
        name="merge",
    )(x2, g, sc, sh, g1, *ys, wg, wb, wo)


def _finish(x, g2, f, fg_ref, o_ref):
    out = x + g2 * f
    if fg_ref is not None:
        out = out * lax.rsqrt(jnp.mean(out * out, axis=-1, keepdims=True) + EPS) * fg_ref[...]
    o_ref[...] = out


def _ffn_kernel(x_ref, g_ref, sc_ref, sh_ref, g2_ref, w1_ref, w3_ref, w2_ref, *rest, final):
    fg_ref = rest[0] if final else None
    o_ref, h_scr, acc_scr = rest[-3:]
    j = pl.program_id(1)

    @pl.when(j == 0)
    def _():
        h_scr[...] = _norm_mod(x_ref[...], g_ref[...], sc_ref[0], sh_ref[0]).astype(bf16)
        acc_scr[...] = jnp.zeros_like(acc_scr)

    h = h_scr[...]
    a = _dot(h, w1_ref[...])
    u = (a * jax.nn.sigmoid(a)) * _dot(h, w3_ref[...])
    acc_scr[...] += _dot(u.astype(bf16), w2_ref[...])

    @pl.when(j == pl.num_programs(1) - 1)
    def _():
        _finish(x_ref[...], g2_ref[0], acc_scr[...], fg_ref, o_ref)


def _ffn(x2, g, sc, sh, g2, w1, w3, w2, final_g, S, *, tm, tn):
    T, D = x2.shape
    F = w1.shape[1]
    tpb = S // tm
    row = pl.BlockSpec((tm, D), lambda i, j: (i, 0))
    per_b = pl.BlockSpec((1, 1, D), lambda i, j: (i // tpb, 0, 0))
    vec = pl.BlockSpec((1, D), lambda i, j: (0, 0))
    in_specs = [row, vec, per_b, per_b, per_b,
                pl.BlockSpec((D, tn), lambda i, j: (0, j)),
                pl.BlockSpec((D, tn), lambda i, j: (0, j)),
                pl.BlockSpec((tn, D), lambda i, j: (j, 0))]
    args = [x2, g, sc, sh, g2, w1, w3, w2]
    if final_g is not None:
        in_specs.append(vec)
        args.append(final_g)
    return pl.pallas_call(
        functools.partial(_ffn_kernel, final=final_g is not None),
        grid=(T // tm, F // tn),
        in_specs=in_specs,
        out_specs=row,
        out_shape=jax.ShapeDtypeStruct((T, D), f32),
        scratch_shapes=[pltpu.VMEM((tm, D), bf16), pltpu.VMEM((tm, D), f32)],
        compiler_params=_cparams(("arbitrary", "arbitrary")),
        name="ffn",
    )(*args)


def _router_kernel(x_ref, g_ref, sc_ref, sh_ref, rw_ref, rb_ref, comb_ref, cnt_ref):
    h = _norm_mod(x_ref[...], g_ref[...], sc_ref[0], sh_ref[0]).astype(bf16)
    logits = _dot(h, rw_ref[...]) + rb_ref[...]
    lane = lax.broadcasted_iota(i32, logits.shape, 1)
    big = jnp.int32(logits.shape[1])
    m1 = jnp.max(logits, axis=1, keepdims=True)
    i1 = jnp.min(jnp.where(logits == m1, lane, big), axis=1, keepdims=True)
    rest_l = jnp.where(lane == i1, NEG_INF, logits)
    m2 = jnp.max(rest_l, axis=1, keepdims=True)
    i2 = jnp.min(jnp.where(rest_l == m2, lane, big), axis=1, keepdims=True)
    e2 = jnp.exp(m2 - m1)
    den = 1.0 + e2
    comb = jnp.where(lane == i1, 1.0 / den, jnp.where(lane == i2, e2 / den, 0.0))
    comb_ref[...] = comb
    cnt_ref[0] = jnp.sum(jnp.where(comb > 0.0, 1.0, 0.0), axis=0, keepdims=True)


def _router(x2, g, sc, sh, rw, rb, S, *, tm):
    T, D = x2.shape
    tpb = S // tm
    nt = T // tm
    return pl.pallas_call(
        _router_kernel,
        grid=(nt,),
        in_specs=[pl.BlockSpec((tm, D), lambda i: (i, 0)),
                  pl.BlockSpec((1, D), lambda i: (0, 0)),
                  pl.BlockSpec((1, 1, D), lambda i: (i // tpb, 0, 0)),
                  pl.BlockSpec((1, 1, D), lambda i: (i // tpb, 0, 0)),
                  pl.BlockSpec(rw.shape, lambda i: (0, 0)),
                  pl.BlockSpec(rb.shape, lambda i: (0, 0))],
        out_specs=[pl.BlockSpec((tm, 128), lambda i: (i, 0)),
                   pl.BlockSpec((1, 1, 128), lambda i: (i, 0, 0))],
        out_shape=[jax.ShapeDtypeStruct((T, 128), f32), jax.ShapeDtypeStruct((nt, 1, 128), f32)],
        compiler_params=_cparams(("arbitrary",)),
        name="router",
    )(x2, g, sc, sh, rw, rb)


def _moe_kernel(sid_ref, nch_ref, x_ref, g_ref, sc_ref, sh_ref, g2_ref, comb_ref, w1_ref, w3_ref, w2_ref, *rest,
                final, sizes):
    fg_ref = rest[0] if final else None
    o_ref, h_scr, tri_scr, rank_scr, rank_t_scr, xg_scr, yg_scr, out_scr = rest[-8:]
    i = pl.program_id(0)
    e = pl.program_id(1)
    j = pl.program_id(2)
    n_e = pl.num_programs(1)
    tm = x_ref.shape[0]
    sid = sid_ref[i * n_e + e]
    nch = nch_ref[i * n_e + e]

    def dispatch(phase):
        for k, r in enumerate(sizes):
            @pl.when(sid == k)
            def _(r=r):
                phase(r, 0)

        @pl.when(sid == len(sizes))
        def _():
            def body(c, _):
                phase(sizes[-1], c)
                return 0
            lax.fori_loop(0, nch, body, 0)

    def chunk_rows(rows, c):
        if isinstance(c, int):
            return pl.ds(c * rows, rows)
        return pl.ds(pl.multiple_of(c * rows, 16), rows)

    def first_slot(rows, c):
        return float(c * rows) if isinstance(c, int) else (c * rows).astype(f32)

    @pl.when(jnp.logical_and(i == 0, jnp.logical_and(e == 0, j == 0)))
    def _():
        rr = lax.broadcasted_iota(i32, (tm, tm), 0)
        cc = lax.broadcasted_iota(i32, (tm, tm), 1)
        tri_scr[...] = jnp.where(cc < rr, 1.0, 0.0).astype(bf16)

    @pl.when(jnp.logical_and(e == 0, j == 0))
    def _():
        h_scr[...] = _norm_mod(x_ref[...], g_ref[...], sc_ref[0], sh_ref[0]).astype(bf16)
        out_scr[...] = jnp.zeros_like(out_scr)
        member = jnp.where(comb_ref[...] > 0.0, 1.0, 0.0)
        rank = jnp.where(member > 0.0, _dot(tri_scr[...], member.astype(bf16)), -1.0)
        rank_scr[...] = rank
        rank_t_scr[...] = rank.T

    @pl.when(j == 0)
    def _():
        rank_row = rank_t_scr[pl.ds(e, 1), :]

        def gather(rows, c):
            slot = lax.broadcasted_iota(i32, (rows, tm), 0).astype(f32) + first_slot(rows, c)
            pick = jnp.where(slot == rank_row, 1.0, 0.0).astype(bf16)
            xg_scr[chunk_rows(rows, c), :] = _dot(pick, h_scr[...]).astype(bf16)
            yg_scr[chunk_rows(rows, c), :] = jnp.zeros((rows, yg_scr.shape[1]), f32)

        dispatch(gather)

    def expert(rows, c):
        xg = xg_scr[chunk_rows(rows, c), :]
        a = _dot(xg, w1_ref[0])
        u = (a * jax.nn.sigmoid(a)) * _dot(xg, w3_ref[0])
        yg_scr[chunk_rows(rows, c), :] += _dot(u.astype(bf16), w2_ref[0])

    dispatch(expert)

    last_j = j == pl.num_programs(2) - 1

    @pl.when(last_j)
    def _():
        lane = lax.broadcasted_iota(i32, (tm, 128), 1)
        sel = lane == e
        rank_col = jnp.sum(jnp.where(sel, rank_scr[...], 0.0), axis=1, keepdims=True)
        w_col = jnp.sum(jnp.where(sel, comb_ref[...], 0.0), axis=1, keepdims=True)

        def scatter(rows, c):
            slot = lax.broadcasted_iota(i32, (tm, rows), 1).astype(f32) + first_slot(rows, c)
            place = jnp.where(slot == rank_col, 1.0, 0.0).astype(bf16)
            out_scr[...] += w_col * _dot(place, yg_scr[chunk_rows(rows, c), :].astype(bf16))

        dispatch(scatter)

    @pl.when(jnp.logical_and(last_j, e == n_e - 1))
    def _():
        _finish(x_ref[...], g2_ref[0], out_scr[...], fg_ref, o_ref)


def _moe_plan(cnt, sizes):
    cnt = cnt.astype(i32).reshape(-1)
    sid = sum((cnt > r).astype(i32) for r in sizes)
    nch = (cnt + (sizes[-1] - 1)) // sizes[-1]
    return sid, nch


def _moe(x2, g, sc, sh, g2, comb, sid, nch, w1, w3, w2, final_g, S, *, tm, tn, sizes):
    T, D = x2.shape
    E, _, F = w1.shape
    nj = F // tn
    tpb = S // tm
    cap = -(-tm // sizes[-1]) * sizes[-1]
    row = pl.BlockSpec((tm, D), lambda i, e, j, *_: (i, 0))
    per_b = pl.BlockSpec((1, 1, D), lambda i, e, j, *_: (i // tpb, 0, 0))
    vec = pl.BlockSpec((1, D), lambda i, e, j, *_: (0, 0))
    in_specs = [row, vec, per_b, per_b, per_b,
                pl.BlockSpec((tm, 128), lambda i, e, j, *_: (i, 0)),
                pl.BlockSpec((1, D, tn), lambda i, e, j, *_: (e, 0, j)),
                pl.BlockSpec((1, D, tn), lambda i, e, j, *_: (e, 0, j)),
                pl.BlockSpec((1, tn, D), lambda i, e, j, *_: (e, j, 0))]
    args = [x2, g, sc, sh, g2, comb, w1, w3, w2]
    if final_g is not None:
        in_specs.append(vec)
        args.append(final_g)
    return pl.pallas_call(
        functools.partial(_moe_kernel, final=final_g is not None, sizes=tuple(sizes)),
        grid_spec=pltpu.PrefetchScalarGridSpec(
            num_scalar_prefetch=2,
            grid=(T // tm, E, nj),
            in_specs=in_specs,
            out_specs=row,
            scratch_shapes=[pltpu.VMEM((tm, D), bf16),
                            pltpu.VMEM((tm, tm), bf16),
                            pltpu.VMEM((tm, 128), f32),
                            pltpu.VMEM((128, tm), f32),
                            pltpu.VMEM((cap, D), bf16),
                            pltpu.VMEM((cap, D), f32),
                            pltpu.VMEM((tm, D), f32)]),
        out_shape=jax.ShapeDtypeStruct((T, D), f32),
        compiler_params=_cparams(("arbitrary", "arbitrary", "arbitrary")),
        name="moe",
    )(sid, nch, *args)


_IN_SIZES = (256, 64, 64, 256, 64, 4, 256, 128, 32, 256, 256, 256, 256, 256, 256, 4096)
_IN_NAMES = ("aq", "ak", "av", "iq", "ik", "iw", "cq", "ckv", "kr", "sq", "sk", "sv", "dq", "dk", "dv", "gates")

G_AQ, G_IQ, G_SQ, G_SK, G_SV, G_DQ, G_DK, G_DV, G_AK, G_AV, G_IK = 0, 4, 8, 12, 16, 20, 24, 28, 32, 33, 34
N_GROUPS = 36
F_IW_COL = 640


def _split_w_in(w):
    out, off = {}, 0
    for name, n in zip(_IN_NAMES, _IN_SIZES):
        out[name] = w[:, off:off + n]
        off += n
    return out


def _rot_half_cols(w):
    half = w.shape[1] // 2
    return jnp.concatenate([-w[:, half:], w[:, :half]], axis=1)


def _layer_weights(w_in, w_uq, w_ukv):
    c = _split_w_in(w_in)
    D = w_in.shape[0]
    z = lambda n: jnp.zeros((D, n), f32)
    w_main = jnp.concatenate([
        c["aq"] * 0.125, c["iq"] * 0.125, c["sq"] * 0.125, c["sk"], c["sv"],
        c["dq"] * (DIFF_QK ** -0.5), c["dk"], c["dv"], c["ak"], c["av"], c["ik"], z(64)], axis=1).astype(bf16)
    w_f = jnp.concatenate([
        c["cq"], c["ckv"],
        z(64), c["kr"], z(32),
        z(64), _rot_half_cols(c["kr"]), z(32),
        c["iw"] * (IDX_HEADS ** -0.5), z(124)], axis=1).astype(bf16)
    w_gates = c["gates"].astype(bf16)

    qr = w_uq.shape[0]
    kvr = w_ukv.shape[0]
    wq, wqp, wk, wv = [], [], [], []
    per_q = MLA_NOPE + MLA_ROPE
    per_kv = MLA_NOPE + MLA_V
    for h in range(HEADS):
        nope = w_uq[:, h * per_q:h * per_q + MLA_NOPE]
        rope = w_uq[:, h * per_q + MLA_NOPE:(h + 1) * per_q]
        wq += [nope, rope, jnp.zeros((qr, 32), f32)]
        wqp += [jnp.zeros((qr, 64), f32), _rot_half_cols(rope), jnp.zeros((qr, 32), f32)]
        wk += [w_ukv[:, h * per_kv:h * per_kv + MLA_NOPE], jnp.zeros((kvr, 64), f32)]
        wv += [w_ukv[:, h * per_kv + MLA_NOPE:(h + 1) * per_kv]]
    cat = lambda xs: jnp.concatenate(xs, axis=1).astype(bf16)
    return w_main, w_f, w_gates, cat(wq), cat(wqp), cat(wk), cat(wv)


def _alibi_slopes():
    n = 2 * HEADS
    sl = [2.0 ** (-(8.0 / n) * (k + 1)) for k in range(n)]
    return sl[0::2], sl[1::2]


def kernel(x, c, positions, ada_w, ada_b, norm1_g, norm2_g, w_in, mla_q_norm_g, mla_kv_norm_g, mla_w_uq,
           mla_w_ukv, diff_lq1, diff_lk1, diff_lq2, diff_lk2, diff_norm_g, w_branch, w_out, ffn_w1, ffn_w3,
           ffn_w2, router_w, router_b, moe_w1, moe_w3, moe_w2, final_norm_g):
    B, S, D = x.shape
    T = B * S
    depth = ada_w.shape[0]
    topk = min(TOPK_MAX, S // 4)
    tm = min(1024, S)
    tm_merge = min(512, S)
    bq = min(512, S)
    ck = min(512, S)

    x2 = x.reshape(T, D)
    pos2 = positions.reshape(T, 1)
    mod = _modulation(c, ada_w, ada_b)

    half = MLA_ROPE // 2
    inv = ROPE_THETA ** (-jnp.arange(half, dtype=f32) / half)
    inv_full = jnp.concatenate([jnp.zeros((64,), f32), inv, inv, jnp.zeros((32,), f32)]).reshape(1, 128)
    cos_t, sin_t = _rope_tables(pos2, inv_full, tm=tm)
    sl_a, sl_d = _alibi_slopes()
    slopes_a = jnp.broadcast_to(jnp.asarray(sl_a, f32)[:, None, None], (HEADS, 1, ck))
    slopes_d = jnp.broadcast_to(jnp.asarray(sl_d, f32)[:, None, None], (HEADS, 1, ck))

    row = lambda v: v.reshape(1, -1)
    for l in range(depth):
        sh1, sc1, g1, sh2, sc2, g2 = [mod[l, :, k * D:(k + 1) * D].reshape(B, 1, D) for k in range(6)]
        lam_init = 0.8 - 0.6 * math.exp(-0.3 * l)
        w_main, w_f, w_gates, wq, wqp, wk, wv = _layer_weights(w_in[l], mla_w_uq[l], mla_w_ukv[l])
        n1 = row(norm1_g[l])

        P = _inproj(x2, n1, sc1, sh1, w_main, S, grouped=True, out_dtype=bf16, tm=tm, tn=INPROJ_TN)
        F = _inproj(x2, n1, sc1, sh1, w_f, S, grouped=False, out_dtype=f32, tm=tm, tn=w_f.shape[1])
        qm, km, vm = _mla_prep(F, cos_t, sin_t, row(mla_q_norm_g[l]), row(mla_kv_norm_g[l]), wq, wqp, wk, wv,
                               tm=tm)

        ya = _dsa_attention(P, F, slopes_a, B, S, bq=bq, ck=ck, topk=topk, gaq=G_AQ, giq=G_IQ, gak=G_AK,
                            gav=G_AV, gik=G_IK, iw_col=F_IW_COL)
        yb = _mla_attention(qm, km, vm, B, S, bq=bq, ck=ck)
        yc = _sb_attention(P, B, S, bq=bq, ck=ck, gq=G_SQ, gk=G_SK, gv=G_SV)
        yd = _diff_attention(P, slopes_d, row(diff_lq1[l]), row(diff_lk1[l]), row(diff_lq2[l]),
                             row(diff_lk2[l]), row(diff_norm_g[l]), B, S, bq=bq, ck=ck, lam_init=lam_init,
                             gq=G_DQ, gk=G_DK, gv=G_DV)

        x2 = _merge(x2, n1, sc1, sh1, g1, (ya, yb, yc, yd), w_gates, w_branch[l].astype(bf16),
                    w_out[l].astype(bf16), S, tm=tm_merge)

        final_g = row(final_norm_g) if l == depth - 1 else None
        n2 = row(norm2_g[l])
        if l % 2 == 0:
            k = l // 2
            x2 = _ffn(x2, n2, sc2, sh2, g2, ffn_w1[k].astype(bf16), ffn_w3[k].astype(bf16),
                      ffn_w2[k].astype(bf16), final_g, S, tm=min(FFN_TM, S), tn=FFN_TN)
        else:
            k = l // 2
            E = router_w.shape[2]
            rw = jnp.concatenate([router_w[k], jnp.zeros((D, 128 - E), f32)], axis=1).astype(bf16)
            rb = jnp.concatenate([router_b[k], jnp.full((128 - E,), NEG_INF, f32)]).reshape(1, 128)
            comb, cnt = _router(x2, n2, sc2, sh2, rw, rb, S, tm=tm)
            sid, nch = _moe_plan(cnt[:, 0, :E], MOE_SIZES)
            x2 = _moe(x2, n2, sc2, sh2, g2, comb, sid, nch, moe_w1[k].astype(bf16), moe_w3[k].astype(bf16),
                      moe_w2[k].astype(bf16), final_g, S, tm=tm, tn=MOE_TN, sizes=MOE_SIZES)
    return x2.reshape(B, S, D)
```

```python
import functools
import math

import jax
import jax.numpy as jnp
from jax import lax
from jax.experimental import pallas as pl
from jax.experimental.pallas import tpu as pltpu

f32 = jnp.float32
bf16 = jnp.bfloat16
i32 = jnp.int32
i16 = jnp.int16

N_BRANCHES = 4
HEADS = 4
HEAD_W = 64
IDX_HEADS = 4
TOPK_MAX = 256
MLA_Q_RANK = 256
MLA_KV_RANK = 128
MLA_NOPE = 64
MLA_ROPE = 32
MLA_V = 64
MLA_QK_PAD = 128
ROPE_THETA = 10000.0
DIFF_QK = 32
N_EXPERTS = 8
TOP_K = 2
INPROJ_TN = 768
FFN_TN = 1408
FFN_TM = 512
MOE_TN = 1792
MOE_SIZES = (128, 192, 256, 320, 384, 448, 512)
EPS = 1e-6
NEG_INF = float("-inf")
LOG2E = math.log2(math.e)
I16_MIN = -2 ** 15

VMEM_LIMIT = 56 * 1024 * 1024


def _cparams(sem):
    return pltpu.CompilerParams(dimension_semantics=sem, vmem_limit_bytes=VMEM_LIMIT)


def _nt_dot(a, b):
    return lax.dot_general(a, b, (((1,), (1,)), ((), ())), preferred_element_type=f32)


def _dot(a, b):
    return jnp.dot(a, b, preferred_element_type=f32)


def _norm_mod(x, g, sc, sh):
    y = x * lax.rsqrt(jnp.mean(x * x, axis=-1, keepdims=True) + EPS)
    return (y * g) * (1.0 + sc) + sh


def _mod_kernel(c_ref, w_ref, b_ref, o_ref):
    c = c_ref[...]
    cond = (c * jax.nn.sigmoid(c)).astype(bf16)
    o_ref[0] = _dot(cond, w_ref[0].astype(bf16)) + b_ref[0]


def _modulation(c, ada_w, ada_b):
    L, D, N = ada_w.shape
    B = c.shape[0]
    tn = 1536
    return pl.pallas_call(
        _mod_kernel,
        grid=(L, N // tn),
        in_specs=[
            pl.BlockSpec((B, D), lambda l, j: (0, 0)),
            pl.BlockSpec((1, D, tn), lambda l, j: (l, 0, j)),
            pl.BlockSpec((1, 1, tn), lambda l, j: (l, 0, j)),
        ],
        out_specs=pl.BlockSpec((1, B, tn), lambda l, j: (l, 0, j)),
        out_shape=jax.ShapeDtypeStruct((L, B, N), f32),
        compiler_params=_cparams(("arbitrary", "arbitrary")),
        name="modulation",
    )(c, ada_w, ada_b.reshape(L, 1, N))


def _inproj_kernel(x_ref, g_ref, sc_ref, sh_ref, w_ref, o_ref, h_scr, *, grouped):
    @pl.when(pl.program_id(1) == 0)
    def _():
        h_scr[...] = _norm_mod(x_ref[...], g_ref[...], sc_ref[0], sh_ref[0]).astype(bf16)

    res = _dot(h_scr[...], w_ref[...])
    if grouped:
        for k in range(res.shape[1] // HEAD_W):
            o_ref[k] = res[:, k * HEAD_W:(k + 1) * HEAD_W].astype(o_ref.dtype)
    else:
        o_ref[...] = res.astype(o_ref.dtype)


def _inproj(x2, g, sc, sh, w, S, *, grouped, out_dtype, tm, tn):
    T, D = x2.shape
    N = w.shape[1]
    tpb = S // tm
    if grouped:
        out_shape = jax.ShapeDtypeStruct((N // HEAD_W, T, HEAD_W), out_dtype)
        out_spec = pl.BlockSpec((tn // HEAD_W, tm, HEAD_W), lambda i, j: (j, i, 0))
    else:
        out_shape = jax.ShapeDtypeStruct((T, N), out_dtype)
        out_spec = pl.BlockSpec((tm, tn), lambda i, j: (i, j))
    return pl.pallas_call(
        functools.partial(_inproj_kernel, grouped=grouped),
        grid=(T // tm, N // tn),
        in_specs=[
            pl.BlockSpec((tm, D), lambda i, j: (i, 0)),
            pl.BlockSpec((1, D), lambda i, j: (0, 0)),
            pl.BlockSpec((1, 1, D), lambda i, j: (i // tpb, 0, 0)),
            pl.BlockSpec((1, 1, D), lambda i, j: (i // tpb, 0, 0)),
            pl.BlockSpec((D, tn), lambda i, j: (0, j)),
        ],
        out_specs=out_spec,
        out_shape=out_shape,
        scratch_shapes=[pltpu.VMEM((tm, D), bf16)],
        compiler_params=_cparams(("arbitrary", "arbitrary")),
        name="inproj_grouped" if grouped else "inproj_plain",
    )(x2, g, sc, sh, w)


def _rope_kernel(pos_ref, inv_ref, cos_ref, sin_ref):
    ang = pos_ref[...].astype(f32) * inv_ref[...]
    cos_ref[...] = jnp.cos(ang)
    sin_ref[...] = jnp.sin(ang)


def _rope_tables(pos2, inv_full, *, tm):
    T = pos2.shape[0]
    tile = pl.BlockSpec((tm, 128), lambda i: (i, 0))
    return pl.pallas_call(
        _rope_kernel,
        grid=(T // tm,),
        in_specs=[pl.BlockSpec((tm, 1), lambda i: (i, 0)), pl.BlockSpec((1, 128), lambda i: (0, 0))],
        out_specs=[tile, tile],
        out_shape=[jax.ShapeDtypeStruct((T, 128), f32)] * 2,
        compiler_params=_cparams(("arbitrary",)),
        name="rope_tables",
    )(pos2, inv_full)


def _mla_prep_kernel(f_ref, cos_ref, sin_ref, gq_ref, gkv_ref, wq_ref, wqp_ref, wk_ref, wv_ref,
                     q_ref, k_ref, v_ref, *, scale):
    cq = f_ref[:, 0:MLA_Q_RANK]
    ckv = f_ref[:, MLA_Q_RANK:MLA_Q_RANK + MLA_KV_RANK]
    kr = f_ref[:, 384:512]
    krp = f_ref[:, 512:640]
    nq = (cq * lax.rsqrt(jnp.mean(cq * cq, axis=-1, keepdims=True) + EPS) * gq_ref[...]).astype(bf16)
    nkv = (ckv * lax.rsqrt(jnp.mean(ckv * ckv, axis=-1, keepdims=True) + EPS) * gkv_ref[...]).astype(bf16)
    cosf = cos_ref[...]
    sinf = sin_ref[...]
    q1 = _dot(nq, wq_ref[...])
    q2 = _dot(nq, wqp_ref[...])
    kn = _dot(nkv, wk_ref[...])
    vv = _dot(nkv, wv_ref[...])
    krope = kr * cosf + krp * sinf
    for h in range(HEADS):
        sl = slice(h * MLA_QK_PAD, (h + 1) * MLA_QK_PAD)
        q_ref[h] = ((q1[:, sl] * cosf + q2[:, sl] * sinf) * scale).astype(bf16)
        k_ref[h] = (kn[:, sl] + krope).astype(bf16)
        v_ref[h] = vv[:, h * MLA_V:(h + 1) * MLA_V].astype(bf16)


def _mla_prep(F, cos_t, sin_t, gq, gkv, wq, wqp, wk, wv, *, tm):
    T = F.shape[0]
    scale = (MLA_NOPE + MLA_ROPE) ** -0.5
    full = lambda a: pl.BlockSpec(a.shape, lambda i: (0,) * a.ndim)
    return pl.pallas_call(
        functools.partial(_mla_prep_kernel, scale=scale),
        grid=(T // tm,),
        in_specs=[
            pl.BlockSpec((tm, F.shape[1]), lambda i: (i, 0)),
            pl.BlockSpec((tm, 128), lambda i: (i, 0)),
            pl.BlockSpec((tm, 128), lambda i: (i, 0)),
            full(gq), full(gkv), full(wq), full(wqp), full(wk), full(wv),
        ],
        out_specs=[
            pl.BlockSpec((HEADS, tm, MLA_QK_PAD), lambda i: (0, i, 0)),
            pl.BlockSpec((HEADS, tm, MLA_QK_PAD), lambda i: (0, i, 0)),
            pl.BlockSpec((HEADS, tm, MLA_V), lambda i: (0, i, 0)),
        ],
        out_shape=[
            jax.ShapeDtypeStruct((HEADS, T, MLA_QK_PAD), bf16),
            jax.ShapeDtypeStruct((HEADS, T, MLA_QK_PAD), bf16),
            jax.ShapeDtypeStruct((HEADS, T, MLA_V), bf16),
        ],
        compiler_params=_cparams(("arbitrary",)),
        name="mla_prep",
    )(F, cos_t, sin_t, gq, gkv, wq, wqp, wk, wv)


def _chunk_plan(i, bq, ck):
    n_full = (i * bq) // ck
    return n_full, n_full * ck


def _tail_valid(i, bq, ck, tail_off):
    row = lax.broadcasted_iota(i32, (bq, ck), 0)
    col = lax.broadcasted_iota(i32, (bq, ck), 1)
    return col + (tail_off - i * bq) <= row


def _fill_v_ones(v_ref, vp_scr):
    for h in range(v_ref.shape[0]):
        vp_scr[h, :, 0:HEAD_W] = v_ref[h]
        vp_scr[h, :, HEAD_W:2 * HEAD_W] = jnp.ones((v_ref.shape[1], HEAD_W), bf16)


def _softmax_step_v1(s, vp, m, acc):
    m_new = jnp.maximum(m, jnp.max(s, axis=1, keepdims=True))
    alpha = jnp.exp(m - m_new)
    p = jnp.exp((s - m_new).astype(bf16))
    return m_new, alpha * acc + _dot(p, vp)


def _softmax_init_v1(bq):
    return (jnp.full((bq, 1), NEG_INF, f32), jnp.zeros((bq, 2 * HEAD_W), f32))


def _softmax_finish_v1(acc):
    return acc[:, 0:HEAD_W] / acc[:, HEAD_W:HEAD_W + 1]


def _mla_attn_kernel(q_ref, k_ref, v_ref, o_ref, vp_scr, *, bq, ck):
    i = pl.program_id(1)
    n_full, tail_off = _chunk_plan(i, bq, ck)

    @pl.when(i == 0)
    def _():
        _fill_v_ones(v_ref, vp_scr)

    def chunk(off, carry, valid):
        off = pl.multiple_of(off, ck)
        out = ()
        for h in range(HEADS):
            s = _nt_dot(q_ref[h], k_ref[h, pl.ds(off, ck), :])
            if valid is not None:
                s = jnp.where(valid, s, NEG_INF)
            out += _softmax_step_v1(s, vp_scr[h, pl.ds(off, ck), :], *carry[2 * h:2 * h + 2])
        return out

    carry = lax.fori_loop(0, n_full, lambda c, carry: chunk(c * ck, carry, None), _softmax_init_v1(bq) * HEADS)
    carry = chunk(tail_off, carry, _tail_valid(i, bq, ck, tail_off))
    for h in range(HEADS):
        o_ref[h] = _softmax_finish_v1(carry[2 * h + 1]).astype(o_ref.dtype)


def _mla_attention(qm, km, vm, B, S, *, bq, ck):
    H, T, dk = qm.shape
    dv = vm.shape[2]
    nq = S // bq
    return pl.pallas_call(
        functools.partial(_mla_attn_kernel, bq=bq, ck=ck),
        grid=(B, nq),
        in_specs=[
            pl.BlockSpec((H, bq, dk), lambda b, i: (0, b * nq + i, 0)),
            pl.BlockSpec((H, S, dk), lambda b, i: (0, b, 0)),
            pl.BlockSpec((H, S, dv), lambda b, i: (0, b, 0)),
        ],
        out_specs=pl.BlockSpec((H, bq, dv), lambda b, i: (0, b * nq + i, 0)),
        out_shape=jax.ShapeDtypeStruct((H, T, dv), bf16),
        scratch_shapes=[pltpu.VMEM((H, S, 2 * HEAD_W), bf16)],
        compiler_params=_cparams(("arbitrary", "arbitrary")),
        name="mla_attention",
    )(qm, km, vm)


def _diff_attn_kernel(q_ref, k_ref, v_ref, slope_ref, lq1_ref, lk1_ref, lq2_ref, lk2_ref, g_ref, o_ref,
                      vp_scr, *, bq, ck, lam_init):
    i = pl.program_id(1)
    n_full, tail_off = _chunk_plan(i, bq, ck)
    lane = lax.broadcasted_iota(i32, (bq, q_ref.shape[2]), 1)
    col = lax.broadcasted_iota(i32, (1, ck), 1).astype(f32)

    @pl.when(i == 0)
    def _():
        _fill_v_ones(v_ref, vp_scr)

    def chunk(off, carry, valid):
        off = pl.multiple_of(off, ck)
        pos = col + (off - i * bq).astype(f32)
        out = ()
        for h in range(HEADS):
            q = q_ref[h]
            zero = jnp.zeros_like(q)
            k = k_ref[h, pl.ds(off, ck), :]
            vp = vp_scr[h, pl.ds(off, ck), :]
            bias = slope_ref[h] * pos
            for mp, qm in enumerate((jnp.where(lane < DIFF_QK, q, zero), jnp.where(lane < DIFF_QK, zero, q))):
                s = _nt_dot(qm, k) + bias
                if valid is not None:
                    s = jnp.where(valid, s, NEG_INF)
                base = 2 * (2 * h + mp)
                out += _softmax_step_v1(s, vp, *carry[base:base + 2])
        return out

    carry = lax.fori_loop(0, n_full, lambda c, carry: chunk(c * ck, carry, None),
                          _softmax_init_v1(bq) * (2 * HEADS))
    carry = chunk(tail_off, carry, _tail_valid(i, bq, ck, tail_off))
    lam = (jnp.exp(jnp.sum(lq1_ref[...] * lk1_ref[...], axis=1, keepdims=True))
           - jnp.exp(jnp.sum(lq2_ref[...] * lk2_ref[...], axis=1, keepdims=True)) + lam_init)
    for h in range(HEADS):
        m0, a0, m1, a1 = carry[4 * h:4 * h + 4]
        o = _softmax_finish_v1(a0) - lam * _softmax_finish_v1(a1)
        y = o * lax.rsqrt(jnp.mean(o * o, axis=-1, keepdims=True) + EPS) * g_ref[...]
        o_ref[h] = (y * (1.0 - lam_init)).astype(o_ref.dtype)


def _diff_attention(P, slopes, lq1, lk1, lq2, lk2, g, B, S, *, bq, ck, lam_init, gq, gk, gv):
    G, T, hw = P.shape
    nq = S // bq
    vec = lambda a: pl.BlockSpec(a.shape, lambda b, i: (0,) * a.ndim)
    return pl.pallas_call(
        functools.partial(_diff_attn_kernel, bq=bq, ck=ck, lam_init=lam_init),
        grid=(B, nq),
        in_specs=[
            pl.BlockSpec((HEADS, bq, hw), lambda b, i: (gq // HEADS, b * nq + i, 0)),
            pl.BlockSpec((HEADS, S, hw), lambda b, i: (gk // HEADS, b, 0)),
            pl.BlockSpec((HEADS, S, hw), lambda b, i: (gv // HEADS, b, 0)),
            vec(slopes), vec(lq1), vec(lk1), vec(lq2), vec(lk2), vec(g),
        ],
        out_specs=pl.BlockSpec((HEADS, bq, hw), lambda b, i: (0, b * nq + i, 0)),
        out_shape=jax.ShapeDtypeStruct((HEADS, T, hw), bf16),
        scratch_shapes=[pltpu.VMEM((HEADS, S, 2 * HEAD_W), bf16)],
        compiler_params=_cparams(("arbitrary", "arbitrary")),
        name="diff_attention",
    )(P, P, P, slopes, lq1, lk1, lq2, lk2, g)


SB_GROUP = 256


def _sb_attn_kernel(q_ref, k_ref, v_ref, o_ref, *, bq, ck):
    i = pl.program_id(1)
    n_full, tail_off = _chunk_plan(i, bq, ck)
    dv = v_ref.shape[2]
    gw = min(SB_GROUP, ck)
    rr = lax.broadcasted_iota(i32, (gw, gw), 0)
    cc = lax.broadcasted_iota(i32, (gw, gw), 1)
    upper = jnp.where(rr > cc, 1.0, 0.0).astype(bf16)

    def chunk(off, carry, strict):
        off = pl.multiple_of(off, ck)
        out = ()
        for h in range(HEADS):
            run, acc = carry[2 * h:2 * h + 2]
            z = _nt_dot(q_ref[h], k_ref[h, pl.ds(off, ck), :])
            lsm = -(jnp.maximum(z, 0.0) + jnp.log(1.0 + jnp.exp2(jnp.abs(z) * (-LOG2E))))
            if strict is not None:
                lsm = jnp.where(strict, lsm, 0.0)
            parts = []
            for g in reversed(range(ck // gw)):
                x = lsm[:, g * gw:(g + 1) * gw]
                parts.append(_dot(x.astype(bf16), upper) + run)
                run = run + jnp.sum(x, axis=1, keepdims=True)
            between = jnp.concatenate(parts[::-1], axis=1) if len(parts) > 1 else parts[0]
            arg = z + lsm + between
            if strict is not None:
                arg = jnp.where(strict, arg, NEG_INF)
            acc = acc + _dot(jnp.exp(arg.astype(bf16)), v_ref[h, pl.ds(off, ck), :])
            out += (run, acc)
        return out

    row = lax.broadcasted_iota(i32, (bq, ck), 0)
    col = lax.broadcasted_iota(i32, (bq, ck), 1)
    strict = col + (tail_off - i * bq) < row
    init = (jnp.zeros((bq, 1), f32), jnp.zeros((bq, dv), f32)) * HEADS
    carry = chunk(tail_off, init, strict)
    carry = lax.fori_loop(0, n_full, lambda n, carry: chunk((n_full - 1 - n) * ck, carry, None), carry)
    for h in range(HEADS):
        o_ref[h] = carry[2 * h + 1].astype(o_ref.dtype)


def _sb_attention(P, B, S, *, bq, ck, gq, gk, gv):
    G, T, hw = P.shape
    nq = S // bq
    return pl.pallas_call(
        functools.partial(_sb_attn_kernel, bq=bq, ck=ck),
        grid=(B, nq),
        in_specs=[
            pl.BlockSpec((HEADS, bq, hw), lambda b, i: (gq // HEADS, b * nq + i, 0)),
            pl.BlockSpec((HEADS, S, hw), lambda b, i: (gk // HEADS, b, 0)),
            pl.BlockSpec((HEADS, S, hw), lambda b, i: (gv // HEADS, b, 0)),
        ],
        out_specs=pl.BlockSpec((HEADS, bq, hw), lambda b, i: (0, b * nq + i, 0)),
        out_shape=jax.ShapeDtypeStruct((HEADS, T, hw), bf16),
        compiler_params=_cparams(("arbitrary", "arbitrary")),
        name="sb_attention",
    )(P, P, P)


def _dsa_kernel(aq_ref, iq_ref, ak_ref, av_ref, ik_ref, iw_ref, slope_ref, o_ref, key_scr, hi_scr, lo_scr,
                vp_scr, *, bq, ck, topk):
    i = pl.program_id(1)
    n_full, tail_off = _chunk_plan(i, bq, ck)
    n_chunks = n_full + 1

    @pl.when(i == 0)
    def _():
        _fill_v_ones(av_ref, vp_scr)

    w_t = iw_ref[...].T
    gw = min(SB_GROUP, ck)
    rr = lax.broadcasted_iota(i32, (gw, gw), 0)
    cc = lax.broadcasted_iota(i32, (gw, gw), 1)
    incl = jnp.where(cc <= rr, 1.0, 0.0).astype(bf16)
    col = lax.broadcasted_iota(i32, (1, ck), 1).astype(f32)
    key_i = lax.broadcasted_iota(i32, (ck, bq), 0)
    qry_i = lax.broadcasted_iota(i32, (ck, bq), 1)
    tail_valid_t = key_i + (tail_off - i * bq) <= qry_i

    def score_chunk(c, valid_t):
        off = pl.multiple_of(c * ck, ck)
        ik = ik_ref[0, pl.ds(off, ck), :]
        sc = jnp.zeros((ck, bq), f32)
        for h in range(IDX_HEADS):
            sc = sc + jnp.maximum(_nt_dot(ik, iq_ref[h]), 0.0) * w_t[h:h + 1, :]
        if valid_t is not None:
            sc = jnp.where(valid_t, sc, NEG_INF)
        bits = lax.bitcast_convert_type(sc + 0.0, i32)
        key = jnp.where(bits < 0, bits ^ jnp.int32(0x7FFFFFFF), bits)
        key_scr[c] = key
        hi_scr[c] = lax.shift_right_arithmetic(key, 16).astype(i16)
        lo_scr[c] = ((key & 0xFFFF) - 32768).astype(i16)

    def score_body(c, _):
        score_chunk(c, None)
        return 0

    lax.fori_loop(0, n_full, score_body, 0)
    score_chunk(n_full, tail_valid_t)

    lanes_acc = 32

    def rows(v):
        return jnp.broadcast_to(v, (lanes_acc, bq))[None]

    def count16(ref, pred):
        def body(c, acc):
            v = ref[c].reshape(ck // lanes_acc, lanes_acc, bq)
            hit = jnp.where(pred(v), jnp.int16(1), jnp.int16(0))
            parts = [hit[g] for g in range(ck // lanes_acc)]
            while len(parts) > 1:
                parts = [a + b for a, b in zip(parts[0::2], parts[1::2])]
            return acc + parts[0]
        acc = lax.fori_loop(0, n_chunks, body, jnp.zeros((lanes_acc, bq), i16))
        return jnp.sum(acc.astype(f32), axis=0, keepdims=True)

    def search16(ref, need):
        def body(it, prefix):
            cand = prefix + lax.shift_left(jnp.int32(1), jnp.int32(15) - it)
            cand_r = rows(cand.astype(i16))
            cnt = count16(ref, lambda v: v >= cand_r)
            return jnp.where(cnt >= need, cand, prefix)
        return lax.fori_loop(0, 16, body, jnp.full((1, bq), I16_MIN, i32))

    tau_hi = search16(hi_scr, float(topk))
    tau_hi_r = rows(tau_hi.astype(i16))
    need_lo = float(topk) - count16(hi_scr, lambda v: v > tau_hi_r)
    tau_hi_b = tau_hi.astype(i16)

    def mask_lo(c, _):
        lo_scr[c] = jnp.where(hi_scr[c] == tau_hi_b, lo_scr[c], jnp.int16(I16_MIN))
        return 0

    lax.fori_loop(0, n_chunks, mask_lo, 0)
    tau_lo = search16(lo_scr, need_lo)
    tau = tau_hi * 65536 + (tau_lo + 32768)

    def count(pred):
        def body(c, acc):
            key = key_scr[c].reshape(ck // lanes_acc, lanes_acc, bq)
            return acc + jnp.sum(jnp.where(pred(key), 1.0, 0.0), axis=0)
        acc = lax.fori_loop(0, n_chunks, body, jnp.zeros((lanes_acc, bq), f32))
        return jnp.sum(acc, axis=0, keepdims=True)

    tau_r = rows(tau)
    n_take = float(topk) - count(lambda key: key > tau_r)

    def attn_chunk(c, carry, valid_t):
        off = pl.multiple_of(c * ck, ck)
        eq_run = carry[0]
        key = key_scr[c]
        eq = key == tau
        eqf = jnp.where(eq, 1.0, 0.0)
        pcs = []
        for g in range(ck // gw):
            e = eqf[g * gw:(g + 1) * gw, :]
            pcs.append(_dot(incl, e.astype(bf16)) + eq_run)
            eq_run = eq_run + jnp.sum(e, axis=0, keepdims=True)
        pc = jnp.concatenate(pcs, axis=0) if len(pcs) > 1 else pcs[0]
        addm_t = jnp.where(key > tau, 0.0, jnp.where(eq, jnp.where(pc <= n_take, 0.0, NEG_INF), NEG_INF))
        if valid_t is not None:
            addm_t = jnp.where(valid_t, addm_t, NEG_INF)
        addm = addm_t.T
        k = ak_ref[0, pl.ds(off, ck), :]
        vp = vp_scr[0, pl.ds(off, ck), :]
        pos = col + (off - i * bq).astype(f32)
        out = (eq_run,)
        for h in range(HEADS):
            m, acc = carry[1 + 2 * h:3 + 2 * h]
            s = _nt_dot(aq_ref[h], k) + slope_ref[h] * pos + addm
            m_new = jnp.maximum(m, jnp.max(s, axis=1, keepdims=True))
            m_safe = jnp.where(m_new == NEG_INF, 0.0, m_new)
            alpha = jnp.exp(m - m_safe)
            p = jnp.exp((s - m_safe).astype(bf16))
            out += (m_new, alpha * acc + _dot(p, vp))
        return out

    init = (jnp.zeros((1, bq), f32),) + _softmax_init_v1(bq) * HEADS
    carry = lax.fori_loop(0, n_full, lambda c, carry: attn_chunk(c, carry, None), init)
    carry = attn_chunk(n_full, carry, tail_valid_t)
    for h in range(HEADS):
        o_ref[h] = _softmax_finish_v1(carry[2 + 2 * h]).astype(o_ref.dtype)


def _dsa_attention(P, F, slopes, B, S, *, bq, ck, topk, gaq, giq, gak, gav, gik, iw_col):
    G, T, hw = P.shape
    nq = S // bq
    return pl.pallas_call(
        functools.partial(_dsa_kernel, bq=bq, ck=ck, topk=topk),
        grid=(B, nq),
        in_specs=[
            pl.BlockSpec((HEADS, bq, hw), lambda b, i: (gaq // HEADS, b * nq + i, 0)),
            pl.BlockSpec((IDX_HEADS, bq, hw), lambda b, i: (giq // IDX_HEADS, b * nq + i, 0)),
            pl.BlockSpec((1, S, hw), lambda b, i: (gak, b, 0)),
            pl.BlockSpec((1, S, hw), lambda b, i: (gav, b, 0)),
            pl.BlockSpec((1, S, hw), lambda b, i: (gik, b, 0)),
            pl.BlockSpec((bq, 128), lambda b, i: (b * nq + i, iw_col // 128)),
            pl.BlockSpec(slopes.shape, lambda b, i: (0, 0, 0)),
        ],
        out_specs=pl.BlockSpec((HEADS, bq, hw), lambda b, i: (0, b * nq + i, 0)),
        out_shape=jax.ShapeDtypeStruct((HEADS, T, hw), bf16),
        scratch_shapes=[pltpu.VMEM((S // ck, ck, bq), i32), pltpu.VMEM((S // ck, ck, bq), i16),
                        pltpu.VMEM((S // ck, ck, bq), i16), pltpu.VMEM((1, S, 2 * HEAD_W), bf16)],
        compiler_params=_cparams(("arbitrary", "arbitrary")),
        name="dsa_attention",
    )(P, P, P, P, P, F, slopes)


def _merge_kernel(x_ref, g_ref, sc_ref, sh_ref, g1_ref, ya_ref, yb_ref, yc_ref, yd_ref,
                  wg_ref, wb_ref, wo_ref, o_ref):
    x = x_ref[...]
    D = x.shape[1]
    h = _norm_mod(x, g_ref[...], sc_ref[0], sh_ref[0]).astype(bf16)
    merged = jnp.zeros(x.shape, f32)
    for n, y_ref in enumerate((ya_ref, yb_ref, yc_ref, yd_ref)):
        gate = jax.nn.sigmoid(_dot(h, wg_ref[:, n * D:(n + 1) * D]))
        y = jnp.concatenate([y_ref[hh] for hh in range(HEADS)], axis=1)
        merged = merged + gate * _dot(y, wb_ref[n])
    o_ref[...] = x + g1_ref[0] * _dot(merged.astype(bf16), wo_ref[...])


def _merge(x2, g, sc, sh, g1, ys, wg, wb, wo, S, *, tm):
    T, D = x2.shape
    tpb = S // tm
    row = pl.BlockSpec((tm, D), lambda i: (i, 0))
    per_b = pl.BlockSpec((1, 1, D), lambda i: (i // tpb, 0, 0))
    yspec = pl.BlockSpec((HEADS, tm, HEAD_W), lambda i: (0, i, 0))
    full = lambda a: pl.BlockSpec(a.shape, lambda i: (0,) * a.ndim)
    return pl.pallas_call(
        _merge_kernel,
        grid=(T // tm,),
        in_specs=[row, full(g), per_b, per_b, per_b, yspec, yspec, yspec, yspec, full(wg), full(wb), full(wo)],
        out_specs=row,
        out_shape=jax.ShapeDtypeStruct((T, D), f32),
        compiler_params=_cparams(("arbitrary",)),
        name="merge",
    )(x2, g, sc, sh, g1, *ys, wg, wb, wo)


def _finish(x, g2, f, fg_ref, o_ref):
    out = x + g2 * f
    if fg_ref is not None:
        out = out * lax.rsqrt(jnp.mean(out * out, axis=-1, keepdims=True) + EPS) * fg_ref[...]
    o_ref[...] = out


def _ffn_kernel(x_ref, g_ref, sc_ref, sh_ref, g2_ref, w1_ref, w3_ref, w2_ref, *rest, final):
    fg_ref = rest[0] if final else None
    o_ref, h_scr, acc_scr = rest[-3:]
    j = pl.program_id(1)

    @pl.when(j == 0)
    def _():
        h_scr[...] = _norm_mod(x_ref[...], g_ref[...], sc_ref[0], sh_ref[0]).astype(bf16)
        acc_scr[...] = jnp.zeros_like(acc_scr)

    h = h_scr[...]
    a = _dot(h, w1_ref[...])
    u = (a * jax.nn.sigmoid(a)) * _dot(h, w3_ref[...])
    acc_scr[...] += _dot(u.astype(bf16), w2_ref[...])

    @pl.when(j == pl.num_programs(1) - 1)
    def _():
        _finish(x_ref[...], g2_ref[0], acc_scr[...], fg_ref, o_ref)


def _ffn(x2, g, sc, sh, g2, w1, w3, w2, final_g, S, *, tm, tn):
    T, D = x2.shape
    F = w1.shape[1]
    tpb = S // tm
    row = pl.BlockSpec((tm, D), lambda i, j: (i, 0))
    per_b = pl.BlockSpec((1, 1, D), lambda i, j: (i // tpb, 0, 0))
    vec = pl.BlockSpec((1, D), lambda i, j: (0, 0))
    in_specs = [row, vec, per_b, per_b, per_b,
                pl.BlockSpec((D, tn), lambda i, j: (0, j)),
                pl.BlockSpec((D, tn), lambda i, j: (0, j)),
                pl.BlockSpec((tn, D), lambda i, j: (j, 0))]
    args = [x2, g, sc, sh, g2, w1, w3, w2]
    if final_g is not None:
        in_specs.append(vec)
        args.append(final_g)
    return pl.pallas_call(
        functools.partial(_ffn_kernel, final=final_g is not None),
        grid=(T // tm, F // tn),
        in_specs=in_specs,
        out_specs=row,
        out_shape=jax.ShapeDtypeStruct((T, D), f32),
        scratch_shapes=[pltpu.VMEM((tm, D), bf16), pltpu.VMEM((tm, D), f32)],
        compiler_params=_cparams(("arbitrary", "arbitrary")),
        name="ffn",
    )(*args)


def _router_kernel(x_ref, g_ref, sc_ref, sh_ref, rw_ref, rb_ref, comb_ref, cnt_ref):
    h = _norm_mod(x_ref[...], g_ref[...], sc_ref[0], sh_ref[0]).astype(bf16)
    logits = _dot(h, rw_ref[...]) + rb_ref[...]
    lane = lax.broadcasted_iota(i32, logits.shape, 1)
    big = jnp.int32(logits.shape[1])
    m1 = jnp.max(logits, axis=1, keepdims=True)
    i1 = jnp.min(jnp.where(logits == m1, lane, big), axis=1, keepdims=True)
    rest_l = jnp.where(lane == i1, NEG_INF, logits)
    m2 = jnp.max(rest_l, axis=1, keepdims=True)
    i2 = jnp.min(jnp.where(rest_l == m2, lane, big), axis=1, keepdims=True)
    e2 = jnp.exp(m2 - m1)
    den = 1.0 + e2
    comb = jnp.where(lane == i1, 1.0 / den, jnp.where(lane == i2, e2 / den, 0.0))
    comb_ref[...] = comb
    cnt_ref[0] = jnp.sum(jnp.where(comb > 0.0, 1.0, 0.0), axis=0, keepdims=True)


def _router(x2, g, sc, sh, rw, rb, S, *, tm):
    T, D = x2.shape
    tpb = S // tm
    nt = T // tm
    return pl.pallas_call(
        _router_kernel,
        grid=(nt,),
        in_specs=[pl.BlockSpec((tm, D), lambda i: (i, 0)),
                  pl.BlockSpec((1, D), lambda i: (0, 0)),
                  pl.BlockSpec((1, 1, D), lambda i: (i // tpb, 0, 0)),
                  pl.BlockSpec((1, 1, D), lambda i: (i // tpb, 0, 0)),
                  pl.BlockSpec(rw.shape, lambda i: (0, 0)),
                  pl.BlockSpec(rb.shape, lambda i: (0, 0))],
        out_specs=[pl.BlockSpec((tm, 128), lambda i: (i, 0)),
                   pl.BlockSpec((1, 1, 128), lambda i: (i, 0, 0))],
        out_shape=[jax.ShapeDtypeStruct((T, 128), f32), jax.ShapeDtypeStruct((nt, 1, 128), f32)],
        compiler_params=_cparams(("arbitrary",)),
        name="router",
    )(x2, g, sc, sh, rw, rb)


def _moe_kernel(sid_ref, nch_ref, x_ref, g_ref, sc_ref, sh_ref, g2_ref, comb_ref, w1_ref, w3_ref, w2_ref, *rest,
                final, sizes):
    fg_ref = rest[0] if final else None
    o_ref, h_scr, tri_scr, rank_scr, rank_t_scr, xg_scr, yg_scr, out_scr = rest[-8:]
    i = pl.program_id(0)
    e = pl.program_id(1)
    j = pl.program_id(2)
    n_e = pl.num_programs(1)
    tm = x_ref.shape[0]
    sid = sid_ref[i * n_e + e]
    nch = nch_ref[i * n_e + e]

    def dispatch(phase):
        for k, r in enumerate(sizes):
            @pl.when(sid == k)
            def _(r=r):
                phase(r, 0)

        @pl.when(sid == len(sizes))
        def _():
            def body(c, _):
                phase(sizes[-1], c)
                return 0
            lax.fori_loop(0, nch, body, 0)

    def chunk_rows(rows, c):
        if isinstance(c, int):
            return pl.ds(c * rows, rows)
        return pl.ds(pl.multiple_of(c * rows, 16), rows)

    def first_slot(rows, c):
        return float(c * rows) if isinstance(c, int) else (c * rows).astype(f32)

    @pl.when(jnp.logical_and(i == 0, jnp.logical_and(e == 0, j == 0)))
    def _():
        rr = lax.broadcasted_iota(i32, (tm, tm), 0)
        cc = lax.broadcasted_iota(i32, (tm, tm), 1)
        tri_scr[...] = jnp.where(cc < rr, 1.0, 0.0).astype(bf16)

    @pl.when(jnp.logical_and(e == 0, j == 0))
    def _():
        h_scr[...] = _norm_mod(x_ref[...], g_ref[...], sc_ref[0], sh_ref[0]).astype(bf16)
        out_scr[...] = jnp.zeros_like(out_scr)
        member = jnp.where(comb_ref[...] > 0.0, 1.0, 0.0)
        rank = jnp.where(member > 0.0, _dot(tri_scr[...], member.astype(bf16)), -1.0)
        rank_scr[...] = rank
        rank_t_scr[...] = rank.T

    @pl.when(j == 0)
    def _():
        rank_row = rank_t_scr[pl.ds(e, 1), :]

        def gather(rows, c):
            slot = lax.broadcasted_iota(i32, (rows, tm), 0).astype(f32) + first_slot(rows, c)
            pick = jnp.where(slot == rank_row, 1.0, 0.0).astype(bf16)
            xg_scr[chunk_rows(rows, c), :] = _dot(pick, h_scr[...]).astype(bf16)
            yg_scr[chunk_rows(rows, c), :] = jnp.zeros((rows, yg_scr.shape[1]), f32)

        dispatch(gather)

    def expert(rows, c):
        xg = xg_scr[chunk_rows(rows, c), :]
        a = _dot(xg, w1_ref[0])
        u = (a * jax.nn.sigmoid(a)) * _dot(xg, w3_ref[0])
        yg_scr[chunk_rows(rows, c), :] += _dot(u.astype(bf16), w2_ref[0])

    dispatch(expert)

    last_j = j == pl.num_programs(2) - 1

    @pl.when(last_j)
    def _():
        lane = lax.broadcasted_iota(i32, (tm, 128), 1)
        sel = lane == e
        rank_col = jnp.sum(jnp.where(sel, rank_scr[...], 0.0), axis=1, keepdims=True)
        w_col = jnp.sum(jnp.where(sel, comb_ref[...], 0.0), axis=1, keepdims=True)

        def scatter(rows, c):
            slot = lax.broadcasted_iota(i32, (tm, rows), 1).astype(f32) + first_slot(rows, c)
            place = jnp.where(slot == rank_col, 1.0, 0.0).astype(bf16)
            out_scr[...] += w_col * _dot(place, yg_scr[chunk_rows(rows, c), :].astype(bf16))

        dispatch(scatter)

    @pl.when(jnp.logical_and(last_j, e == n_e - 1))
    def _():
        _finish(x_ref[...], g2_ref[0], out_scr[...], fg_ref, o_ref)


def _moe_plan(cnt, sizes):
    cnt = cnt.astype(i32).reshape(-1)
    sid = sum((cnt > r).astype(i32) for r in sizes)
    nch = (cnt + (sizes[-1] - 1)) // sizes[-1]
    return sid, nch


def _moe(x2, g, sc, sh, g2, comb, sid, nch, w1, w3, w2, final_g, S, *, tm, tn, sizes):
    T, D = x2.shape
    E, _, F = w1.shape
    nj = F // tn
    tpb = S // tm
    cap = -(-tm // sizes[-1]) * sizes[-1]
    row = pl.BlockSpec((tm, D), lambda i, e, j, *_: (i, 0), pipeline_mode=pl.Buffered(1))
    per_b = pl.BlockSpec((1, 1, D), lambda i, e, j, *_: (i // tpb, 0, 0))
    vec = pl.BlockSpec((1, D), lambda i, e, j, *_: (0, 0))
    in_specs = [row, vec, per_b, per_b, per_b,
                pl.BlockSpec((tm, 128), lambda i, e, j, *_: (i, 0)),
                pl.BlockSpec((1, D, tn), lambda i, e, j, *_: (e, 0, j)),
                pl.BlockSpec((1, D, tn), lambda i, e, j, *_: (e, 0, j)),
                pl.BlockSpec((1, tn, D), lambda i, e, j, *_: (e, j, 0))]
    args = [x2, g, sc, sh, g2, comb, w1, w3, w2]
    if final_g is not None:
        in_specs.append(vec)
        args.append(final_g)
    return pl.pallas_call(
        functools.partial(_moe_kernel, final=final_g is not None, sizes=tuple(sizes)),
        grid_spec=pltpu.PrefetchScalarGridSpec(
            num_scalar_prefetch=2,
            grid=(T // tm, E, nj),
            in_specs=in_specs,
            out_specs=row,
            scratch_shapes=[pltpu.VMEM((tm, D), bf16),
                            pltpu.VMEM((tm, tm), bf16),
                            pltpu.VMEM((tm, 128), f32),
                            pltpu.VMEM((128, tm), f32),
                            pltpu.VMEM((cap, D), bf16),
                            pltpu.VMEM((cap, D), f32),
                            pltpu.VMEM((tm, D), f32)]),
        out_shape=jax.ShapeDtypeStruct((T, D), f32),
        compiler_params=_cparams(("arbitrary", "arbitrary", "arbitrary")),
        name="moe",
    )(sid, nch, *args)


_IN_SIZES = (256, 64, 64, 256, 64, 4, 256, 128, 32, 256, 256, 256, 256, 256, 256, 4096)
_IN_NAMES = ("aq", "ak", "av", "iq", "ik", "iw", "cq", "ckv", "kr", "sq", "sk", "sv", "dq", "dk", "dv", "gates")

G_AQ, G_IQ, G_SQ, G_SK, G_SV, G_DQ, G_DK, G_DV, G_AK, G_AV, G_IK = 0, 4, 8, 12, 16, 20, 24, 28, 32, 33, 34
N_GROUPS = 36
F_IW_COL = 640


def _split_w_in(w):
    out, off = {}, 0
    for name, n in zip(_IN_NAMES, _IN_SIZES):
        out[name] = w[:, off:off + n]
        off += n
    return out


def _rot_half_cols(w):
    half = w.shape[1] // 2
    return jnp.concatenate([-w[:, half:], w[:, :half]], axis=1)


def _layer_weights(w_in, w_uq, w_ukv):
    c = _split_w_in(w_in)
    D = w_in.shape[0]
    z = lambda n: jnp.zeros((D, n), f32)
    w_main = jnp.concatenate([
        c["aq"] * 0.125, c["iq"] * 0.125, c["sq"] * 0.125, c["sk"], c["sv"],
        c["dq"] * (DIFF_QK ** -0.5), c["dk"], c["dv"], c["ak"], c["av"], c["ik"], z(64)], axis=1).astype(bf16)
    w_f = jnp.concatenate([
        c["cq"], c["ckv"],
        z(64), c["kr"], z(32),
        z(64), _rot_half_cols(c["kr"]), z(32),
        c["iw"] * (IDX_HEADS ** -0.5), z(124)], axis=1).astype(bf16)
    w_gates = c["gates"].astype(bf16)

    qr = w_uq.shape[0]
    kvr = w_ukv.shape[0]
    wq, wqp, wk, wv = [], [], [], []
    per_q = MLA_NOPE + MLA_ROPE
    per_kv = MLA_NOPE + MLA_V
    for h in range(HEADS):
        nope = w_uq[:, h * per_q:h * per_q + MLA_NOPE]
        rope = w_uq[:, h * per_q + MLA_NOPE:(h + 1) * per_q]
        wq += [nope, rope, jnp.zeros((qr, 32), f32)]
        wqp += [jnp.zeros((qr, 64), f32), _rot_half_cols(rope), jnp.zeros((qr, 32), f32)]
        wk += [w_ukv[:, h * per_kv:h * per_kv + MLA_NOPE], jnp.zeros((kvr, 64), f32)]
        wv += [w_ukv[:, h * per_kv + MLA_NOPE:(h + 1) * per_kv]]
    cat = lambda xs: jnp.concatenate(xs, axis=1).astype(bf16)
    return w_main, w_f, w_gates, cat(wq), cat(wqp), cat(wk), cat(wv)


def _alibi_slopes():
    n = 2 * HEADS
    sl = [2.0 ** (-(8.0 / n) * (k + 1)) for k in range(n)]
    return sl[0::2], sl[1::2]


def kernel(x, c, positions, ada_w, ada_b, norm1_g, norm2_g, w_in, mla_q_norm_g, mla_kv_norm_g, mla_w_uq,
           mla_w_ukv, diff_lq1, diff_lk1, diff_lq2, diff_lk2, diff_norm_g, w_branch, w_out, ffn_w1, ffn_w3,
           ffn_w2, router_w, router_b, moe_w1, moe_w3, moe_w2, final_norm_g):
    B, S, D = x.shape
    T = B * S
    depth = ada_w.shape[0]
    topk = min(TOPK_MAX, S // 4)
    tm = min(1024, S)
    tm_merge = min(512, S)
    bq = min(512, S)
    ck = min(512, S)

    x2 = x.reshape(T, D)
    pos2 = positions.reshape(T, 1)
    mod = _modulation(c, ada_w, ada_b)

    half = MLA_ROPE // 2
    inv = ROPE_THETA ** (-jnp.arange(half, dtype=f32) / half)
    inv_full = jnp.concatenate([jnp.zeros((64,), f32), inv, inv, jnp.zeros((32,), f32)]).reshape(1, 128)
    cos_t, sin_t = _rope_tables(pos2, inv_full, tm=tm)
    sl_a, sl_d = _alibi_slopes()
    slopes_a = jnp.broadcast_to(jnp.asarray(sl_a, f32)[:, None, None], (HEADS, 1, ck))
    slopes_d = jnp.broadcast_to(jnp.asarray(sl_d, f32)[:, None, None], (HEADS, 1, ck))

    row = lambda v: v.reshape(1, -1)
    for l in range(depth):
        sh1, sc1, g1, sh2, sc2, g2 = [mod[l, :, k * D:(k + 1) * D].reshape(B, 1, D) for k in range(6)]
        lam_init = 0.8 - 0.6 * math.exp(-0.3 * l)
        w_main, w_f, w_gates, wq, wqp, wk, wv = _layer_weights(w_in[l], mla_w_uq[l], mla_w_ukv[l])
        n1 = row(norm1_g[l])

        P = _inproj(x2, n1, sc1, sh1, w_main, S, grouped=True, out_dtype=bf16, tm=tm, tn=INPROJ_TN)
        F = _inproj(x2, n1, sc1, sh1, w_f, S, grouped=False, out_dtype=f32, tm=tm, tn=w_f.shape[1])
        qm, km, vm = _mla_prep(F, cos_t, sin_t, row(mla_q_norm_g[l]), row(mla_kv_norm_g[l]), wq, wqp, wk, wv,
                               tm=tm)

        ya = _dsa_attention(P, F, slopes_a, B, S, bq=bq, ck=ck, topk=topk, gaq=G_AQ, giq=G_IQ, gak=G_AK,
                            gav=G_AV, gik=G_IK, iw_col=F_IW_COL)
        yb = _mla_attention(qm, km, vm, B, S, bq=bq, ck=ck)
        yc = _sb_attention(P, B, S, bq=bq, ck=ck, gq=G_SQ, gk=G_SK, gv=G_SV)
        yd = _diff_attention(P, slopes_d, row(diff_lq1[l]), row(diff_lk1[l]), row(diff_lq2[l]),
                             row(diff_lk2[l]), row(diff_norm_g[l]), B, S, bq=bq, ck=ck, lam_init=lam_init,
                             gq=G_DQ, gk=G_DK, gv=G_DV)

        x2 = _merge(x2, n1, sc1, sh1, g1, (ya, yb, yc, yd), w_gates, w_branch[l].astype(bf16),
                    w_out[l].astype(bf16), S, tm=tm_merge)

        final_g = row(final_norm_g) if l == depth - 1 else None
        n2 = row(norm2_g[l])
        if l % 2 == 0:
            k = l // 2
            x2 = _ffn(x2, n2, sc2, sh2, g2, ffn_w1[k].astype(bf16), ffn_w3[k].astype(bf16),
                      ffn_w2[k].astype(bf16), final_g, S, tm=min(FFN_TM, S), tn=FFN_TN)
        else:
            k = l // 2
            E = router_w.shape[2]
            rw = jnp.concatenate([router_w[k], jnp.zeros((D, 128 - E), f32)], axis=1).astype(bf16)
            rb = jnp.concatenate([router_b[k], jnp.full((128 - E,), NEG_INF, f32)]).reshape(1, 128)
            comb, cnt = _router(x2, n2, sc2, sh2, rw, rb, S, tm=tm)
            sid, nch = _moe_plan(cnt[:, 0, :E], MOE_SIZES)
            x2 = _moe(x2, n2, sc2, sh2, g2, comb, sid, nch, moe_w1[k].astype(bf16), moe_w3[k].astype(bf16),
                      moe_w2[k].astype(bf16), final_g, S, tm=tm, tn=MOE_TN, sizes=MOE_SIZES)
    return x2.reshape(B, S, D)
```

```python
import functools
import math

import jax
import jax.numpy as jnp
from jax import lax
from jax.experimental import pallas as pl
from jax.experimental.pallas import tpu as pltpu

f32 = jnp.float32
bf16 = jnp.bfloat16
i32 = jnp.int32
i16 = jnp.int16

N_BRANCHES = 4
HEADS = 4
HEAD_W = 64
IDX_HEADS = 4
TOPK_MAX = 256
MLA_Q_RANK = 256
MLA_KV_RANK = 128
MLA_NOPE = 64
MLA_ROPE = 32
MLA_V = 64
MLA_QK_PAD = 128
ROPE_THETA = 10000.0
DIFF_QK = 32
N_EXPERTS = 8
TOP_K = 2
INPROJ_TN = 768
FFN_TN = 1408
FFN_TM = 512
MOE_TN = 1792
MOE_SIZES = (128, 192, 256, 320, 384, 448, 512)
EPS = 1e-6
NEG_INF = float("-inf")
LOG2E = math.log2(math.e)
I16_MIN = -2 ** 15

VMEM_LIMIT = 56 * 1024 * 1024
MOE_VMEM_LIMIT = 60 * 1024 * 1024


def _cparams(sem, vmem_limit=VMEM_LIMIT):
    return pltpu.CompilerParams(dimension_semantics=sem, vmem_limit_bytes=vmem_limit)


def _nt_dot(a, b):
    return lax.dot_general(a, b, (((1,), (1,)), ((), ())), preferred_element_type=f32)


def _dot(a, b):
    return jnp.dot(a, b, preferred_element_type=f32)


def _norm_mod(x, g, sc, sh):
    y = x * lax.rsqrt(jnp.mean(x * x, axis=-1, keepdims=True) + EPS)
    return (y * g) * (1.0 + sc) + sh


def _mod_kernel(c_ref, w_ref, b_ref, o_ref):
    c = c_ref[...]
    cond = (c * jax.nn.sigmoid(c)).astype(bf16)
    o_ref[0] = _dot(cond, w_ref[0].astype(bf16)) + b_ref[0]


def _modulation(c, ada_w, ada_b):
    L, D, N = ada_w.shape
    B = c.shape[0]
    tn = 1536
    return pl.pallas_call(
        _mod_kernel,
        grid=(L, N // tn),
        in_specs=[
            pl.BlockSpec((B, D), lambda l, j: (0, 0)),
            pl.BlockSpec((1, D, tn), lambda l, j: (l, 0, j)),
            pl.BlockSpec((1, 1, tn), lambda l, j: (l, 0, j)),
        ],
        out_specs=pl.BlockSpec((1, B, tn), lambda l, j: (l, 0, j)),
        out_shape=jax.ShapeDtypeStruct((L, B, N), f32),
        compiler_params=_cparams(("arbitrary", "arbitrary")),
        name="modulation",
    )(c, ada_w, ada_b.reshape(L, 1, N))


def _inproj_kernel(x_ref, g_ref, sc_ref, sh_ref, w_ref, o_ref, h_scr, *, grouped):
    @pl.when(pl.program_id(1) == 0)
    def _():
        h_scr[...] = _norm_mod(x_ref[...], g_ref[...], sc_ref[0], sh_ref[0]).astype(bf16)

    res = _dot(h_scr[...], w_ref[...])
    if grouped:
        for k in range(res.shape[1] // HEAD_W):
            o_ref[k] = res[:, k * HEAD_W:(k + 1) * HEAD_W].astype(o_ref.dtype)
    else:
        o_ref[...] = res.astype(o_ref.dtype)


def _inproj(x2, g, sc, sh, w, S, *, grouped, out_dtype, tm, tn):
    T, D = x2.shape
    N = w.shape[1]
    tpb = S // tm
    if grouped:
        out_shape = jax.ShapeDtypeStruct((N // HEAD_W, T, HEAD_W), out_dtype)
        out_spec = pl.BlockSpec((tn // HEAD_W, tm, HEAD_W), lambda i, j: (j, i, 0))
    else:
        out_shape = jax.ShapeDtypeStruct((T, N), out_dtype)
        out_spec = pl.BlockSpec((tm, tn), lambda i, j: (i, j))
    return pl.pallas_call(
        functools.partial(_inproj_kernel, grouped=grouped),
        grid=(T // tm, N // tn),
        in_specs=[
            pl.BlockSpec((tm, D), lambda i, j: (i, 0)),
            pl.BlockSpec((1, D), lambda i, j: (0, 0)),
            pl.BlockSpec((1, 1, D), lambda i, j: (i // tpb, 0, 0)),
            pl.BlockSpec((1, 1, D), lambda i, j: (i // tpb, 0, 0)),
            pl.BlockSpec((D, tn), lambda i, j: (0, j)),
        ],
        out_specs=out_spec,
        out_shape=out_shape,
        scratch_shapes=[pltpu.VMEM((tm, D), bf16)],
        compiler_params=_cparams(("arbitrary", "arbitrary")),
        name="inproj_grouped" if grouped else "inproj_plain",
    )(x2, g, sc, sh, w)


def _rope_kernel(pos_ref, inv_ref, cos_ref, sin_ref):
    ang = pos_ref[...].astype(f32) * inv_ref[...]
    cos_ref[...] = jnp.cos(ang)
    sin_ref[...] = jnp.sin(ang)


def _rope_tables(pos2, inv_full, *, tm):
    T = pos2.shape[0]
    tile = pl.BlockSpec((tm, 128), lambda i: (i, 0))
    return pl.pallas_call(
        _rope_kernel,
        grid=(T // tm,),
        in_specs=[pl.BlockSpec((tm, 1), lambda i: (i, 0)), pl.BlockSpec((1, 128), lambda i: (0, 0))],
        out_specs=[tile, tile],
        out_shape=[jax.ShapeDtypeStruct((T, 128), f32)] * 2,
        compiler_params=_cparams(("arbitrary",)),
        name="rope_tables",
    )(pos2, inv_full)


def _mla_prep_kernel(f_ref, cos_ref, sin_ref, gq_ref, gkv_ref, wq_ref, wqp_ref, wk_ref, wv_ref,
                     q_ref, k_ref, v_ref, *, scale):
    cq = f_ref[:, 0:MLA_Q_RANK]
    ckv = f_ref[:, MLA_Q_RANK:MLA_Q_RANK + MLA_KV_RANK]
    kr = f_ref[:, 384:512]
    krp = f_ref[:, 512:640]
    nq = (cq * lax.rsqrt(jnp.mean(cq * cq, axis=-1, keepdims=True) + EPS) * gq_ref[...]).astype(bf16)
    nkv = (ckv * lax.rsqrt(jnp.mean(ckv * ckv, axis=-1, keepdims=True) + EPS) * gkv_ref[...]).astype(bf16)
    cosf = cos_ref[...]
    sinf = sin_ref[...]
    q1 = _dot(nq, wq_ref[...])
    q2 = _dot(nq, wqp_ref[...])
    kn = _dot(nkv, wk_ref[...])
    vv = _dot(nkv, wv_ref[...])
    krope = kr * cosf + krp * sinf
    for h in range(HEADS):
        sl = slice(h * MLA_QK_PAD, (h + 1) * MLA_QK_PAD)
        q_ref[h] = ((q1[:, sl] * cosf + q2[:, sl] * sinf) * scale).astype(bf16)
        k_ref[h] = (kn[:, sl] + krope).astype(bf16)
        v_ref[h] = vv[:, h * MLA_V:(h + 1) * MLA_V].astype(bf16)


def _mla_prep(F, cos_t, sin_t, gq, gkv, wq, wqp, wk, wv, *, tm):
    T = F.shape[0]
    scale = (MLA_NOPE + MLA_ROPE) ** -0.5
    full = lambda a: pl.BlockSpec(a.shape, lambda i: (0,) * a.ndim)
    return pl.pallas_call(
        functools.partial(_mla_prep_kernel, scale=scale),
        grid=(T // tm,),
        in_specs=[
            pl.BlockSpec((tm, F.shape[1]), lambda i: (i, 0)),
            pl.BlockSpec((tm, 128), lambda i: (i, 0)),
            pl.BlockSpec((tm, 128), lambda i: (i, 0)),
            full(gq), full(gkv), full(wq), full(wqp), full(wk), full(wv),
        ],
        out_specs=[
            pl.BlockSpec((HEADS, tm, MLA_QK_PAD), lambda i: (0, i, 0)),
            pl.BlockSpec((HEADS, tm, MLA_QK_PAD), lambda i: (0, i, 0)),
            pl.BlockSpec((HEADS, tm, MLA_V), lambda i: (0, i, 0)),
        ],
        out_shape=[
            jax.ShapeDtypeStruct((HEADS, T, MLA_QK_PAD), bf16),
            jax.ShapeDtypeStruct((HEADS, T, MLA_QK_PAD), bf16),
            jax.ShapeDtypeStruct((HEADS, T, MLA_V), bf16),
        ],
        compiler_params=_cparams(("arbitrary",)),
        name="mla_prep",
    )(F, cos_t, sin_t, gq, gkv, wq, wqp, wk, wv)


def _chunk_plan(i, bq, ck):
    n_full = (i * bq) // ck
    return n_full, n_full * ck


def _tail_valid(i, bq, ck, tail_off):
    row = lax.broadcasted_iota(i32, (bq, ck), 0)
    col = lax.broadcasted_iota(i32, (bq, ck), 1)
    return col + (tail_off - i * bq) <= row


def _fill_v_ones(v_ref, vp_scr):
    for h in range(v_ref.shape[0]):
        vp_scr[h, :, 0:HEAD_W] = v_ref[h]
        vp_scr[h, :, HEAD_W:2 * HEAD_W] = jnp.ones((v_ref.shape[1], HEAD_W), bf16)


def _softmax_step_v1(s, vp, m, acc):
    m_new = jnp.maximum(m, jnp.max(s, axis=1, keepdims=True))
    alpha = jnp.exp(m - m_new)
    p = jnp.exp((s - m_new).astype(bf16))
    return m_new, alpha * acc + _dot(p, vp)


def _softmax_init_v1(bq):
    return (jnp.full((bq, 1), NEG_INF, f32), jnp.zeros((bq, 2 * HEAD_W), f32))


def _softmax_finish_v1(acc):
    return acc[:, 0:HEAD_W] / acc[:, HEAD_W:HEAD_W + 1]


def _mla_attn_kernel(q_ref, k_ref, v_ref, o_ref, vp_scr, *, bq, ck):
    i = pl.program_id(1)
    n_full, tail_off = _chunk_plan(i, bq, ck)

    @pl.when(i == 0)
    def _():
        _fill_v_ones(v_ref, vp_scr)

    def chunk(off, carry, valid):
        off = pl.multiple_of(off, ck)
        out = ()
        for h in range(HEADS):
            s = _nt_dot(q_ref[h], k_ref[h, pl.ds(off, ck), :])
            if valid is not None:
                s = jnp.where(valid, s, NEG_INF)
            out += _softmax_step_v1(s, vp_scr[h, pl.ds(off, ck), :], *carry[2 * h:2 * h + 2])
        return out

    carry = lax.fori_loop(0, n_full, lambda c, carry: chunk(c * ck, carry, None), _softmax_init_v1(bq) * HEADS)
    carry = chunk(tail_off, carry, _tail_valid(i, bq, ck, tail_off))
    for h in range(HEADS):
        o_ref[h] = _softmax_finish_v1(carry[2 * h + 1]).astype(o_ref.dtype)


def _mla_attention(qm, km, vm, B, S, *, bq, ck):
    H, T, dk = qm.shape
    dv = vm.shape[2]
    nq = S // bq
    return pl.pallas_call(
        functools.partial(_mla_attn_kernel, bq=bq, ck=ck),
        grid=(B, nq),
        in_specs=[
            pl.BlockSpec((H, bq, dk), lambda b, i: (0, b * nq + i, 0)),
            pl.BlockSpec((H, S, dk), lambda b, i: (0, b, 0)),
            pl.BlockSpec((H, S, dv), lambda b, i: (0, b, 0)),
        ],
        out_specs=pl.BlockSpec((H, bq, dv), lambda b, i: (0, b * nq + i, 0)),
        out_shape=jax.ShapeDtypeStruct((H, T, dv), bf16),
        scratch_shapes=[pltpu.VMEM((H, S, 2 * HEAD_W), bf16)],
        compiler_params=_cparams(("arbitrary", "arbitrary")),
        name="mla_attention",
    )(qm, km, vm)


def _diff_attn_kernel(q_ref, k_ref, v_ref, slope_ref, lq1_ref, lk1_ref, lq2_ref, lk2_ref, g_ref, o_ref,
                      vp_scr, *, bq, ck, lam_init):
    i = pl.program_id(1)
    n_full, tail_off = _chunk_plan(i, bq, ck)
    lane = lax.broadcasted_iota(i32, (bq, q_ref.shape[2]), 1)
    col = lax.broadcasted_iota(i32, (1, ck), 1).astype(f32)

    @pl.when(i == 0)
    def _():
        _fill_v_ones(v_ref, vp_scr)

    def chunk(off, carry, valid):
        off = pl.multiple_of(off, ck)
        pos = col + (off - i * bq).astype(f32)
        out = ()
        for h in range(HEADS):
            q = q_ref[h]
            zero = jnp.zeros_like(q)
            k = k_ref[h, pl.ds(off, ck), :]
            vp = vp_scr[h, pl.ds(off, ck), :]
            bias = slope_ref[h] * pos
            for mp, qm in enumerate((jnp.where(lane < DIFF_QK, q, zero), jnp.where(lane < DIFF_QK, zero, q))):
                s = _nt_dot(qm, k) + bias
                if valid is not None:
                    s = jnp.where(valid, s, NEG_INF)
                base = 2 * (2 * h + mp)
                out += _softmax_step_v1(s, vp, *carry[base:base + 2])
        return out

    carry = lax.fori_loop(0, n_full, lambda c, carry: chunk(c * ck, carry, None),
                          _softmax_init_v1(bq) * (2 * HEADS))
    carry = chunk(tail_off, carry, _tail_valid(i, bq, ck, tail_off))
    lam = (jnp.exp(jnp.sum(lq1_ref[...] * lk1_ref[...], axis=1, keepdims=True))
           - jnp.exp(jnp.sum(lq2_ref[...] * lk2_ref[...], axis=1, keepdims=True)) + lam_init)
    for h in range(HEADS):
        m0, a0, m1, a1 = carry[4 * h:4 * h + 4]
        o = _softmax_finish_v1(a0) - lam * _softmax_finish_v1(a1)
        y = o * lax.rsqrt(jnp.mean(o * o, axis=-1, keepdims=True) + EPS) * g_ref[...]
        o_ref[h] = (y * (1.0 - lam_init)).astype(o_ref.dtype)


def _diff_attention(P, slopes, lq1, lk1, lq2, lk2, g, B, S, *, bq, ck, lam_init, gq, gk, gv):
    G, T, hw = P.shape
    nq = S // bq
    vec = lambda a: pl.BlockSpec(a.shape, lambda b, i: (0,) * a.ndim)
    return pl.pallas_call(
        functools.partial(_diff_attn_kernel, bq=bq, ck=ck, lam_init=lam_init),
        grid=(B, nq),
        in_specs=[
            pl.BlockSpec((HEADS, bq, hw), lambda b, i: (gq // HEADS, b * nq + i, 0)),
            pl.BlockSpec((HEADS, S, hw), lambda b, i: (gk // HEADS, b, 0)),
            pl.BlockSpec((HEADS, S, hw), lambda b, i: (gv // HEADS, b, 0)),
            vec(slopes), vec(lq1), vec(lk1), vec(lq2), vec(lk2), vec(g),
        ],
        out_specs=pl.BlockSpec((HEADS, bq, hw), lambda b, i: (0, b * nq + i, 0)),
        out_shape=jax.ShapeDtypeStruct((HEADS, T, hw), bf16),
        scratch_shapes=[pltpu.VMEM((HEADS, S, 2 * HEAD_W), bf16)],
        compiler_params=_cparams(("arbitrary", "arbitrary")),
        name="diff_attention",
    )(P, P, P, slopes, lq1, lk1, lq2, lk2, g)


SB_GROUP = 256


def _sb_attn_kernel(q_ref, k_ref, v_ref, o_ref, *, bq, ck):
    i = pl.program_id(1)
    n_full, tail_off = _chunk_plan(i, bq, ck)
    dv = v_ref.shape[2]
    gw = min(SB_GROUP, ck)
    rr = lax.broadcasted_iota(i32, (gw, gw), 0)
    cc = lax.broadcasted_iota(i32, (gw, gw), 1)
    upper = jnp.where(rr > cc, 1.0, 0.0).astype(bf16)

    def chunk(off, carry, strict):
        off = pl.multiple_of(off, ck)
        out = ()
        for h in range(HEADS):
            run, acc = carry[2 * h:2 * h + 2]
            z = _nt_dot(q_ref[h], k_ref[h, pl.ds(off, ck), :])
            lsm = -(jnp.maximum(z, 0.0) + jnp.log(1.0 + jnp.exp2(jnp.abs(z) * (-LOG2E))))
            if strict is not None:
                lsm = jnp.where(strict, lsm, 0.0)
            parts = []
            for g in reversed(range(ck // gw)):
                x = lsm[:, g * gw:(g + 1) * gw]
                parts.append(_dot(x.astype(bf16), upper) + run)
                run = run + jnp.sum(x, axis=1, keepdims=True)
            between = jnp.concatenate(parts[::-1], axis=1) if len(parts) > 1 else parts[0]
            arg = z + lsm + between
            if strict is not None:
                arg = jnp.where(strict, arg, NEG_INF)
            acc = acc + _dot(jnp.exp(arg.astype(bf16)), v_ref[h, pl.ds(off, ck), :])
            out += (run, acc)
        return out

    row = lax.broadcasted_iota(i32, (bq, ck), 0)
    col = lax.broadcasted_iota(i32, (bq, ck), 1)
    strict = col + (tail_off - i * bq) < row
    init = (jnp.zeros((bq, 1), f32), jnp.zeros((bq, dv), f32)) * HEADS
    carry = chunk(tail_off, init, strict)
    carry = lax.fori_loop(0, n_full, lambda n, carry: chunk((n_full - 1 - n) * ck, carry, None), carry)
    for h in range(HEADS):
        o_ref[h] = carry[2 * h + 1].astype(o_ref.dtype)


def _sb_attention(P, B, S, *, bq, ck, gq, gk, gv):
    G, T, hw = P.shape
    nq = S // bq
    return pl.pallas_call(
        functools.partial(_sb_attn_kernel, bq=bq, ck=ck),
        grid=(B, nq),
        in_specs=[
            pl.BlockSpec((HEADS, bq, hw), lambda b, i: (gq // HEADS, b * nq + i, 0)),
            pl.BlockSpec((HEADS, S, hw), lambda b, i: (gk // HEADS, b, 0)),
            pl.BlockSpec((HEADS, S, hw), lambda b, i: (gv // HEADS, b, 0)),
        ],
        out_specs=pl.BlockSpec((HEADS, bq, hw), lambda b, i: (0, b * nq + i, 0)),
        out_shape=jax.ShapeDtypeStruct((HEADS, T, hw), bf16),
        compiler_params=_cparams(("arbitrary", "arbitrary")),
        name="sb_attention",
    )(P, P, P)


def _dsa_kernel(aq_ref, iq_ref, ak_ref, av_ref, ik_ref, iw_ref, slope_ref, o_ref, key_scr, hi_scr, lo_scr,
                vp_scr, *, bq, ck, topk):
    i = pl.program_id(1)
    n_full, tail_off = _chunk_plan(i, bq, ck)
    n_chunks = n_full + 1

    @pl.when(i == 0)
    def _():
        _fill_v_ones(av_ref, vp_scr)

    w_t = iw_ref[...].T
    gw = min(SB_GROUP, ck)
    rr = lax.broadcasted_iota(i32, (gw, gw), 0)
    cc = lax.broadcasted_iota(i32, (gw, gw), 1)
    incl = jnp.where(cc <= rr, 1.0, 0.0).astype(bf16)
    col = lax.broadcasted_iota(i32, (1, ck), 1).astype(f32)
    key_i = lax.broadcasted_iota(i32, (ck, bq), 0)
    qry_i = lax.broadcasted_iota(i32, (ck, bq), 1)
    tail_valid_t = key_i + (tail_off - i * bq) <= qry_i

    def score_chunk(c, valid_t):
        off = pl.multiple_of(c * ck, ck)
        ik = ik_ref[0, pl.ds(off, ck), :]
        sc = jnp.zeros((ck, bq), f32)
        for h in range(IDX_HEADS):
            sc = sc + jnp.maximum(_nt_dot(ik, iq_ref[h]), 0.0) * w_t[h:h + 1, :]
        if valid_t is not None:
            sc = jnp.where(valid_t, sc, NEG_INF)
        bits = lax.bitcast_convert_type(sc + 0.0, i32)
        key = jnp.where(bits < 0, bits ^ jnp.int32(0x7FFFFFFF), bits)
        key_scr[c] = key
        hi_scr[c] = lax.shift_right_arithmetic(key, 16).astype(i16)
        lo_scr[c] = ((key & 0xFFFF) - 32768).astype(i16)

    def score_body(c, _):
        score_chunk(c, None)
        return 0

    lax.fori_loop(0, n_full, score_body, 0)
    score_chunk(n_full, tail_valid_t)

    lanes_acc = 32

    def rows(v):
        return jnp.broadcast_to(v, (lanes_acc, bq))[None]

    def count16(ref, pred):
        def body(c, acc):
            v = ref[c].reshape(ck // lanes_acc, lanes_acc, bq)
            hit = jnp.where(pred(v), jnp.int16(1), jnp.int16(0))
            parts = [hit[g] for g in range(ck // lanes_acc)]
            while len(parts) > 1:
                parts = [a + b for a, b in zip(parts[0::2], parts[1::2])]
            return acc + parts[0]
        acc = lax.fori_loop(0, n_chunks, body, jnp.zeros((lanes_acc, bq), i16))
        return jnp.sum(acc.astype(f32), axis=0, keepdims=True)

    def search16(ref, need):
        def body(it, prefix):
            cand = prefix + lax.shift_left(jnp.int32(1), jnp.int32(15) - it)
            cand_r = rows(cand.astype(i16))
            cnt = count16(ref, lambda v: v >= cand_r)
            return jnp.where(cnt >= need, cand, prefix)
        return lax.fori_loop(0, 16, body, jnp.full((1, bq), I16_MIN, i32))

    tau_hi = search16(hi_scr, float(topk))
    tau_hi_r = rows(tau_hi.astype(i16))
    need_lo = float(topk) - count16(hi_scr, lambda v: v > tau_hi_r)
    tau_hi_b = tau_hi.astype(i16)

    def mask_lo(c, _):
        lo_scr[c] = jnp.where(hi_scr[c] == tau_hi_b, lo_scr[c], jnp.int16(I16_MIN))
        return 0

    lax.fori_loop(0, n_chunks, mask_lo, 0)
    tau_lo = search16(lo_scr, need_lo)
    tau = tau_hi * 65536 + (tau_lo + 32768)

    def count(pred):
        def body(c, acc):
            key = key_scr[c].reshape(ck // lanes_acc, lanes_acc, bq)
            return acc + jnp.sum(jnp.where(pred(key), 1.0, 0.0), axis=0)
        acc = lax.fori_loop(0, n_chunks, body, jnp.zeros((lanes_acc, bq), f32))
        return jnp.sum(acc, axis=0, keepdims=True)

    tau_r = rows(tau)
    n_take = float(topk) - count(lambda key: key > tau_r)

    def attn_chunk(c, carry, valid_t):
        off = pl.multiple_of(c * ck, ck)
        eq_run = carry[0]
        key = key_scr[c]
        eq = key == tau
        eqf = jnp.where(eq, 1.0, 0.0)
        pcs = []
        for g in range(ck // gw):
            e = eqf[g * gw:(g + 1) * gw, :]
            pcs.append(_dot(incl, e.astype(bf16)) + eq_run)
            eq_run = eq_run + jnp.sum(e, axis=0, keepdims=True)
        pc = jnp.concatenate(pcs, axis=0) if len(pcs) > 1 else pcs[0]
        addm_t = jnp.where(key > tau, 0.0, jnp.where(eq, jnp.where(pc <= n_take, 0.0, NEG_INF), NEG_INF))
        if valid_t is not None:
            addm_t = jnp.where(valid_t, addm_t, NEG_INF)
        addm = addm_t.T
        k = ak_ref[0, pl.ds(off, ck), :]
        vp = vp_scr[0, pl.ds(off, ck), :]
        pos = col + (off - i * bq).astype(f32)
        out = (eq_run,)
        for h in range(HEADS):
            m, acc = carry[1 + 2 * h:3 + 2 * h]
            s = _nt_dot(aq_ref[h], k) + slope_ref[h] * pos + addm
            m_new = jnp.maximum(m, jnp.max(s, axis=1, keepdims=True))
            m_safe = jnp.where(m_new == NEG_INF, 0.0, m_new)
            alpha = jnp.exp(m - m_safe)
            p = jnp.exp((s - m_safe).astype(bf16))
            out += (m_new, alpha * acc + _dot(p, vp))
        return out

    init = (jnp.zeros((1, bq), f32),) + _softmax_init_v1(bq) * HEADS
    carry = lax.fori_loop(0, n_full, lambda c, carry: attn_chunk(c, carry, None), init)
    carry = attn_chunk(n_full, carry, tail_valid_t)
    for h in range(HEADS):
        o_ref[h] = _softmax_finish_v1(carry[2 + 2 * h]).astype(o_ref.dtype)


def _dsa_attention(P, F, slopes, B, S, *, bq, ck, topk, gaq, giq, gak, gav, gik, iw_col):
    G, T, hw = P.shape
    nq = S // bq
    return pl.pallas_call(
        functools.partial(_dsa_kernel, bq=bq, ck=ck, topk=topk),
        grid=(B, nq),
        in_specs=[
            pl.BlockSpec((HEADS, bq, hw), lambda b, i: (gaq // HEADS, b * nq + i, 0)),
            pl.BlockSpec((IDX_HEADS, bq, hw), lambda b, i: (giq // IDX_HEADS, b * nq + i, 0)),
            pl.BlockSpec((1, S, hw), lambda b, i: (gak, b, 0)),
            pl.BlockSpec((1, S, hw), lambda b, i: (gav, b, 0)),
            pl.BlockSpec((1, S, hw), lambda b, i: (gik, b, 0)),
            pl.BlockSpec((bq, 128), lambda b, i: (b * nq + i, iw_col // 128)),
            pl.BlockSpec(slopes.shape, lambda b, i: (0, 0, 0)),
        ],
        out_specs=pl.BlockSpec((HEADS, bq, hw), lambda b, i: (0, b * nq + i, 0)),
        out_shape=jax.ShapeDtypeStruct((HEADS, T, hw), bf16),
        scratch_shapes=[pltpu.VMEM((S // ck, ck, bq), i32), pltpu.VMEM((S // ck, ck, bq), i16),
                        pltpu.VMEM((S // ck, ck, bq), i16), pltpu.VMEM((1, S, 2 * HEAD_W), bf16)],
        compiler_params=_cparams(("arbitrary", "arbitrary")),
        name="dsa_attention",
    )(P, P, P, P, P, F, slopes)


def _merge_kernel(x_ref, g_ref, sc_ref, sh_ref, g1_ref, ya_ref, yb_ref, yc_ref, yd_ref,
                  wg_ref, wb_ref, wo_ref, o_ref):
    x = x_ref[...]
    D = x.shape[1]
    h = _norm_mod(x, g_ref[...], sc_ref[0], sh_ref[0]).astype(bf16)
    merged = jnp.zeros(x.shape, f32)
    for n, y_ref in enumerate((ya_ref, yb_ref, yc_ref, yd_ref)):
        gate = jax.nn.sigmoid(_dot(h, wg_ref[:, n * D:(n + 1) * D]))
        y = jnp.concatenate([y_ref[hh] for hh in range(HEADS)], axis=1)
        merged = merged + gate * _dot(y, wb_ref[n])
    o_ref[...] = x + g1_ref[0] * _dot(merged.astype(bf16), wo_ref[...])


def _merge(x2, g, sc, sh, g1, ys, wg, wb, wo, S, *, tm):
    T, D = x2.shape
    tpb = S // tm
    row = pl.BlockSpec((tm, D), lambda i: (i, 0))
    per_b = pl.BlockSpec((1, 1, D), lambda i: (i // tpb, 0, 0))
    yspec = pl.BlockSpec((HEADS, tm, HEAD_W), lambda i: (0, i, 0))
    full = lambda a: pl.BlockSpec(a.shape, lambda i: (0,) * a.ndim)
    return pl.pallas_call(
        _merge_kernel,
        grid=(T // tm,),
        in_specs=[row, full(g), per_b, per_b, per_b, yspec, yspec, yspec, yspec, full(wg), full(wb), full(wo)],
        out_specs=row,
        out_shape=jax.ShapeDtypeStruct((T, D), f32),
        compiler_params=_cparams(("arbitrary",)),
        name="merge",
    )(x2, g, sc, sh, g1, *ys, wg, wb, wo)


def _finish(x, g2, f, fg_ref, o_ref):
    out = x + g2 * f
    if fg_ref is not None:
        out = out * lax.rsqrt(jnp.mean(out * out, axis=-1, keepdims=True) + EPS) * fg_ref[...]
    o_ref[...] = out


def _ffn_kernel(x_ref, g_ref, sc_ref, sh_ref, g2_ref, w1_ref, w3_ref, w2_ref, *rest, final):
    fg_ref = rest[0] if final else None
    o_ref, h_scr, acc_scr = rest[-3:]
    j = pl.program_id(1)

    @pl.when(j == 0)
    def _():
        h_scr[...] = _norm_mod(x_ref[...], g_ref[...], sc_ref[0], sh_ref[0]).astype(bf16)
        acc_scr[...] = jnp.zeros_like(acc_scr)

    h = h_scr[...]
    a = _dot(h, w1_ref[...])
    u = (a * jax.nn.sigmoid(a)) * _dot(h, w3_ref[...])
    acc_scr[...] += _dot(u.astype(bf16), w2_ref[...])

    @pl.when(j == pl.num_programs(1) - 1)
    def _():
        _finish(x_ref[...], g2_ref[0], acc_scr[...], fg_ref, o_ref)


def _ffn(x2, g, sc, sh, g2, w1, w3, w2, final_g, S, *, tm, tn):
    T, D = x2.shape
    F = w1.shape[1]
    tpb = S // tm
    row = pl.BlockSpec((tm, D), lambda i, j: (i, 0))
    per_b = pl.BlockSpec((1, 1, D), lambda i, j: (i // tpb, 0, 0))
    vec = pl.BlockSpec((1, D), lambda i, j: (0, 0))
    in_specs = [row, vec, per_b, per_b, per_b,
                pl.BlockSpec((D, tn), lambda i, j: (0, j)),
                pl.BlockSpec((D, tn), lambda i, j: (0, j)),
                pl.BlockSpec((tn, D), lambda i, j: (j, 0))]
    args = [x2, g, sc, sh, g2, w1, w3, w2]
    if final_g is not None:
        in_specs.append(vec)
        args.append(final_g)
    return pl.pallas_call(
        functools.partial(_ffn_kernel, final=final_g is not None),
        grid=(T // tm, F // tn),
        in_specs=in_specs,
        out_specs=row,
        out_shape=jax.ShapeDtypeStruct((T, D), f32),
        scratch_shapes=[pltpu.VMEM((tm, D), bf16), pltpu.VMEM((tm, D), f32)],
        compiler_params=_cparams(("arbitrary", "arbitrary")),
        name="ffn",
    )(*args)


def _router_kernel(x_ref, g_ref, sc_ref, sh_ref, rw_ref, rb_ref, comb_ref, cnt_ref):
    h = _norm_mod(x_ref[...], g_ref[...], sc_ref[0], sh_ref[0]).astype(bf16)
    logits = _dot(h, rw_ref[...]) + rb_ref[...]
    lane = lax.broadcasted_iota(i32, logits.shape, 1)
    big = jnp.int32(logits.shape[1])
    m1 = jnp.max(logits, axis=1, keepdims=True)
    i1 = jnp.min(jnp.where(logits == m1, lane, big), axis=1, keepdims=True)
    rest_l = jnp.where(lane == i1, NEG_INF, logits)
    m2 = jnp.max(rest_l, axis=1, keepdims=True)
    i2 = jnp.min(jnp.where(rest_l == m2, lane, big), axis=1, keepdims=True)
    e2 = jnp.exp(m2 - m1)
    den = 1.0 + e2
    comb = jnp.where(lane == i1, 1.0 / den, jnp.where(lane == i2, e2 / den, 0.0))
    comb_ref[...] = comb
    cnt_ref[0] = jnp.sum(jnp.where(comb > 0.0, 1.0, 0.0), axis=0, keepdims=True)


def _router(x2, g, sc, sh, rw, rb, S, *, tm):
    T, D = x2.shape
    tpb = S // tm
    nt = T // tm
    return pl.pallas_call(
        _router_kernel,
        grid=(nt,),
        in_specs=[pl.BlockSpec((tm, D), lambda i: (i, 0)),
                  pl.BlockSpec((1, D), lambda i: (0, 0)),
                  pl.BlockSpec((1, 1, D), lambda i: (i // tpb, 0, 0)),
                  pl.BlockSpec((1, 1, D), lambda i: (i // tpb, 0, 0)),
                  pl.BlockSpec(rw.shape, lambda i: (0, 0)),
                  pl.BlockSpec(rb.shape, lambda i: (0, 0))],
        out_specs=[pl.BlockSpec((tm, 128), lambda i: (i, 0)),
                   pl.BlockSpec((1, 1, 128), lambda i: (i, 0, 0))],
        out_shape=[jax.ShapeDtypeStruct((T, 128), f32), jax.ShapeDtypeStruct((nt, 1, 128), f32)],
        compiler_params=_cparams(("arbitrary",)),
        name="router",
    )(x2, g, sc, sh, rw, rb)


def _moe_kernel(sid_ref, nch_ref, x_ref, g_ref, sc_ref, sh_ref, g2_ref, comb_ref, w1_ref, w3_ref, w2_ref, *rest,
                final, sizes):
    fg_ref = rest[0] if final else None
    o_ref, h_scr, tri_scr, rank_scr, rank_t_scr, xg_scr, yg_scr, out_scr = rest[-8:]
    i = pl.program_id(0)
    e = pl.program_id(1)
    j = pl.program_id(2)
    n_e = pl.num_programs(1)
    tm = x_ref.shape[0]
    sid = sid_ref[i * n_e + e]
    nch = nch_ref[i * n_e + e]

    def dispatch(phase):
        for k, r in enumerate(sizes):
            @pl.when(sid == k)
            def _(r=r):
                phase(r, 0)

        @pl.when(sid == len(sizes))
        def _():
            def body(c, _):
                phase(sizes[-1], c)
                return 0
            lax.fori_loop(0, nch, body, 0)

    def chunk_rows(rows, c):
        if isinstance(c, int):
            return pl.ds(c * rows, rows)
        return pl.ds(pl.multiple_of(c * rows, 16), rows)

    def first_slot(rows, c):
        return float(c * rows) if isinstance(c, int) else (c * rows).astype(f32)

    @pl.when(jnp.logical_and(i == 0, jnp.logical_and(e == 0, j == 0)))
    def _():
        rr = lax.broadcasted_iota(i32, (tm, tm), 0)
        cc = lax.broadcasted_iota(i32, (tm, tm), 1)
        tri_scr[...] = jnp.where(cc < rr, 1.0, 0.0).astype(bf16)

    @pl.when(jnp.logical_and(e == 0, j == 0))
    def _():
        h_scr[...] = _norm_mod(x_ref[...], g_ref[...], sc_ref[0], sh_ref[0]).astype(bf16)
        out_scr[...] = jnp.zeros_like(out_scr)
        member = jnp.where(comb_ref[...] > 0.0, 1.0, 0.0)
        rank = jnp.where(member > 0.0, _dot(tri_scr[...], member.astype(bf16)), -1.0)
        rank_scr[...] = rank
        rank_t_scr[...] = rank.T

    @pl.when(j == 0)
    def _():
        rank_row = rank_t_scr[pl.ds(e, 1), :]

        def gather(rows, c):
            slot = lax.broadcasted_iota(i32, (rows, tm), 0).astype(f32) + first_slot(rows, c)
            pick = jnp.where(slot == rank_row, 1.0, 0.0).astype(bf16)
            xg_scr[chunk_rows(rows, c), :] = _dot(pick, h_scr[...]).astype(bf16)
            yg_scr[chunk_rows(rows, c), :] = jnp.zeros((rows, yg_scr.shape[1]), f32)

        dispatch(gather)

    def expert(rows, c):
        xg = xg_scr[chunk_rows(rows, c), :]
        a = _dot(xg, w1_ref[0])
        u = (a * jax.nn.sigmoid(a)) * _dot(xg, w3_ref[0])
        yg_scr[chunk_rows(rows, c), :] += _dot(u.astype(bf16), w2_ref[0])

    dispatch(expert)

    last_j = j == pl.num_programs(2) - 1

    @pl.when(last_j)
    def _():
        lane = lax.broadcasted_iota(i32, (tm, 128), 1)
        sel = lane == e
        rank_col = jnp.sum(jnp.where(sel, rank_scr[...], 0.0), axis=1, keepdims=True)
        w_col = jnp.sum(jnp.where(sel, comb_ref[...], 0.0), axis=1, keepdims=True)

        def scatter(rows, c):
            slot = lax.broadcasted_iota(i32, (tm, rows), 1).astype(f32) + first_slot(rows, c)
            place = jnp.where(slot == rank_col, 1.0, 0.0).astype(bf16)
            out_scr[...] += w_col * _dot(place, yg_scr[chunk_rows(rows, c), :].astype(bf16))

        dispatch(scatter)

    @pl.when(jnp.logical_and(last_j, e == n_e - 1))
    def _():
        _finish(x_ref[...], g2_ref[0], out_scr[...], fg_ref, o_ref)


def _moe_plan(cnt, sizes):
    cnt = cnt.astype(i32).reshape(-1)
    sid = sum((cnt > r).astype(i32) for r in sizes)
    nch = (cnt + (sizes[-1] - 1)) // sizes[-1]
    return sid, nch


def _moe(x2, g, sc, sh, g2, comb, sid, nch, w1, w3, w2, final_g, S, *, tm, tn, sizes):
    T, D = x2.shape
    E, _, F = w1.shape
    nj = F // tn
    tpb = S // tm
    cap = -(-tm // sizes[-1]) * sizes[-1]
    row = pl.BlockSpec((tm, D), lambda i, e, j, *_: (i, 0))
    per_b = pl.BlockSpec((1, 1, D), lambda i, e, j, *_: (i // tpb, 0, 0))
    vec = pl.BlockSpec((1, D), lambda i, e, j, *_: (0, 0))
    in_specs = [row, vec, per_b, per_b, per_b,
                pl.BlockSpec((tm, 128), lambda i, e, j, *_: (i, 0)),
                pl.BlockSpec((1, D, tn), lambda i, e, j, *_: (e, 0, j)),
                pl.BlockSpec((1, D, tn), lambda i, e, j, *_: (e, 0, j)),
                pl.BlockSpec((1, tn, D), lambda i, e, j, *_: (e, j, 0))]
    args = [x2, g, sc, sh, g2, comb, w1, w3, w2]
    if final_g is not None:
        in_specs.append(vec)
        args.append(final_g)
    return pl.pallas_call(
        functools.partial(_moe_kernel, final=final_g is not None, sizes=tuple(sizes)),
        grid_spec=pltpu.PrefetchScalarGridSpec(
            num_scalar_prefetch=2,
            grid=(T // tm, E, nj),
            in_specs=in_specs,
            out_specs=row,
            scratch_shapes=[pltpu.VMEM((tm, D), bf16),
                            pltpu.VMEM((tm, tm), bf16),
                            pltpu.VMEM((tm, 128), f32),
                            pltpu.VMEM((128, tm), f32),
                            pltpu.VMEM((cap, D), bf16),
                            pltpu.VMEM((cap, D), f32),
                            pltpu.VMEM((tm, D), f32)]),
        out_shape=jax.ShapeDtypeStruct((T, D), f32),
        compiler_params=_cparams(("arbitrary", "arbitrary", "arbitrary"), MOE_VMEM_LIMIT),
        name="moe",
    )(sid, nch, *args)


_IN_SIZES = (256, 64, 64, 256, 64, 4, 256, 128, 32, 256, 256, 256, 256, 256, 256, 4096)
_IN_NAMES = ("aq", "ak", "av", "iq", "ik", "iw", "cq", "ckv", "kr", "sq", "sk", "sv", "dq", "dk", "dv", "gates")

G_AQ, G_IQ, G_SQ, G_SK, G_SV, G_DQ, G_DK, G_DV, G_AK, G_AV, G_IK = 0, 4, 8, 12, 16, 20, 24, 28, 32, 33, 34
N_GROUPS = 36
F_IW_COL = 640


def _split_w_in(w):
    out, off = {}, 0
    for name, n in zip(_IN_NAMES, _IN_SIZES):
        out[name] = w[:, off:off + n]
        off += n
    return out


def _rot_half_cols(w):
    half = w.shape[1] // 2
    return jnp.concatenate([-w[:, half:], w[:, :half]], axis=1)


def _layer_weights(w_in, w_uq, w_ukv):
    c = _split_w_in(w_in)
    D = w_in.shape[0]
    z = lambda n: jnp.zeros((D, n), f32)
    w_main = jnp.concatenate([
        c["aq"] * 0.125, c["iq"] * 0.125, c["sq"] * 0.125, c["sk"], c["sv"],
        c["dq"] * (DIFF_QK ** -0.5), c["dk"], c["dv"], c["ak"], c["av"], c["ik"], z(64)], axis=1).astype(bf16)
    w_f = jnp.concatenate([
        c["cq"], c["ckv"],
        z(64), c["kr"], z(32),
        z(64), _rot_half_cols(c["kr"]), z(32),
        c["iw"] * (IDX_HEADS ** -0.5), z(124)], axis=1).astype(bf16)
    w_gates = c["gates"].astype(bf16)

    qr = w_uq.shape[0]
    kvr = w_ukv.shape[0]
    wq, wqp, wk, wv = [], [], [], []
    per_q = MLA_NOPE + MLA_ROPE
    per_kv = MLA_NOPE + MLA_V
    for h in range(HEADS):
        nope = w_uq[:, h * per_q:h * per_q + MLA_NOPE]
        rope = w_uq[:, h * per_q + MLA_NOPE:(h + 1) * per_q]
        wq += [nope, rope, jnp.zeros((qr, 32), f32)]
        wqp += [jnp.zeros((qr, 64), f32), _rot_half_cols(rope), jnp.zeros((qr, 32), f32)]
        wk += [w_ukv[:, h * per_kv:h * per_kv + MLA_NOPE], jnp.zeros((kvr, 64), f32)]
        wv += [w_ukv[:, h * per_kv + MLA_NOPE:(h + 1) * per_kv]]
    cat = lambda xs: jnp.concatenate(xs, axis=1).astype(bf16)
    return w_main, w_f, w_gates, cat(wq), cat(wqp), cat(wk), cat(wv)


def _alibi_slopes():
    n = 2 * HEADS
    sl = [2.0 ** (-(8.0 / n) * (k + 1)) for k in range(n)]
    return sl[0::2], sl[1::2]


def kernel(x, c, positions, ada_w, ada_b, norm1_g, norm2_g, w_in, mla_q_norm_g, mla_kv_norm_g, mla_w_uq,
           mla_w_ukv, diff_lq1, diff_lk1, diff_lq2, diff_lk2, diff_norm_g, w_branch, w_out, ffn_w1, ffn_w3,
           ffn_w2, router_w, router_b, moe_w1, moe_w3, moe_w2, final_norm_g):
    B, S, D = x.shape
    T = B * S
    depth = ada_w.shape[0]
    topk = min(TOPK_MAX, S // 4)
    tm = min(1024, S)
    tm_merge = min(512, S)
    bq = min(512, S)
    ck = min(512, S)

    x2 = x.reshape(T, D)
    pos2 = positions.reshape(T, 1)
    mod = _modulation(c, ada_w, ada_b)

    half = MLA_ROPE // 2
    inv = ROPE_THETA ** (-jnp.arange(half, dtype=f32) / half)
    inv_full = jnp.concatenate([jnp.zeros((64,), f32), inv, inv, jnp.zeros((32,), f32)]).reshape(1, 128)
    cos_t, sin_t = _rope_tables(pos2, inv_full, tm=tm)
    sl_a, sl_d = _alibi_slopes()
    slopes_a = jnp.broadcast_to(jnp.asarray(sl_a, f32)[:, None, None], (HEADS, 1, ck))
    slopes_d = jnp.broadcast_to(jnp.asarray(sl_d, f32)[:, None, None], (HEADS, 1, ck))

    row = lambda v: v.reshape(1, -1)
    for l in range(depth):
        sh1, sc1, g1, sh2, sc2, g2 = [mod[l, :, k * D:(k + 1) * D].reshape(B, 1, D) for k in range(6)]
        lam_init = 0.8 - 0.6 * math.exp(-0.3 * l)
        w_main, w_f, w_gates, wq, wqp, wk, wv = _layer_weights(w_in[l], mla_w_uq[l], mla_w_ukv[l])
        n1 = row(norm1_g[l])

        P = _inproj(x2, n1, sc1, sh1, w_main, S, grouped=True, out_dtype=bf16, tm=tm, tn=INPROJ_TN)
        F = _inproj(x2, n1, sc1, sh1, w_f, S, grouped=False, out_dtype=f32, tm=tm, tn=w_f.shape[1])
        qm, km, vm = _mla_prep(F, cos_t, sin_t, row(mla_q_norm_g[l]), row(mla_kv_norm_g[l]), wq, wqp, wk, wv,
                               tm=tm)

        ya = _dsa_attention(P, F, slopes_a, B, S, bq=bq, ck=ck, topk=topk, gaq=G_AQ, giq=G_IQ, gak=G_AK,
                            gav=G_AV, gik=G_IK, iw_col=F_IW_COL)
        yb = _mla_attention(qm, km, vm, B, S, bq=bq, ck=ck)
        yc = _sb_attention(P, B, S, bq=bq, ck=ck, gq=G_SQ, gk=G_SK, gv=G_SV)
        yd = _diff_attention(P, slopes_d, row(diff_lq1[l]), row(diff_lk1[l]), row(diff_lq2[l]),
                             row(diff_lk2[l]), row(diff_norm_g[l]), B, S, bq=bq, ck=ck, lam_init=lam_init,
                             gq=G_DQ, gk=G_DK, gv=G_DV)

        x2 = _merge(x2, n1, sc1, sh1, g1, (ya, yb, yc, yd), w_gates, w_branch[l].astype(bf16),
                    w_out[l].astype(bf16), S, tm=tm_merge)

        final_g = row(final_norm_g) if l == depth - 1 else None
        n2 = row(norm2_g[l])
        if l % 2 == 0:
            k = l // 2
            x2 = _ffn(x2, n2, sc2, sh2, g2, ffn_w1[k].astype(bf16), ffn_w3[k].astype(bf16),
                      ffn_w2[k].astype(bf16), final_g, S, tm=min(FFN_TM, S), tn=FFN_TN)
        else:
            k = l // 2
            E = router_w.shape[2]
            rw = jnp.concatenate([router_w[k], jnp.zeros((D, 128 - E), f32)], axis=1).astype(bf16)
            rb = jnp.concatenate([router_b[k], jnp.full((128 - E,), NEG_INF, f32)]).reshape(1, 128)
            comb, cnt = _router(x2, n2, sc2, sh2, rw, rb, S, tm=tm)
            sid, nch = _moe_plan(cnt[:, 0, :E], MOE_SIZES)
            x2 = _moe(x2, n2, sc2, sh2, g2, comb, sid, nch, moe_w1[k].astype(bf16), moe_w3[k].astype(bf16),
                      moe_w2[k].astype(bf16), final_g, S, tm=tm, tn=MOE_TN, sizes=MOE_SIZES)
    return x2.reshape(B, S, D)
```

```python
import functools
import math

import jax
import jax.numpy as jnp
from jax import lax
from jax.experimental import pallas as pl
from jax.experimental.pallas import tpu as pltpu

f32 = jnp.float32
bf16 = jnp.bfloat16
i32 = jnp.int32
i16 = jnp.int16

N_BRANCHES = 4
HEADS = 4
HEAD_W = 64
IDX_HEADS = 4
TOPK_MAX = 256
MLA_Q_RANK = 256
MLA_KV_RANK = 128
MLA_NOPE = 64
MLA_ROPE = 32
MLA_V = 64
MLA_QK_PAD = 128
ROPE_THETA = 10000.0
DIFF_QK = 32
N_EXPERTS = 8
TOP_K = 2
INPROJ_TN = 768
FFN_TN = 1408
FFN_TM = 512
MOE_TN = 896
MOE_SIZES = (128, 192, 256, 320, 384, 448, 512)
EPS = 1e-6
NEG_INF = float("-inf")
LOG2E = math.log2(math.e)
I16_MIN = -2 ** 15

VMEM_LIMIT = 56 * 1024 * 1024


def _cparams(sem):
    return pltpu.CompilerParams(dimension_semantics=sem, vmem_limit_bytes=VMEM_LIMIT)


def _nt_dot(a, b):
    return lax.dot_general(a, b, (((1,), (1,)), ((), ())), preferred_element_type=f32)


def _dot(a, b):
    return jnp.dot(a, b, preferred_element_type=f32)


def _norm_mod(x, g, sc, sh):
    y = x * lax.rsqrt(jnp.mean(x * x, axis=-1, keepdims=True) + EPS)
    return (y * g) * (1.0 + sc) + sh


def _mod_kernel(c_ref, w_ref, b_ref, o_ref):
    c = c_ref[...]
    cond = (c * jax.nn.sigmoid(c)).astype(bf16)
    o_ref[0] = _dot(cond, w_ref[0].astype(bf16)) + b_ref[0]


def _modulation(c, ada_w, ada_b):
    L, D, N = ada_w.shape
    B = c.shape[0]
    tn = 1536
    return pl.pallas_call(
        _mod_kernel,
        grid=(L, N // tn),
        in_specs=[
            pl.BlockSpec((B, D), lambda l, j: (0, 0)),
            pl.BlockSpec((1, D, tn), lambda l, j: (l, 0, j)),
            pl.BlockSpec((1, 1, tn), lambda l, j: (l, 0, j)),
        ],
        out_specs=pl.BlockSpec((1, B, tn), lambda l, j: (l, 0, j)),
        out_shape=jax.ShapeDtypeStruct((L, B, N), f32),
        compiler_params=_cparams(("arbitrary", "arbitrary")),
        name="modulation",
    )(c, ada_w, ada_b.reshape(L, 1, N))


def _inproj_kernel(x_ref, g_ref, sc_ref, sh_ref, w_ref, o_ref, h_scr, *, grouped):
    @pl.when(pl.program_id(1) == 0)
    def _():
        h_scr[...] = _norm_mod(x_ref[...], g_ref[...], sc_ref[0], sh_ref[0]).astype(bf16)

    res = _dot(h_scr[...], w_ref[...])
    if grouped:
        for k in range(res.shape[1] // HEAD_W):
            o_ref[k] = res[:, k * HEAD_W:(k + 1) * HEAD_W].astype(o_ref.dtype)
    else:
        o_ref[...] = res.astype(o_ref.dtype)


def _inproj(x2, g, sc, sh, w, S, *, grouped, out_dtype, tm, tn):
    T, D = x2.shape
    N = w.shape[1]
    tpb = S // tm
    if grouped:
        out_shape = jax.ShapeDtypeStruct((N // HEAD_W, T, HEAD_W), out_dtype)
        out_spec = pl.BlockSpec((tn // HEAD_W, tm, HEAD_W), lambda i, j: (j, i, 0))
    else:
        out_shape = jax.ShapeDtypeStruct((T, N), out_dtype)
        out_spec = pl.BlockSpec((tm, tn), lambda i, j: (i, j))
    return pl.pallas_call(
        functools.partial(_inproj_kernel, grouped=grouped),
        grid=(T // tm, N // tn),
        in_specs=[
            pl.BlockSpec((tm, D), lambda i, j: (i, 0)),
            pl.BlockSpec((1, D), lambda i, j: (0, 0)),
            pl.BlockSpec((1, 1, D), lambda i, j: (i // tpb, 0, 0)),
            pl.BlockSpec((1, 1, D), lambda i, j: (i // tpb, 0, 0)),
            pl.BlockSpec((D, tn), lambda i, j: (0, j)),
        ],
        out_specs=out_spec,
        out_shape=out_shape,
        scratch_shapes=[pltpu.VMEM((tm, D), bf16)],
        compiler_params=_cparams(("arbitrary", "arbitrary")),
        name="inproj_grouped" if grouped else "inproj_plain",
    )(x2, g, sc, sh, w)


def _rope_kernel(pos_ref, inv_ref, cos_ref, sin_ref):
    ang = pos_ref[...].astype(f32) * inv_ref[...]
    cos_ref[...] = jnp.cos(ang)
    sin_ref[...] = jnp.sin(ang)


def _rope_tables(pos2, inv_full, *, tm):
    T = pos2.shape[0]
    tile = pl.BlockSpec((tm, 128), lambda i: (i, 0))
    return pl.pallas_call(
        _rope_kernel,
        grid=(T // tm,),
        in_specs=[pl.BlockSpec((tm, 1), lambda i: (i, 0)), pl.BlockSpec((1, 128), lambda i: (0, 0))],
        out_specs=[tile, tile],
        out_shape=[jax.ShapeDtypeStruct((T, 128), f32)] * 2,
        compiler_params=_cparams(("arbitrary",)),
        name="rope_tables",
    )(pos2, inv_full)


def _mla_prep_kernel(f_ref, cos_ref, sin_ref, gq_ref, gkv_ref, wq_ref, wqp_ref, wk_ref, wv_ref,
                     q_ref, k_ref, v_ref, *, scale):
    cq = f_ref[:, 0:MLA_Q_RANK]
    ckv = f_ref[:, MLA_Q_RANK:MLA_Q_RANK + MLA_KV_RANK]
    kr = f_ref[:, 384:512]
    krp = f_ref[:, 512:640]
    nq = (cq * lax.rsqrt(jnp.mean(cq * cq, axis=-1, keepdims=True) + EPS) * gq_ref[...]).astype(bf16)
    nkv = (ckv * lax.rsqrt(jnp.mean(ckv * ckv, axis=-1, keepdims=True) + EPS) * gkv_ref[...]).astype(bf16)
    cosf = cos_ref[...]
    sinf = sin_ref[...]
    q1 = _dot(nq, wq_ref[...])
    q2 = _dot(nq, wqp_ref[...])
    kn = _dot(nkv, wk_ref[...])
    vv = _dot(nkv, wv_ref[...])
    krope = kr * cosf + krp * sinf
    for h in range(HEADS):
        sl = slice(h * MLA_QK_PAD, (h + 1) * MLA_QK_PAD)
        q_ref[h] = ((q1[:, sl] * cosf + q2[:, sl] * sinf) * scale).astype(bf16)
        k_ref[h] = (kn[:, sl] + krope).astype(bf16)
        v_ref[h] = vv[:, h * MLA_V:(h + 1) * MLA_V].astype(bf16)


def _mla_prep(F, cos_t, sin_t, gq, gkv, wq, wqp, wk, wv, *, tm):
    T = F.shape[0]
    scale = (MLA_NOPE + MLA_ROPE) ** -0.5
    full = lambda a: pl.BlockSpec(a.shape, lambda i: (0,) * a.ndim)
    return pl.pallas_call(
        functools.partial(_mla_prep_kernel, scale=scale),
        grid=(T // tm,),
        in_specs=[
            pl.BlockSpec((tm, F.shape[1]), lambda i: (i, 0)),
            pl.BlockSpec((tm, 128), lambda i: (i, 0)),
            pl.BlockSpec((tm, 128), lambda i: (i, 0)),
            full(gq), full(gkv), full(wq), full(wqp), full(wk), full(wv),
        ],
        out_specs=[
            pl.BlockSpec((HEADS, tm, MLA_QK_PAD), lambda i: (0, i, 0)),
            pl.BlockSpec((HEADS, tm, MLA_QK_PAD), lambda i: (0, i, 0)),
            pl.BlockSpec((HEADS, tm, MLA_V), lambda i: (0, i, 0)),
        ],
        out_shape=[
            jax.ShapeDtypeStruct((HEADS, T, MLA_QK_PAD), bf16),
            jax.ShapeDtypeStruct((HEADS, T, MLA_QK_PAD), bf16),
            jax.ShapeDtypeStruct((HEADS, T, MLA_V), bf16),
        ],
        compiler_params=_cparams(("arbitrary",)),
        name="mla_prep",
    )(F, cos_t, sin_t, gq, gkv, wq, wqp, wk, wv)


def _chunk_plan(i, bq, ck):
    n_full = (i * bq) // ck
    return n_full, n_full * ck


def _tail_valid(i, bq, ck, tail_off):
    row = lax.broadcasted_iota(i32, (bq, ck), 0)
    col = lax.broadcasted_iota(i32, (bq, ck), 1)
    return col + (tail_off - i * bq) <= row


def _fill_v_ones(v_ref, vp_scr):
    for h in range(v_ref.shape[0]):
        vp_scr[h, :, 0:HEAD_W] = v_ref[h]
        vp_scr[h, :, HEAD_W:2 * HEAD_W] = jnp.ones((v_ref.shape[1], HEAD_W), bf16)


def _softmax_steps_v1(scores, vps, carry, rows_may_be_empty=False):
    n = len(scores)
    ms = [carry[2 * k] for k in range(n)]
    accs = [carry[2 * k + 1] for k in range(n)]
    m_new = [jnp.maximum(ms[k], jnp.max(scores[k], axis=1, keepdims=True)) for k in range(n)]
    m_ref = [jnp.where(m == NEG_INF, 0.0, m) for m in m_new] if rows_may_be_empty else m_new
    ps = [jnp.exp((scores[k] - m_ref[k]).astype(bf16)) for k in range(n)]
    out = ()
    for k in range(n):
        out += (m_new[k], jnp.exp(ms[k] - m_ref[k]) * accs[k] + _dot(ps[k], vps[k]))
    return out


def _softmax_init_v1(bq):
    return (jnp.full((bq, 1), NEG_INF, f32), jnp.zeros((bq, 2 * HEAD_W), f32))


def _softmax_finish_v1(acc):
    return acc[:, 0:HEAD_W] / acc[:, HEAD_W:HEAD_W + 1]


def _mla_attn_kernel(q_ref, k_ref, v_ref, o_ref, vp_scr, *, bq, ck):
    i = pl.program_id(1)
    n_full, tail_off = _chunk_plan(i, bq, ck)

    @pl.when(i == 0)
    def _():
        _fill_v_ones(v_ref, vp_scr)

    def chunk(off, carry, valid):
        off = pl.multiple_of(off, ck)
        scores = [_nt_dot(q_ref[h], k_ref[h, pl.ds(off, ck), :]) for h in range(HEADS)]
        if valid is not None:
            scores = [jnp.where(valid, s, NEG_INF) for s in scores]
        return _softmax_steps_v1(scores, [vp_scr[h, pl.ds(off, ck), :] for h in range(HEADS)], carry)

    carry = lax.fori_loop(0, n_full, lambda c, carry: chunk(c * ck, carry, None), _softmax_init_v1(bq) * HEADS)
    carry = chunk(tail_off, carry, _tail_valid(i, bq, ck, tail_off))
    for h in range(HEADS):
        o_ref[h] = _softmax_finish_v1(carry[2 * h + 1]).astype(o_ref.dtype)


def _mla_attention(qm, km, vm, B, S, *, bq, ck):
    H, T, dk = qm.shape
    dv = vm.shape[2]
    nq = S // bq
    return pl.pallas_call(
        functools.partial(_mla_attn_kernel, bq=bq, ck=ck),
        grid=(B, nq),
        in_specs=[
            pl.BlockSpec((H, bq, dk), lambda b, i: (0, b * nq + i, 0)),
            pl.BlockSpec((H, S, dk), lambda b, i: (0, b, 0)),
            pl.BlockSpec((H, S, dv), lambda b, i: (0, b, 0)),
        ],
        out_specs=pl.BlockSpec((H, bq, dv), lambda b, i: (0, b * nq + i, 0)),
        out_shape=jax.ShapeDtypeStruct((H, T, dv), bf16),
        scratch_shapes=[pltpu.VMEM((H, S, 2 * HEAD_W), bf16)],
        compiler_params=_cparams(("arbitrary", "arbitrary")),
        name="mla_attention",
    )(qm, km, vm)


def _diff_attn_kernel(q_ref, k_ref, v_ref, slope_ref, lq1_ref, lk1_ref, lq2_ref, lk2_ref, g_ref, o_ref,
                      vp_scr, *, bq, ck, lam_init):
    i = pl.program_id(1)
    n_full, tail_off = _chunk_plan(i, bq, ck)
    lane = lax.broadcasted_iota(i32, (bq, q_ref.shape[2]), 1)
    col = lax.broadcasted_iota(i32, (1, ck), 1).astype(f32)

    @pl.when(i == 0)
    def _():
        _fill_v_ones(v_ref, vp_scr)

    def chunk(off, carry, valid):
        off = pl.multiple_of(off, ck)
        pos = col + (off - i * bq).astype(f32)
        scores, vps = [], []
        for h in range(HEADS):
            q = q_ref[h]
            zero = jnp.zeros_like(q)
            k = k_ref[h, pl.ds(off, ck), :]
            for qm in (jnp.where(lane < DIFF_QK, q, zero), jnp.where(lane < DIFF_QK, zero, q)):
                scores.append(_nt_dot(qm, k))
                vps.append(vp_scr[h, pl.ds(off, ck), :])
        for n in range(len(scores)):
            s = scores[n] + slope_ref[n // 2] * pos
            scores[n] = s if valid is None else jnp.where(valid, s, NEG_INF)
        return _softmax_steps_v1(scores, vps, carry)

    carry = lax.fori_loop(0, n_full, lambda c, carry: chunk(c * ck, carry, None),
                          _softmax_init_v1(bq) * (2 * HEADS))
    carry = chunk(tail_off, carry, _tail_valid(i, bq, ck, tail_off))
    lam = (jnp.exp(jnp.sum(lq1_ref[...] * lk1_ref[...], axis=1, keepdims=True))
           - jnp.exp(jnp.sum(lq2_ref[...] * lk2_ref[...], axis=1, keepdims=True)) + lam_init)
    for h in range(HEADS):
        m0, a0, m1, a1 = carry[4 * h:4 * h + 4]
        o = _softmax_finish_v1(a0) - lam * _softmax_finish_v1(a1)
        y = o * lax.rsqrt(jnp.mean(o * o, axis=-1, keepdims=True) + EPS) * g_ref[...]
        o_ref[h] = (y * (1.0 - lam_init)).astype(o_ref.dtype)


def _diff_attention(P, slopes, lq1, lk1, lq2, lk2, g, B, S, *, bq, ck, lam_init, gq, gk, gv):
    G, T, hw = P.shape
    nq = S // bq
    vec = lambda a: pl.BlockSpec(a.shape, lambda b, i: (0,) * a.ndim)
    return pl.pallas_call(
        functools.partial(_diff_attn_kernel, bq=bq, ck=ck, lam_init=lam_init),
        grid=(B, nq),
        in_specs=[
            pl.BlockSpec((HEADS, bq, hw), lambda b, i: (gq // HEADS, b * nq + i, 0)),
            pl.BlockSpec((HEADS, S, hw), lambda b, i: (gk // HEADS, b, 0)),
            pl.BlockSpec((HEADS, S, hw), lambda b, i: (gv // HEADS, b, 0)),
            vec(slopes), vec(lq1), vec(lk1), vec(lq2), vec(lk2), vec(g),
        ],
        out_specs=pl.BlockSpec((HEADS, bq, hw), lambda b, i: (0, b * nq + i, 0)),
        out_shape=jax.ShapeDtypeStruct((HEADS, T, hw), bf16),
        scratch_shapes=[pltpu.VMEM((HEADS, S, 2 * HEAD_W), bf16)],
        compiler_params=_cparams(("arbitrary", "arbitrary")),
        name="diff_attention",
    )(P, P, P, slopes, lq1, lk1, lq2, lk2, g)


SB_GROUP = 256


def _sb_attn_kernel(q_ref, k_ref, v_ref, o_ref, *, bq, ck):
    i = pl.program_id(1)
    n_full, tail_off = _chunk_plan(i, bq, ck)
    dv = v_ref.shape[2]
    gw = min(SB_GROUP, ck)
    rr = lax.broadcasted_iota(i32, (gw, gw), 0)
    cc = lax.broadcasted_iota(i32, (gw, gw), 1)
    upper = jnp.where(rr > cc, 1.0, 0.0).astype(bf16)

    def chunk(off, carry, strict):
        off = pl.multiple_of(off, ck)
        out = ()
        zs = [_nt_dot(q_ref[h], k_ref[h, pl.ds(off, ck), :]) for h in range(HEADS)]
        for h in range(HEADS):
            run, acc = carry[2 * h:2 * h + 2]
            z = zs[h]
            lsm = -(jnp.maximum(z, 0.0) + jnp.log(1.0 + jnp.exp2(jnp.abs(z) * (-LOG2E))))
            if strict is not None:
                lsm = jnp.where(strict, lsm, 0.0)
            parts = []
            for g in reversed(range(ck // gw)):
                x = lsm[:, g * gw:(g + 1) * gw]
                parts.append(_dot(x.astype(bf16), upper) + run)
                run = run + jnp.sum(x, axis=1, keepdims=True)
            between = jnp.concatenate(parts[::-1], axis=1) if len(parts) > 1 else parts[0]
            arg = z + lsm + between
            if strict is not None:
                arg = jnp.where(strict, arg, NEG_INF)
            acc = acc + _dot(jnp.exp(arg.astype(bf16)), v_ref[h, pl.ds(off, ck), :])
            out += (run, acc)
        return out

    row = lax.broadcasted_iota(i32, (bq, ck), 0)
    col = lax.broadcasted_iota(i32, (bq, ck), 1)
    strict = col + (tail_off - i * bq) < row
    init = (jnp.zeros((bq, 1), f32), jnp.zeros((bq, dv), f32)) * HEADS
    carry = chunk(tail_off, init, strict)
    carry = lax.fori_loop(0, n_full, lambda n, carry: chunk((n_full - 1 - n) * ck, carry, None), carry)
    for h in range(HEADS):
        o_ref[h] = carry[2 * h + 1].astype(o_ref.dtype)


def _sb_attention(P, B, S, *, bq, ck, gq, gk, gv):
    G, T, hw = P.shape
    nq = S // bq
    return pl.pallas_call(
        functools.partial(_sb_attn_kernel, bq=bq, ck=ck),
        grid=(B, nq),
        in_specs=[
            pl.BlockSpec((HEADS, bq, hw), lambda b, i: (gq // HEADS, b * nq + i, 0)),
            pl.BlockSpec((HEADS, S, hw), lambda b, i: (gk // HEADS, b, 0)),
            pl.BlockSpec((HEADS, S, hw), lambda b, i: (gv // HEADS, b, 0)),
        ],
        out_specs=pl.BlockSpec((HEADS, bq, hw), lambda b, i: (0, b * nq + i, 0)),
        out_shape=jax.ShapeDtypeStruct((HEADS, T, hw), bf16),
        compiler_params=_cparams(("arbitrary", "arbitrary")),
        name="sb_attention",
    )(P, P, P)


def _dsa_kernel(aq_ref, iq_ref, ak_ref, av_ref, ik_ref, iw_ref, slope_ref, o_ref, key_scr, hi_scr, lo_scr,
                vp_scr, *, bq, ck, topk):
    i = pl.program_id(1)
    n_full, tail_off = _chunk_plan(i, bq, ck)
    n_chunks = n_full + 1

    @pl.when(i == 0)
    def _():
        _fill_v_ones(av_ref, vp_scr)

    w_t = iw_ref[...].T
    gw = min(SB_GROUP, ck)
    rr = lax.broadcasted_iota(i32, (gw, gw), 0)
    cc = lax.broadcasted_iota(i32, (gw, gw), 1)
    incl = jnp.where(cc <= rr, 1.0, 0.0).astype(bf16)
    col = lax.broadcasted_iota(i32, (1, ck), 1).astype(f32)
    key_i = lax.broadcasted_iota(i32, (ck, bq), 0)
    qry_i = lax.broadcasted_iota(i32, (ck, bq), 1)
    tail_valid_t = key_i + (tail_off - i * bq) <= qry_i

    def score_chunk(c, valid_t):
        off = pl.multiple_of(c * ck, ck)
        ik = ik_ref[0, pl.ds(off, ck), :]
        sc = jnp.zeros((ck, bq), f32)
        for h in range(IDX_HEADS):
            sc = sc + jnp.maximum(_nt_dot(ik, iq_ref[h]), 0.0) * w_t[h:h + 1, :]
        if valid_t is not None:
            sc = jnp.where(valid_t, sc, NEG_INF)
        bits = lax.bitcast_convert_type(sc + 0.0, i32)
        key = jnp.where(bits < 0, bits ^ jnp.int32(0x7FFFFFFF), bits)
        key_scr[c] = key
        hi_scr[c] = lax.shift_right_arithmetic(key, 16).astype(i16)
        lo_scr[c] = ((key & 0xFFFF) - 32768).astype(i16)

    def score_body(c, _):
        score_chunk(c, None)
        return 0

    lax.fori_loop(0, n_full, score_body, 0)
    score_chunk(n_full, tail_valid_t)

    lanes_acc = 32

    def rows(v):
        return jnp.broadcast_to(v, (lanes_acc, bq))[None]

    def count16(ref, pred):
        def body(c, acc):
            v = ref[c].reshape(ck // lanes_acc, lanes_acc, bq)
            hit = jnp.where(pred(v), jnp.int16(1), jnp.int16(0))
            parts = [hit[g] for g in range(ck // lanes_acc)]
            while len(parts) > 1:
                parts = [a + b for a, b in zip(parts[0::2], parts[1::2])]
            return acc + parts[0]
        acc = lax.fori_loop(0, n_chunks, body, jnp.zeros((lanes_acc, bq), i16))
        return jnp.sum(acc.astype(f32), axis=0, keepdims=True)

    def search16(ref, need):
        def body(it, prefix):
            cand = prefix + lax.shift_left(jnp.int32(1), jnp.int32(15) - it)
            cand_r = rows(cand.astype(i16))
            cnt = count16(ref, lambda v: v >= cand_r)
            return jnp.where(cnt >= need, cand, prefix)
        return lax.fori_loop(0, 16, body, jnp.full((1, bq), I16_MIN, i32))

    tau_hi = search16(hi_scr, float(topk))
    tau_hi_r = rows(tau_hi.astype(i16))
    need_lo = float(topk) - count16(hi_scr, lambda v: v > tau_hi_r)
    tau_hi_b = tau_hi.astype(i16)

    def mask_lo(c, _):
        lo_scr[c] = jnp.where(hi_scr[c] == tau_hi_b, lo_scr[c], jnp.int16(I16_MIN))
        return 0

    lax.fori_loop(0, n_chunks, mask_lo, 0)
    tau_lo = search16(lo_scr, need_lo)
    tau = tau_hi * 65536 + (tau_lo + 32768)

    def count(pred):
        def body(c, acc):
            key = key_scr[c].reshape(ck // lanes_acc, lanes_acc, bq)
            return acc + jnp.sum(jnp.where(pred(key), 1.0, 0.0), axis=0)
        acc = lax.fori_loop(0, n_chunks, body, jnp.zeros((lanes_acc, bq), f32))
        return jnp.sum(acc, axis=0, keepdims=True)

    tau_r = rows(tau)
    n_take = float(topk) - count(lambda key: key > tau_r)

    def attn_chunk(c, carry, valid_t):
        off = pl.multiple_of(c * ck, ck)
        eq_run = carry[0]
        key = key_scr[c]
        eq = key == tau
        eqf = jnp.where(eq, 1.0, 0.0)
        pcs = []
        for g in range(ck // gw):
            e = eqf[g * gw:(g + 1) * gw, :]
            pcs.append(_dot(incl, e.astype(bf16)) + eq_run)
            eq_run = eq_run + jnp.sum(e, axis=0, keepdims=True)
        pc = jnp.concatenate(pcs, axis=0) if len(pcs) > 1 else pcs[0]
        addm_t = jnp.where(key > tau, 0.0, jnp.where(eq, jnp.where(pc <= n_take, 0.0, NEG_INF), NEG_INF))
        if valid_t is not None:
            addm_t = jnp.where(valid_t, addm_t, NEG_INF)
        addm = addm_t.T
        k = ak_ref[0, pl.ds(off, ck), :]
        vp = vp_scr[0, pl.ds(off, ck), :]
        pos = col + (off - i * bq).astype(f32)
        scores = [_nt_dot(aq_ref[h], k) for h in range(HEADS)]
        scores = [scores[h] + slope_ref[h] * pos + addm for h in range(HEADS)]
        return (eq_run,) + _softmax_steps_v1(scores, [vp] * HEADS, carry[1:], rows_may_be_empty=True)

    init = (jnp.zeros((1, bq), f32),) + _softmax_init_v1(bq) * HEADS
    carry = lax.fori_loop(0, n_full, lambda c, carry: attn_chunk(c, carry, None), init)
    carry = attn_chunk(n_full, carry, tail_valid_t)
    for h in range(HEADS):
        o_ref[h] = _softmax_finish_v1(carry[2 + 2 * h]).astype(o_ref.dtype)


def _dsa_attention(P, F, slopes, B, S, *, bq, ck, topk, gaq, giq, gak, gav, gik, iw_col):
    G, T, hw = P.shape
    nq = S // bq
    return pl.pallas_call(
        functools.partial(_dsa_kernel, bq=bq, ck=ck, topk=topk),
        grid=(B, nq),
        in_specs=[
            pl.BlockSpec((HEADS, bq, hw), lambda b, i: (gaq // HEADS, b * nq + i, 0)),
            pl.BlockSpec((IDX_HEADS, bq, hw), lambda b, i: (giq // IDX_HEADS, b * nq + i, 0)),
            pl.BlockSpec((1, S, hw), lambda b, i: (gak, b, 0)),
            pl.BlockSpec((1, S, hw), lambda b, i: (gav, b, 0)),
            pl.BlockSpec((1, S, hw), lambda b, i: (gik, b, 0)),
            pl.BlockSpec((bq, 128), lambda b, i: (b * nq + i, iw_col // 128)),
            pl.BlockSpec(slopes.shape, lambda b, i: (0, 0, 0)),
        ],
        out_specs=pl.BlockSpec((HEADS, bq, hw), lambda b, i: (0, b * nq + i, 0)),
        out_shape=jax.ShapeDtypeStruct((HEADS, T, hw), bf16),
        scratch_shapes=[pltpu.VMEM((S // ck, ck, bq), i32), pltpu.VMEM((S // ck, ck, bq), i16),
                        pltpu.VMEM((S // ck, ck, bq), i16), pltpu.VMEM((1, S, 2 * HEAD_W), bf16)],
        compiler_params=_cparams(("arbitrary", "arbitrary")),
        name="dsa_attention",
    )(P, P, P, P, P, F, slopes)


def _merge_kernel(x_ref, g_ref, sc_ref, sh_ref, g1_ref, ya_ref, yb_ref, yc_ref, yd_ref,
                  wg_ref, wb_ref, wo_ref, o_ref):
    x = x_ref[...]
    D = x.shape[1]
    h = _norm_mod(x, g_ref[...], sc_ref[0], sh_ref[0]).astype(bf16)
    merged = jnp.zeros(x.shape, f32)
    for n, y_ref in enumerate((ya_ref, yb_ref, yc_ref, yd_ref)):
        gate = jax.nn.sigmoid(_dot(h, wg_ref[:, n * D:(n + 1) * D]))
        y = jnp.concatenate([y_ref[hh] for hh in range(HEADS)], axis=1)
        merged = merged + gate * _dot(y, wb_ref[n])
    o_ref[...] = x + g1_ref[0] * _dot(merged.astype(bf16), wo_ref[...])


def _merge(x2, g, sc, sh, g1, ys, wg, wb, wo, S, *, tm):
    T, D = x2.shape
    tpb = S // tm
    row = pl.BlockSpec((tm, D), lambda i: (i, 0))
    per_b = pl.BlockSpec((1, 1, D), lambda i: (i // tpb, 0, 0))
    yspec = pl.BlockSpec((HEADS, tm, HEAD_W), lambda i: (0, i, 0))
    full = lambda a: pl.BlockSpec(a.shape, lambda i: (0,) * a.ndim)
    return pl.pallas_call(
        _merge_kernel,
        grid=(T // tm,),
        in_specs=[row, full(g), per_b, per_b, per_b, yspec, yspec, yspec, yspec, full(wg), full(wb), full(wo)],
        out_specs=row,
        out_shape=jax.ShapeDtypeStruct((T, D), f32),
        compiler_params=_cparams(("arbitrary",)),
        name="merge",
    )(x2, g, sc, sh, g1, *ys, wg, wb, wo)


def _finish(x, g2, f, fg_ref, o_ref):
    out = x + g2 * f
    if fg_ref is not None:
        out = out * lax.rsqrt(jnp.mean(out * out, axis=-1, keepdims=True) + EPS) * fg_ref[...]
    o_ref[...] = out


def _ffn_kernel(x_ref, g_ref, sc_ref, sh_ref, g2_ref, w1_ref, w3_ref, w2_ref, *rest, final):
    fg_ref = rest[0] if final else None
    o_ref, h_scr, acc_scr = rest[-3:]
    j = pl.program_id(1)

    @pl.when(j == 0)
    def _():
        h_scr[...] = _norm_mod(x_ref[...], g_ref[...], sc_ref[0], sh_ref[0]).astype(bf16)
        acc_scr[...] = jnp.zeros_like(acc_scr)

    h = h_scr[...]
    a = _dot(h, w1_ref[...])
    u = (a * jax.nn.sigmoid(a)) * _dot(h, w3_ref[...])
    acc_scr[...] += _dot(u.astype(bf16), w2_ref[...])

    @pl.when(j == pl.num_programs(1) - 1)
    def _():
        _finish(x_ref[...], g2_ref[0], acc_scr[...], fg_ref, o_ref)


def _ffn(x2, g, sc, sh, g2, w1, w3, w2, final_g, S, *, tm, tn):
    T, D = x2.shape
    F = w1.shape[1]
    tpb = S // tm
    row = pl.BlockSpec((tm, D), lambda i, j: (i, 0))
    per_b = pl.BlockSpec((1, 1, D), lambda i, j: (i // tpb, 0, 0))
    vec = pl.BlockSpec((1, D), lambda i, j: (0, 0))
    in_specs = [row, vec, per_b, per_b, per_b,
                pl.BlockSpec((D, tn), lambda i, j: (0, j)),
                pl.BlockSpec((D, tn), lambda i, j: (0, j)),
                pl.BlockSpec((tn, D), lambda i, j: (j, 0))]
    args = [x2, g, sc, sh, g2, w1, w3, w2]
    if final_g is not None:
        in_specs.append(vec)
        args.append(final_g)
    return pl.pallas_call(
        functools.partial(_ffn_kernel, final=final_g is not None),
        grid=(T // tm, F // tn),
        in_specs=in_specs,
        out_specs=row,
        out_shape=jax.ShapeDtypeStruct((T, D), f32),
        scratch_shapes=[pltpu.VMEM((tm, D), bf16), pltpu.VMEM((tm, D), f32)],
        compiler_params=_cparams(("arbitrary", "arbitrary")),
        name="ffn",
    )(*args)


def _router_kernel(x_ref, g_ref, sc_ref, sh_ref, rw_ref, rb_ref, comb_ref, cnt_ref):
    h = _norm_mod(x_ref[...], g_ref[...], sc_ref[0], sh_ref[0]).astype(bf16)
    logits = _dot(h, rw_ref[...]) + rb_ref[...]
    lane = lax.broadcasted_iota(i32, logits.shape, 1)
    big = jnp.int32(logits.shape[1])
    m1 = jnp.max(logits, axis=1, keepdims=True)
    i1 = jnp.min(jnp.where(logits == m1, lane, big), axis=1, keepdims=True)
    rest_l = jnp.where(lane == i1, NEG_INF, logits)
    m2 = jnp.max(rest_l, axis=1, keepdims=True)
    i2 = jnp.min(jnp.where(rest_l == m2, lane, big), axis=1, keepdims=True)
    e2 = jnp.exp(m2 - m1)
    den = 1.0 + e2
    comb = jnp.where(lane == i1, 1.0 / den, jnp.where(lane == i2, e2 / den, 0.0))
    comb_ref[...] = comb
    cnt_ref[0] = jnp.sum(jnp.where(comb > 0.0, 1.0, 0.0), axis=0, keepdims=True)


def _router(x2, g, sc, sh, rw, rb, S, *, tm):
    T, D = x2.shape
    tpb = S // tm
    nt = T // tm
    return pl.pallas_call(
        _router_kernel,
        grid=(nt,),
        in_specs=[pl.BlockSpec((tm, D), lambda i: (i, 0)),
                  pl.BlockSpec((1, D), lambda i: (0, 0)),
                  pl.BlockSpec((1, 1, D), lambda i: (i // tpb, 0, 0)),
                  pl.BlockSpec((1, 1, D), lambda i: (i // tpb, 0, 0)),
                  pl.BlockSpec(rw.shape, lambda i: (0, 0)),
                  pl.BlockSpec(rb.shape, lambda i: (0, 0))],
        out_specs=[pl.BlockSpec((tm, 128), lambda i: (i, 0)),
                   pl.BlockSpec((1, 1, 128), lambda i: (i, 0, 0))],
        out_shape=[jax.ShapeDtypeStruct((T, 128), f32), jax.ShapeDtypeStruct((nt, 1, 128), f32)],
        compiler_params=_cparams(("arbitrary",)),
        name="router",
    )(x2, g, sc, sh, rw, rb)


def _moe_kernel(sid_ref, nch_ref, x_ref, g_ref, sc_ref, sh_ref, g2_ref, comb_ref, w1_ref, w3_ref, w2_ref, *rest,
                final, sizes):
    fg_ref = rest[0] if final else None
    o_ref, h_scr, tri_scr, rank_scr, rank_t_scr, xg_scr, yg_scr, out_scr = rest[-8:]
    i = pl.program_id(0)
    e = pl.program_id(1)
    j = pl.program_id(2)
    n_e = pl.num_programs(1)
    tm = x_ref.shape[0]
    sid = sid_ref[i * n_e + e]
    nch = nch_ref[i * n_e + e]

    def dispatch(phase):
        for k, r in enumerate(sizes):
            @pl.when(sid == k)
            def _(r=r):
                phase(r, 0)

        @pl.when(sid == len(sizes))
        def _():
            def body(c, _):
                phase(sizes[-1], c)
                return 0
            lax.fori_loop(0, nch, body, 0)

    def chunk_rows(rows, c):
        if isinstance(c, int):
            return pl.ds(c * rows, rows)
        return pl.ds(pl.multiple_of(c * rows, 16), rows)

    def first_slot(rows, c):
        return float(c * rows) if isinstance(c, int) else (c * rows).astype(f32)

    @pl.when(jnp.logical_and(i == 0, jnp.logical_and(e == 0, j == 0)))
    def _():
        rr = lax.broadcasted_iota(i32, (tm, tm), 0)
        cc = lax.broadcasted_iota(i32, (tm, tm), 1)
        tri_scr[...] = jnp.where(cc < rr, 1.0, 0.0).astype(bf16)

    @pl.when(jnp.logical_and(e == 0, j == 0))
    def _():
        h_scr[...] = _norm_mod(x_ref[...], g_ref[...], sc_ref[0], sh_ref[0]).astype(bf16)
        out_scr[...] = jnp.zeros_like(out_scr)
        member = jnp.where(comb_ref[...] > 0.0, 1.0, 0.0)
        rank = jnp.where(member > 0.0, _dot(tri_scr[...], member.astype(bf16)), -1.0)
        rank_scr[...] = rank
        rank_t_scr[...] = rank.T

    @pl.when(j == 0)
    def _():
        rank_row = rank_t_scr[pl.ds(e, 1), :]

        def gather(rows, c):
            slot = lax.broadcasted_iota(i32, (rows, tm), 0).astype(f32) + first_slot(rows, c)
            pick = jnp.where(slot == rank_row, 1.0, 0.0).astype(bf16)
            xg_scr[chunk_rows(rows, c), :] = _dot(pick, h_scr[...]).astype(bf16)
            yg_scr[chunk_rows(rows, c), :] = jnp.zeros((rows, yg_scr.shape[1]), f32)

        dispatch(gather)

    def expert(rows, c):
        xg = xg_scr[chunk_rows(rows, c), :]
        a = _dot(xg, w1_ref[0])
        u = (a * jax.nn.sigmoid(a)) * _dot(xg, w3_ref[0])
        yg_scr[chunk_rows(rows, c), :] += _dot(u.astype(bf16), w2_ref[0])

    dispatch(expert)

    last_j = j == pl.num_programs(2) - 1

    @pl.when(last_j)
    def _():
        lane = lax.broadcasted_iota(i32, (tm, 128), 1)
        sel = lane == e
        rank_col = jnp.sum(jnp.where(sel, rank_scr[...], 0.0), axis=1, keepdims=True)
        w_col = jnp.sum(jnp.where(sel, comb_ref[...], 0.0), axis=1, keepdims=True)

        def scatter(rows, c):
            slot = lax.broadcasted_iota(i32, (tm, rows), 1).astype(f32) + first_slot(rows, c)
            place = jnp.where(slot == rank_col, 1.0, 0.0).astype(bf16)
            out_scr[...] += w_col * _dot(place, yg_scr[chunk_rows(rows, c), :].astype(bf16))

        dispatch(scatter)

    @pl.when(jnp.logical_and(last_j, e == n_e - 1))
    def _():
        _finish(x_ref[...], g2_ref[0], out_scr[...], fg_ref, o_ref)


def _moe_plan(cnt, sizes):
    cnt = cnt.astype(i32).reshape(-1)
    sid = sum((cnt > r).astype(i32) for r in sizes)
    nch = (cnt + (sizes[-1] - 1)) // sizes[-1]
    return sid, nch


def _moe(x2, g, sc, sh, g2, comb, sid, nch, w1, w3, w2, final_g, S, *, tm, tn, sizes):
    T, D = x2.shape
    E, _, F = w1.shape
    nj = F // tn
    tpb = S // tm
    cap = -(-tm // sizes[-1]) * sizes[-1]
    row = pl.BlockSpec((tm, D), lambda i, e, j, *_: (i, 0))
    per_b = pl.BlockSpec((1, 1, D), lambda i, e, j, *_: (i // tpb, 0, 0))
    vec = pl.BlockSpec((1, D), lambda i, e, j, *_: (0, 0))
    in_specs = [row, vec, per_b, per_b, per_b,
                pl.BlockSpec((tm, 128), lambda i, e, j, *_: (i, 0)),
                pl.BlockSpec((1, D, tn), lambda i, e, j, *_: (e, 0, j)),
                pl.BlockSpec((1, D, tn), lambda i, e, j, *_: (e, 0, j)),
                pl.BlockSpec((1, tn, D), lambda i, e, j, *_: (e, j, 0))]
    args = [x2, g, sc, sh, g2, comb, w1, w3, w2]
    if final_g is not None:
        in_specs.append(vec)
        args.append(final_g)
    return pl.pallas_call(
        functools.partial(_moe_kernel, final=final_g is not None, sizes=tuple(sizes)),
        grid_spec=pltpu.PrefetchScalarGridSpec(
            num_scalar_prefetch=2,
            grid=(T // tm, E, nj),
            in_specs=in_specs,
            out_specs=row,
            scratch_shapes=[pltpu.VMEM((tm, D), bf16),
                            pltpu.VMEM((tm, tm), bf16),
                            pltpu.VMEM((tm, 128), f32),
                            pltpu.VMEM((128, tm), f32),
                            pltpu.VMEM((cap, D), bf16),
                            pltpu.VMEM((cap, D), f32),
                            pltpu.VMEM((tm, D), f32)]),
        out_shape=jax.ShapeDtypeStruct((T, D), f32),
        compiler_params=_cparams(("arbitrary", "arbitrary", "arbitrary")),
        name="moe",
    )(sid, nch, *args)


_IN_SIZES = (256, 64, 64, 256, 64, 4, 256, 128, 32, 256, 256, 256, 256, 256, 256, 4096)
_IN_NAMES = ("aq", "ak", "av", "iq", "ik", "iw", "cq", "ckv", "kr", "sq", "sk", "sv", "dq", "dk", "dv", "gates")

G_AQ, G_IQ, G_SQ, G_SK, G_SV, G_DQ, G_DK, G_DV, G_AK, G_AV, G_IK = 0, 4, 8, 12, 16, 20, 24, 28, 32, 33, 34
N_GROUPS = 36
F_IW_COL = 640


def _split_w_in(w):
    out, off = {}, 0
    for name, n in zip(_IN_NAMES, _IN_SIZES):
        out[name] = w[:, off:off + n]
        off += n
    return out


def _rot_half_cols(w):
    half = w.shape[1] // 2
    return jnp.concatenate([-w[:, half:], w[:, :half]], axis=1)


def _layer_weights(w_in, w_uq, w_ukv):
    c = _split_w_in(w_in)
    D = w_in.shape[0]
    z = lambda n: jnp.zeros((D, n), f32)
    w_main = jnp.concatenate([
        c["aq"] * 0.125, c["iq"] * 0.125, c["sq"] * 0.125, c["sk"], c["sv"],
        c["dq"] * (DIFF_QK ** -0.5), c["dk"], c["dv"], c["ak"], c["av"], c["ik"], z(64)], axis=1).astype(bf16)
    w_f = jnp.concatenate([
        c["cq"], c["ckv"],
        z(64), c["kr"], z(32),
        z(64), _rot_half_cols(c["kr"]), z(32),
        c["iw"] * (IDX_HEADS ** -0.5), z(124)], axis=1).astype(bf16)
    w_gates = c["gates"].astype(bf16)

    qr = w_uq.shape[0]
    kvr = w_ukv.shape[0]
    wq, wqp, wk, wv = [], [], [], []
    per_q = MLA_NOPE + MLA_ROPE
    per_kv = MLA_NOPE + MLA_V
    for h in range(HEADS):
        nope = w_uq[:, h * per_q:h * per_q + MLA_NOPE]
        rope = w_uq[:, h * per_q + MLA_NOPE:(h + 1) * per_q]
        wq += [nope, rope, jnp.zeros((qr, 32), f32)]
        wqp += [jnp.zeros((qr, 64), f32), _rot_half_cols(rope), jnp.zeros((qr, 32), f32)]
        wk += [w_ukv[:, h * per_kv:h * per_kv + MLA_NOPE], jnp.zeros((kvr, 64), f32)]
        wv += [w_ukv[:, h * per_kv + MLA_NOPE:(h + 1) * per_kv]]
    cat = lambda xs: jnp.concatenate(xs, axis=1).astype(bf16)
    return w_main, w_f, w_gates, cat(wq), cat(wqp), cat(wk), cat(wv)


def _alibi_slopes():
    n = 2 * HEADS
    sl = [2.0 ** (-(8.0 / n) * (k + 1)) for k in range(n)]
    return sl[0::2], sl[1::2]


def kernel(x, c, positions, ada_w, ada_b, norm1_g, norm2_g, w_in, mla_q_norm_g, mla_kv_norm_g, mla_w_uq,
           mla_w_ukv, diff_lq1, diff_lk1, diff_lq2, diff_lk2, diff_norm_g, w_branch, w_out, ffn_w1, ffn_w3,
           ffn_w2, router_w, router_b, moe_w1, moe_w3, moe_w2, final_norm_g):
    B, S, D = x.shape
    T = B * S
    depth = ada_w.shape[0]
    topk = min(TOPK_MAX, S // 4)
    tm = min(1024, S)
    tm_merge = min(512, S)
    bq = min(512, S)
    ck = min(512, S)

    x2 = x.reshape(T, D)
    pos2 = positions.reshape(T, 1)
    mod = _modulation(c, ada_w, ada_b)

    half = MLA_ROPE // 2
    inv = ROPE_THETA ** (-jnp.arange(half, dtype=f32) / half)
    inv_full = jnp.concatenate([jnp.zeros((64,), f32), inv, inv, jnp.zeros((32,), f32)]).reshape(1, 128)
    cos_t, sin_t = _rope_tables(pos2, inv_full, tm=tm)
    sl_a, sl_d = _alibi_slopes()
    slopes_a = jnp.broadcast_to(jnp.asarray(sl_a, f32)[:, None, None], (HEADS, 1, ck))
    slopes_d = jnp.broadcast_to(jnp.asarray(sl_d, f32)[:, None, None], (HEADS, 1, ck))

    row = lambda v: v.reshape(1, -1)
    for l in range(depth):
        sh1, sc1, g1, sh2, sc2, g2 = [mod[l, :, k * D:(k + 1) * D].reshape(B, 1, D) for k in range(6)]
        lam_init = 0.8 - 0.6 * math.exp(-0.3 * l)
        w_main, w_f, w_gates, wq, wqp, wk, wv = _layer_weights(w_in[l], mla_w_uq[l], mla_w_ukv[l])
        n1 = row(norm1_g[l])

        P = _inproj(x2, n1, sc1, sh1, w_main, S, grouped=True, out_dtype=bf16, tm=tm, tn=INPROJ_TN)
        F = _inproj(x2, n1, sc1, sh1, w_f, S, grouped=False, out_dtype=f32, tm=tm, tn=w_f.shape[1])
        qm, km, vm = _mla_prep(F, cos_t, sin_t, row(mla_q_norm_g[l]), row(mla_kv_norm_g[l]), wq, wqp, wk, wv,
                               tm=tm)

        ya = _dsa_attention(P, F, slopes_a, B, S, bq=bq, ck=ck, topk=topk, gaq=G_AQ, giq=G_IQ, gak=G_AK,
                            gav=G_AV, gik=G_IK, iw_col=F_IW_COL)
        yb = _mla_attention(qm, km, vm, B, S, bq=bq, ck=ck)
        yc = _sb_attention(P, B, S, bq=bq, ck=ck, gq=G_SQ, gk=G_SK, gv=G_SV)
        yd = _diff_attention(P, slopes_d, row(diff_lq1[l]), row(diff_lk1[l]), row(diff_lq2[l]),
                             row(diff_lk2[l]), row(diff_norm_g[l]), B, S, bq=bq, ck=ck, lam_init=lam_init,
                             gq=G_DQ, gk=G_DK, gv=G_DV)

        x2 = _merge(x2, n1, sc1, sh1, g1, (ya, yb, yc, yd), w_gates, w_branch[l].astype(bf16),
                    w_out[l].astype(bf16), S, tm=tm_merge)

        final_g = row(final_norm_g) if l == depth - 1 else None
        n2 = row(norm2_g[l])
        if l % 2 == 0:
            k = l // 2
            x2 = _ffn(x2, n2, sc2, sh2, g2, ffn_w1[k].astype(bf16), ffn_w3[k].astype(bf16),
                      ffn_w2[k].astype(bf16), final_g, S, tm=min(FFN_TM, S), tn=FFN_TN)
        else:
            k = l // 2
            E = router_w.shape[2]
            rw = jnp.concatenate([router_w[k], jnp.zeros((D, 128 - E), f32)], axis=1).astype(bf16)
            rb = jnp.concatenate([router_b[k], jnp.full((128 - E,), NEG_INF, f32)]).reshape(1, 128)
            comb, cnt = _router(x2, n2, sc2, sh2, rw, rb, S, tm=tm)
            sid, nch = _moe_plan(cnt[:, 0, :E], MOE_SIZES)
            x2 = _moe(x2, n2, sc2, sh2, g2, comb, sid, nch, moe_w1[k].astype(bf16), moe_w3[k].astype(bf16),
                      moe_w2[k].astype(bf16), final_g, S, tm=tm, tn=MOE_TN, sizes=MOE_SIZES)
    return x2.reshape(B, S, D)
```

```python
import functools
import math

import jax
import jax.numpy as jnp
from jax import lax
from jax.experimental import pallas as pl
from jax.experimental.pallas import tpu as pltpu

f32 = jnp.float32
bf16 = jnp.bfloat16
i32 = jnp.int32
i16 = jnp.int16

N_BRANCHES = 4
HEADS = 4
HEAD_W = 64
IDX_HEADS = 4
TOPK_MAX = 256
MLA_Q_RANK = 256
MLA_KV_RANK = 128
MLA_NOPE = 64
MLA_ROPE = 32
MLA_V = 64
MLA_QK_PAD = 128
ROPE_THETA = 10000.0
DIFF_QK = 32
N_EXPERTS = 8
TOP_K = 2
INPROJ_TN = 768
FFN_TN = 1408
FFN_TM = 512
MOE_TN = 896
MOE_SIZES = (128, 192, 256, 320, 384, 448, 512)
EPS = 1e-6
NEG_INF = float("-inf")
LOG2E = math.log2(math.e)
I16_MIN = -2 ** 15

VMEM_LIMIT = 56 * 1024 * 1024


def _cparams(sem):
    return pltpu.CompilerParams(dimension_semantics=sem, vmem_limit_bytes=VMEM_LIMIT)


def _nt_dot(a, b):
    return lax.dot_general(a, b, (((1,), (1,)), ((), ())), preferred_element_type=f32)


def _dot(a, b):
    return jnp.dot(a, b, preferred_element_type=f32)


def _norm_mod(x, g, sc, sh):
    y = x * lax.rsqrt(jnp.mean(x * x, axis=-1, keepdims=True) + EPS)
    return (y * g) * (1.0 + sc) + sh


def _mod_kernel(c_ref, w_ref, b_ref, o_ref):
    c = c_ref[...]
    cond = (c * jax.nn.sigmoid(c)).astype(bf16)
    o_ref[0] = _dot(cond, w_ref[0].astype(bf16)) + b_ref[0]


def _modulation(c, ada_w, ada_b):
    L, D, N = ada_w.shape
    B = c.shape[0]
    tn = 1536
    return pl.pallas_call(
        _mod_kernel,
        grid=(L, N // tn),
        in_specs=[
            pl.BlockSpec((B, D), lambda l, j: (0, 0)),
            pl.BlockSpec((1, D, tn), lambda l, j: (l, 0, j)),
            pl.BlockSpec((1, 1, tn), lambda l, j: (l, 0, j)),
        ],
        out_specs=pl.BlockSpec((1, B, tn), lambda l, j: (l, 0, j)),
        out_shape=jax.ShapeDtypeStruct((L, B, N), f32),
        compiler_params=_cparams(("arbitrary", "arbitrary")),
        name="modulation",
    )(c, ada_w, ada_b.reshape(L, 1, N))


def _inproj_kernel(x_ref, g_ref, sc_ref, sh_ref, w_ref, o_ref, h_scr, *, grouped):
    @pl.when(pl.program_id(1) == 0)
    def _():
        h_scr[...] = _norm_mod(x_ref[...], g_ref[...], sc_ref[0], sh_ref[0]).astype(bf16)

    res = _dot(h_scr[...], w_ref[...])
    if grouped:
        for k in range(res.shape[1] // HEAD_W):
            o_ref[k] = res[:, k * HEAD_W:(k + 1) * HEAD_W].astype(o_ref.dtype)
    else:
        o_ref[...] = res.astype(o_ref.dtype)


def _inproj_both_kernel(x_ref, g_ref, sc_ref, sh_ref, w_ref, wf_ref, o_ref, f_ref, h_scr):
    @pl.when(pl.program_id(1) == 0)
    def _():
        h = _norm_mod(x_ref[...], g_ref[...], sc_ref[0], sh_ref[0]).astype(bf16)
        h_scr[...] = h
        f_ref[...] = _dot(h, wf_ref[...])

    res = _dot(h_scr[...], w_ref[...])
    for k in range(res.shape[1] // HEAD_W):
        o_ref[k] = res[:, k * HEAD_W:(k + 1) * HEAD_W].astype(o_ref.dtype)


def _inproj_both(x2, g, sc, sh, w, wf, S, *, tm, tn):
    T, D = x2.shape
    N, NF = w.shape[1], wf.shape[1]
    tpb = S // tm
    return pl.pallas_call(
        _inproj_both_kernel,
        grid=(T // tm, N // tn),
        in_specs=[
            pl.BlockSpec((tm, D), lambda i, j: (i, 0)),
            pl.BlockSpec((1, D), lambda i, j: (0, 0)),
            pl.BlockSpec((1, 1, D), lambda i, j: (i // tpb, 0, 0)),
            pl.BlockSpec((1, 1, D), lambda i, j: (i // tpb, 0, 0)),
            pl.BlockSpec((D, tn), lambda i, j: (0, j)),
            pl.BlockSpec((D, NF), lambda i, j: (0, 0)),
        ],
        out_specs=[pl.BlockSpec((tn // HEAD_W, tm, HEAD_W), lambda i, j: (j, i, 0)),
                   pl.BlockSpec((tm, NF), lambda i, j: (i, 0))],
        out_shape=[jax.ShapeDtypeStruct((N // HEAD_W, T, HEAD_W), bf16), jax.ShapeDtypeStruct((T, NF), f32)],
        scratch_shapes=[pltpu.VMEM((tm, D), bf16)],
        compiler_params=_cparams(("arbitrary", "arbitrary")),
        name="inproj_both",
    )(x2, g, sc, sh, w, wf)


def _inproj(x2, g, sc, sh, w, S, *, grouped, out_dtype, tm, tn):
    T, D = x2.shape
    N = w.shape[1]
    tpb = S // tm
    if grouped:
        out_shape = jax.ShapeDtypeStruct((N // HEAD_W, T, HEAD_W), out_dtype)
        out_spec = pl.BlockSpec((tn // HEAD_W, tm, HEAD_W), lambda i, j: (j, i, 0))
    else:
        out_shape = jax.ShapeDtypeStruct((T, N), out_dtype)
        out_spec = pl.BlockSpec((tm, tn), lambda i, j: (i, j))
    return pl.pallas_call(
        functools.partial(_inproj_kernel, grouped=grouped),
        grid=(T // tm, N // tn),
        in_specs=[
            pl.BlockSpec((tm, D), lambda i, j: (i, 0)),
            pl.BlockSpec((1, D), lambda i, j: (0, 0)),
            pl.BlockSpec((1, 1, D), lambda i, j: (i // tpb, 0, 0)),
            pl.BlockSpec((1, 1, D), lambda i, j: (i // tpb, 0, 0)),
            pl.BlockSpec((D, tn), lambda i, j: (0, j)),
        ],
        out_specs=out_spec,
        out_shape=out_shape,
        scratch_shapes=[pltpu.VMEM((tm, D), bf16)],
        compiler_params=_cparams(("arbitrary", "arbitrary")),
        name="inproj_grouped" if grouped else "inproj_plain",
    )(x2, g, sc, sh, w)


def _rope_kernel(pos_ref, inv_ref, cos_ref, sin_ref):
    ang = pos_ref[...].astype(f32) * inv_ref[...]
    cos_ref[...] = jnp.cos(ang)
    sin_ref[...] = jnp.sin(ang)


def _rope_tables(pos2, inv_full, *, tm):
    T = pos2.shape[0]
    tile = pl.BlockSpec((tm, 128), lambda i: (i, 0))
    return pl.pallas_call(
        _rope_kernel,
        grid=(T // tm,),
        in_specs=[pl.BlockSpec((tm, 1), lambda i: (i, 0)), pl.BlockSpec((1, 128), lambda i: (0, 0))],
        out_specs=[tile, tile],
        out_shape=[jax.ShapeDtypeStruct((T, 128), f32)] * 2,
        compiler_params=_cparams(("arbitrary",)),
        name="rope_tables",
    )(pos2, inv_full)


def _mla_prep_kernel(f_ref, cos_ref, sin_ref, gq_ref, gkv_ref, wq_ref, wqp_ref, wk_ref, wv_ref,
                     q_ref, k_ref, v_ref, *, scale):
    cq = f_ref[:, 0:MLA_Q_RANK]
    ckv = f_ref[:, MLA_Q_RANK:MLA_Q_RANK + MLA_KV_RANK]
    kr = f_ref[:, 384:512]
    krp = f_ref[:, 512:640]
    nq = (cq * lax.rsqrt(jnp.mean(cq * cq, axis=-1, keepdims=True) + EPS) * gq_ref[...]).astype(bf16)
    nkv = (ckv * lax.rsqrt(jnp.mean(ckv * ckv, axis=-1, keepdims=True) + EPS) * gkv_ref[...]).astype(bf16)
    cosf = cos_ref[...]
    sinf = sin_ref[...]
    q1 = _dot(nq, wq_ref[...])
    q2 = _dot(nq, wqp_ref[...])
    kn = _dot(nkv, wk_ref[...])
    vv = _dot(nkv, wv_ref[...])
    krope = kr * cosf + krp * sinf
    for h in range(HEADS):
        sl = slice(h * MLA_QK_PAD, (h + 1) * MLA_QK_PAD)
        q_ref[h] = ((q1[:, sl] * cosf + q2[:, sl] * sinf) * scale).astype(bf16)
        k_ref[h] = (kn[:, sl] + krope).astype(bf16)
        v_ref[h] = vv[:, h * MLA_V:(h + 1) * MLA_V].astype(bf16)


def _mla_prep(F, cos_t, sin_t, gq, gkv, wq, wqp, wk, wv, *, tm):
    T = F.shape[0]
    scale = (MLA_NOPE + MLA_ROPE) ** -0.5
    full = lambda a: pl.BlockSpec(a.shape, lambda i: (0,) * a.ndim)
    return pl.pallas_call(
        functools.partial(_mla_prep_kernel, scale=scale),
        grid=(T // tm,),
        in_specs=[
            pl.BlockSpec((tm, F.shape[1]), lambda i: (i, 0)),
            pl.BlockSpec((tm, 128), lambda i: (i, 0)),
            pl.BlockSpec((tm, 128), lambda i: (i, 0)),
            full(gq), full(gkv), full(wq), full(wqp), full(wk), full(wv),
        ],
        out_specs=[
            pl.BlockSpec((HEADS, tm, MLA_QK_PAD), lambda i: (0, i, 0)),
            pl.BlockSpec((HEADS, tm, MLA_QK_PAD), lambda i: (0, i, 0)),
            pl.BlockSpec((HEADS, tm, MLA_V), lambda i: (0, i, 0)),
        ],
        out_shape=[
            jax.ShapeDtypeStruct((HEADS, T, MLA_QK_PAD), bf16),
            jax.ShapeDtypeStruct((HEADS, T, MLA_QK_PAD), bf16),
            jax.ShapeDtypeStruct((HEADS, T, MLA_V), bf16),
        ],
        compiler_params=_cparams(("arbitrary",)),
        name="mla_prep",
    )(F, cos_t, sin_t, gq, gkv, wq, wqp, wk, wv)


def _chunk_plan(i, bq, ck):
    n_full = (i * bq) // ck
    return n_full, n_full * ck


def _tail_valid(i, bq, ck, tail_off):
    row = lax.broadcasted_iota(i32, (bq, ck), 0)
    col = lax.broadcasted_iota(i32, (bq, ck), 1)
    return col + (tail_off - i * bq) <= row


def _fill_v_ones(v_ref, vp_scr):
    for h in range(v_ref.shape[0]):
        vp_scr[h, :, 0:HEAD_W] = v_ref[h]
        vp_scr[h, :, HEAD_W:2 * HEAD_W] = jnp.ones((v_ref.shape[1], HEAD_W), bf16)


def _softmax_steps_v1(scores, vps, carry, rows_may_be_empty=False):
    n = len(scores)
    ms = [carry[2 * k] for k in range(n)]
    accs = [carry[2 * k + 1] for k in range(n)]
    m_new = [jnp.maximum(ms[k], jnp.max(scores[k], axis=1, keepdims=True)) for k in range(n)]
    m_ref = [jnp.where(m == NEG_INF, 0.0, m) for m in m_new] if rows_may_be_empty else m_new
    ps = [jnp.exp((scores[k] - m_ref[k]).astype(bf16)) for k in range(n)]
    out = ()
    for k in range(n):
        out += (m_new[k], jnp.exp(ms[k] - m_ref[k]) * accs[k] + _dot(ps[k], vps[k]))
    return out


def _softmax_init_v1(bq):
    return (jnp.full((bq, 1), NEG_INF, f32), jnp.zeros((bq, 2 * HEAD_W), f32))


def _softmax_finish_v1(acc):
    return acc[:, 0:HEAD_W] / acc[:, HEAD_W:HEAD_W + 1]


def _mla_attn_kernel(q_ref, k_ref, v_ref, o_ref, vp_scr, *, bq, ck):
    i = pl.program_id(1)
    n_full, tail_off = _chunk_plan(i, bq, ck)

    @pl.when(i == 0)
    def _():
        _fill_v_ones(v_ref, vp_scr)

    def chunk(off, carry, valid):
        off = pl.multiple_of(off, ck)
        scores = [_nt_dot(q_ref[h], k_ref[h, pl.ds(off, ck), :]) for h in range(HEADS)]
        if valid is not None:
            scores = [jnp.where(valid, s, NEG_INF) for s in scores]
        return _softmax_steps_v1(scores, [vp_scr[h, pl.ds(off, ck), :] for h in range(HEADS)], carry)

    carry = lax.fori_loop(0, n_full, lambda c, carry: chunk(c * ck, carry, None), _softmax_init_v1(bq) * HEADS)
    carry = chunk(tail_off, carry, _tail_valid(i, bq, ck, tail_off))
    for h in range(HEADS):
        o_ref[h] = _softmax_finish_v1(carry[2 * h + 1]).astype(o_ref.dtype)


def _mla_attention(qm, km, vm, B, S, *, bq, ck):
    H, T, dk = qm.shape
    dv = vm.shape[2]
    nq = S // bq
    return pl.pallas_call(
        functools.partial(_mla_attn_kernel, bq=bq, ck=ck),
        grid=(B, nq),
        in_specs=[
            pl.BlockSpec((H, bq, dk), lambda b, i: (0, b * nq + i, 0)),
            pl.BlockSpec((H, S, dk), lambda b, i: (0, b, 0)),
            pl.BlockSpec((H, S, dv), lambda b, i: (0, b, 0)),
        ],
        out_specs=pl.BlockSpec((H, bq, dv), lambda b, i: (0, b * nq + i, 0)),
        out_shape=jax.ShapeDtypeStruct((H, T, dv), bf16),
        scratch_shapes=[pltpu.VMEM((H, S, 2 * HEAD_W), bf16)],
        compiler_params=_cparams(("arbitrary", "arbitrary")),
        name="mla_attention",
    )(qm, km, vm)


def _diff_attn_kernel(q_ref, k_ref, v_ref, slope_ref, lq1_ref, lk1_ref, lq2_ref, lk2_ref, g_ref, o_ref,
                      vp_scr, *, bq, ck, lam_init):
    i = pl.program_id(1)
    n_full, tail_off = _chunk_plan(i, bq, ck)
    lane = lax.broadcasted_iota(i32, (bq, q_ref.shape[2]), 1)
    col = lax.broadcasted_iota(i32, (1, ck), 1).astype(f32)

    @pl.when(i == 0)
    def _():
        _fill_v_ones(v_ref, vp_scr)

    def chunk(off, carry, valid):
        off = pl.multiple_of(off, ck)
        pos = col + (off - i * bq).astype(f32)
        scores, vps = [], []
        for h in range(HEADS):
            q = q_ref[h]
            zero = jnp.zeros_like(q)
            k = k_ref[h, pl.ds(off, ck), :]
            for qm in (jnp.where(lane < DIFF_QK, q, zero), jnp.where(lane < DIFF_QK, zero, q)):
                scores.append(_nt_dot(qm, k))
                vps.append(vp_scr[h, pl.ds(off, ck), :])
        for n in range(len(scores)):
            s = scores[n] + slope_ref[n // 2] * pos
            scores[n] = s if valid is None else jnp.where(valid, s, NEG_INF)
        return _softmax_steps_v1(scores, vps, carry)

    carry = lax.fori_loop(0, n_full, lambda c, carry: chunk(c * ck, carry, None),
                          _softmax_init_v1(bq) * (2 * HEADS))
    carry = chunk(tail_off, carry, _tail_valid(i, bq, ck, tail_off))
    lam = (jnp.exp(jnp.sum(lq1_ref[...] * lk1_ref[...], axis=1, keepdims=True))
           - jnp.exp(jnp.sum(lq2_ref[...] * lk2_ref[...], axis=1, keepdims=True)) + lam_init)
    for h in range(HEADS):
        m0, a0, m1, a1 = carry[4 * h:4 * h + 4]
        o = _softmax_finish_v1(a0) - lam * _softmax_finish_v1(a1)
        y = o * lax.rsqrt(jnp.mean(o * o, axis=-1, keepdims=True) + EPS) * g_ref[...]
        o_ref[h] = (y * (1.0 - lam_init)).astype(o_ref.dtype)


def _diff_attention(P, slopes, lq1, lk1, lq2, lk2, g, B, S, *, bq, ck, lam_init, gq, gk, gv):
    G, T, hw = P.shape
    nq = S // bq
    vec = lambda a: pl.BlockSpec(a.shape, lambda b, i: (0,) * a.ndim)
    return pl.pallas_call(
        functools.partial(_diff_attn_kernel, bq=bq, ck=ck, lam_init=lam_init),
        grid=(B, nq),
        in_specs=[
            pl.BlockSpec((HEADS, bq, hw), lambda b, i: (gq // HEADS, b * nq + i, 0)),
            pl.BlockSpec((HEADS, S, hw), lambda b, i: (gk // HEADS, b, 0)),
            pl.BlockSpec((HEADS, S, hw), lambda b, i: (gv // HEADS, b, 0)),
            vec(slopes), vec(lq1), vec(lk1), vec(lq2), vec(lk2), vec(g),
        ],
        out_specs=pl.BlockSpec((HEADS, bq, hw), lambda b, i: (0, b * nq + i, 0)),
        out_shape=jax.ShapeDtypeStruct((HEADS, T, hw), bf16),
        scratch_shapes=[pltpu.VMEM((HEADS, S, 2 * HEAD_W), bf16)],
        compiler_params=_cparams(("arbitrary", "arbitrary")),
        name="diff_attention",
    )(P, P, P, slopes, lq1, lk1, lq2, lk2, g)


SB_GROUP = 256


def _sb_attn_kernel(q_ref, k_ref, v_ref, o_ref, *, bq, ck):
    i = pl.program_id(1)
    n_full, tail_off = _chunk_plan(i, bq, ck)
    dv = v_ref.shape[2]
    gw = min(SB_GROUP, ck)
    rr = lax.broadcasted_iota(i32, (gw, gw), 0)
    cc = lax.broadcasted_iota(i32, (gw, gw), 1)
    upper = jnp.where(rr > cc, 1.0, 0.0).astype(bf16)

    def chunk(off, carry, strict):
        off = pl.multiple_of(off, ck)
        out = ()
        zs = [_nt_dot(q_ref[h], k_ref[h, pl.ds(off, ck), :]) for h in range(HEADS)]
        for h in range(HEADS):
            run, acc = carry[2 * h:2 * h + 2]
            z = zs[h]
            lsm = -(jnp.maximum(z, 0.0) + jnp.log(1.0 + jnp.exp2(jnp.abs(z) * (-LOG2E))))
            if strict is not None:
                lsm = jnp.where(strict, lsm, 0.0)
            parts = []
            for g in reversed(range(ck // gw)):
                x = lsm[:, g * gw:(g + 1) * gw]
                parts.append(_dot(x.astype(bf16), upper) + run)
                run = run + jnp.sum(x, axis=1, keepdims=True)
            between = jnp.concatenate(parts[::-1], axis=1) if len(parts) > 1 else parts[0]
            arg = z + lsm + between
            if strict is not None:
                arg = jnp.where(strict, arg, NEG_INF)
            acc = acc + _dot(jnp.exp(arg.astype(bf16)), v_ref[h, pl.ds(off, ck), :])
            out += (run, acc)
        return out

    row = lax.broadcasted_iota(i32, (bq, ck), 0)
    col = lax.broadcasted_iota(i32, (bq, ck), 1)
    strict = col + (tail_off - i * bq) < row
    init = (jnp.zeros((bq, 1), f32), jnp.zeros((bq, dv), f32)) * HEADS
    carry = chunk(tail_off, init, strict)
    carry = lax.fori_loop(0, n_full, lambda n, carry: chunk((n_full - 1 - n) * ck, carry, None), carry)
    for h in range(HEADS):
        o_ref[h] = carry[2 * h + 1].astype(o_ref.dtype)


def _sb_attention(P, B, S, *, bq, ck, gq, gk, gv):
    G, T, hw = P.shape
    nq = S // bq
    return pl.pallas_call(
        functools.partial(_sb_attn_kernel, bq=bq, ck=ck),
        grid=(B, nq),
        in_specs=[
            pl.BlockSpec((HEADS, bq, hw), lambda b, i: (gq // HEADS, b * nq + i, 0)),
            pl.BlockSpec((HEADS, S, hw), lambda b, i: (gk // HEADS, b, 0)),
            pl.BlockSpec((HEADS, S, hw), lambda b, i: (gv // HEADS, b, 0)),
        ],
        out_specs=pl.BlockSpec((HEADS, bq, hw), lambda b, i: (0, b * nq + i, 0)),
        out_shape=jax.ShapeDtypeStruct((HEADS, T, hw), bf16),
        compiler_params=_cparams(("arbitrary", "arbitrary")),
        name="sb_attention",
    )(P, P, P)


def _dsa_kernel(aq_ref, iq_ref, ak_ref, av_ref, ik_ref, iw_ref, slope_ref, o_ref, key_scr, hi_scr, lo_scr,
                vp_scr, *, bq, ck, topk):
    i = pl.program_id(1)
    n_full, tail_off = _chunk_plan(i, bq, ck)
    n_chunks = n_full + 1

    @pl.when(i == 0)
    def _():
        _fill_v_ones(av_ref, vp_scr)

    w_t = iw_ref[...].T
    gw = min(SB_GROUP, ck)
    rr = lax.broadcasted_iota(i32, (gw, gw), 0)
    cc = lax.broadcasted_iota(i32, (gw, gw), 1)
    incl = jnp.where(cc <= rr, 1.0, 0.0).astype(bf16)
    col = lax.broadcasted_iota(i32, (1, ck), 1).astype(f32)
    key_i = lax.broadcasted_iota(i32, (ck, bq), 0)
    qry_i = lax.broadcasted_iota(i32, (ck, bq), 1)
    tail_valid_t = key_i + (tail_off - i * bq) <= qry_i

    def score_chunk(c, valid_t):
        off = pl.multiple_of(c * ck, ck)
        ik = ik_ref[0, pl.ds(off, ck), :]
        sc = jnp.zeros((ck, bq), f32)
        for h in range(IDX_HEADS):
            sc = sc + jnp.maximum(_nt_dot(ik, iq_ref[h]), 0.0) * w_t[h:h + 1, :]
        if valid_t is not None:
            sc = jnp.where(valid_t, sc, NEG_INF)
        bits = lax.bitcast_convert_type(sc + 0.0, i32)
        key = jnp.where(bits < 0, bits ^ jnp.int32(0x7FFFFFFF), bits)
        key_scr[c] = key
        hi_scr[c] = lax.shift_right_arithmetic(key, 16).astype(i16)
        lo_scr[c] = ((key & 0xFFFF) - 32768).astype(i16)

    def score_body(c, _):
        score_chunk(c, None)
        return 0

    lax.fori_loop(0, n_full, score_body, 0)
    score_chunk(n_full, tail_valid_t)

    lanes_acc = 32

    def rows(v):
        return jnp.broadcast_to(v, (lanes_acc, bq))[None]

    def count16(ref, pred):
        def body(c, acc):
            v = ref[c].reshape(ck // lanes_acc, lanes_acc, bq)
            hit = jnp.where(pred(v), jnp.int16(1), jnp.int16(0))
            parts = [hit[g] for g in range(ck // lanes_acc)]
            while len(parts) > 1:
                parts = [a + b for a, b in zip(parts[0::2], parts[1::2])]
            return acc + parts[0]
        acc = lax.fori_loop(0, n_chunks, body, jnp.zeros((lanes_acc, bq), i16))
        return jnp.sum(acc.astype(f32), axis=0, keepdims=True)

    def search16(ref, need):
        def body(it, prefix):
            cand = prefix + lax.shift_left(jnp.int32(1), jnp.int32(15) - it)
            cand_r = rows(cand.astype(i16))
            cnt = count16(ref, lambda v: v >= cand_r)
            return jnp.where(cnt >= need, cand, prefix)
        return lax.fori_loop(0, 16, body, jnp.full((1, bq), I16_MIN, i32))

    tau_hi = search16(hi_scr, float(topk))
    tau_hi_r = rows(tau_hi.astype(i16))
    need_lo = float(topk) - count16(hi_scr, lambda v: v > tau_hi_r)
    tau_hi_b = tau_hi.astype(i16)

    def mask_lo(c, _):
        lo_scr[c] = jnp.where(hi_scr[c] == tau_hi_b, lo_scr[c], jnp.int16(I16_MIN))
        return 0

    lax.fori_loop(0, n_chunks, mask_lo, 0)
    tau_lo = search16(lo_scr, need_lo)
    tau = tau_hi * 65536 + (tau_lo + 32768)

    def count(pred):
        def body(c, acc):
            key = key_scr[c].reshape(ck // lanes_acc, lanes_acc, bq)
            return acc + jnp.sum(jnp.where(pred(key), 1.0, 0.0), axis=0)
        acc = lax.fori_loop(0, n_chunks, body, jnp.zeros((lanes_acc, bq), f32))
        return jnp.sum(acc, axis=0, keepdims=True)

    tau_r = rows(tau)
    n_take = float(topk) - count(lambda key: key > tau_r)

    def attn_chunk(c, carry, valid_t):
        off = pl.multiple_of(c * ck, ck)
        eq_run = carry[0]
        key = key_scr[c]
        eq = key == tau
        eqf = jnp.where(eq, 1.0, 0.0)
        pcs = []
        for g in range(ck // gw):
            e = eqf[g * gw:(g + 1) * gw, :]
            pcs.append(_dot(incl, e.astype(bf16)) + eq_run)
            eq_run = eq_run + jnp.sum(e, axis=0, keepdims=True)
        pc = jnp.concatenate(pcs, axis=0) if len(pcs) > 1 else pcs[0]
        addm_t = jnp.where(key > tau, 0.0, jnp.where(eq, jnp.where(pc <= n_take, 0.0, NEG_INF), NEG_INF))
        if valid_t is not None:
            addm_t = jnp.where(valid_t, addm_t, NEG_INF)
        addm = addm_t.T
        k = ak_ref[0, pl.ds(off, ck), :]
        vp = vp_scr[0, pl.ds(off, ck), :]
        pos = col + (off - i * bq).astype(f32)
        scores = [_nt_dot(aq_ref[h], k) for h in range(HEADS)]
        scores = [scores[h] + slope_ref[h] * pos + addm for h in range(HEADS)]
        return (eq_run,) + _softmax_steps_v1(scores, [vp] * HEADS, carry[1:], rows_may_be_empty=True)

    init = (jnp.zeros((1, bq), f32),) + _softmax_init_v1(bq) * HEADS
    carry = lax.fori_loop(0, n_full, lambda c, carry: attn_chunk(c, carry, None), init)
    carry = attn_chunk(n_full, carry, tail_valid_t)
    for h in range(HEADS):
        o_ref[h] = _softmax_finish_v1(carry[2 + 2 * h]).astype(o_ref.dtype)


def _dsa_attention(P, F, slopes, B, S, *, bq, ck, topk, gaq, giq, gak, gav, gik, iw_col):
    G, T, hw = P.shape
    nq = S // bq
    return pl.pallas_call(
        functools.partial(_dsa_kernel, bq=bq, ck=ck, topk=topk),
        grid=(B, nq),
        in_specs=[
            pl.BlockSpec((HEADS, bq, hw), lambda b, i: (gaq // HEADS, b * nq + i, 0)),
            pl.BlockSpec((IDX_HEADS, bq, hw), lambda b, i: (giq // IDX_HEADS, b * nq + i, 0)),
            pl.BlockSpec((1, S, hw), lambda b, i: (gak, b, 0)),
            pl.BlockSpec((1, S, hw), lambda b, i: (gav, b, 0)),
            pl.BlockSpec((1, S, hw), lambda b, i: (gik, b, 0)),
            pl.BlockSpec((bq, 128), lambda b, i: (b * nq + i, iw_col // 128)),
            pl.BlockSpec(slopes.shape, lambda b, i: (0, 0, 0)),
        ],
        out_specs=pl.BlockSpec((HEADS, bq, hw), lambda b, i: (0, b * nq + i, 0)),
        out_shape=jax.ShapeDtypeStruct((HEADS, T, hw), bf16),
        scratch_shapes=[pltpu.VMEM((S // ck, ck, bq), i32), pltpu.VMEM((S // ck, ck, bq), i16),
                        pltpu.VMEM((S // ck, ck, bq), i16), pltpu.VMEM((1, S, 2 * HEAD_W), bf16)],
        compiler_params=_cparams(("arbitrary", "arbitrary")),
        name="dsa_attention",
    )(P, P, P, P, P, F, slopes)


def _merge_kernel(x_ref, g_ref, sc_ref, sh_ref, g1_ref, ya_ref, yb_ref, yc_ref, yd_ref,
                  wg_ref, wb_ref, wo_ref, o_ref):
    x = x_ref[...]
    D = x.shape[1]
    h = _norm_mod(x, g_ref[...], sc_ref[0], sh_ref[0]).astype(bf16)
    merged = jnp.zeros(x.shape, f32)
    for n, y_ref in enumerate((ya_ref, yb_ref, yc_ref, yd_ref)):
        gate = jax.nn.sigmoid(_dot(h, wg_ref[:, n * D:(n + 1) * D]))
        y = jnp.concatenate([y_ref[hh] for hh in range(HEADS)], axis=1)
        merged = merged + gate * _dot(y, wb_ref[n])
    o_ref[...] = x + g1_ref[0] * _dot(merged.astype(bf16), wo_ref[...])


def _merge(x2, g, sc, sh, g1, ys, wg, wb, wo, S, *, tm):
    T, D = x2.shape
    tpb = S // tm
    row = pl.BlockSpec((tm, D), lambda i: (i, 0))
    per_b = pl.BlockSpec((1, 1, D), lambda i: (i // tpb, 0, 0))
    yspec = pl.BlockSpec((HEADS, tm, HEAD_W), lambda i: (0, i, 0))
    full = lambda a: pl.BlockSpec(a.shape, lambda i: (0,) * a.ndim)
    return pl.pallas_call(
        _merge_kernel,
        grid=(T // tm,),
        in_specs=[row, full(g), per_b, per_b, per_b, yspec, yspec, yspec, yspec, full(wg), full(wb), full(wo)],
        out_specs=row,
        out_shape=jax.ShapeDtypeStruct((T, D), f32),
        compiler_params=_cparams(("arbitrary",)),
        name="merge",
    )(x2, g, sc, sh, g1, *ys, wg, wb, wo)


def _finish(x, g2, f, fg_ref, o_ref):
    out = x + g2 * f
    if fg_ref is not None:
        out = out * lax.rsqrt(jnp.mean(out * out, axis=-1, keepdims=True) + EPS) * fg_ref[...]
    o_ref[...] = out


def _ffn_kernel(x_ref, g_ref, sc_ref, sh_ref, g2_ref, w1_ref, w3_ref, w2_ref, *rest, final):
    fg_ref = rest[0] if final else None
    o_ref, h_scr, acc_scr = rest[-3:]
    j = pl.program_id(1)

    @pl.when(j == 0)
    def _():
        h_scr[...] = _norm_mod(x_ref[...], g_ref[...], sc_ref[0], sh_ref[0]).astype(bf16)
        acc_scr[...] = jnp.zeros_like(acc_scr)

    h = h_scr[...]
    a = _dot(h, w1_ref[...])
    u = (a * jax.nn.sigmoid(a)) * _dot(h, w3_ref[...])
    acc_scr[...] += _dot(u.astype(bf16), w2_ref[...])

    @pl.when(j == pl.num_programs(1) - 1)
    def _():
        _finish(x_ref[...], g2_ref[0], acc_scr[...], fg_ref, o_ref)


def _ffn(x2, g, sc, sh, g2, w1, w3, w2, final_g, S, *, tm, tn):
    T, D = x2.shape
    F = w1.shape[1]
    tpb = S // tm
    row = pl.BlockSpec((tm, D), lambda i, j: (i, 0))
    per_b = pl.BlockSpec((1, 1, D), lambda i, j: (i // tpb, 0, 0))
    vec = pl.BlockSpec((1, D), lambda i, j: (0, 0))
    in_specs = [row, vec, per_b, per_b, per_b,
                pl.BlockSpec((D, tn), lambda i, j: (0, j)),
                pl.BlockSpec((D, tn), lambda i, j: (0, j)),
                pl.BlockSpec((tn, D), lambda i, j: (j, 0))]
    args = [x2, g, sc, sh, g2, w1, w3, w2]
    if final_g is not None:
        in_specs.append(vec)
        args.append(final_g)
    return pl.pallas_call(
        functools.partial(_ffn_kernel, final=final_g is not None),
        grid=(T // tm, F // tn),
        in_specs=in_specs,
        out_specs=row,
        out_shape=jax.ShapeDtypeStruct((T, D), f32),
        scratch_shapes=[pltpu.VMEM((tm, D), bf16), pltpu.VMEM((tm, D), f32)],
        compiler_params=_cparams(("arbitrary", "arbitrary")),
        name="ffn",
    )(*args)


def _router_kernel(x_ref, g_ref, sc_ref, sh_ref, rw_ref, rb_ref, comb_ref, cnt_ref):
    h = _norm_mod(x_ref[...], g_ref[...], sc_ref[0], sh_ref[0]).astype(bf16)
    logits = _dot(h, rw_ref[...]) + rb_ref[...]
    lane = lax.broadcasted_iota(i32, logits.shape, 1)
    big = jnp.int32(logits.shape[1])
    m1 = jnp.max(logits, axis=1, keepdims=True)
    i1 = jnp.min(jnp.where(logits == m1, lane, big), axis=1, keepdims=True)
    rest_l = jnp.where(lane == i1, NEG_INF, logits)
    m2 = jnp.max(rest_l, axis=1, keepdims=True)
    i2 = jnp.min(jnp.where(rest_l == m2, lane, big), axis=1, keepdims=True)
    e2 = jnp.exp(m2 - m1)
    den = 1.0 + e2
    comb = jnp.where(lane == i1, 1.0 / den, jnp.where(lane == i2, e2 / den, 0.0))
    comb_ref[...] = comb
    cnt_ref[0] = jnp.sum(jnp.where(comb > 0.0, 1.0, 0.0), axis=0, keepdims=True)


def _router(x2, g, sc, sh, rw, rb, S, *, tm):
    T, D = x2.shape
    tpb = S // tm
    nt = T // tm
    return pl.pallas_call(
        _router_kernel,
        grid=(nt,),
        in_specs=[pl.BlockSpec((tm, D), lambda i: (i, 0)),
                  pl.BlockSpec((1, D), lambda i: (0, 0)),
                  pl.BlockSpec((1, 1, D), lambda i: (i // tpb, 0, 0)),
                  pl.BlockSpec((1, 1, D), lambda i: (i // tpb, 0, 0)),
                  pl.BlockSpec(rw.shape, lambda i: (0, 0)),
                  pl.BlockSpec(rb.shape, lambda i: (0, 0))],
        out_specs=[pl.BlockSpec((tm, 128), lambda i: (i, 0)),
                   pl.BlockSpec((1, 1, 128), lambda i: (i, 0, 0))],
        out_shape=[jax.ShapeDtypeStruct((T, 128), f32), jax.ShapeDtypeStruct((nt, 1, 128), f32)],
        compiler_params=_cparams(("arbitrary",)),
        name="router",
    )(x2, g, sc, sh, rw, rb)


def _moe_kernel(sid_ref, nch_ref, x_ref, g_ref, sc_ref, sh_ref, g2_ref, comb_ref, w1_ref, w3_ref, w2_ref, *rest,
                final, sizes):
    fg_ref = rest[0] if final else None
    o_ref, h_scr, tri_scr, rank_scr, rank_t_scr, xg_scr, yg_scr, out_scr = rest[-8:]
    i = pl.program_id(0)
    e = pl.program_id(1)
    j = pl.program_id(2)
    n_e = pl.num_programs(1)
    tm = x_ref.shape[0]
    sid = sid_ref[i * n_e + e]
    nch = nch_ref[i * n_e + e]

    def dispatch(phase):
        for k, r in enumerate(sizes):
            @pl.when(sid == k)
            def _(r=r):
                phase(r, 0)

        @pl.when(sid == len(sizes))
        def _():
            def body(c, _):
                phase(sizes[-1], c)
                return 0
            lax.fori_loop(0, nch, body, 0)

    def chunk_rows(rows, c):
        if isinstance(c, int):
            return pl.ds(c * rows, rows)
        return pl.ds(pl.multiple_of(c * rows, 16), rows)

    def first_slot(rows, c):
        return float(c * rows) if isinstance(c, int) else (c * rows).astype(f32)

    @pl.when(jnp.logical_and(i == 0, jnp.logical_and(e == 0, j == 0)))
    def _():
        rr = lax.broadcasted_iota(i32, (tm, tm), 0)
        cc = lax.broadcasted_iota(i32, (tm, tm), 1)
        tri_scr[...] = jnp.where(cc < rr, 1.0, 0.0).astype(bf16)

    @pl.when(jnp.logical_and(e == 0, j == 0))
    def _():
        h_scr[...] = _norm_mod(x_ref[...], g_ref[...], sc_ref[0], sh_ref[0]).astype(bf16)
        out_scr[...] = jnp.zeros_like(out_scr)
        member = jnp.where(comb_ref[...] > 0.0, 1.0, 0.0)
        rank = jnp.where(member > 0.0, _dot(tri_scr[...], member.astype(bf16)), -1.0)
        rank_scr[...] = rank
        rank_t_scr[...] = rank.T

    @pl.when(j == 0)
    def _():
        rank_row = rank_t_scr[pl.ds(e, 1), :]

        def gather(rows, c):
            slot = lax.broadcasted_iota(i32, (rows, tm), 0).astype(f32) + first_slot(rows, c)
            pick = jnp.where(slot == rank_row, 1.0, 0.0).astype(bf16)
            xg_scr[chunk_rows(rows, c), :] = _dot(pick, h_scr[...]).astype(bf16)
            yg_scr[chunk_rows(rows, c), :] = jnp.zeros((rows, yg_scr.shape[1]), f32)

        dispatch(gather)

    def expert(rows, c):
        xg = xg_scr[chunk_rows(rows, c), :]
        a = _dot(xg, w1_ref[0])
        u = (a * jax.nn.sigmoid(a)) * _dot(xg, w3_ref[0])
        yg_scr[chunk_rows(rows, c), :] += _dot(u.astype(bf16), w2_ref[0])

    dispatch(expert)

    last_j = j == pl.num_programs(2) - 1

    @pl.when(last_j)
    def _():
        lane = lax.broadcasted_iota(i32, (tm, 128), 1)
        sel = lane == e
        rank_col = jnp.sum(jnp.where(sel, rank_scr[...], 0.0), axis=1, keepdims=True)
        w_col = jnp.sum(jnp.where(sel, comb_ref[...], 0.0), axis=1, keepdims=True)

        def scatter(rows, c):
            slot = lax.broadcasted_iota(i32, (tm, rows), 1).astype(f32) + first_slot(rows, c)
            place = jnp.where(slot == rank_col, 1.0, 0.0).astype(bf16)
            out_scr[...] += w_col * _dot(place, yg_scr[chunk_rows(rows, c), :].astype(bf16))

        dispatch(scatter)

    @pl.when(jnp.logical_and(last_j, e == n_e - 1))
    def _():
        _finish(x_ref[...], g2_ref[0], out_scr[...], fg_ref, o_ref)


def _moe_plan(cnt, sizes):
    cnt = cnt.astype(i32).reshape(-1)
    sid = sum((cnt > r).astype(i32) for r in sizes)
    nch = (cnt + (sizes[-1] - 1)) // sizes[-1]
    return sid, nch


def _moe(x2, g, sc, sh, g2, comb, sid, nch, w1, w3, w2, final_g, S, *, tm, tn, sizes):
    T, D = x2.shape
    E, _, F = w1.shape
    nj = F // tn
    tpb = S // tm
    cap = -(-tm // sizes[-1]) * sizes[-1]
    row = pl.BlockSpec((tm, D), lambda i, e, j, *_: (i, 0))
    per_b = pl.BlockSpec((1, 1, D), lambda i, e, j, *_: (i // tpb, 0, 0))
    vec = pl.BlockSpec((1, D), lambda i, e, j, *_: (0, 0))
    in_specs = [row, vec, per_b, per_b, per_b,
                pl.BlockSpec((tm, 128), lambda i, e, j, *_: (i, 0)),
                pl.BlockSpec((1, D, tn), lambda i, e, j, *_: (e, 0, j)),
                pl.BlockSpec((1, D, tn), lambda i, e, j, *_: (e, 0, j)),
                pl.BlockSpec((1, tn, D), lambda i, e, j, *_: (e, j, 0))]
    args = [x2, g, sc, sh, g2, comb, w1, w3, w2]
    if final_g is not None:
        in_specs.append(vec)
        args.append(final_g)
    return pl.pallas_call(
        functools.partial(_moe_kernel, final=final_g is not None, sizes=tuple(sizes)),
        grid_spec=pltpu.PrefetchScalarGridSpec(
            num_scalar_prefetch=2,
            grid=(T // tm, E, nj),
            in_specs=in_specs,
            out_specs=row,
            scratch_shapes=[pltpu.VMEM((tm, D), bf16),
                            pltpu.VMEM((tm, tm), bf16),
                            pltpu.VMEM((tm, 128), f32),
                            pltpu.VMEM((128, tm), f32),
                            pltpu.VMEM((cap, D), bf16),
                            pltpu.VMEM((cap, D), f32),
                            pltpu.VMEM((tm, D), f32)]),
        out_shape=jax.ShapeDtypeStruct((T, D), f32),
        compiler_params=_cparams(("arbitrary", "arbitrary", "arbitrary")),
        name="moe",
    )(sid, nch, *args)


_IN_SIZES = (256, 64, 64, 256, 64, 4, 256, 128, 32, 256, 256, 256, 256, 256, 256, 4096)
_IN_NAMES = ("aq", "ak", "av", "iq", "ik", "iw", "cq", "ckv", "kr", "sq", "sk", "sv", "dq", "dk", "dv", "gates")

G_AQ, G_IQ, G_SQ, G_SK, G_SV, G_DQ, G_DK, G_DV, G_AK, G_AV, G_IK = 0, 4, 8, 12, 16, 20, 24, 28, 32, 33, 34
N_GROUPS = 36
F_IW_COL = 640


def _split_w_in(w):
    out, off = {}, 0
    for name, n in zip(_IN_NAMES, _IN_SIZES):
        out[name] = w[:, off:off + n]
        off += n
    return out


def _rot_half_cols(w):
    half = w.shape[1] // 2
    return jnp.concatenate([-w[:, half:], w[:, :half]], axis=1)


def _layer_weights(w_in, w_uq, w_ukv):
    c = _split_w_in(w_in)
    D = w_in.shape[0]
    z = lambda n: jnp.zeros((D, n), f32)
    w_main = jnp.concatenate([
        c["aq"] * 0.125, c["iq"] * 0.125, c["sq"] * 0.125, c["sk"], c["sv"],
        c["dq"] * (DIFF_QK ** -0.5), c["dk"], c["dv"], c["ak"], c["av"], c["ik"], z(64)], axis=1).astype(bf16)
    w_f = jnp.concatenate([
        c["cq"], c["ckv"],
        z(64), c["kr"], z(32),
        z(64), _rot_half_cols(c["kr"]), z(32),
        c["iw"] * (IDX_HEADS ** -0.5), z(124)], axis=1).astype(bf16)
    w_gates = c["gates"].astype(bf16)

    qr = w_uq.shape[0]
    kvr = w_ukv.shape[0]
    wq, wqp, wk, wv = [], [], [], []
    per_q = MLA_NOPE + MLA_ROPE
    per_kv = MLA_NOPE + MLA_V
    for h in range(HEADS):
        nope = w_uq[:, h * per_q:h * per_q + MLA_NOPE]
        rope = w_uq[:, h * per_q + MLA_NOPE:(h + 1) * per_q]
        wq += [nope, rope, jnp.zeros((qr, 32), f32)]
        wqp += [jnp.zeros((qr, 64), f32), _rot_half_cols(rope), jnp.zeros((qr, 32), f32)]
        wk += [w_ukv[:, h * per_kv:h * per_kv + MLA_NOPE], jnp.zeros((kvr, 64), f32)]
        wv += [w_ukv[:, h * per_kv + MLA_NOPE:(h + 1) * per_kv]]
    cat = lambda xs: jnp.concatenate(xs, axis=1).astype(bf16)
    return w_main, w_f, w_gates, cat(wq), cat(wqp), cat(wk), cat(wv)


def _alibi_slopes():
    n = 2 * HEADS
    sl = [2.0 ** (-(8.0 / n) * (k + 1)) for k in range(n)]
    return sl[0::2], sl[1::2]


def kernel(x, c, positions, ada_w, ada_b, norm1_g, norm2_g, w_in, mla_q_norm_g, mla_kv_norm_g, mla_w_uq,
           mla_w_ukv, diff_lq1, diff_lk1, diff_lq2, diff_lk2, diff_norm_g, w_branch, w_out, ffn_w1, ffn_w3,
           ffn_w2, router_w, router_b, moe_w1, moe_w3, moe_w2, final_norm_g):
    B, S, D = x.shape
    T = B * S
    depth = ada_w.shape[0]
    topk = min(TOPK_MAX, S // 4)
    tm = min(1024, S)
    tm_merge = min(512, S)
    bq = min(512, S)
    ck = min(512, S)

    x2 = x.reshape(T, D)
    pos2 = positions.reshape(T, 1)
    mod = _modulation(c, ada_w, ada_b)

    half = MLA_ROPE // 2
    inv = ROPE_THETA ** (-jnp.arange(half, dtype=f32) / half)
    inv_full = jnp.concatenate([jnp.zeros((64,), f32), inv, inv, jnp.zeros((32,), f32)]).reshape(1, 128)
    cos_t, sin_t = _rope_tables(pos2, inv_full, tm=tm)
    sl_a, sl_d = _alibi_slopes()
    slopes_a = jnp.broadcast_to(jnp.asarray(sl_a, f32)[:, None, None], (HEADS, 1, ck))
    slopes_d = jnp.broadcast_to(jnp.asarray(sl_d, f32)[:, None, None], (HEADS, 1, ck))

    row = lambda v: v.reshape(1, -1)
    for l in range(depth):
        sh1, sc1, g1, sh2, sc2, g2 = [mod[l, :, k * D:(k + 1) * D].reshape(B, 1, D) for k in range(6)]
        lam_init = 0.8 - 0.6 * math.exp(-0.3 * l)
        w_main, w_f, w_gates, wq, wqp, wk, wv = _layer_weights(w_in[l], mla_w_uq[l], mla_w_ukv[l])
        n1 = row(norm1_g[l])

        P, F = _inproj_both(x2, n1, sc1, sh1, w_main, w_f, S, tm=tm, tn=INPROJ_TN)
        qm, km, vm = _mla_prep(F, cos_t, sin_t, row(mla_q_norm_g[l]), row(mla_kv_norm_g[l]), wq, wqp, wk, wv,
                               tm=tm)

        ya = _dsa_attention(P, F, slopes_a, B, S, bq=bq, ck=ck, topk=topk, gaq=G_AQ, giq=G_IQ, gak=G_AK,
                            gav=G_AV, gik=G_IK, iw_col=F_IW_COL)
        yb = _mla_attention(qm, km, vm, B, S, bq=bq, ck=ck)
        yc = _sb_attention(P, B, S, bq=bq, ck=ck, gq=G_SQ, gk=G_SK, gv=G_SV)
        yd = _diff_attention(P, slopes_d, row(diff_lq1[l]), row(diff_lk1[l]), row(diff_lq2[l]),
                             row(diff_lk2[l]), row(diff_norm_g[l]), B, S, bq=bq, ck=ck, lam_init=lam_init,
                             gq=G_DQ, gk=G_DK, gv=G_DV)

        x2 = _merge(x2, n1, sc1, sh1, g1, (ya, yb, yc, yd), w_gates, w_branch[l].astype(bf16),
                    w_out[l].astype(bf16), S, tm=tm_merge)

        final_g = row(final_norm_g) if l == depth - 1 else None
        n2 = row(norm2_g[l])
        if l % 2 == 0:
            k = l // 2
            x2 = _ffn(x2, n2, sc2, sh2, g2, ffn_w1[k].astype(bf16), ffn_w3[k].astype(bf16),
                      ffn_w2[k].astype(bf16), final_g, S, tm=min(FFN_TM, S), tn=FFN_TN)
        else:
            k = l // 2
            E = router_w.shape[2]
            rw = jnp.concatenate([router_w[k], jnp.zeros((D, 128 - E), f32)], axis=1).astype(bf16)
            rb = jnp.concatenate([router_b[k], jnp.full((128 - E,), NEG_INF, f32)]).reshape(1, 128)
            comb, cnt = _router(x2, n2, sc2, sh2, rw, rb, S, tm=tm)
            sid, nch = _moe_plan(cnt[:, 0, :E], MOE_SIZES)
            x2 = _moe(x2, n2, sc2, sh2, g2, comb, sid, nch, moe_w1[k].astype(bf16), moe_w3[k].astype(bf16),
                      moe_w2[k].astype(bf16), final_g, S, tm=tm, tn=MOE_TN, sizes=MOE_SIZES)
    return x2.reshape(B, S, D)
```
